```python
import math
import jax
import jax.numpy as jnp
from jax import lax
import numpy as np

D_MODEL = 1024
BATCH = 16
SEQ = 256
DEPTH = 4
DEC_BATCH = 2
DEC_SEQ = 1024
PAST_LEN = 256

GRID_W = 64
NA_HEADS = 8
NA_HEAD_DIM = 64
NA_WIN_H = 8
NA_WIN_W = 16
ML_HEADS = 4
ML_HEAD_DIM = 128
ML_CHUNK = 128
CONV_WIDTH = 3
D_FF = 2816
N_MOD = 9
EPS = 1e-6
ATTN_QBLOCK = 128
N_AB = (DEPTH + 1) // 2
N_C = DEPTH // 2
NA_WIDTH = NA_HEADS * NA_HEAD_DIM
ML_WIDTH = ML_HEADS * ML_HEAD_DIM
AB_IN = 3 * NA_WIDTH + 4 * ML_WIDTH + 4 * ML_HEADS
AB_OUT = NA_WIDTH + ML_WIDTH
AB_SPLITS = (NA_WIDTH, 2 * NA_WIDTH, 3 * NA_WIDTH, 3 * NA_WIDTH + ML_WIDTH, 3 * NA_WIDTH + 2 * ML_WIDTH, 3 * NA_WIDTH + 3 * ML_WIDTH, 3 * NA_WIDTH + 4 * ML_WIDTH)
SC_WIDTH = D_MODEL

kernel_name = 'hybrid_dit_natten_mlstm_shortconv_step'


def rmsnorm(x, g):
    xf = x.astype(jnp.float32)
    y = xf * lax.rsqrt(jnp.mean(xf * xf, axis=-1, keepdims=True) + EPS)
    return (y * g.astype(jnp.float32)).astype(x.dtype)


def modulation(cond, w, b):
    m = jax.nn.silu(cond) @ w + b
    return m.reshape(cond.shape[0], 1, N_MOD, D_MODEL)


def modulate_in(x, g_pre, mod, j):
    return rmsnorm(x, g_pre) * (1.0 + mod[:, :, 3 * j + 1]) + mod[:, :, 3 * j]


def gated_residual(x, out, g_post, mod, j, weight):
    return x + weight * mod[:, :, 3 * j + 2] * rmsnorm(out, g_post)


def swiglu(h, w13, w2):
    a, g = jnp.split(h @ w13, 2, axis=-1)
    return (jax.nn.silu(g) * a) @ w2


def ffn_sublayer(x, mod, j, g_pre, g_post, w13, w2):
    h = modulate_in(x, g_pre, mod, j)
    return gated_residual(x, swiglu(h, w13, w2), g_post, mod, j, 0.5)


def split_heads(t, n_heads, head_dim):
    B, L, _ = t.shape
    return t.reshape(B, L, n_heads, head_dim).transpose(0, 2, 1, 3)


def merge_heads(t):
    B, H, L, dh = t.shape
    return t.transpose(0, 2, 1, 3).reshape(B, L, H * dh)


def dwconv_centred(u, w, b):
    L = u.shape[1]
    pad = CONV_WIDTH // 2
    up = jnp.pad(u, ((0, 0), (pad, pad), (0, 0)))
    y = b
    for i in range(CONV_WIDTH):
        y = y + up[:, i:i + L] * w[i]
    return y


def shortconv_mixer(h, w_in, conv_w, conv_b, w_out):
    gb, gc, xin = jnp.split(h @ w_in, 3, axis=-1)
    return (gb * dwconv_centred(gc * xin, conv_w, conv_b)) @ w_out


def context_attention(q, k, v):
    B, H, Lc, dh = q.shape
    nb = Lc // ATTN_QBLOCK
    scale = dh ** -0.5
    qb = jnp.moveaxis(q.reshape(B, H, nb, ATTN_QBLOCK, dh), 2, 0)

    def block(qblk):
        s = jnp.einsum('bhqd,bhkd->bhqk', qblk, k).astype(jnp.float32) * scale
        p = jax.nn.softmax(s, axis=-1).astype(v.dtype)
        return jnp.einsum('bhqk,bhkd->bhqd', p, v)

    o = lax.map(block, qb)
    return jnp.moveaxis(o, 0, 2).reshape(B, H, Lc, dh)


def na_latent(q, k, v, k_ctx, v_ctx, rpb):
    B, L, _ = q.shape
    R = L // GRID_W
    kh = min(NA_WIN_H, R)
    scale = NA_HEAD_DIM ** -0.5
    grid = lambda t: t.reshape(B, R, GRID_W, NA_HEADS, NA_HEAD_DIM).transpose(0, 3, 1, 2, 4)
    qg, kg, vg = grid(q), grid(k), grid(v)
    rows = jnp.arange(R)
    cols = jnp.arange(GRID_W)
    rs = jnp.clip(rows - kh // 2, 0, R - kh)
    cs = jnp.clip(cols - NA_WIN_W // 2, 0, GRID_W - NA_WIN_W)
    row_idx = rs[:, None] + jnp.arange(kh)[None, :]
    k_rows = kg[:, :, row_idx]
    v_rows = vg[:, :, row_idx]
    s_loc = jnp.einsum('bhrcd,bhrjwd->bhrcjw', qg, k_rows).astype(jnp.float32) * scale
    col_in = (cols[None, :] >= cs[:, None]) & (cols[None, :] < cs[:, None] + NA_WIN_W)
    dr = row_idx - rows[:, None] + (NA_WIN_H - 1)
    dc = jnp.clip(cols[None, :] - cols[:, None], -(NA_WIN_W - 1), NA_WIN_W - 1) + (NA_WIN_W - 1)
    bias = rpb[:, dr[:, None, :, None], dc[None, :, None, :]].astype(jnp.float32)
    s_loc = jnp.where(col_in[None, None, None, :, None, :], s_loc + bias[None], -jnp.inf)
    s_ctx = jnp.einsum('bhrcd,bhsd->bhrcs', qg, k_ctx).astype(jnp.float32) * scale
    n_loc = kh * GRID_W
    s_all = jnp.concatenate([s_loc.reshape(B, NA_HEADS, R, GRID_W, n_loc), s_ctx], axis=-1)
    p = jax.nn.softmax(s_all, axis=-1).astype(v.dtype)
    p_loc = p[..., :n_loc].reshape(B, NA_HEADS, R, GRID_W, kh, GRID_W)
    p_ctx = p[..., n_loc:]
    out = jnp.einsum('bhrcjw,bhrjwd->bhrcd', p_loc, v_rows) + jnp.einsum('bhrcs,bhsd->bhrcd', p_ctx, v_ctx)
    return out.transpose(0, 2, 3, 1, 4).reshape(B, L, NA_WIDTH)


def mlstm_scan(q, k, v, log_i, log_f, C0, n0, m0):
    B, H, L, dh = q.shape
    nc = L // ML_CHUNK
    chunks = lambda t: jnp.moveaxis(t.reshape((B, H, nc, ML_CHUNK) + t.shape[3:]), 2, 0)
    tril = jnp.tril(jnp.ones((ML_CHUNK, ML_CHUNK), dtype=bool))

    def step(carry, inp):
        C, n, m = carry
        qb, kb, vb, ib, fb = inp
        bcum = jnp.cumsum(fb, axis=-1)
        a = bcum + m[..., None]
        Dm = jnp.where(tril, bcum[..., :, None] - bcum[..., None, :] + ib[..., None, :], -jnp.inf)
        m_t = jnp.maximum(a, jnp.max(Dm, axis=-1))
        w_inter = jnp.exp(a - m_t)
        Wd = jnp.exp(Dm - m_t[..., None])
        s = jnp.einsum('bhtd,bhsd->bhts', qb, kb) * Wd
        num = w_inter[..., None] * jnp.einsum('bhtd,bhde->bhte', qb, C) + jnp.einsum('bhts,bhse->bhte', s, vb)
        den = w_inter * jnp.einsum('bhtd,bhd->bht', qb, n) + jnp.sum(s, axis=-1)
        h = num / jnp.maximum(jnp.abs(den), jnp.exp(-m_t))[..., None]
        w_last = Wd[..., -1, :]
        wi_last = w_inter[..., -1]
        C_new = wi_last[..., None, None] * C + jnp.einsum('bhs,bhsd,bhse->bhde', w_last, kb, vb)
        n_new = wi_last[..., None] * n + jnp.einsum('bhs,bhsd->bhd', w_last, kb)
        return (C_new, n_new, m_t[..., -1]), h

    (C, n, m), hs = lax.scan(step, (C0, n0, m0), (chunks(q), chunks(k), chunks(v), chunks(log_i), chunks(log_f)))
    h = jnp.moveaxis(hs, 0, 2).reshape(B, H, L, dh)
    return h, C, n, m


def mlstm_mixer(q, k, v, o, gates, gate_b, hnorm, C0, n0, m0):
    B, L, _ = q.shape
    dt = q.dtype
    f32 = jnp.float32
    qh = split_heads(q.astype(f32), ML_HEADS, ML_HEAD_DIM)
    kh = split_heads(k.astype(f32), ML_HEADS, ML_HEAD_DIM) * (ML_HEAD_DIM ** -0.5)
    vh = split_heads(v.astype(f32), ML_HEADS, ML_HEAD_DIM)
    g = (gates.astype(f32).reshape(B, L, 4, ML_HEADS) + gate_b.astype(f32)).transpose(2, 0, 3, 1)
    C0, n0, m0 = C0.astype(f32), n0.astype(f32), m0.astype(f32)
    h_f, Cf, nf, mf = mlstm_scan(qh, kh, vh, g[0], jax.nn.log_sigmoid(g[1]), C0[:, 0], n0[:, 0], m0[:, 0])
    flip = lambda t: jnp.flip(t, axis=2)
    h_b, Cb, nb, mb = mlstm_scan(flip(qh), flip(kh), flip(vh), flip(g[2]), flip(jax.nn.log_sigmoid(g[3])), C0[:, 1], n0[:, 1], m0[:, 1])
    h = (h_f + flip(h_b)).transpose(0, 2, 1, 3)
    h = rmsnorm(h, hnorm.reshape(ML_HEADS, ML_HEAD_DIM)).reshape(B, L, ML_WIDTH)
    out = (jax.nn.sigmoid(o.astype(f32)) * h).astype(dt)
    return out, jnp.stack([Cf, Cb], axis=1), jnp.stack([nf, nb], axis=1), jnp.stack([mf, mb], axis=1)


def ab_mixer_context(h, w_in, w_out, gate_b, hnorm):
    B = h.shape[0]
    qa, ka, va, qm, km, vm, om, gates = jnp.split(h @ w_in, AB_SPLITS, axis=-1)
    k_h = split_heads(ka, NA_HEADS, NA_HEAD_DIM)
    v_h = split_heads(va, NA_HEADS, NA_HEAD_DIM)
    a_out = merge_heads(context_attention(split_heads(qa, NA_HEADS, NA_HEAD_DIM), k_h, v_h))
    C0 = jnp.zeros((B, 2, ML_HEADS, ML_HEAD_DIM, ML_HEAD_DIM), jnp.float32)
    n0 = jnp.zeros((B, 2, ML_HEADS, ML_HEAD_DIM), jnp.float32)
    m0 = jnp.zeros((B, 2, ML_HEADS), jnp.float32)
    m_out, C, n, m = mlstm_mixer(qm, km, vm, om, gates, gate_b, hnorm, C0, n0, m0)
    out = jnp.concatenate([a_out, m_out], axis=-1) @ w_out
    return out, k_h, v_h, C, n, m


def ab_mixer_latent(h, k_ctx, v_ctx, C0, n0, m0, w_in, w_out, rpb, gate_b, hnorm):
    qa, ka, va, qm, km, vm, om, gates = jnp.split(h @ w_in, AB_SPLITS, axis=-1)
    a_out = na_latent(qa, ka, va, k_ctx, v_ctx, rpb)
    m_out, _, _, _ = mlstm_mixer(qm, km, vm, om, gates, gate_b, hnorm, C0, n0, m0)
    return jnp.concatenate([a_out, m_out], axis=-1) @ w_out


def setup_inputs(seed: int = 0) -> dict:
    key = jax.random.key(seed)
    ks = jax.random.split(key, 24)
    f32 = jnp.float32
    D = D_MODEL
    nrm = lambda k, shape, s: jax.random.normal(k, shape, f32) * s
    ig_b = nrm(ks[17], (N_AB, 2, 1, ML_HEADS), 0.1)
    fg_b = 3.0 + 3.0 * jax.random.uniform(ks[18], (N_AB, 2, 1, ML_HEADS), f32)
    return {
        'x_prompt': nrm(ks[0], (BATCH, SEQ, D), 1.0),
        'x_sample': nrm(ks[1], (DEC_BATCH, DEC_SEQ, D), 1.0),
        'cache_k': nrm(ks[2], (DEC_BATCH, N_AB, NA_HEADS, PAST_LEN, NA_HEAD_DIM), 1.0),
        'cache_v': nrm(ks[3], (DEC_BATCH, N_AB, NA_HEADS, PAST_LEN, NA_HEAD_DIM), 1.0),
        'state_C': nrm(ks[4], (DEC_BATCH, N_AB, 2, ML_HEADS, ML_HEAD_DIM, ML_HEAD_DIM), 0.3),
        'state_n': nrm(ks[5], (DEC_BATCH, N_AB, 2, ML_HEADS, ML_HEAD_DIM), 0.3),
        'state_m': nrm(ks[6], (DEC_BATCH, N_AB, 2, ML_HEADS), 1.0),
        'c': nrm(ks[7], (DEC_BATCH, D), 1.0),
        'c_ctx': nrm(ks[8], (D,), 1.0),
        'ada_w': nrm(ks[9], (DEPTH, D, N_MOD * D), 0.5 * D ** -0.5),
        'ada_b': nrm(ks[10], (DEPTH, N_MOD * D), 0.1),
        'norm_g': 1.0 + nrm(ks[11], (DEPTH, 3, 2, D), 0.05),
        'ffn_w13': nrm(ks[12], (DEPTH, 2, D, 2 * D_FF), D ** -0.5),
        'ffn_w2': nrm(ks[13], (DEPTH, 2, D_FF, D), D_FF ** -0.5),
        'ab_w_in': nrm(ks[14], (N_AB, D, AB_IN), D ** -0.5),
        'ab_w_out': nrm(ks[15], (N_AB, AB_OUT, D), AB_OUT ** -0.5),
        'na_rpb': nrm(ks[16], (N_AB, NA_HEADS, 2 * NA_WIN_H - 1, 2 * NA_WIN_W - 1), 0.1),
        'ml_gate_b': jnp.concatenate([ig_b, fg_b], axis=2).reshape(N_AB, 4, ML_HEADS),
        'ml_hnorm': 1.0 + nrm(ks[19], (N_AB, ML_WIDTH), 0.05),
        'sc_w_in': nrm(ks[20], (N_C, D, 3 * SC_WIDTH), D ** -0.5),
        'sc_conv_w': nrm(ks[21], (N_C, CONV_WIDTH, SC_WIDTH), CONV_WIDTH ** -0.5),
        'sc_conv_b': nrm(ks[22], (N_C, SC_WIDTH), 0.02),
        'sc_w_out': nrm(ks[23], (N_C, SC_WIDTH, D), SC_WIDTH ** -0.5),
    }


def reference(x_prompt, x_sample, cache_k, cache_v, state_C, state_n, state_m, c, c_ctx,
              ada_w, ada_b, norm_g, ffn_w13, ffn_w2, ab_w_in, ab_w_out, na_rpb, ml_gate_b, ml_hnorm,
              sc_w_in, sc_conv_w, sc_conv_b, sc_w_out):
    y_p, y_s = x_prompt, x_sample
    new_k, new_v, new_C, new_n, new_m = [], [], [], [], []
    for l in range(DEPTH):
        mod_p = modulation(c_ctx[None, :], ada_w[l], ada_b[l])
        mod_s = modulation(c, ada_w[l], ada_b[l])
        g = norm_g[l]
        y_p = ffn_sublayer(y_p, mod_p, 0, g[0, 0], g[0, 1], ffn_w13[l, 0], ffn_w2[l, 0])
        y_s = ffn_sublayer(y_s, mod_s, 0, g[0, 0], g[0, 1], ffn_w13[l, 0], ffn_w2[l, 0])
        h_p = modulate_in(y_p, g[1, 0], mod_p, 1)
        h_s = modulate_in(y_s, g[1, 0], mod_s, 1)
        j = l // 2
        if l % 2 == 0:
            out_p, k_c, v_c, C_c, n_c, m_c = ab_mixer_context(h_p, ab_w_in[j], ab_w_out[j], ml_gate_b[j], ml_hnorm[j])
            new_k.append(k_c)
            new_v.append(v_c)
            new_C.append(C_c)
            new_n.append(n_c)
            new_m.append(m_c)
            out_s = ab_mixer_latent(h_s, cache_k[:, j], cache_v[:, j], state_C[:, j], state_n[:, j], state_m[:, j],
                                    ab_w_in[j], ab_w_out[j], na_rpb[j], ml_gate_b[j], ml_hnorm[j])
        else:
            out_p = shortconv_mixer(h_p, sc_w_in[j], sc_conv_w[j], sc_conv_b[j], sc_w_out[j])
            out_s = shortconv_mixer(h_s, sc_w_in[j], sc_conv_w[j], sc_conv_b[j], sc_w_out[j])
        y_p = gated_residual(y_p, out_p, g[1, 1], mod_p, 1, 1.0)
        y_s = gated_residual(y_s, out_s, g[1, 1], mod_s, 1, 1.0)
        y_p = ffn_sublayer(y_p, mod_p, 2, g[2, 0], g[2, 1], ffn_w13[l, 1], ffn_w2[l, 1])
        y_s = ffn_sublayer(y_s, mod_s, 2, g[2, 0], g[2, 1], ffn_w13[l, 1], ffn_w2[l, 1])
    new_cache_k = jnp.stack(new_k, axis=1)
    new_cache_v = jnp.stack(new_v, axis=1)
    new_state_C = jnp.stack(new_C, axis=1)
    new_state_n = jnp.stack(new_n, axis=1)
    new_state_m = jnp.stack(new_m, axis=1)
    return (y_p, y_s, new_cache_k, new_cache_v, new_state_C, new_state_n, new_state_m)
```

```python
import functools

import jax
import jax.numpy as jnp
from jax import lax
from jax.experimental import pallas as pl
from jax.experimental.pallas import tpu as pltpu

D_MODEL = 1024
BATCH = 16
SEQ = 256
DEPTH = 4
DEC_BATCH = 2
DEC_SEQ = 1024
PAST_LEN = 256
GRID_W = 64
NA_HEADS = 8
NA_HEAD_DIM = 64
NA_WIN_H = 8
NA_WIN_W = 16
ML_HEADS = 4
ML_HEAD_DIM = 128
ML_CHUNK = 128
CONV_WIDTH = 3
D_FF = 2816
N_MOD = 9
EPS = 1e-6
N_AB = (DEPTH + 1) // 2
N_C = DEPTH // 2
NA_WIDTH = NA_HEADS * NA_HEAD_DIM
ML_WIDTH = ML_HEADS * ML_HEAD_DIM
AB_MAIN = 3 * NA_WIDTH + 4 * ML_WIDTH
N_GATES = 4 * ML_HEADS
AB_IN = AB_MAIN + N_GATES
SC_WIDTH = D_MODEL

N_PROMPT = BATCH * SEQ
N_SAMPLE = DEC_BATCH * DEC_SEQ
N_TOK = N_PROMPT + N_SAMPLE
GRID_ROWS = DEC_SEQ // GRID_W
NA_KH = min(NA_WIN_H, GRID_ROWS)
N_DR = 2 * NA_WIN_H - 1
N_DC = 2 * NA_WIN_W - 1
N_PAIR = N_DR - 1

LANES = 128
COND_ROWS = 8
VMEM_CAP = 60 * 1024 * 1024

TM = 1024
TF = 256
TN_PROJ = 512
TN_MOD = 1536

F32 = jnp.float32
BF16 = jnp.bfloat16


def _vmem_limit(block_bytes, scratch_bytes):
    return int(min(VMEM_CAP, 2 * (2 * block_bytes + scratch_bytes) + (8 << 20)))


def _params(semantics, block_bytes, scratch_bytes=0):
    return pltpu.CompilerParams(dimension_semantics=semantics,
                                vmem_limit_bytes=_vmem_limit(block_bytes, scratch_bytes))


def _cond_row(i):
    return jnp.maximum(i * TM - (N_PROMPT - DEC_SEQ), 0) // DEC_SEQ


def _rms(x, g):
    ms = jnp.mean(x * x, axis=-1, keepdims=True)
    return x * lax.rsqrt(ms + EPS) * g


def _modulate_in(x, g_ref, mod_ref, j):
    h = _rms(x, g_ref[2 * j:2 * j + 1, :])
    return h * (1.0 + mod_ref[3 * j + 1:3 * j + 2, :]) + mod_ref[3 * j:3 * j + 1, :]


def _gated_residual(x, out, g_ref, mod_ref, j, weight):
    return x + weight * mod_ref[3 * j + 2:3 * j + 3, :] * _rms(out, g_ref[2 * j + 1:2 * j + 2, :])


def _dot(a, b):
    return jnp.dot(a.astype(BF16), b.astype(BF16), preferred_element_type=F32)


def _dot_nt(a, b):
    return lax.dot_general(a.astype(BF16), b.astype(BF16), (((1,), (1,)), ((), ())),
                           preferred_element_type=F32)


def _mod_kernel(c_ref, w_ref, b_ref, o_ref):
    c = c_ref[...]
    s = c * jax.nn.sigmoid(c)
    o_ref[...] = _dot(s, w_ref[...]) + b_ref[...]


def _modulation(cond, ada_w, ada_b):
    n = N_MOD * D_MODEL
    out = pl.pallas_call(
        _mod_kernel,
        grid=(DEPTH, n // TN_MOD),
        in_specs=[
            pl.BlockSpec((COND_ROWS, D_MODEL), lambda l, t: (0, 0)),
            pl.BlockSpec((None, D_MODEL, TN_MOD), lambda l, t: (l, 0, t)),
            pl.BlockSpec((None, 1, TN_MOD), lambda l, t: (l, 0, t)),
        ],
        out_specs=pl.BlockSpec((None, COND_ROWS, TN_MOD), lambda l, t: (l, 0, t)),
        out_shape=jax.ShapeDtypeStruct((DEPTH, COND_ROWS, n), F32),
        compiler_params=_params(("parallel", "parallel"), 4 * D_MODEL * TN_MOD),
        name="modulation",
    )(cond, ada_w, ada_b.reshape(DEPTH, 1, n))
    return out.reshape(DEPTH, COND_ROWS, N_MOD, D_MODEL)


def _ffn_kernel(x_ref, mod_ref, g_ref, wa_ref, wg_ref, w2_ref, o_ref, h_ref, acc_ref, *, j):
    k = pl.program_id(1)

    @pl.when(k == 0)
    def _():
        h_ref[...] = _modulate_in(x_ref[...], g_ref, mod_ref, j).astype(BF16)
        acc_ref[...] = jnp.zeros_like(acc_ref)

    h = h_ref[...]
    a = _dot(h, wa_ref[...])
    g = _dot(h, wg_ref[...])
    u = (g * jax.nn.sigmoid(g)) * a
    acc_ref[...] += _dot(u, w2_ref[...])

    @pl.when(k == pl.num_programs(1) - 1)
    def _():
        o_ref[...] = _gated_residual(x_ref[...], acc_ref[...], g_ref, mod_ref, j, 0.5)


def _ffn_sublayer(y, mod_l, g_l, ffn_w13, ffn_w2, l, s):
    j = 2 * s
    nk = D_FF // TF
    blocks = 4 * (2 * TM * D_MODEL + 3 * D_MODEL * TF)
    scratch = TM * D_MODEL * (2 + 4)
    return pl.pallas_call(
        functools.partial(_ffn_kernel, j=j),
        grid=(N_TOK // TM, nk),
        in_specs=[
            pl.BlockSpec((TM, D_MODEL), lambda i, k: (i, 0)),
            pl.BlockSpec((None, N_MOD, D_MODEL), lambda i, k: (_cond_row(i), 0, 0)),
            pl.BlockSpec((6, D_MODEL), lambda i, k: (0, 0)),
            pl.BlockSpec((None, None, D_MODEL, TF), lambda i, k: (l, s, 0, k)),
            pl.BlockSpec((None, None, D_MODEL, TF), lambda i, k: (l, s, 0, k + nk)),
            pl.BlockSpec((None, None, TF, D_MODEL), lambda i, k: (l, s, k, 0)),
        ],
        out_specs=pl.BlockSpec((TM, D_MODEL), lambda i, k: (i, 0)),
        out_shape=jax.ShapeDtypeStruct((N_TOK, D_MODEL), F32),
        scratch_shapes=[pltpu.VMEM((TM, D_MODEL), BF16), pltpu.VMEM((TM, D_MODEL), F32)],
        compiler_params=_params(("parallel", "arbitrary"), blocks, scratch),
        name=f"ffn_l{l}_s{s}",
    )(y, mod_l, g_l, ffn_w13, ffn_w13, ffn_w2)


def _sconv_kernel(x_ref, mod_ref, g_ref, wb_ref, wc_ref, wx_ref, cw_ref, cb_ref, wo_ref, o_ref,
                  h_ref, acc_ref):
    i = pl.program_id(0)
    k = pl.program_id(1)

    @pl.when(k == 0)
    def _():
        h_ref[...] = _modulate_in(x_ref[...], g_ref, mod_ref, 1).astype(BF16)
        acc_ref[...] = jnp.zeros_like(acc_ref)

    h = h_ref[...]
    gb = _dot(h, wb_ref[...])
    u = _dot(h, wc_ref[...]) * _dot(h, wx_ref[...])
    seq = jnp.where(i < N_PROMPT // TM, SEQ, DEC_SEQ)
    t = lax.broadcasted_iota(jnp.int32, u.shape, 0) & (seq - 1)
    u_prev = jnp.where(t == 0, 0.0, pltpu.roll(u, 1, 0))
    u_next = jnp.where(t == seq - 1, 0.0, pltpu.roll(u, TM - 1, 0))
    y = cb_ref[...] + u_prev * cw_ref[0:1, :] + u * cw_ref[1:2, :] + u_next * cw_ref[2:3, :]
    acc_ref[...] += _dot(gb * y, wo_ref[...])

    @pl.when(k == pl.num_programs(1) - 1)
    def _():
        o_ref[...] = _gated_residual(x_ref[...], acc_ref[...], g_ref, mod_ref, 1, 1.0)


def _sconv_sublayer(y, mod_l, g_l, sc_w_in, sc_conv_w, sc_conv_b, sc_w_out, jc):
    nk = SC_WIDTH // TF
    blocks = 4 * (2 * TM * D_MODEL + 4 * D_MODEL * TF)
    scratch = TM * D_MODEL * (2 + 4)
    return pl.pallas_call(
        _sconv_kernel,
        grid=(N_TOK // TM, nk),
        in_specs=[
            pl.BlockSpec((TM, D_MODEL), lambda i, k: (i, 0)),
            pl.BlockSpec((None, N_MOD, D_MODEL), lambda i, k: (_cond_row(i), 0, 0)),
            pl.BlockSpec((6, D_MODEL), lambda i, k: (0, 0)),
            pl.BlockSpec((None, D_MODEL, TF), lambda i, k: (jc, 0, k)),
            pl.BlockSpec((None, D_MODEL, TF), lambda i, k: (jc, 0, k + nk)),
            pl.BlockSpec((None, D_MODEL, TF), lambda i, k: (jc, 0, k + 2 * nk)),
            pl.BlockSpec((None, CONV_WIDTH, TF), lambda i, k: (jc, 0, k)),
            pl.BlockSpec((None, 1, TF), lambda i, k: (jc, 0, k)),
            pl.BlockSpec((None, TF, D_MODEL), lambda i, k: (jc, k, 0)),
        ],
        out_specs=pl.BlockSpec((TM, D_MODEL), lambda i, k: (i, 0)),
        out_shape=jax.ShapeDtypeStruct((N_TOK, D_MODEL), F32),
        scratch_shapes=[pltpu.VMEM((TM, D_MODEL), BF16), pltpu.VMEM((TM, D_MODEL), F32)],
        compiler_params=_params(("parallel", "arbitrary"), blocks, scratch),
        name=f"sconv_{jc}",
    )(y, mod_l, g_l, sc_w_in, sc_w_in, sc_w_in, sc_conv_w, sc_conv_b.reshape(N_C, 1, SC_WIDTH), sc_w_out)


def _inproj_kernel(x_ref, mod_ref, g_ref, w_ref, wg_ref, o_ref, og_ref, h_ref):
    n = pl.program_id(1)

    @pl.when(n == 0)
    def _():
        h = _modulate_in(x_ref[...], g_ref, mod_ref, 1).astype(BF16)
        h_ref[...] = h
        og_ref[...] = _dot(h, wg_ref[...])

    o_ref[...] = _dot(h_ref[...], w_ref[...])


def _inproj(y, mod_l, g_l, ab_w_in, jb):
    w_gates = jnp.pad(ab_w_in[jb, :, AB_MAIN:], ((0, 0), (0, LANES - N_GATES)))
    blocks = 4 * (TM * D_MODEL + D_MODEL * TN_PROJ + D_MODEL * LANES + TM * TN_PROJ + TM * LANES)
    return pl.pallas_call(
        _inproj_kernel,
        grid=(N_TOK // TM, AB_MAIN // TN_PROJ),
        in_specs=[
            pl.BlockSpec((TM, D_MODEL), lambda i, n: (i, 0)),
            pl.BlockSpec((None, N_MOD, D_MODEL), lambda i, n: (_cond_row(i), 0, 0)),
            pl.BlockSpec((6, D_MODEL), lambda i, n: (0, 0)),
            pl.BlockSpec((None, D_MODEL, TN_PROJ), lambda i, n: (jb, 0, n)),
            pl.BlockSpec((D_MODEL, LANES), lambda i, n: (0, 0)),
        ],
        out_specs=[
            pl.BlockSpec((TM, TN_PROJ), lambda i, n: (i, n)),
            pl.BlockSpec((TM, LANES), lambda i, n: (i, 0)),
        ],
        out_shape=[jax.ShapeDtypeStruct((N_TOK, AB_MAIN), F32),
                   jax.ShapeDtypeStruct((N_TOK, LANES), F32)],
        scratch_shapes=[pltpu.VMEM((TM, D_MODEL), BF16)],
        compiler_params=_params(("parallel", "arbitrary"), blocks, 2 * TM * D_MODEL),
        name=f"inproj_{jb}",
    )(y, mod_l, g_l, ab_w_in, w_gates)


def _softmax_rows(parts):
    m = functools.reduce(jnp.maximum, [jnp.max(s, axis=-1, keepdims=True) for s in parts])
    ps = [jnp.exp(s - m) for s in parts]
    inv = 1.0 / functools.reduce(jnp.add, [jnp.sum(p, axis=-1, keepdims=True) for p in ps])
    return [p * inv for p in ps]


def _ctx_attn_kernel(q_ref, k_ref, v_ref, o_ref, kc_ref, vc_ref):
    scale = NA_HEAD_DIM ** -0.5
    for h in range(NA_HEADS):
        sl = slice(h * NA_HEAD_DIM, (h + 1) * NA_HEAD_DIM)
        kh = k_ref[:, sl]
        vh = v_ref[:, sl]
        kc_ref[h] = kh
        vc_ref[h] = vh
        (p,) = _softmax_rows([_dot_nt(q_ref[:, sl], kh) * scale])
        o_ref[:, sl] = _dot(p, vh)


def _ctx_attention(proj):
    cache = jax.ShapeDtypeStruct((BATCH, NA_HEADS, SEQ, NA_HEAD_DIM), F32)
    cache_spec = pl.BlockSpec((None, NA_HEADS, SEQ, NA_HEAD_DIM), lambda b: (b, 0, 0, 0))
    blocks = 4 * (4 * SEQ * NA_WIDTH + 2 * NA_HEADS * SEQ * LANES)
    return pl.pallas_call(
        _ctx_attn_kernel,
        grid=(BATCH,),
        in_specs=[pl.BlockSpec((SEQ, NA_WIDTH), lambda b, c=c: (b, c)) for c in range(3)],
        out_specs=[pl.BlockSpec((SEQ, NA_WIDTH), lambda b: (b, 0)), cache_spec, cache_spec],
        out_shape=[jax.ShapeDtypeStruct((N_PROMPT, NA_WIDTH), F32), cache, cache],
        compiler_params=_params(("parallel",), blocks),
        name="ctx_attention",
    )(proj, proj, proj)


def _na_bias_kernel(rpb_ref, p_ref):
    base = (pl.program_id(0) * NA_HEADS + pl.program_id(1)) * (N_DR * N_DC)
    shape = (GRID_W, LANES)
    lane = lax.broadcasted_iota(jnp.int32, shape, 1)
    qc = lax.broadcasted_iota(jnp.int32, shape, 0)
    kc = lane & (GRID_W - 1)
    upper = lane >= GRID_W
    cs = jnp.clip(qc - NA_WIN_W // 2, 0, GRID_W - NA_WIN_W)
    col_in = (kc >= cs) & (kc < cs + NA_WIN_W)
    dc = jnp.clip(kc - qc, -(NA_WIN_W - 1), NA_WIN_W - 1) + (NA_WIN_W - 1)
    for pair in range(N_PAIR):
        acc = jnp.zeros(shape, F32)
        for d in range(N_DC):
            lo = rpb_ref[base + pair * N_DC + d]
            hi = rpb_ref[base + (pair + 1) * N_DC + d]
            acc = jnp.where(dc == d, jnp.where(upper, hi, lo), acc)
        p_ref[pair] = jnp.where(col_in, acc, -jnp.inf)


def _na_bias(na_rpb):
    return pl.pallas_call(
        _na_bias_kernel,
        grid=(N_AB, NA_HEADS),
        in_specs=[pl.BlockSpec(memory_space=pltpu.SMEM)],
        out_specs=pl.BlockSpec((None, None, N_PAIR, GRID_W, LANES), lambda j, h: (j, h, 0, 0, 0)),
        out_shape=jax.ShapeDtypeStruct((N_AB, NA_HEADS, N_PAIR, GRID_W, LANES), F32),
        compiler_params=_params(("parallel", "parallel"), 4 * N_PAIR * GRID_W * LANES),
        name="na_bias",
    )(na_rpb.reshape(-1))


def _na_kernel(q_ref, k_ref, v_ref, kc_ref, vc_ref, p_ref, o_ref):
    scale = NA_HEAD_DIM ** -0.5
    n_loc = NA_KH * GRID_W
    for hh in range(LANES // NA_HEAD_DIM):
        sl = slice(hh * NA_HEAD_DIM, (hh + 1) * NA_HEAD_DIM)
        k_ctx = kc_ref[hh]
        v_ctx = vc_ref[hh]

        def row_body(r, carry, hh=hh, sl=sl, k_ctx=k_ctx, v_ctx=v_ctx):
            rs = jnp.clip(r - NA_KH // 2, 0, GRID_ROWS - NA_KH)
            dr0 = rs - r + (NA_WIN_H - 1)
            q_rows = pl.ds(pl.multiple_of(r * GRID_W, GRID_W), GRID_W)
            k_rows = pl.ds(pl.multiple_of(rs * GRID_W, GRID_W), n_loc)
            q = q_ref[q_rows, sl]
            bias = jnp.concatenate([p_ref[hh, dr0 + 2 * jj] for jj in range(NA_KH // 2)], axis=1)
            s_loc = _dot_nt(q, k_ref[k_rows, sl]) * scale + bias
            s_ctx = _dot_nt(q, k_ctx) * scale
            p_loc, p_ctx = _softmax_rows([s_loc, s_ctx])
            o_ref[q_rows, sl] = _dot(p_loc, v_ref[k_rows, sl]) + _dot(p_ctx, v_ctx)
            return carry

        lax.fori_loop(0, GRID_ROWS, row_body, 0)


def _na_attention(proj, cache_k, cache_v, bias_pairs, jb):
    hp = LANES // NA_HEAD_DIM
    s0 = N_PROMPT // DEC_SEQ
    nb = NA_WIDTH // LANES
    ctx_spec = pl.BlockSpec((None, None, hp, PAST_LEN, NA_HEAD_DIM), lambda b, c: (b, jb, c, 0, 0))
    blocks = 4 * (4 * DEC_SEQ * LANES + 2 * hp * PAST_LEN * LANES + hp * N_PAIR * GRID_W * LANES)
    return pl.pallas_call(
        _na_kernel,
        grid=(DEC_BATCH, nb),
        in_specs=[
            pl.BlockSpec((DEC_SEQ, LANES), lambda b, c: (s0 + b, c)),
            pl.BlockSpec((DEC_SEQ, LANES), lambda b, c: (s0 + b, nb + c)),
            pl.BlockSpec((DEC_SEQ, LANES), lambda b, c: (s0 + b, 2 * nb + c)),
            ctx_spec, ctx_spec,
            pl.BlockSpec((None, hp, N_PAIR, GRID_W, LANES), lambda b, c: (jb, c, 0, 0, 0)),
        ],
        out_specs=pl.BlockSpec((DEC_SEQ, LANES), lambda b, c: (b, c)),
        out_shape=jax.ShapeDtypeStruct((N_SAMPLE, NA_WIDTH), F32),
        compiler_params=_params(("parallel", "parallel"), blocks),
        name=f"na_attention_{jb}",
    )(proj, proj, proj, cache_k, cache_v, bias_pairs)


def _mlstm_kernel(*refs, n_chunks, has_state):
    if has_state:
        (q_ref, k_ref, v_ref, o_ref, gt_ref, gb_ref, hn_ref, c0_ref, n0_ref, m0_ref,
         out_ref, cout_ref, nout_ref, mout_ref, hacc_ref, c_ref) = refs
    else:
        (q_ref, k_ref, v_ref, o_ref, gt_ref, gb_ref, hn_ref,
         out_ref, cout_ref, nout_ref, mout_ref, hacc_ref, c_ref) = refs
    T = ML_CHUNK
    dh = ML_HEAD_DIM
    b = pl.program_id(0)
    row = lax.broadcasted_iota(jnp.int32, (T, T), 0)
    col = lax.broadcasted_iota(jnp.int32, (T, T), 1)
    lane = lax.broadcasted_iota(jnp.int32, (T, LANES), 1)
    is_forget = ((lane >= ML_HEADS) & (lane < 2 * ML_HEADS)) | ((lane >= 3 * ML_HEADS) & (lane < 4 * ML_HEADS))

    for d in range(2):
        causal = (col <= row) if d == 0 else (col >= row)
        cum_mat = causal.astype(F32)
        last = T - 1 if d == 0 else 0
        for h in range(ML_HEADS):
            c_ref[h] = c0_ref[d, h] if has_state else jnp.zeros((dh, dh), F32)
        if has_state:
            n_init = tuple(n0_ref[d, h:h + 1, :] for h in range(ML_HEADS))
            m_init = tuple(jnp.full((1, 1), m0_ref[(b * 2 + d) * ML_HEADS + h], F32) for h in range(ML_HEADS))
        else:
            n_init = tuple(jnp.zeros((1, dh), F32) for _ in range(ML_HEADS))
            m_init = tuple(jnp.zeros((1, 1), F32) for _ in range(ML_HEADS))

        def chunk_body(ci, carry, d=d, causal=causal, cum_mat=cum_mat, last=last):
            n_st, m_st = carry
            c = ci if d == 0 else n_chunks - 1 - ci
            rows = pl.ds(pl.multiple_of(c * T, T), T)
            g = gt_ref[rows, :] + gb_ref[...]
            log_sig = jnp.minimum(g, 0.0) - jnp.log1p(jnp.exp(-jnp.abs(g)))
            g = jnp.where(is_forget, log_sig, g)
            cum = jnp.dot(cum_mat, g, precision=lax.Precision.HIGHEST, preferred_element_type=F32)
            g_t = g.T
            cum_t = cum.T
            n_new, m_new = [], []
            for h in range(ML_HEADS):
                hl = slice(h * dh, (h + 1) * dh)
                gi = 2 * d * ML_HEADS + h
                gf = gi + ML_HEADS
                b_col = cum[:, gf:gf + 1]
                b_row = cum_t[gf:gf + 1, :]
                i_col = g[:, gi:gi + 1]
                i_row = g_t[gi:gi + 1, :]
                m_prev = m_st[h]
                a = b_col + m_prev
                dm = jnp.where(causal, b_col - b_row + i_row, -jnp.inf)
                m_t = jnp.maximum(a, jnp.max(dm, axis=-1, keepdims=True))
                w_inter = jnp.exp(a - m_t)
                w_d = jnp.exp(dm - m_t)
                qh = q_ref[rows, hl]
                kh = k_ref[rows, hl] * (dh ** -0.5)
                vh = v_ref[rows, hl]
                s = _dot_nt(qh, kh) * w_d
                c_prev = c_ref[h]
                num = w_inter * _dot(qh, c_prev) + _dot(s, vh)
                den = (w_inter * jnp.sum(qh * n_st[h], axis=-1, keepdims=True)
                       + jnp.sum(s, axis=-1, keepdims=True))
                h_out = num / jnp.maximum(jnp.abs(den), jnp.exp(-m_t))
                if d == 0:
                    hacc_ref[rows, hl] = h_out
                else:
                    hacc_ref[rows, hl] += h_out
                m_last = m_t[last:last + 1, :]
                wi_last = w_inter[last:last + 1, :]
                w_last = jnp.exp(b_col[last:last + 1, :] - b_col + i_col - m_last)
                kw = kh * w_last
                c_ref[h] = wi_last * c_prev + _dot(kw.T, vh)
                n_new.append(wi_last * n_st[h] + jnp.sum(kw, axis=0, keepdims=True))
                m_new.append(m_last)
            return tuple(n_new), tuple(m_new)

        n_fin, m_fin = lax.fori_loop(0, n_chunks, chunk_body, (n_init, m_init))
        for h in range(ML_HEADS):
            cout_ref[d, h] = c_ref[h]
            nout_ref[d, h:h + 1, :] = n_fin[h]
            mout_ref[d * ML_HEADS + h:d * ML_HEADS + h + 1, :] = jnp.broadcast_to(m_fin[h], (1, LANES))

    for h in range(ML_HEADS):
        hl = slice(h * dh, (h + 1) * dh)
        out_ref[:, hl] = jax.nn.sigmoid(o_ref[:, hl]) * _rms(hacc_ref[:, hl], hn_ref[:, hl])


def _mlstm(proj, gates, gate_b_row, hnorm_row, seq_len, n_seq, row0, state=None):
    has_state = state is not None
    r0 = row0 // seq_len
    c0 = 3 * NA_WIDTH // ML_WIDTH
    in_specs = [pl.BlockSpec((seq_len, ML_WIDTH), lambda b, c=c: (r0 + b, c0 + c)) for c in range(4)]
    in_specs += [
        pl.BlockSpec((seq_len, LANES), lambda b: (r0 + b, 0)),
        pl.BlockSpec((1, LANES), lambda b: (0, 0)),
        pl.BlockSpec((1, ML_WIDTH), lambda b: (0, 0)),
    ]
    args = [proj, proj, proj, proj, gates, gate_b_row, hnorm_row]
    c_spec = pl.BlockSpec((None, 2, ML_HEADS, ML_HEAD_DIM, ML_HEAD_DIM), lambda b: (b, 0, 0, 0, 0))
    n_spec = pl.BlockSpec((None, 2, ML_HEADS, ML_HEAD_DIM), lambda b: (b, 0, 0, 0))
    if has_state:
        st_c, st_n, st_m = state
        in_specs += [c_spec, n_spec, pl.BlockSpec(memory_space=pltpu.SMEM)]
        args += [st_c, st_n, st_m.reshape(-1)]
    c_bytes = 2 * ML_HEADS * ML_HEAD_DIM * ML_HEAD_DIM
    blocks = 4 * (5 * seq_len * ML_WIDTH + seq_len * LANES + 2 * c_bytes)
    scratch = 4 * (seq_len * ML_WIDTH + c_bytes // 2)
    out, c_out, n_out, m_out = pl.pallas_call(
        functools.partial(_mlstm_kernel, n_chunks=seq_len // ML_CHUNK, has_state=has_state),
        grid=(n_seq,),
        in_specs=in_specs,
        out_specs=[
            pl.BlockSpec((seq_len, ML_WIDTH), lambda b: (b, 0)),
            c_spec, n_spec,
            pl.BlockSpec((None, 2 * ML_HEADS, LANES), lambda b: (b, 0, 0)),
        ],
        out_shape=[
            jax.ShapeDtypeStruct((n_seq * seq_len, ML_WIDTH), F32),
            jax.ShapeDtypeStruct((n_seq, 2, ML_HEADS, ML_HEAD_DIM, ML_HEAD_DIM), F32),
            jax.ShapeDtypeStruct((n_seq, 2, ML_HEADS, ML_HEAD_DIM), F32),
            jax.ShapeDtypeStruct((n_seq, 2 * ML_HEADS, LANES), F32),
        ],
        scratch_shapes=[pltpu.VMEM((seq_len, ML_WIDTH), F32),
                        pltpu.VMEM((ML_HEADS, ML_HEAD_DIM, ML_HEAD_DIM), F32)],
        compiler_params=_params(("parallel",), blocks, scratch),
        name=f"mlstm_{'latent' if has_state else 'context'}",
    )(*args)
    return out, c_out, n_out, m_out[:, :, 0].reshape(n_seq, 2, ML_HEADS)


def _outproj_kernel(x_ref, mod_ref, g_ref, ap_ref, mp_ref, as_ref, ms_ref, wa_ref, wm_ref, o_ref):
    def finish(a_ref, m_ref):
        out = _dot(a_ref[...], wa_ref[...]) + _dot(m_ref[...], wm_ref[...])
        o_ref[...] = _gated_residual(x_ref[...], out, g_ref, mod_ref, 1, 1.0)

    is_prompt = pl.program_id(0) < N_PROMPT // TM
    pl.when(is_prompt)(functools.partial(finish, ap_ref, mp_ref))
    pl.when(jnp.logical_not(is_prompt))(functools.partial(finish, as_ref, ms_ref))


def _outproj(y, mod_l, g_l, a_p, m_p, a_s, m_s, ab_w_out, jb):
    n_p = N_PROMPT // TM
    n_s = N_SAMPLE // TM
    p_spec = pl.BlockSpec((TM, NA_WIDTH), lambda i: (jnp.minimum(i, n_p - 1), 0))
    s_spec = pl.BlockSpec((TM, NA_WIDTH), lambda i: (jnp.clip(i - n_p, 0, n_s - 1), 0))
    blocks = 4 * (2 * TM * D_MODEL + 4 * TM * NA_WIDTH + D_MODEL * D_MODEL)
    return pl.pallas_call(
        _outproj_kernel,
        grid=(N_TOK // TM,),
        in_specs=[
            pl.BlockSpec((TM, D_MODEL), lambda i: (i, 0)),
            pl.BlockSpec((None, N_MOD, D_MODEL), lambda i: (_cond_row(i), 0, 0)),
            pl.BlockSpec((6, D_MODEL), lambda i: (0, 0)),
            p_spec, p_spec, s_spec, s_spec,
            pl.BlockSpec((None, NA_WIDTH, D_MODEL), lambda i: (jb, 0, 0)),
            pl.BlockSpec((None, ML_WIDTH, D_MODEL), lambda i: (jb, 1, 0)),
        ],
        out_specs=pl.BlockSpec((TM, D_MODEL), lambda i: (i, 0)),
        out_shape=jax.ShapeDtypeStruct((N_TOK, D_MODEL), F32),
        compiler_params=_params(("parallel",), blocks),
        name=f"outproj_{jb}",
    )(y, mod_l, g_l, a_p, m_p, a_s, m_s, ab_w_out, ab_w_out)


def kernel(x_prompt, x_sample, cache_k, cache_v, state_C, state_n, state_m, c, c_ctx, ada_w, ada_b, norm_g,
           ffn_w13, ffn_w2, ab_w_in, ab_w_out, na_rpb, ml_gate_b, ml_hnorm, sc_w_in, sc_conv_w, sc_conv_b,
           sc_w_out):
    assert NA_WIDTH == ML_WIDTH and N_PROMPT % TM == 0 and DEC_SEQ % TM == 0
    cond = jnp.concatenate([c_ctx[None, :], c, jnp.zeros((COND_ROWS - 1 - DEC_BATCH, D_MODEL), F32)], axis=0)
    mod = _modulation(cond, ada_w, ada_b)
    bias_pairs = _na_bias(na_rpb)
    y = jnp.concatenate([x_prompt.reshape(N_PROMPT, D_MODEL), x_sample.reshape(N_SAMPLE, D_MODEL)], axis=0)

    new_k, new_v, new_c, new_n, new_m = [], [], [], [], []
    for l in range(DEPTH):
        mod_l = mod[l]
        g_l = norm_g[l].reshape(6, D_MODEL)
        y = _ffn_sublayer(y, mod_l, g_l, ffn_w13, ffn_w2, l, 0)
        j = l // 2
        if l % 2 == 0:
            proj, gates = _inproj(y, mod_l, g_l, ab_w_in, j)
            gate_b_row = jnp.pad(ml_gate_b[j].reshape(1, N_GATES), ((0, 0), (0, LANES - N_GATES)))
            hnorm_row = ml_hnorm[j].reshape(1, ML_WIDTH)
            a_p, k_c, v_c = _ctx_attention(proj)
            m_p, c_c, n_c, m_c = _mlstm(proj, gates, gate_b_row, hnorm_row, SEQ, BATCH, 0)
            a_s = _na_attention(proj, cache_k, cache_v, bias_pairs, j)
            m_s, _, _, _ = _mlstm(proj, gates, gate_b_row, hnorm_row, DEC_SEQ, DEC_BATCH, N_PROMPT,
                                  state=(state_C[:, j], state_n[:, j], state_m[:, j]))
            new_k.append(k_c)
            new_v.append(v_c)
            new_c.append(c_c)
            new_n.append(n_c)
            new_m.append(m_c)
            y = _outproj(y, mod_l, g_l, a_p, m_p, a_s, m_s, ab_w_out, j)
        else:
            y = _sconv_sublayer(y, mod_l, g_l, sc_w_in, sc_conv_w, sc_conv_b, sc_w_out, j)
        y = _ffn_sublayer(y, mod_l, g_l, ffn_w13, ffn_w2, l, 1)

    y_p = y[:N_PROMPT].reshape(BATCH, SEQ, D_MODEL)
    y_s = y[N_PROMPT:].reshape(DEC_BATCH, DEC_SEQ, D_MODEL)
    return (y_p, y_s, jnp.stack(new_k, axis=1), jnp.stack(new_v, axis=1), jnp.stack(new_c, axis=1),
            jnp.stack(new_n, axis=1), jnp.stack(new_m, axis=1))
```

```python
import functools

import jax
import jax.numpy as jnp
from jax import lax
from jax.experimental import pallas as pl
from jax.experimental.pallas import tpu as pltpu

D_MODEL = 1024
BATCH = 16
SEQ = 256
DEPTH = 4
DEC_BATCH = 2
DEC_SEQ = 1024
PAST_LEN = 256
GRID_W = 64
NA_HEADS = 8
NA_HEAD_DIM = 64
NA_WIN_H = 8
NA_WIN_W = 16
ML_HEADS = 4
ML_HEAD_DIM = 128
ML_CHUNK = 128
CONV_WIDTH = 3
D_FF = 2816
N_MOD = 9
EPS = 1e-6
N_AB = (DEPTH + 1) // 2
N_C = DEPTH // 2
NA_WIDTH = NA_HEADS * NA_HEAD_DIM
ML_WIDTH = ML_HEADS * ML_HEAD_DIM
AB_MAIN = 3 * NA_WIDTH + 4 * ML_WIDTH
N_GATES = 4 * ML_HEADS
AB_IN = AB_MAIN + N_GATES
SC_WIDTH = D_MODEL

N_PROMPT = BATCH * SEQ
N_SAMPLE = DEC_BATCH * DEC_SEQ
N_TOK = N_PROMPT + N_SAMPLE
GRID_ROWS = DEC_SEQ // GRID_W
NA_KH = min(NA_WIN_H, GRID_ROWS)
N_DR = 2 * NA_WIN_H - 1
N_DC = 2 * NA_WIN_W - 1
N_PAIR = N_DR - 1

LANES = 128
COND_ROWS = 8
VMEM_CAP = 60 * 1024 * 1024

COND_UNIT = DEC_SEQ
TM = 2048
TM_OUT = 1024
TF = 256
TN_PROJ = 896
TN_MOD = 1536
NA_QROWS = 4

F32 = jnp.float32
BF16 = jnp.bfloat16


TEMP_BYTES = 16 * 1024 * 1024


def _vmem_limit(block_bytes, scratch_bytes):
    return int(min(VMEM_CAP, 2 * block_bytes + scratch_bytes + TEMP_BYTES))


def _params(semantics, block_bytes, scratch_bytes=0):
    return pltpu.CompilerParams(dimension_semantics=semantics,
                                vmem_limit_bytes=_vmem_limit(block_bytes, scratch_bytes))


def _unit_rows(i, tm=TM):
    first_latent = N_PROMPT // COND_UNIT
    units = tm // COND_UNIT
    return [(slice(u * COND_UNIT, (u + 1) * COND_UNIT), jnp.maximum(i * units + u - (first_latent - 1), 0))
            for u in range(units)]


def _rms(x, g):
    ms = jnp.mean(x * x, axis=-1, keepdims=True)
    return x * lax.rsqrt(ms + EPS) * g


def _modulate_in(x, g_ref, mod_ref, r, j):
    h = _rms(x, g_ref[2 * j:2 * j + 1, :])
    return h * (1.0 + mod_ref[r, 3 * j + 1:3 * j + 2, :]) + mod_ref[r, 3 * j:3 * j + 1, :]


def _gated_residual(x, out, g_ref, mod_ref, r, j, weight):
    return x + weight * mod_ref[r, 3 * j + 2:3 * j + 3, :] * _rms(out, g_ref[2 * j + 1:2 * j + 2, :])


_MOD_SPEC_SHAPE = (COND_ROWS, N_MOD, D_MODEL)


def _dot(a, b):
    return jnp.dot(a.astype(BF16), b.astype(BF16), preferred_element_type=F32)


def _dot_nt(a, b):
    return lax.dot_general(a.astype(BF16), b.astype(BF16), (((1,), (1,)), ((), ())),
                           preferred_element_type=F32)


def _mod_kernel(c_ref, w_ref, b_ref, o_ref):
    c = c_ref[...]
    s = c * jax.nn.sigmoid(c)
    o_ref[...] = _dot(s, w_ref[...]) + b_ref[...]


def _modulation(cond, ada_w, ada_b):
    n = N_MOD * D_MODEL
    out = pl.pallas_call(
        _mod_kernel,
        grid=(DEPTH, n // TN_MOD),
        in_specs=[
            pl.BlockSpec((COND_ROWS, D_MODEL), lambda l, t: (0, 0)),
            pl.BlockSpec((None, D_MODEL, TN_MOD), lambda l, t: (l, 0, t)),
            pl.BlockSpec((None, 1, TN_MOD), lambda l, t: (l, 0, t)),
        ],
        out_specs=pl.BlockSpec((None, COND_ROWS, TN_MOD), lambda l, t: (l, 0, t)),
        out_shape=jax.ShapeDtypeStruct((DEPTH, COND_ROWS, n), F32),
        compiler_params=_params(("parallel", "parallel"), 4 * D_MODEL * TN_MOD),
        name="modulation",
    )(cond, ada_w, ada_b.reshape(DEPTH, 1, n))
    return out.reshape(DEPTH, COND_ROWS, N_MOD, D_MODEL)


def _ffn_kernel(x_ref, mod_ref, g_ref, wa_ref, wg_ref, w2_ref, o_ref, h_ref, *, j):
    i = pl.program_id(0)
    k = pl.program_id(1)

    @pl.when(k == 0)
    def _():
        for rows, r in _unit_rows(i):
            h_ref[rows, :] = _modulate_in(x_ref[rows, :], g_ref, mod_ref, r, j).astype(BF16)
        o_ref[...] = jnp.zeros_like(o_ref)

    h = h_ref[...]
    a = _dot(h, wa_ref[...])
    g = _dot(h, wg_ref[...])
    u = (g * jax.nn.sigmoid(g)) * a
    o_ref[...] += _dot(u, w2_ref[...])

    @pl.when(k == pl.num_programs(1) - 1)
    def _():
        for rows, r in _unit_rows(i):
            o_ref[rows, :] = _gated_residual(x_ref[rows, :], o_ref[rows, :], g_ref, mod_ref, r, j, 0.5)


def _ffn_sublayer(y, mod_l, g_l, ffn_w13, ffn_w2, l, s):
    j = 2 * s
    nk = D_FF // TF
    blocks = 4 * (2 * TM * D_MODEL + 3 * D_MODEL * TF)
    scratch = 2 * TM * D_MODEL
    return pl.pallas_call(
        functools.partial(_ffn_kernel, j=j),
        grid=(N_TOK // TM, nk),
        in_specs=[
            pl.BlockSpec((TM, D_MODEL), lambda i, k: (i, 0)),
            pl.BlockSpec(_MOD_SPEC_SHAPE, lambda i, k: (0, 0, 0)),
            pl.BlockSpec((6, D_MODEL), lambda i, k: (0, 0)),
            pl.BlockSpec((None, None, D_MODEL, TF), lambda i, k: (l, s, 0, k)),
            pl.BlockSpec((None, None, D_MODEL, TF), lambda i, k: (l, s, 0, k + nk)),
            pl.BlockSpec((None, None, TF, D_MODEL), lambda i, k: (l, s, k, 0)),
        ],
        out_specs=pl.BlockSpec((TM, D_MODEL), lambda i, k: (i, 0)),
        out_shape=jax.ShapeDtypeStruct((N_TOK, D_MODEL), F32),
        scratch_shapes=[pltpu.VMEM((TM, D_MODEL), BF16)],
        compiler_params=_params(("parallel", "arbitrary"), blocks, scratch),
        name=f"ffn_l{l}_s{s}",
    )(y, mod_l, g_l, ffn_w13, ffn_w13, ffn_w2)


def _sconv_kernel(x_ref, mod_ref, g_ref, wb_ref, wc_ref, wx_ref, cw_ref, cb_ref, wo_ref, o_ref, h_ref):
    i = pl.program_id(0)
    k = pl.program_id(1)

    @pl.when(k == 0)
    def _():
        for rows, r in _unit_rows(i):
            h_ref[rows, :] = _modulate_in(x_ref[rows, :], g_ref, mod_ref, r, 1).astype(BF16)
        o_ref[...] = jnp.zeros_like(o_ref)

    h = h_ref[...]
    gb = _dot(h, wb_ref[...])
    u = _dot(h, wc_ref[...]) * _dot(h, wx_ref[...])
    seq = jnp.where(i < N_PROMPT // TM, SEQ, DEC_SEQ)
    t = lax.broadcasted_iota(jnp.int32, u.shape, 0) & (seq - 1)
    u_prev = jnp.where(t == 0, 0.0, pltpu.roll(u, 1, 0))
    u_next = jnp.where(t == seq - 1, 0.0, pltpu.roll(u, TM - 1, 0))
    y = cb_ref[...] + u_prev * cw_ref[0:1, :] + u * cw_ref[1:2, :] + u_next * cw_ref[2:3, :]
    o_ref[...] += _dot(gb * y, wo_ref[...])

    @pl.when(k == pl.num_programs(1) - 1)
    def _():
        for rows, r in _unit_rows(i):
            o_ref[rows, :] = _gated_residual(x_ref[rows, :], o_ref[rows, :], g_ref, mod_ref, r, 1, 1.0)


def _sconv_sublayer(y, mod_l, g_l, sc_w_in, sc_conv_w, sc_conv_b, sc_w_out, jc):
    nk = SC_WIDTH // TF
    blocks = 4 * (2 * TM * D_MODEL + 4 * D_MODEL * TF)
    scratch = 2 * TM * D_MODEL
    return pl.pallas_call(
        _sconv_kernel,
        grid=(N_TOK // TM, nk),
        in_specs=[
            pl.BlockSpec((TM, D_MODEL), lambda i, k: (i, 0)),
            pl.BlockSpec(_MOD_SPEC_SHAPE, lambda i, k: (0, 0, 0)),
            pl.BlockSpec((6, D_MODEL), lambda i, k: (0, 0)),
            pl.BlockSpec((None, D_MODEL, TF), lambda i, k: (jc, 0, k)),
            pl.BlockSpec((None, D_MODEL, TF), lambda i, k: (jc, 0, k + nk)),
            pl.BlockSpec((None, D_MODEL, TF), lambda i, k: (jc, 0, k + 2 * nk)),
            pl.BlockSpec((None, CONV_WIDTH, TF), lambda i, k: (jc, 0, k)),
            pl.BlockSpec((None, 1, TF), lambda i, k: (jc, 0, k)),
            pl.BlockSpec((None, TF, D_MODEL), lambda i, k: (jc, k, 0)),
        ],
        out_specs=pl.BlockSpec((TM, D_MODEL), lambda i, k: (i, 0)),
        out_shape=jax.ShapeDtypeStruct((N_TOK, D_MODEL), F32),
        scratch_shapes=[pltpu.VMEM((TM, D_MODEL), BF16)],
        compiler_params=_params(("parallel", "arbitrary"), blocks, scratch),
        name=f"sconv_{jc}",
    )(y, mod_l, g_l, sc_w_in, sc_w_in, sc_w_in, sc_conv_w, sc_conv_b.reshape(N_C, 1, SC_WIDTH), sc_w_out)


def _inproj_kernel(x_ref, mod_ref, g_ref, w_ref, wg_ref, o_ref, og_ref, h_ref):
    i = pl.program_id(0)
    n = pl.program_id(1)

    @pl.when(n == 0)
    def _():
        for rows, r in _unit_rows(i):
            h_ref[rows, :] = _modulate_in(x_ref[rows, :], g_ref, mod_ref, r, 1).astype(BF16)
        og_ref[...] = _dot(h_ref[...], wg_ref[...])

    o_ref[...] = _dot(h_ref[...], w_ref[...])


def _inproj(y, mod_l, g_l, ab_w_in, jb):
    w_gates = jnp.pad(ab_w_in[jb, :, AB_MAIN:], ((0, 0), (0, LANES - N_GATES)))
    blocks = 4 * (TM * D_MODEL + D_MODEL * TN_PROJ + D_MODEL * LANES + TM * TN_PROJ + TM * LANES)
    return pl.pallas_call(
        _inproj_kernel,
        grid=(N_TOK // TM, AB_MAIN // TN_PROJ),
        in_specs=[
            pl.BlockSpec((TM, D_MODEL), lambda i, n: (i, 0)),
            pl.BlockSpec(_MOD_SPEC_SHAPE, lambda i, n: (0, 0, 0)),
            pl.BlockSpec((6, D_MODEL), lambda i, n: (0, 0)),
            pl.BlockSpec((None, D_MODEL, TN_PROJ), lambda i, n: (jb, 0, n)),
            pl.BlockSpec((D_MODEL, LANES), lambda i, n: (0, 0)),
        ],
        out_specs=[
            pl.BlockSpec((TM, TN_PROJ), lambda i, n: (i, n)),
            pl.BlockSpec((TM, LANES), lambda i, n: (i, 0)),
        ],
        out_shape=[jax.ShapeDtypeStruct((N_TOK, AB_MAIN), F32),
                   jax.ShapeDtypeStruct((N_TOK, LANES), F32)],
        scratch_shapes=[pltpu.VMEM((TM, D_MODEL), BF16)],
        compiler_params=_params(("parallel", "arbitrary"), blocks, 2 * TM * D_MODEL),
        name=f"inproj_{jb}",
    )(y, mod_l, g_l, ab_w_in, w_gates)


def _softmax_rows(parts):
    m = functools.reduce(jnp.maximum, [jnp.max(s, axis=-1, keepdims=True) for s in parts])
    ps = [jnp.exp(s - m) for s in parts]
    inv = 1.0 / functools.reduce(jnp.add, [jnp.sum(p, axis=-1, keepdims=True) for p in ps])
    return [p * inv for p in ps]


def _ctx_attn_kernel(q_ref, k_ref, v_ref, o_ref, kc_ref, vc_ref):
    scale = NA_HEAD_DIM ** -0.5
    for h in range(NA_HEADS):
        sl = slice(h * NA_HEAD_DIM, (h + 1) * NA_HEAD_DIM)
        kh = k_ref[:, sl]
        vh = v_ref[:, sl]
        kc_ref[h] = kh
        vc_ref[h] = vh
        (p,) = _softmax_rows([_dot_nt(q_ref[:, sl], kh) * scale])
        o_ref[:, sl] = _dot(p, vh)


def _ctx_attention(proj):
    cache = jax.ShapeDtypeStruct((BATCH, NA_HEADS, SEQ, NA_HEAD_DIM), F32)
    cache_spec = pl.BlockSpec((None, NA_HEADS, SEQ, NA_HEAD_DIM), lambda b: (b, 0, 0, 0))
    blocks = 4 * (4 * SEQ * NA_WIDTH + 2 * NA_HEADS * SEQ * LANES)
    return pl.pallas_call(
        _ctx_attn_kernel,
        grid=(BATCH,),
        in_specs=[pl.BlockSpec((SEQ, NA_WIDTH), lambda b, c=c: (b, c)) for c in range(3)],
        out_specs=[pl.BlockSpec((SEQ, NA_WIDTH), lambda b: (b, 0)), cache_spec, cache_spec],
        out_shape=[jax.ShapeDtypeStruct((N_PROMPT, NA_WIDTH), F32), cache, cache],
        compiler_params=_params(("parallel",), blocks),
        name="ctx_attention",
    )(proj, proj, proj)


def _na_bias_kernel(rpb_ref, p_ref):
    base = (pl.program_id(0) * NA_HEADS + pl.program_id(1)) * (N_DR * N_DC)
    shape = (GRID_W, LANES)
    lane = lax.broadcasted_iota(jnp.int32, shape, 1)
    qc = lax.broadcasted_iota(jnp.int32, shape, 0)
    kc = lane & (GRID_W - 1)
    upper = lane >= GRID_W
    cs = jnp.clip(qc - NA_WIN_W // 2, 0, GRID_W - NA_WIN_W)
    col_in = (kc >= cs) & (kc < cs + NA_WIN_W)
    dc = jnp.clip(kc - qc, -(NA_WIN_W - 1), NA_WIN_W - 1) + (NA_WIN_W - 1)
    for pair in range(N_PAIR):
        acc = jnp.zeros(shape, F32)
        for d in range(N_DC):
            lo = rpb_ref[base + pair * N_DC + d]
            hi = rpb_ref[base + (pair + 1) * N_DC + d]
            acc = jnp.where(dc == d, jnp.where(upper, hi, lo), acc)
        p_ref[pair] = jnp.where(col_in, acc, -jnp.inf)


def _na_bias(na_rpb):
    return pl.pallas_call(
        _na_bias_kernel,
        grid=(N_AB, NA_HEADS),
        in_specs=[pl.BlockSpec(memory_space=pltpu.SMEM)],
        out_specs=pl.BlockSpec((None, None, N_PAIR, GRID_W, LANES), lambda j, h: (j, h, 0, 0, 0)),
        out_shape=jax.ShapeDtypeStruct((N_AB, NA_HEADS, N_PAIR, GRID_W, LANES), F32),
        compiler_params=_params(("parallel", "parallel"), 4 * N_PAIR * GRID_W * LANES),
        name="na_bias",
    )(na_rpb.reshape(-1))


def _na_window_start(r):
    return min(max(r - NA_KH // 2, 0), GRID_ROWS - NA_KH)


def _na_bias_block(p_ref, hh, qr, kr):
    rs = _na_window_start(qr)
    lo_ok = rs <= kr < rs + NA_KH
    hi_ok = rs <= kr + 1 < rs + NA_KH
    if not (lo_ok or hi_ok):
        return jnp.full((GRID_W, LANES), -jnp.inf, F32)
    dr = kr - qr + (NA_WIN_H - 1)
    assert 0 <= dr < N_PAIR
    blk = p_ref[hh, dr]
    if lo_ok and hi_ok:
        return blk
    lane = lax.broadcasted_iota(jnp.int32, (GRID_W, LANES), 1)
    keep = (lane < GRID_W) if lo_ok else (lane >= GRID_W)
    return jnp.where(keep, blk, -jnp.inf)


def _na_kernel(q_ref, k_ref, v_ref, kc_ref, vc_ref, p_ref, o_ref):
    scale = NA_HEAD_DIM ** -0.5
    for hh in range(LANES // NA_HEAD_DIM):
        sl = slice(hh * NA_HEAD_DIM, (hh + 1) * NA_HEAD_DIM)
        k_ctx = kc_ref[hh]
        v_ctx = vc_ref[hh]
        for q0 in range(0, GRID_ROWS, NA_QROWS):
            k_lo = _na_window_start(q0) // 2 * 2
            k_hi = -(-(_na_window_start(q0 + NA_QROWS - 1) + NA_KH) // 2) * 2
            q_rows = slice(q0 * GRID_W, (q0 + NA_QROWS) * GRID_W)
            k_rows = slice(k_lo * GRID_W, k_hi * GRID_W)
            bias = jnp.concatenate(
                [jnp.concatenate([_na_bias_block(p_ref, hh, qr, kr) for kr in range(k_lo, k_hi, 2)], axis=1)
                 for qr in range(q0, q0 + NA_QROWS)], axis=0)
            q = q_ref[q_rows, sl]
            s_loc = _dot_nt(q, k_ref[k_rows, sl]) * scale + bias
            s_ctx = _dot_nt(q, k_ctx) * scale
            m = jnp.maximum(jnp.max(s_loc, axis=-1, keepdims=True), jnp.max(s_ctx, axis=-1, keepdims=True))
            p_loc = jnp.exp(s_loc - m)
            p_ctx = jnp.exp(s_ctx - m)
            denom = jnp.sum(p_loc, axis=-1, keepdims=True) + jnp.sum(p_ctx, axis=-1, keepdims=True)
            o_ref[q_rows, sl] = (_dot(p_loc, v_ref[k_rows, sl]) + _dot(p_ctx, v_ctx)) / denom


def _na_attention(proj, cache_k, cache_v, bias_pairs, jb):
    hp = LANES // NA_HEAD_DIM
    s0 = N_PROMPT // DEC_SEQ
    nb = NA_WIDTH // LANES
    ctx_spec = pl.BlockSpec((None, None, hp, PAST_LEN, NA_HEAD_DIM), lambda b, c: (b, jb, c, 0, 0))
    blocks = 4 * (4 * DEC_SEQ * LANES + 2 * hp * PAST_LEN * LANES + hp * N_PAIR * GRID_W * LANES)
    return pl.pallas_call(
        _na_kernel,
        grid=(DEC_BATCH, nb),
        in_specs=[
            pl.BlockSpec((DEC_SEQ, LANES), lambda b, c: (s0 + b, c)),
            pl.BlockSpec((DEC_SEQ, LANES), lambda b, c: (s0 + b, nb + c)),
            pl.BlockSpec((DEC_SEQ, LANES), lambda b, c: (s0 + b, 2 * nb + c)),
            ctx_spec, ctx_spec,
            pl.BlockSpec((None, hp, N_PAIR, GRID_W, LANES), lambda b, c: (jb, c, 0, 0, 0)),
        ],
        out_specs=pl.BlockSpec((DEC_SEQ, LANES), lambda b, c: (b, c)),
        out_shape=jax.ShapeDtypeStruct((N_SAMPLE, NA_WIDTH), F32),
        compiler_params=_params(("parallel", "parallel"), blocks),
        name=f"na_attention_{jb}",
    )(proj, proj, proj, cache_k, cache_v, bias_pairs)


def _mlstm_kernel(*refs, n_chunks, has_state):
    if has_state:
        (q_ref, k_ref, v_ref, o_ref, gt_ref, gb_ref, hn_ref, c0_ref, n0_ref, m0_ref,
         out_ref, c_ref, nout_ref, mout_ref, hdir_ref) = refs
    else:
        (q_ref, k_ref, v_ref, o_ref, gt_ref, gb_ref, hn_ref,
         out_ref, c_ref, nout_ref, mout_ref, hdir_ref) = refs
    T = ML_CHUNK
    dh = ML_HEAD_DIM
    b = pl.program_id(0)
    row = lax.broadcasted_iota(jnp.int32, (T, T), 0)
    col = lax.broadcasted_iota(jnp.int32, (T, T), 1)
    lane = lax.broadcasted_iota(jnp.int32, (T, LANES), 1)
    is_forget = ((lane >= ML_HEADS) & (lane < 2 * ML_HEADS)) | ((lane >= 3 * ML_HEADS) & (lane < 4 * ML_HEADS))
    states = [(d, h) for d in range(2) for h in range(ML_HEADS)]

    for d, h in states:
        c_ref[d, h] = c0_ref[d, h] if has_state else jnp.zeros((dh, dh), F32)
    if has_state:
        n_init = tuple(n0_ref[d, h:h + 1, :] for d, h in states)
        m_init = tuple(jnp.full((1, 1), m0_ref[(b * 2 + d) * ML_HEADS + h], F32) for d, h in states)
    else:
        n_init = tuple(jnp.zeros((1, dh), F32) for _ in states)
        m_init = tuple(jnp.zeros((1, 1), F32) for _ in states)

    def chunk_body(ci, carry):
        n_st, m_st = carry
        n_new, m_new = [], []
        for d in range(2):
            causal = (col <= row) if d == 0 else (col >= row)
            last = T - 1 if d == 0 else 0
            c = ci if d == 0 else n_chunks - 1 - ci
            rows = pl.ds(pl.multiple_of(c * T, T), T)
            g = gt_ref[rows, :] + gb_ref[...]
            log_sig = jnp.minimum(g, 0.0) - jnp.log1p(jnp.exp(-jnp.abs(g)))
            g = jnp.where(is_forget, log_sig, g)
            cum = jnp.dot(causal.astype(F32), g, precision=lax.Precision.HIGHEST, preferred_element_type=F32)
            g_t = g.T
            cum_t = cum.T
            for h in range(ML_HEADS):
                st = d * ML_HEADS + h
                hl = slice(h * dh, (h + 1) * dh)
                gi = 2 * d * ML_HEADS + h
                gf = gi + ML_HEADS
                b_col = cum[:, gf:gf + 1]
                b_row = cum_t[gf:gf + 1, :]
                i_col = g[:, gi:gi + 1]
                i_row = g_t[gi:gi + 1, :]
                a = b_col + m_st[st]
                dm = jnp.where(causal, b_col - b_row + i_row, -jnp.inf)
                m_t = jnp.maximum(a, jnp.max(dm, axis=-1, keepdims=True))
                w_inter = jnp.exp(a - m_t)
                w_d = jnp.exp(dm - m_t)
                qh = q_ref[rows, hl]
                kh = k_ref[rows, hl] * (dh ** -0.5)
                vh = v_ref[rows, hl]
                s = _dot_nt(qh, kh) * w_d
                c_prev = c_ref[d, h]
                num = w_inter * _dot(qh, c_prev) + _dot(s, vh)
                den = (w_inter * jnp.sum(qh * n_st[st], axis=-1, keepdims=True)
                       + jnp.sum(s, axis=-1, keepdims=True))
                hdir_ref[d, rows, hl] = num / jnp.maximum(jnp.abs(den), jnp.exp(-m_t))
                m_last = m_t[last:last + 1, :]
                wi_last = w_inter[last:last + 1, :]
                w_last = jnp.exp(b_col[last:last + 1, :] - b_col + i_col - m_last)
                kw = kh * w_last
                c_ref[d, h] = wi_last * c_prev + _dot(kw.T, vh)
                n_new.append(wi_last * n_st[st] + jnp.sum(kw, axis=0, keepdims=True))
                m_new.append(m_last)
        return tuple(n_new), tuple(m_new)

    n_fin, m_fin = lax.fori_loop(0, n_chunks, chunk_body, (n_init, m_init))
    for st, (d, h) in enumerate(states):
        nout_ref[d, h:h + 1, :] = n_fin[st]
        mout_ref[st:st + 1, :] = jnp.broadcast_to(m_fin[st], (1, LANES))

    for h in range(ML_HEADS):
        hl = slice(h * dh, (h + 1) * dh)
        h_sum = hdir_ref[0, :, hl] + hdir_ref[1, :, hl]
        out_ref[:, hl] = jax.nn.sigmoid(o_ref[:, hl]) * _rms(h_sum, hn_ref[:, hl])


def _mlstm(proj, gates, gate_b_row, hnorm_row, seq_len, n_seq, row0, state=None):
    has_state = state is not None
    r0 = row0 // seq_len
    c0 = 3 * NA_WIDTH // ML_WIDTH
    in_specs = [pl.BlockSpec((seq_len, ML_WIDTH), lambda b, c=c: (r0 + b, c0 + c)) for c in range(4)]
    in_specs += [
        pl.BlockSpec((seq_len, LANES), lambda b: (r0 + b, 0)),
        pl.BlockSpec((1, LANES), lambda b: (0, 0)),
        pl.BlockSpec((1, ML_WIDTH), lambda b: (0, 0)),
    ]
    args = [proj, proj, proj, proj, gates, gate_b_row, hnorm_row]
    c_spec = pl.BlockSpec((None, 2, ML_HEADS, ML_HEAD_DIM, ML_HEAD_DIM), lambda b: (b, 0, 0, 0, 0))
    n_spec = pl.BlockSpec((None, 2, ML_HEADS, ML_HEAD_DIM), lambda b: (b, 0, 0, 0))
    if has_state:
        st_c, st_n, st_m = state
        in_specs += [c_spec, n_spec, pl.BlockSpec(memory_space=pltpu.SMEM)]
        args += [st_c, st_n, st_m.reshape(-1)]
    c_bytes = 2 * ML_HEADS * ML_HEAD_DIM * ML_HEAD_DIM
    blocks = 4 * (5 * seq_len * ML_WIDTH + seq_len * LANES + 2 * c_bytes)
    scratch = 4 * 2 * seq_len * ML_WIDTH
    out, c_out, n_out, m_out = pl.pallas_call(
        functools.partial(_mlstm_kernel, n_chunks=seq_len // ML_CHUNK, has_state=has_state),
        grid=(n_seq,),
        in_specs=in_specs,
        out_specs=[
            pl.BlockSpec((seq_len, ML_WIDTH), lambda b: (b, 0)),
            c_spec, n_spec,
            pl.BlockSpec((None, 2 * ML_HEADS, LANES), lambda b: (b, 0, 0)),
        ],
        out_shape=[
            jax.ShapeDtypeStruct((n_seq * seq_len, ML_WIDTH), F32),
            jax.ShapeDtypeStruct((n_seq, 2, ML_HEADS, ML_HEAD_DIM, ML_HEAD_DIM), F32),
            jax.ShapeDtypeStruct((n_seq, 2, ML_HEADS, ML_HEAD_DIM), F32),
            jax.ShapeDtypeStruct((n_seq, 2 * ML_HEADS, LANES), F32),
        ],
        scratch_shapes=[pltpu.VMEM((2, seq_len, ML_WIDTH), F32)],
        compiler_params=_params(("parallel",), blocks, scratch),
        name=f"mlstm_{'latent' if has_state else 'context'}",
    )(*args)
    return out, c_out, n_out, m_out[:, :, 0].reshape(n_seq, 2, ML_HEADS)


def _outproj_kernel(x_ref, mod_ref, g_ref, ap_ref, mp_ref, as_ref, ms_ref, wa_ref, wm_ref, o_ref):
    i = pl.program_id(0)

    def finish(a_ref, m_ref):
        out = _dot(a_ref[...], wa_ref[...]) + _dot(m_ref[...], wm_ref[...])
        for rows, r in _unit_rows(i, TM_OUT):
            o_ref[rows, :] = _gated_residual(x_ref[rows, :], out[rows, :], g_ref, mod_ref, r, 1, 1.0)

    is_prompt = i < N_PROMPT // TM_OUT
    pl.when(is_prompt)(functools.partial(finish, ap_ref, mp_ref))
    pl.when(jnp.logical_not(is_prompt))(functools.partial(finish, as_ref, ms_ref))


def _outproj(y, mod_l, g_l, a_p, m_p, a_s, m_s, ab_w_out, jb):
    tm = TM_OUT
    n_p = N_PROMPT // tm
    n_s = N_SAMPLE // tm
    p_spec = pl.BlockSpec((tm, NA_WIDTH), lambda i: (jnp.minimum(i, n_p - 1), 0))
    s_spec = pl.BlockSpec((tm, NA_WIDTH), lambda i: (jnp.clip(i - n_p, 0, n_s - 1), 0))
    blocks = 4 * (2 * tm * D_MODEL + 4 * tm * NA_WIDTH + D_MODEL * D_MODEL)
    return pl.pallas_call(
        _outproj_kernel,
        grid=(N_TOK // tm,),
        in_specs=[
            pl.BlockSpec((tm, D_MODEL), lambda i: (i, 0)),
            pl.BlockSpec(_MOD_SPEC_SHAPE, lambda i: (0, 0, 0)),
            pl.BlockSpec((6, D_MODEL), lambda i: (0, 0)),
            p_spec, p_spec, s_spec, s_spec,
            pl.BlockSpec((None, NA_WIDTH, D_MODEL), lambda i: (jb, 0, 0)),
            pl.BlockSpec((None, ML_WIDTH, D_MODEL), lambda i: (jb, 1, 0)),
        ],
        out_specs=pl.BlockSpec((tm, D_MODEL), lambda i: (i, 0)),
        out_shape=jax.ShapeDtypeStruct((N_TOK, D_MODEL), F32),
        compiler_params=_params(("parallel",), blocks),
        name=f"outproj_{jb}",
    )(y, mod_l, g_l, a_p, m_p, a_s, m_s, ab_w_out, ab_w_out)


def kernel(x_prompt, x_sample, cache_k, cache_v, state_C, state_n, state_m, c, c_ctx, ada_w, ada_b, norm_g,
           ffn_w13, ffn_w2, ab_w_in, ab_w_out, na_rpb, ml_gate_b, ml_hnorm, sc_w_in, sc_conv_w, sc_conv_b,
           sc_w_out):
    assert NA_WIDTH == ML_WIDTH and N_PROMPT % TM == 0 and N_SAMPLE % TM == 0 and TM % COND_UNIT == 0
    cond = jnp.concatenate([c_ctx[None, :], c, jnp.zeros((COND_ROWS - 1 - DEC_BATCH, D_MODEL), F32)], axis=0)
    mod = _modulation(cond, ada_w, ada_b)
    bias_pairs = _na_bias(na_rpb)
    y = jnp.concatenate([x_prompt.reshape(N_PROMPT, D_MODEL), x_sample.reshape(N_SAMPLE, D_MODEL)], axis=0)

    new_k, new_v, new_c, new_n, new_m = [], [], [], [], []
    for l in range(DEPTH):
        mod_l = mod[l]
        g_l = norm_g[l].reshape(6, D_MODEL)
        y = _ffn_sublayer(y, mod_l, g_l, ffn_w13, ffn_w2, l, 0)
        j = l // 2
        if l % 2 == 0:
            proj, gates = _inproj(y, mod_l, g_l, ab_w_in, j)
            gate_b_row = jnp.pad(ml_gate_b[j].reshape(1, N_GATES), ((0, 0), (0, LANES - N_GATES)))
            hnorm_row = ml_hnorm[j].reshape(1, ML_WIDTH)
            a_p, k_c, v_c = _ctx_attention(proj)
            m_p, c_c, n_c, m_c = _mlstm(proj, gates, gate_b_row, hnorm_row, SEQ, BATCH, 0)
            a_s = _na_attention(proj, cache_k, cache_v, bias_pairs, j)
            m_s, _, _, _ = _mlstm(proj, gates, gate_b_row, hnorm_row, DEC_SEQ, DEC_BATCH, N_PROMPT,
                                  state=(state_C[:, j], state_n[:, j], state_m[:, j]))
            new_k.append(k_c)
            new_v.append(v_c)
            new_c.append(c_c)
            new_n.append(n_c)
            new_m.append(m_c)
            y = _outproj(y, mod_l, g_l, a_p, m_p, a_s, m_s, ab_w_out, j)
        else:
            y = _sconv_sublayer(y, mod_l, g_l, sc_w_in, sc_conv_w, sc_conv_b, sc_w_out, j)
        y = _ffn_sublayer(y, mod_l, g_l, ffn_w13, ffn_w2, l, 1)

    y_p = y[:N_PROMPT].reshape(BATCH, SEQ, D_MODEL)
    y_s = y[N_PROMPT:].reshape(DEC_BATCH, DEC_SEQ, D_MODEL)
    return (y_p, y_s, jnp.stack(new_k, axis=1), jnp.stack(new_v, axis=1), jnp.stack(new_c, axis=1),
            jnp.stack(new_n, axis=1), jnp.stack(new_m, axis=1))
```

```python
import functools

import jax
import jax.numpy as jnp
from jax import lax
from jax.experimental import pallas as pl
from jax.experimental.pallas import tpu as pltpu

D_MODEL = 1024
BATCH = 16
SEQ = 256
DEPTH = 4
DEC_BATCH = 2
DEC_SEQ = 1024
PAST_LEN = 256
GRID_W = 64
NA_HEADS = 8
NA_HEAD_DIM = 64
NA_WIN_H = 8
NA_WIN_W = 16
ML_HEADS = 4
ML_HEAD_DIM = 128
ML_CHUNK = 128
CONV_WIDTH = 3
D_FF = 2816
N_MOD = 9
EPS = 1e-6
N_AB = (DEPTH + 1) // 2
N_C = DEPTH // 2
NA_WIDTH = NA_HEADS * NA_HEAD_DIM
ML_WIDTH = ML_HEADS * ML_HEAD_DIM
AB_MAIN = 3 * NA_WIDTH + 4 * ML_WIDTH
N_GATES = 4 * ML_HEADS
AB_IN = AB_MAIN + N_GATES
SC_WIDTH = D_MODEL

N_PROMPT = BATCH * SEQ
N_SAMPLE = DEC_BATCH * DEC_SEQ
N_TOK = N_PROMPT + N_SAMPLE
GRID_ROWS = DEC_SEQ // GRID_W
NA_KH = min(NA_WIN_H, GRID_ROWS)
N_DR = 2 * NA_WIN_H - 1
N_DC = 2 * NA_WIN_W - 1
N_PAIR = N_DR - 1

LANES = 128
COND_ROWS = 8
VMEM_CAP = 60 * 1024 * 1024
TEMP_BYTES = 16 * 1024 * 1024

COND_UNIT = DEC_SEQ
TM = 2048
TM_OUT = 1024
TF = 256
TN_PROJ = 512
TN_MOD = 1536
NA_QROWS = 4

PROJ_BLOCKS = (0, 1, 2, 3, 5, 6)
KM_BLOCK = 4
P_QA, P_KA, P_VA, P_QM, P_VM, P_OM = range(6)

F32 = jnp.float32
BF16 = jnp.bfloat16


def _vmem_limit(block_bytes, scratch_bytes):
    return int(min(VMEM_CAP, 2 * block_bytes + scratch_bytes + TEMP_BYTES))


def _params(semantics, block_bytes, scratch_bytes=0):
    return pltpu.CompilerParams(dimension_semantics=semantics,
                                vmem_limit_bytes=_vmem_limit(block_bytes, scratch_bytes))


def _unit_rows(i, tm=TM):
    first_latent = N_PROMPT // COND_UNIT
    units = tm // COND_UNIT
    return [(slice(u * COND_UNIT, (u + 1) * COND_UNIT), jnp.maximum(i * units + u - (first_latent - 1), 0))
            for u in range(units)]


def _rms_scale(x):
    return x * lax.rsqrt(jnp.mean(x * x, axis=-1, keepdims=True) + EPS)


def _modulate_in(x, g_ref, mod_ref, r, j):
    gain = g_ref[2 * j:2 * j + 1, :] * (1.0 + mod_ref[r, 3 * j + 1:3 * j + 2, :])
    return _rms_scale(x) * gain + mod_ref[r, 3 * j:3 * j + 1, :]


def _gated_residual(x, out, g_ref, mod_ref, r, j, weight):
    gain = (weight * mod_ref[r, 3 * j + 2:3 * j + 3, :]) * g_ref[2 * j + 1:2 * j + 2, :]
    return x + _rms_scale(out) * gain


_MOD_SPEC_SHAPE = (COND_ROWS, N_MOD, D_MODEL)


def _dot(a, b):
    return jnp.dot(a.astype(BF16), b.astype(BF16), preferred_element_type=F32)


def _dot_nt(a, b):
    return lax.dot_general(a.astype(BF16), b.astype(BF16), (((1,), (1,)), ((), ())),
                           preferred_element_type=F32)


def _mod_kernel(c_ref, w_ref, b_ref, o_ref):
    c = c_ref[...]
    s = c * jax.nn.sigmoid(c)
    o_ref[...] = _dot(s, w_ref[...]) + b_ref[...]


def _modulation(cond, ada_w, ada_b):
    n = N_MOD * D_MODEL
    out = pl.pallas_call(
        _mod_kernel,
        grid=(DEPTH, n // TN_MOD),
        in_specs=[
            pl.BlockSpec((COND_ROWS, D_MODEL), lambda l, t: (0, 0)),
            pl.BlockSpec((None, D_MODEL, TN_MOD), lambda l, t: (l, 0, t)),
            pl.BlockSpec((None, 1, TN_MOD), lambda l, t: (l, 0, t)),
        ],
        out_specs=pl.BlockSpec((None, COND_ROWS, TN_MOD), lambda l, t: (l, 0, t)),
        out_shape=jax.ShapeDtypeStruct((DEPTH, COND_ROWS, n), F32),
        compiler_params=_params(("parallel", "parallel"), 4 * D_MODEL * TN_MOD),
        name="modulation",
    )(cond, ada_w, ada_b.reshape(DEPTH, 1, n))
    return out.reshape(DEPTH, COND_ROWS, N_MOD, D_MODEL)


def _ffn_kernel(x_ref, mod_ref, g_ref, wa_ref, wg_ref, w2_ref, o_ref, h_ref, *, j):
    i = pl.program_id(0)
    k = pl.program_id(1)

    @pl.when(k == 0)
    def _():
        for rows, r in _unit_rows(i):
            h_ref[rows, :] = _modulate_in(x_ref[rows, :], g_ref, mod_ref, r, j).astype(BF16)
        o_ref[...] = jnp.zeros_like(o_ref)

    h = h_ref[...]
    a = _dot(h, wa_ref[...])
    g = _dot(h, wg_ref[...])
    u = (g * jax.nn.sigmoid(g)) * a
    o_ref[...] += _dot(u, w2_ref[...])

    @pl.when(k == pl.num_programs(1) - 1)
    def _():
        for rows, r in _unit_rows(i):
            o_ref[rows, :] = _gated_residual(x_ref[rows, :], o_ref[rows, :], g_ref, mod_ref, r, j, 0.5)


def _ffn_sublayer(y, mod_l, g_l, ffn_w13, ffn_w2, l, s):
    j = 2 * s
    nk = D_FF // TF
    blocks = 4 * (2 * TM * D_MODEL + 3 * D_MODEL * TF)
    scratch = 2 * TM * D_MODEL
    return pl.pallas_call(
        functools.partial(_ffn_kernel, j=j),
        grid=(N_TOK // TM, nk),
        in_specs=[
            pl.BlockSpec((TM, D_MODEL), lambda i, k: (i, 0)),
            pl.BlockSpec(_MOD_SPEC_SHAPE, lambda i, k: (0, 0, 0)),
            pl.BlockSpec((6, D_MODEL), lambda i, k: (0, 0)),
            pl.BlockSpec((None, None, D_MODEL, TF), lambda i, k: (l, s, 0, k)),
            pl.BlockSpec((None, None, D_MODEL, TF), lambda i, k: (l, s, 0, k + nk)),
            pl.BlockSpec((None, None, TF, D_MODEL), lambda i, k: (l, s, k, 0)),
        ],
        out_specs=pl.BlockSpec((TM, D_MODEL), lambda i, k: (i, 0)),
        out_shape=jax.ShapeDtypeStruct((N_TOK, D_MODEL), F32),
        scratch_shapes=[pltpu.VMEM((TM, D_MODEL), BF16)],
        compiler_params=_params(("parallel", "arbitrary"), blocks, scratch),
        name=f"ffn_l{l}_s{s}",
    )(y, mod_l, g_l, ffn_w13, ffn_w13, ffn_w2)


def _sconv_kernel(x_ref, mod_ref, g_ref, wb_ref, wc_ref, wx_ref, cw_ref, cb_ref, wo_ref, o_ref, h_ref):
    i = pl.program_id(0)
    k = pl.program_id(1)

    @pl.when(k == 0)
    def _():
        for rows, r in _unit_rows(i):
            h_ref[rows, :] = _modulate_in(x_ref[rows, :], g_ref, mod_ref, r, 1).astype(BF16)
        o_ref[...] = jnp.zeros_like(o_ref)

    wb = wb_ref[...].astype(BF16)
    wc = wc_ref[...].astype(BF16)
    wx = wx_ref[...].astype(BF16)
    wo = wo_ref[...].astype(BF16)
    seq = jnp.where(i < N_PROMPT // TM, SEQ, DEC_SEQ)
    t = lax.broadcasted_iota(jnp.int32, (COND_UNIT, TF), 0) & (seq - 1)
    for rows, _ in _unit_rows(i):
        h = h_ref[rows, :]
        gb = _dot(h, wb)
        u = _dot(h, wc) * _dot(h, wx)
        u_prev = jnp.where(t == 0, 0.0, pltpu.roll(u, 1, 0))
        u_next = jnp.where(t == seq - 1, 0.0, pltpu.roll(u, COND_UNIT - 1, 0))
        y = cb_ref[...] + u_prev * cw_ref[0:1, :] + u * cw_ref[1:2, :] + u_next * cw_ref[2:3, :]
        o_ref[rows, :] += _dot(gb * y, wo)

    @pl.when(k == pl.num_programs(1) - 1)
    def _():
        for rows, r in _unit_rows(i):
            o_ref[rows, :] = _gated_residual(x_ref[rows, :], o_ref[rows, :], g_ref, mod_ref, r, 1, 1.0)


def _sconv_sublayer(y, mod_l, g_l, sc_w_in, sc_conv_w, sc_conv_b, sc_w_out, jc):
    nk = SC_WIDTH // TF
    blocks = 4 * (2 * TM * D_MODEL + 4 * D_MODEL * TF)
    scratch = 2 * TM * D_MODEL
    return pl.pallas_call(
        _sconv_kernel,
        grid=(N_TOK // TM, nk),
        in_specs=[
            pl.BlockSpec((TM, D_MODEL), lambda i, k: (i, 0)),
            pl.BlockSpec(_MOD_SPEC_SHAPE, lambda i, k: (0, 0, 0)),
            pl.BlockSpec((6, D_MODEL), lambda i, k: (0, 0)),
            pl.BlockSpec((None, D_MODEL, TF), lambda i, k: (jc, 0, k)),
            pl.BlockSpec((None, D_MODEL, TF), lambda i, k: (jc, 0, k + nk)),
            pl.BlockSpec((None, D_MODEL, TF), lambda i, k: (jc, 0, k + 2 * nk)),
            pl.BlockSpec((None, CONV_WIDTH, TF), lambda i, k: (jc, 0, k)),
            pl.BlockSpec((None, 1, TF), lambda i, k: (jc, 0, k)),
            pl.BlockSpec((None, TF, D_MODEL), lambda i, k: (jc, k, 0)),
        ],
        out_specs=pl.BlockSpec((TM, D_MODEL), lambda i, k: (i, 0)),
        out_shape=jax.ShapeDtypeStruct((N_TOK, D_MODEL), F32),
        scratch_shapes=[pltpu.VMEM((TM, D_MODEL), BF16)],
        compiler_params=_params(("parallel", "arbitrary"), blocks, scratch),
        name=f"sconv_{jc}",
    )(y, mod_l, g_l, sc_w_in, sc_w_in, sc_w_in, sc_conv_w, sc_conv_b.reshape(N_C, 1, SC_WIDTH), sc_w_out)


def _inproj_kernel(x_ref, mod_ref, g_ref, w_ref, wk_ref, wg_ref, o_ref, okt_ref, og_ref, h_ref):
    i = pl.program_id(0)
    n = pl.program_id(1)
    n_main = len(PROJ_BLOCKS)

    @pl.when(n == 0)
    def _():
        for rows, r in _unit_rows(i):
            h_ref[rows, :] = _modulate_in(x_ref[rows, :], g_ref, mod_ref, r, 1).astype(BF16)

    @pl.when(n < n_main)
    def _():
        o_ref[...] = _dot_nt(h_ref[...], w_ref[...])

    @pl.when(n == n_main)
    def _():
        kt = _dot_nt(wk_ref[...], h_ref[...])
        for c in range(TM // ML_CHUNK):
            okt_ref[c] = kt[:, c * ML_CHUNK:(c + 1) * ML_CHUNK]
        w_gates = jnp.concatenate([wg_ref[...], jnp.zeros((LANES - N_GATES, D_MODEL), F32)], axis=0)
        og_ref[...] = _dot_nt(h_ref[...], w_gates)


def _inproj(y, mod_l, g_l, w_in_t, jb):
    n_main = len(PROJ_BLOCKS)
    assert PROJ_BLOCKS == tuple(b for b in range(n_main + 1) if b != KM_BLOCK)

    def w_block(i, n):
        m = jnp.minimum(n, n_main - 1)
        return (jb, m + (m >= KM_BLOCK).astype(jnp.int32), 0)

    cpt = TM // ML_CHUNK
    blocks = 4 * (TM * D_MODEL + 2 * TN_PROJ * D_MODEL + N_GATES * D_MODEL + TM * TN_PROJ
                  + ML_WIDTH * TM + TM * LANES)
    return pl.pallas_call(
        _inproj_kernel,
        grid=(N_TOK // TM, n_main + 1),
        in_specs=[
            pl.BlockSpec((TM, D_MODEL), lambda i, n: (i, 0)),
            pl.BlockSpec(_MOD_SPEC_SHAPE, lambda i, n: (0, 0, 0)),
            pl.BlockSpec((6, D_MODEL), lambda i, n: (0, 0)),
            pl.BlockSpec((None, TN_PROJ, D_MODEL), w_block),
            pl.BlockSpec((None, ML_WIDTH, D_MODEL), lambda i, n: (jb, KM_BLOCK, 0)),
            pl.BlockSpec((None, N_GATES, D_MODEL), lambda i, n: (jb, AB_MAIN // N_GATES, 0)),
        ],
        out_specs=[
            pl.BlockSpec((TM, TN_PROJ), lambda i, n: (i, jnp.minimum(n, n_main - 1))),
            pl.BlockSpec((cpt, ML_WIDTH, ML_CHUNK), lambda i, n: (i, 0, 0)),
            pl.BlockSpec((TM, LANES), lambda i, n: (i, 0)),
        ],
        out_shape=[jax.ShapeDtypeStruct((N_TOK, n_main * TN_PROJ), F32),
                   jax.ShapeDtypeStruct((N_TOK // ML_CHUNK, ML_WIDTH, ML_CHUNK), F32),
                   jax.ShapeDtypeStruct((N_TOK, LANES), F32)],
        scratch_shapes=[pltpu.VMEM((TM, D_MODEL), BF16)],
        compiler_params=_params(("parallel", "arbitrary"), blocks, 2 * TM * D_MODEL),
        name=f"inproj_{jb}",
    )(y, mod_l, g_l, w_in_t, w_in_t, w_in_t)


def _ctx_attn_kernel(q_ref, k_ref, v_ref, *rest):
    o_ref, kc_ref, vc_ref = rest[-3:]
    scale = NA_HEAD_DIM ** -0.5
    k_t = k_ref[...].T
    v_t = v_ref[...].T
    for h in range(NA_HEADS):
        sl = slice(h * NA_HEAD_DIM, (h + 1) * NA_HEAD_DIM)
        kc_ref[h] = k_t[sl, :]
        vc_ref[h] = v_t[sl, :]
        s = _dot(q_ref[:, sl], k_t[sl, :]) * scale
        p = jnp.exp(s - jnp.max(s, axis=-1, keepdims=True))
        p = p * (1.0 / jnp.sum(p, axis=-1, keepdims=True))
        o_ref[:, sl] = _dot(p, v_ref[:, sl])


def _ctx_attention(proj, jb, prev):
    cache = jax.ShapeDtypeStruct((BATCH, N_AB, NA_HEADS, NA_HEAD_DIM, SEQ), F32)
    cache_spec = pl.BlockSpec((None, None, NA_HEADS, NA_HEAD_DIM, SEQ), lambda b: (b, jb, 0, 0, 0))
    in_specs = [pl.BlockSpec((SEQ, NA_WIDTH), lambda b, c=c: (b, c)) for c in (P_QA, P_KA, P_VA)]
    args = [proj, proj, proj]
    aliases = {}
    if prev is not None:
        in_specs += [pl.BlockSpec(memory_space=pl.ANY)] * 2
        args += list(prev)
        aliases = {3: 1, 4: 2}
    blocks = 4 * (4 * SEQ * NA_WIDTH + 2 * NA_WIDTH * SEQ)
    return pl.pallas_call(
        _ctx_attn_kernel,
        grid=(BATCH,),
        in_specs=in_specs,
        out_specs=[pl.BlockSpec((SEQ, NA_WIDTH), lambda b: (b, 0)), cache_spec, cache_spec],
        out_shape=[jax.ShapeDtypeStruct((N_PROMPT, NA_WIDTH), F32), cache, cache],
        input_output_aliases=aliases,
        compiler_params=_params(("parallel",), blocks),
        name=f"ctx_attention_{jb}",
    )(*args)


def _na_bias_kernel(rpb_ref, p_ref):
    base = (pl.program_id(0) * NA_HEADS + pl.program_id(1)) * (N_DR * N_DC)
    shape = (GRID_W, LANES)
    lane = lax.broadcasted_iota(jnp.int32, shape, 1)
    qc = lax.broadcasted_iota(jnp.int32, shape, 0)
    kc = lane & (GRID_W - 1)
    upper = lane >= GRID_W
    cs = jnp.clip(qc - NA_WIN_W // 2, 0, GRID_W - NA_WIN_W)
    col_in = (kc >= cs) & (kc < cs + NA_WIN_W)
    dc = jnp.clip(kc - qc, -(NA_WIN_W - 1), NA_WIN_W - 1) + (NA_WIN_W - 1)
    for pair in range(N_PAIR):
        acc = jnp.zeros(shape, F32)
        for d in range(N_DC):
            lo = rpb_ref[base + pair * N_DC + d]
            hi = rpb_ref[base + (pair + 1) * N_DC + d]
            acc = jnp.where(dc == d, jnp.where(upper, hi, lo), acc)
        p_ref[pair] = jnp.where(col_in, acc, -jnp.inf)


def _na_bias(na_rpb):
    return pl.pallas_call(
        _na_bias_kernel,
        grid=(N_AB, NA_HEADS),
        in_specs=[pl.BlockSpec(memory_space=pltpu.SMEM)],
        out_specs=pl.BlockSpec((None, None, N_PAIR, GRID_W, LANES), lambda j, h: (j, h, 0, 0, 0)),
        out_shape=jax.ShapeDtypeStruct((N_AB, NA_HEADS, N_PAIR, GRID_W, LANES), F32),
        compiler_params=_params(("parallel", "parallel"), 4 * N_PAIR * GRID_W * LANES),
        name="na_bias",
    )(na_rpb.reshape(-1))


def _na_window_start(r):
    return min(max(r - NA_KH // 2, 0), GRID_ROWS - NA_KH)


def _na_bias_block(p_ref, hh, qr, kr):
    rs = _na_window_start(qr)
    lo_ok = rs <= kr < rs + NA_KH
    hi_ok = rs <= kr + 1 < rs + NA_KH
    if not (lo_ok or hi_ok):
        return jnp.full((GRID_W, LANES), -jnp.inf, F32)
    dr = kr - qr + (NA_WIN_H - 1)
    assert 0 <= dr < N_PAIR
    blk = p_ref[hh, dr]
    if lo_ok and hi_ok:
        return blk
    lane = lax.broadcasted_iota(jnp.int32, (GRID_W, LANES), 1)
    keep = (lane < GRID_W) if lo_ok else (lane >= GRID_W)
    return jnp.where(keep, blk, -jnp.inf)


def _na_kernel(q_ref, k_ref, v_ref, kc_ref, vc_ref, p_ref, o_ref):
    scale = NA_HEAD_DIM ** -0.5
    for hh in range(LANES // NA_HEAD_DIM):
        sl = slice(hh * NA_HEAD_DIM, (hh + 1) * NA_HEAD_DIM)
        k_ctx_t = kc_ref[hh]
        v_ctx_t = vc_ref[hh]
        for q0 in range(0, GRID_ROWS, NA_QROWS):
            k_lo = _na_window_start(q0) // 2 * 2
            k_hi = -(-(_na_window_start(q0 + NA_QROWS - 1) + NA_KH) // 2) * 2
            q_rows = slice(q0 * GRID_W, (q0 + NA_QROWS) * GRID_W)
            k_rows = slice(k_lo * GRID_W, k_hi * GRID_W)
            bias = jnp.concatenate(
                [jnp.concatenate([_na_bias_block(p_ref, hh, qr, kr) for kr in range(k_lo, k_hi, 2)], axis=1)
                 for qr in range(q0, q0 + NA_QROWS)], axis=0)
            q = q_ref[q_rows, sl]
            s_loc = _dot_nt(q, k_ref[k_rows, sl]) * scale + bias
            s_ctx = _dot(q, k_ctx_t) * scale
            m = jnp.maximum(jnp.max(s_loc, axis=-1, keepdims=True), jnp.max(s_ctx, axis=-1, keepdims=True))
            p_loc = jnp.exp(s_loc - m)
            p_ctx = jnp.exp(s_ctx - m)
            denom = jnp.sum(p_loc, axis=-1, keepdims=True) + jnp.sum(p_ctx, axis=-1, keepdims=True)
            o_ref[q_rows, sl] = (_dot(p_loc, v_ref[k_rows, sl]) + _dot_nt(p_ctx, v_ctx_t)) / denom


def _na_attention(proj, cache_k_t, cache_v_t, bias_pairs, jb):
    hp = LANES // NA_HEAD_DIM
    s0 = N_PROMPT // DEC_SEQ
    nb = NA_WIDTH // LANES
    ctx_spec = pl.BlockSpec((None, None, hp, NA_HEAD_DIM, PAST_LEN), lambda b, c: (b, jb, c, 0, 0))
    blocks = 4 * (4 * DEC_SEQ * LANES + 2 * hp * NA_HEAD_DIM * PAST_LEN + hp * N_PAIR * GRID_W * LANES)
    return pl.pallas_call(
        _na_kernel,
        grid=(DEC_BATCH, nb),
        in_specs=[
            pl.BlockSpec((DEC_SEQ, LANES), lambda b, c: (s0 + b, P_QA * nb + c)),
            pl.BlockSpec((DEC_SEQ, LANES), lambda b, c: (s0 + b, P_KA * nb + c)),
            pl.BlockSpec((DEC_SEQ, LANES), lambda b, c: (s0 + b, P_VA * nb + c)),
            ctx_spec, ctx_spec,
            pl.BlockSpec((None, hp, N_PAIR, GRID_W, LANES), lambda b, c: (jb, c, 0, 0, 0)),
        ],
        out_specs=pl.BlockSpec((DEC_SEQ, LANES), lambda b, c: (b, c)),
        out_shape=jax.ShapeDtypeStruct((N_SAMPLE, NA_WIDTH), F32),
        compiler_params=_params(("parallel", "parallel"), blocks),
        name=f"na_attention_{jb}",
    )(proj, proj, proj, cache_k_t, cache_v_t, bias_pairs)


def _running_max_rows(x, reverse):
    n = x.shape[0]
    ridx = lax.broadcasted_iota(jnp.int32, x.shape, 0)
    sh = 1
    while sh < n:
        if reverse:
            x = jnp.maximum(x, jnp.where(ridx < n - sh, pltpu.roll(x, n - sh, 0), -jnp.inf))
        else:
            x = jnp.maximum(x, jnp.where(ridx >= sh, pltpu.roll(x, sh, 0), -jnp.inf))
        sh *= 2
    return x


def _mlstm_kernel(*refs, n_chunks, has_state):
    if has_state:
        (q_ref, kt_ref, v_ref, o_ref, gt_ref, gb_ref, hn_ref, c0_ref, n0_ref, m0_ref,
         out_ref, cout_ref, nout_ref, mout_ref, hdir_ref, caug_ref) = refs
    else:
        (q_ref, kt_ref, v_ref, o_ref, gt_ref, gb_ref, hn_ref, *_unused,
         out_ref, cout_ref, nout_ref, mout_ref, hdir_ref, caug_ref) = refs
    T = ML_CHUNK
    dh = ML_HEAD_DIM
    b = pl.program_id(0)
    row = lax.broadcasted_iota(jnp.int32, (T, T), 0)
    col = lax.broadcasted_iota(jnp.int32, (T, T), 1)
    lane = lax.broadcasted_iota(jnp.int32, (T, LANES), 1)
    lane_row = lax.broadcasted_iota(jnp.int32, (1, LANES), 1)
    is_forget = ((lane >= ML_HEADS) & (lane < 2 * ML_HEADS)) | ((lane >= 3 * ML_HEADS) & (lane < 4 * ML_HEADS))
    ones = jnp.ones((T, dh), F32)
    states = [(d, h) for d in range(2) for h in range(ML_HEADS)]

    def gate_lanes(d, h):
        gi = 2 * d * ML_HEADS + h
        return gi, gi + ML_HEADS

    m_init = []
    for d in range(2):
        m_row = jnp.zeros((1, LANES), F32)
        for h in range(ML_HEADS):
            if has_state:
                n_bcast = jnp.broadcast_to(n0_ref[d, h:h + 1, :], (dh, dh)).T
                caug_ref[d, h] = jnp.concatenate([c0_ref[d, h], n_bcast], axis=1)
                m_row = jnp.where(lane_row == gate_lanes(d, h)[1], m0_ref[(b * 2 + d) * ML_HEADS + h], m_row)
            else:
                caug_ref[d, h] = jnp.zeros((dh, 2 * dh), F32)
        m_init.append(m_row)

    def chunk_body(ci, m_rows):
        m_next = []
        for d in range(2):
            causal = (col <= row) if d == 0 else (col >= row)
            last = T - 1 if d == 0 else 0
            c = ci if d == 0 else n_chunks - 1 - ci
            rows = pl.ds(pl.multiple_of(c * T, T), T)
            mine = (lane >= (2 * d + 1) * ML_HEADS) & (lane < (2 * d + 2) * ML_HEADS)
            g = gt_ref[rows, :] + gb_ref[...]
            log_sig = jnp.minimum(g, 0.0) - jnp.log1p(jnp.exp(-jnp.abs(g)))
            g = jnp.where(is_forget, log_sig, g)
            cum = jnp.dot(causal.astype(F32), g, precision=lax.Precision.HIGHEST, preferred_element_type=F32)
            i_al = pltpu.roll(g, ML_HEADS, 1)
            cvec = jnp.where(mine, i_al - cum, 0.0)
            cum = jnp.where(mine, cum, 0.0)
            m_run = jnp.maximum(m_rows[d], _running_max_rows(cvec, reverse=(d == 1)))
            m_t = cum + m_run
            w_inter = jnp.exp(m_rows[d] - m_run)
            e_negm = jnp.exp(-m_t)
            m_run_last = m_run[last:last + 1, :]
            wi_last = w_inter[last:last + 1, :]
            w_last = jnp.exp(cvec - m_run_last)
            packed_t = jnp.where(mine, cvec, pltpu.roll(w_last, LANES - ML_HEADS, 1)).T
            for h in range(ML_HEADS):
                gi, gf = gate_lanes(d, h)
                hl = slice(h * dh, (h + 1) * dh)
                m_run_b = jnp.broadcast_to(m_run[:, gf:gf + 1], (T, T))
                e_negm_b = jnp.broadcast_to(e_negm[:, gf:gf + 1], (T, dh))
                w_d = jnp.exp(jnp.where(causal, packed_t[gf:gf + 1, :] - m_run_b, -jnp.inf))
                w_inter_b = jnp.exp(m_rows[d][:, gf:gf + 1] - m_run_b)
                qh = q_ref[rows, hl]
                kh_t = kt_ref[c, hl, :] * (dh ** -0.5)
                v_aug = jnp.concatenate([v_ref[rows, hl], ones], axis=1)
                s = _dot(qh, kh_t) * w_d
                c_aug = caug_ref[d, h]
                nd = jnp.concatenate([w_inter_b, w_inter_b], axis=1) * _dot(qh, c_aug) + _dot(s, v_aug)
                hdir_ref[d, rows, hl] = nd[:, :dh] / jnp.maximum(jnp.abs(nd[:, dh:]), e_negm_b)
                kw_t = kh_t * packed_t[gi:gi + 1, :]
                caug_ref[d, h] = wi_last[:, gf:gf + 1] * c_aug + _dot(kw_t, v_aug)
            m_next.append(m_t[last:last + 1, :])
        return tuple(m_next)

    m_fin = lax.fori_loop(0, n_chunks, chunk_body, tuple(m_init))
    for d, h in states:
        c_aug = caug_ref[d, h]
        cout_ref[d, h] = c_aug[:, :dh]
        nout_ref[d, h:h + 1, :] = c_aug[:, dh:].T[0:1, :]
    m_heads = [pltpu.roll(m_fin[d], LANES - gate_lanes(d, 0)[1], 1) for d in range(2)]
    mout_ref[...] = jnp.concatenate(m_heads + [jnp.zeros((COND_ROWS - 2, LANES), F32)], axis=0)

    for h in range(ML_HEADS):
        hl = slice(h * dh, (h + 1) * dh)
        h_sum = hdir_ref[0, :, hl] + hdir_ref[1, :, hl]
        out_ref[:, hl] = jax.nn.sigmoid(o_ref[:, hl]) * (_rms_scale(h_sum) * hn_ref[:, hl])


def _mlstm(proj, keys_t, gates, gate_b_row, hnorm_row, seq_len, n_seq, row0, n_stack=1, jb=0, prev=None,
           state=None):
    has_state = state is not None
    r0 = row0 // seq_len
    nb = TN_PROJ // ML_WIDTH
    n_chunks = seq_len // ML_CHUNK
    in_specs = [
        pl.BlockSpec((seq_len, ML_WIDTH), lambda b: (r0 + b, P_QM * nb)),
        pl.BlockSpec((n_chunks, ML_WIDTH, ML_CHUNK), lambda b: (r0 + b, 0, 0)),
        pl.BlockSpec((seq_len, ML_WIDTH), lambda b: (r0 + b, P_VM * nb)),
        pl.BlockSpec((seq_len, ML_WIDTH), lambda b: (r0 + b, P_OM * nb)),
        pl.BlockSpec((seq_len, LANES), lambda b: (r0 + b, 0)),
        pl.BlockSpec((1, LANES), lambda b: (0, 0)),
        pl.BlockSpec((1, ML_WIDTH), lambda b: (0, 0)),
    ]
    args = [proj, keys_t, proj, proj, gates, gate_b_row, hnorm_row]
    aliases = {}
    if has_state:
        st_c, st_n, st_m = state
        in_specs += [
            pl.BlockSpec((None, 2, ML_HEADS, ML_HEAD_DIM, ML_HEAD_DIM), lambda b: (b, 0, 0, 0, 0)),
            pl.BlockSpec((None, 2, ML_HEADS, ML_HEAD_DIM), lambda b: (b, 0, 0, 0)),
            pl.BlockSpec(memory_space=pltpu.SMEM),
        ]
        args += [st_c, st_n, st_m.reshape(-1)]
    elif prev is not None:
        aliases = {len(args) + k: 1 + k for k in range(3)}
        in_specs += [pl.BlockSpec(memory_space=pl.ANY)] * 3
        args += list(prev)
    c_bytes = 2 * ML_HEADS * ML_HEAD_DIM * ML_HEAD_DIM
    blocks = 4 * (5 * seq_len * ML_WIDTH + seq_len * LANES + 2 * c_bytes)
    scratch = 4 * (2 * seq_len * ML_WIDTH + 2 * c_bytes)
    return pl.pallas_call(
        functools.partial(_mlstm_kernel, n_chunks=n_chunks, has_state=has_state),
        grid=(n_seq,),
        in_specs=in_specs,
        out_specs=[
            pl.BlockSpec((seq_len, ML_WIDTH), lambda b: (b, 0)),
            pl.BlockSpec((None, None, 2, ML_HEADS, ML_HEAD_DIM, ML_HEAD_DIM), lambda b: (b, jb, 0, 0, 0, 0)),
            pl.BlockSpec((None, None, 2, ML_HEADS, ML_HEAD_DIM), lambda b: (b, jb, 0, 0, 0)),
            pl.BlockSpec((None, None, COND_ROWS, LANES), lambda b: (b, jb, 0, 0)),
        ],
        out_shape=[
            jax.ShapeDtypeStruct((n_seq * seq_len, ML_WIDTH), F32),
            jax.ShapeDtypeStruct((n_seq, n_stack, 2, ML_HEADS, ML_HEAD_DIM, ML_HEAD_DIM), F32),
            jax.ShapeDtypeStruct((n_seq, n_stack, 2, ML_HEADS, ML_HEAD_DIM), F32),
            jax.ShapeDtypeStruct((n_seq, n_stack, COND_ROWS, LANES), F32),
        ],
        scratch_shapes=[pltpu.VMEM((2, seq_len, ML_WIDTH), F32),
                        pltpu.VMEM((2, ML_HEADS, ML_HEAD_DIM, 2 * ML_HEAD_DIM), F32)],
        input_output_aliases=aliases,
        compiler_params=_params(("parallel",), blocks, scratch),
        name=f"mlstm_{'latent' if has_state else 'context'}_{jb}",
    )(*args)


def _outproj_kernel(x_ref, mod_ref, g_ref, ap_ref, mp_ref, as_ref, ms_ref, wa_ref, wm_ref, o_ref):
    i = pl.program_id(0)

    def finish(a_ref, m_ref):
        out = _dot(a_ref[...], wa_ref[...]) + _dot(m_ref[...], wm_ref[...])
        for rows, r in _unit_rows(i, TM_OUT):
            o_ref[rows, :] = _gated_residual(x_ref[rows, :], out[rows, :], g_ref, mod_ref, r, 1, 1.0)

    is_prompt = i < N_PROMPT // TM_OUT
    pl.when(is_prompt)(functools.partial(finish, ap_ref, mp_ref))
    pl.when(jnp.logical_not(is_prompt))(functools.partial(finish, as_ref, ms_ref))


def _outproj(y, mod_l, g_l, a_p, m_p, a_s, m_s, ab_w_out, jb):
    tm = TM_OUT
    n_p = N_PROMPT // tm
    n_s = N_SAMPLE // tm
    p_spec = pl.BlockSpec((tm, NA_WIDTH), lambda i: (jnp.minimum(i, n_p - 1), 0))
    s_spec = pl.BlockSpec((tm, NA_WIDTH), lambda i: (jnp.clip(i - n_p, 0, n_s - 1), 0))
    blocks = 4 * (2 * tm * D_MODEL + 4 * tm * NA_WIDTH + D_MODEL * D_MODEL)
    return pl.pallas_call(
        _outproj_kernel,
        grid=(N_TOK // tm,),
        in_specs=[
            pl.BlockSpec((tm, D_MODEL), lambda i: (i, 0)),
            pl.BlockSpec(_MOD_SPEC_SHAPE, lambda i: (0, 0, 0)),
            pl.BlockSpec((6, D_MODEL), lambda i: (0, 0)),
            p_spec, p_spec, s_spec, s_spec,
            pl.BlockSpec((None, NA_WIDTH, D_MODEL), lambda i: (jb, 0, 0)),
            pl.BlockSpec((None, ML_WIDTH, D_MODEL), lambda i: (jb, 1, 0)),
        ],
        out_specs=pl.BlockSpec((tm, D_MODEL), lambda i: (i, 0)),
        out_shape=jax.ShapeDtypeStruct((N_TOK, D_MODEL), F32),
        compiler_params=_params(("parallel",), blocks),
        name=f"outproj_{jb}",
    )(y, mod_l, g_l, a_p, m_p, a_s, m_s, ab_w_out, ab_w_out)


def kernel(x_prompt, x_sample, cache_k, cache_v, state_C, state_n, state_m, c, c_ctx, ada_w, ada_b, norm_g,
           ffn_w13, ffn_w2, ab_w_in, ab_w_out, na_rpb, ml_gate_b, ml_hnorm, sc_w_in, sc_conv_w, sc_conv_b,
           sc_w_out):
    assert NA_WIDTH == ML_WIDTH == TN_PROJ and AB_MAIN % N_GATES == 0
    assert N_PROMPT % TM == 0 and N_SAMPLE % TM == 0 and TM % COND_UNIT == 0
    cond = jnp.concatenate([c_ctx[None, :], c, jnp.zeros((COND_ROWS - 1 - DEC_BATCH, D_MODEL), F32)], axis=0)
    mod = _modulation(cond, ada_w, ada_b)
    bias_pairs = _na_bias(na_rpb)
    y = jnp.concatenate([x_prompt.reshape(N_PROMPT, D_MODEL), x_sample.reshape(N_SAMPLE, D_MODEL)], axis=0)
    w_in_t = jnp.swapaxes(ab_w_in, 1, 2)
    cache_k_t = jnp.swapaxes(cache_k, 3, 4)
    cache_v_t = jnp.swapaxes(cache_v, 3, 4)

    kv_new = None
    st_new = None
    for l in range(DEPTH):
        mod_l = mod[l]
        g_l = norm_g[l].reshape(6, D_MODEL)
        y = _ffn_sublayer(y, mod_l, g_l, ffn_w13, ffn_w2, l, 0)
        j = l // 2
        if l % 2 == 0:
            proj, keys_t, gates = _inproj(y, mod_l, g_l, w_in_t, j)
            gate_b_row = jnp.pad(ml_gate_b[j].reshape(1, N_GATES), ((0, 0), (0, LANES - N_GATES)))
            hnorm_row = ml_hnorm[j].reshape(1, ML_WIDTH)
            a_p, *kv_new = _ctx_attention(proj, j, kv_new)
            m_p, *st_new = _mlstm(proj, keys_t, gates, gate_b_row, hnorm_row, SEQ, BATCH, 0,
                                  n_stack=N_AB, jb=j, prev=st_new)
            a_s = _na_attention(proj, cache_k_t, cache_v_t, bias_pairs, j)
            m_s, _, _, _ = _mlstm(proj, keys_t, gates, gate_b_row, hnorm_row, DEC_SEQ, DEC_BATCH, N_PROMPT,
                                  state=(state_C[:, j], state_n[:, j], state_m[:, j]))
            y = _outproj(y, mod_l, g_l, a_p, m_p, a_s, m_s, ab_w_out, j)
        else:
            y = _sconv_sublayer(y, mod_l, g_l, sc_w_in, sc_conv_w, sc_conv_b, sc_w_out, j)
        y = _ffn_sublayer(y, mod_l, g_l, ffn_w13, ffn_w2, l, 1)

    y_p = y[:N_PROMPT].reshape(BATCH, SEQ, D_MODEL)
    y_s = y[N_PROMPT:].reshape(DEC_BATCH, DEC_SEQ, D_MODEL)
    new_c, new_n, m_rows = st_new
    new_m = m_rows[:, :, :2, :ML_HEADS]
    return (y_p, y_s, jnp.swapaxes(kv_new[0], 3, 4), jnp.swapaxes(kv_new[1], 3, 4), new_c, new_n, new_m)
```

```python
import functools

import jax
import jax.numpy as jnp
from jax import lax
from jax.experimental import pallas as pl
from jax.experimental.pallas import tpu as pltpu

D_MODEL = 1024
BATCH = 16
SEQ = 256
DEPTH = 4
DEC_BATCH = 2
DEC_SEQ = 1024
PAST_LEN = 256
GRID_W = 64
NA_HEADS = 8
NA_HEAD_DIM = 64
NA_WIN_H = 8
NA_WIN_W = 16
ML_HEADS = 4
ML_HEAD_DIM = 128
ML_CHUNK = 128
CONV_WIDTH = 3
D_FF = 2816
N_MOD = 9
EPS = 1e-6
N_AB = (DEPTH + 1) // 2
N_C = DEPTH // 2
NA_WIDTH = NA_HEADS * NA_HEAD_DIM
ML_WIDTH = ML_HEADS * ML_HEAD_DIM
AB_MAIN = 3 * NA_WIDTH + 4 * ML_WIDTH
N_GATES = 4 * ML_HEADS
AB_IN = AB_MAIN + N_GATES
SC_WIDTH = D_MODEL

N_PROMPT = BATCH * SEQ
N_SAMPLE = DEC_BATCH * DEC_SEQ
N_TOK = N_PROMPT + N_SAMPLE
GRID_ROWS = DEC_SEQ // GRID_W
NA_KH = min(NA_WIN_H, GRID_ROWS)
N_DR = 2 * NA_WIN_H - 1
N_DC = 2 * NA_WIN_W - 1
N_PAIR = N_DR - 1

LANES = 128
COND_ROWS = 8
VMEM_CAP = 60 * 1024 * 1024
TEMP_BYTES = 16 * 1024 * 1024

COND_UNIT = DEC_SEQ
TM = 2048
TM_OUT = 1024
TM_CONV = COND_UNIT
TF_CONV = 512
TF = 256
TN_PROJ = 512
TN_MOD = 1536
NA_QROWS = 4

PROJ_BLOCKS = (0, 1, 2, 3, 5, 6)
KM_BLOCK = 4
P_QA, P_KA, P_VA, P_QM, P_VM, P_OM = range(6)

F32 = jnp.float32
BF16 = jnp.bfloat16


def _vmem_limit(block_bytes, scratch_bytes):
    return int(min(VMEM_CAP, 2 * block_bytes + scratch_bytes + TEMP_BYTES))


def _params(semantics, block_bytes, scratch_bytes=0):
    return pltpu.CompilerParams(dimension_semantics=semantics,
                                vmem_limit_bytes=_vmem_limit(block_bytes, scratch_bytes))


def _unit_rows(i, tm=TM):
    first_latent = N_PROMPT // COND_UNIT
    units = tm // COND_UNIT
    return [(slice(u * COND_UNIT, (u + 1) * COND_UNIT), jnp.maximum(i * units + u - (first_latent - 1), 0))
            for u in range(units)]


def _rms_scale(x):
    return x * lax.rsqrt(jnp.mean(x * x, axis=-1, keepdims=True) + EPS)


def _modulate_in(x, g_ref, mod_ref, r, j):
    gain = g_ref[2 * j:2 * j + 1, :] * (1.0 + mod_ref[r, 3 * j + 1:3 * j + 2, :])
    return _rms_scale(x) * gain + mod_ref[r, 3 * j:3 * j + 1, :]


def _gated_residual(x, out, g_ref, mod_ref, r, j, weight):
    gain = (weight * mod_ref[r, 3 * j + 2:3 * j + 3, :]) * g_ref[2 * j + 1:2 * j + 2, :]
    return x + _rms_scale(out) * gain


_MOD_SPEC_SHAPE = (COND_ROWS, N_MOD, D_MODEL)


def _dot(a, b):
    return jnp.dot(a.astype(BF16), b.astype(BF16), preferred_element_type=F32)


def _dot_nt(a, b):
    return lax.dot_general(a.astype(BF16), b.astype(BF16), (((1,), (1,)), ((), ())),
                           preferred_element_type=F32)


def _mod_kernel(c_ref, w_ref, b_ref, o_ref):
    c = c_ref[...]
    s = c * jax.nn.sigmoid(c)
    o_ref[...] = _dot(s, w_ref[...]) + b_ref[...]


def _modulation(cond, ada_w, ada_b):
    n = N_MOD * D_MODEL
    out = pl.pallas_call(
        _mod_kernel,
        grid=(DEPTH, n // TN_MOD),
        in_specs=[
            pl.BlockSpec((COND_ROWS, D_MODEL), lambda l, t: (0, 0)),
            pl.BlockSpec((None, D_MODEL, TN_MOD), lambda l, t: (l, 0, t)),
            pl.BlockSpec((None, 1, TN_MOD), lambda l, t: (l, 0, t)),
        ],
        out_specs=pl.BlockSpec((None, COND_ROWS, TN_MOD), lambda l, t: (l, 0, t)),
        out_shape=jax.ShapeDtypeStruct((DEPTH, COND_ROWS, n), F32),
        compiler_params=_params(("parallel", "parallel"), 4 * D_MODEL * TN_MOD),
        name="modulation",
    )(cond, ada_w, ada_b.reshape(DEPTH, 1, n))
    return out.reshape(DEPTH, COND_ROWS, N_MOD, D_MODEL)


def _ffn_kernel(x_ref, mod_ref, g_ref, wa_ref, wg_ref, w2_ref, o_ref, h_ref, *, j):
    i = pl.program_id(0)
    k = pl.program_id(1)

    @pl.when(k == 0)
    def _():
        for rows, r in _unit_rows(i):
            h_ref[rows, :] = _modulate_in(x_ref[rows, :], g_ref, mod_ref, r, j).astype(BF16)
        o_ref[...] = jnp.zeros_like(o_ref)

    h = h_ref[...]
    a = _dot(h, wa_ref[...])
    g = _dot(h, wg_ref[...])
    u = (g * jax.nn.sigmoid(g)) * a
    o_ref[...] += _dot(u, w2_ref[...])

    @pl.when(k == pl.num_programs(1) - 1)
    def _():
        for rows, r in _unit_rows(i):
            o_ref[rows, :] = _gated_residual(x_ref[rows, :], o_ref[rows, :], g_ref, mod_ref, r, j, 0.5)


def _ffn_sublayer(y, mod_l, g_l, ffn_w13, ffn_w2, l, s):
    j = 2 * s
    nk = D_FF // TF
    blocks = 4 * (2 * TM * D_MODEL + 3 * D_MODEL * TF)
    scratch = 2 * TM * D_MODEL
    return pl.pallas_call(
        functools.partial(_ffn_kernel, j=j),
        grid=(N_TOK // TM, nk),
        in_specs=[
            pl.BlockSpec((TM, D_MODEL), lambda i, k: (i, 0)),
            pl.BlockSpec(_MOD_SPEC_SHAPE, lambda i, k: (0, 0, 0)),
            pl.BlockSpec((6, D_MODEL), lambda i, k: (0, 0)),
            pl.BlockSpec((None, None, D_MODEL, TF), lambda i, k: (l, s, 0, k)),
            pl.BlockSpec((None, None, D_MODEL, TF), lambda i, k: (l, s, 0, k + nk)),
            pl.BlockSpec((None, None, TF, D_MODEL), lambda i, k: (l, s, k, 0)),
        ],
        out_specs=pl.BlockSpec((TM, D_MODEL), lambda i, k: (i, 0)),
        out_shape=jax.ShapeDtypeStruct((N_TOK, D_MODEL), F32),
        scratch_shapes=[pltpu.VMEM((TM, D_MODEL), BF16)],
        compiler_params=_params(("parallel", "arbitrary"), blocks, scratch),
        name=f"ffn_l{l}_s{s}",
    )(y, mod_l, g_l, ffn_w13, ffn_w13, ffn_w2)


def _sconv_kernel(x_ref, mod_ref, g_ref, wb_ref, wc_ref, wx_ref, cw_ref, cb_ref, wo_ref, o_ref, h_ref):
    i = pl.program_id(0)
    k = pl.program_id(1)
    ((_, r),) = _unit_rows(i, TM_CONV)

    @pl.when(k == 0)
    def _():
        h_ref[...] = _modulate_in(x_ref[...], g_ref, mod_ref, r, 1).astype(BF16)
        o_ref[...] = jnp.zeros_like(o_ref)

    seq = jnp.where(i < N_PROMPT // TM_CONV, SEQ, DEC_SEQ)
    wb = wb_ref[...].astype(BF16)
    wc = wc_ref[...].astype(BF16)
    wx = wx_ref[...].astype(BF16)
    wo = wo_ref[...].astype(BF16)
    blk = SEQ
    n_blk = TM_CONV // blk
    row = lax.broadcasted_iota(jnp.int32, (blk, TF_CONV), 0)
    zero_row = jnp.zeros((1, TF_CONV), F32)
    gbs, us = [], []
    for bi in range(n_blk):
        h = h_ref[bi * blk:(bi + 1) * blk, :]
        gbs.append(_dot(h, wb))
        us.append(_dot(h, wc) * _dot(h, wx))
    for bi in range(n_blk):
        ub = us[bi]
        t = (row + bi * blk) & (seq - 1)
        above = us[bi - 1][blk - 1:blk, :] if bi > 0 else zero_row
        below = us[bi + 1][0:1, :] if bi < n_blk - 1 else zero_row
        u_prev = jnp.where(t == 0, 0.0, jnp.where(row == 0, above, pltpu.roll(ub, 1, 0)))
        u_next = jnp.where(t == seq - 1, 0.0, jnp.where(row == blk - 1, below, pltpu.roll(ub, blk - 1, 0)))
        y = cb_ref[...] + u_prev * cw_ref[0:1, :] + ub * cw_ref[1:2, :] + u_next * cw_ref[2:3, :]
        o_ref[bi * blk:(bi + 1) * blk, :] += _dot(gbs[bi] * y, wo)

    @pl.when(k == pl.num_programs(1) - 1)
    def _():
        o_ref[...] = _gated_residual(x_ref[...], o_ref[...], g_ref, mod_ref, r, 1, 1.0)


def _sconv_sublayer(y, mod_l, g_l, sc_w_in, sc_conv_w, sc_conv_b, sc_w_out, jc):
    tm, tf = TM_CONV, TF_CONV
    nk = SC_WIDTH // tf
    blocks = 4 * (2 * tm * D_MODEL + 4 * D_MODEL * tf)
    scratch = 2 * tm * D_MODEL
    return pl.pallas_call(
        _sconv_kernel,
        grid=(N_TOK // tm, nk),
        in_specs=[
            pl.BlockSpec((tm, D_MODEL), lambda i, k: (i, 0)),
            pl.BlockSpec(_MOD_SPEC_SHAPE, lambda i, k: (0, 0, 0)),
            pl.BlockSpec((6, D_MODEL), lambda i, k: (0, 0)),
            pl.BlockSpec((None, D_MODEL, tf), lambda i, k: (jc, 0, k)),
            pl.BlockSpec((None, D_MODEL, tf), lambda i, k: (jc, 0, k + nk)),
            pl.BlockSpec((None, D_MODEL, tf), lambda i, k: (jc, 0, k + 2 * nk)),
            pl.BlockSpec((None, CONV_WIDTH, tf), lambda i, k: (jc, 0, k)),
            pl.BlockSpec((None, 1, tf), lambda i, k: (jc, 0, k)),
            pl.BlockSpec((None, tf, D_MODEL), lambda i, k: (jc, k, 0)),
        ],
        out_specs=pl.BlockSpec((tm, D_MODEL), lambda i, k: (i, 0)),
        out_shape=jax.ShapeDtypeStruct((N_TOK, D_MODEL), F32),
        scratch_shapes=[pltpu.VMEM((tm, D_MODEL), BF16)],
        compiler_params=_params(("parallel", "arbitrary"), blocks, scratch),
        name=f"sconv_{jc}",
    )(y, mod_l, g_l, sc_w_in, sc_w_in, sc_w_in, sc_conv_w, sc_conv_b.reshape(N_C, 1, SC_WIDTH), sc_w_out)


def _inproj_kernel(x_ref, mod_ref, g_ref, w_ref, wk_ref, wg_ref, o_ref, okt_ref, og_ref, h_ref):
    i = pl.program_id(0)
    n = pl.program_id(1)
    n_main = len(PROJ_BLOCKS)

    @pl.when(n == 0)
    def _():
        for rows, r in _unit_rows(i):
            h_ref[rows, :] = _modulate_in(x_ref[rows, :], g_ref, mod_ref, r, 1).astype(BF16)

    @pl.when(n < n_main)
    def _():
        o_ref[...] = _dot_nt(h_ref[...], w_ref[...])

    @pl.when(n == n_main)
    def _():
        kt = _dot_nt(wk_ref[...], h_ref[...])
        for c in range(TM // ML_CHUNK):
            okt_ref[c] = kt[:, c * ML_CHUNK:(c + 1) * ML_CHUNK]
        w_gates = jnp.concatenate([wg_ref[...], jnp.zeros((LANES - N_GATES, D_MODEL), F32)], axis=0)
        og_ref[...] = _dot_nt(h_ref[...], w_gates)


def _inproj(y, mod_l, g_l, w_in_t, jb):
    n_main = len(PROJ_BLOCKS)
    assert PROJ_BLOCKS == tuple(b for b in range(n_main + 1) if b != KM_BLOCK)

    def w_block(i, n):
        m = jnp.minimum(n, n_main - 1)
        return (jb, m + (m >= KM_BLOCK).astype(jnp.int32), 0)

    cpt = TM // ML_CHUNK
    blocks = 4 * (TM * D_MODEL + 2 * TN_PROJ * D_MODEL + N_GATES * D_MODEL + TM * TN_PROJ
                  + ML_WIDTH * TM + TM * LANES)
    return pl.pallas_call(
        _inproj_kernel,
        grid=(N_TOK // TM, n_main + 1),
        in_specs=[
            pl.BlockSpec((TM, D_MODEL), lambda i, n: (i, 0)),
            pl.BlockSpec(_MOD_SPEC_SHAPE, lambda i, n: (0, 0, 0)),
            pl.BlockSpec((6, D_MODEL), lambda i, n: (0, 0)),
            pl.BlockSpec((None, TN_PROJ, D_MODEL), w_block),
            pl.BlockSpec((None, ML_WIDTH, D_MODEL), lambda i, n: (jb, KM_BLOCK, 0)),
            pl.BlockSpec((None, N_GATES, D_MODEL), lambda i, n: (jb, AB_MAIN // N_GATES, 0)),
        ],
        out_specs=[
            pl.BlockSpec((TM, TN_PROJ), lambda i, n: (i, jnp.minimum(n, n_main - 1))),
            pl.BlockSpec((cpt, ML_WIDTH, ML_CHUNK), lambda i, n: (i, 0, 0)),
            pl.BlockSpec((TM, LANES), lambda i, n: (i, 0)),
        ],
        out_shape=[jax.ShapeDtypeStruct((N_TOK, n_main * TN_PROJ), F32),
                   jax.ShapeDtypeStruct((N_TOK // ML_CHUNK, ML_WIDTH, ML_CHUNK), F32),
                   jax.ShapeDtypeStruct((N_TOK, LANES), F32)],
        scratch_shapes=[pltpu.VMEM((TM, D_MODEL), BF16)],
        compiler_params=_params(("parallel", "arbitrary"), blocks, 2 * TM * D_MODEL),
        name=f"inproj_{jb}",
    )(y, mod_l, g_l, w_in_t, w_in_t, w_in_t)


def _stack_entry(ref, jb, fills_stack):
    if not fills_stack:
        return ref
    for j in range(ref.shape[0]):
        if j != jb:
            ref[j] = jnp.zeros(ref.shape[1:], ref.dtype)
    return ref.at[jb]


def _stack_spec(entry_shape, jb, fills_stack, n_stack=N_AB):
    zeros = (0,) * len(entry_shape)
    if fills_stack:
        return pl.BlockSpec((None, n_stack) + entry_shape, lambda b: (b, 0) + zeros)
    return pl.BlockSpec((None, None) + entry_shape, lambda b: (b, jb) + zeros)


def _ctx_attn_kernel(q_ref, k_ref, v_ref, *rest, jb, fills_stack):
    o_ref = rest[-3]
    kc_ref = _stack_entry(rest[-2], jb, fills_stack)
    vc_ref = _stack_entry(rest[-1], jb, fills_stack)
    scale = NA_HEAD_DIM ** -0.5
    k_t = k_ref[...].T
    v_t = v_ref[...].T
    for h in range(NA_HEADS):
        sl = slice(h * NA_HEAD_DIM, (h + 1) * NA_HEAD_DIM)
        kc_ref[h] = k_t[sl, :]
        vc_ref[h] = v_t[sl, :]
        s = _dot(q_ref[:, sl], k_t[sl, :]) * scale
        p = jnp.exp(s - jnp.max(s, axis=-1, keepdims=True))
        p = p * (1.0 / jnp.sum(p, axis=-1, keepdims=True))
        o_ref[:, sl] = _dot(p, v_ref[:, sl])


def _ctx_attention(proj, jb, prev):
    cache = jax.ShapeDtypeStruct((BATCH, N_AB, NA_HEADS, NA_HEAD_DIM, SEQ), F32)
    cache_spec = _stack_spec((NA_HEADS, NA_HEAD_DIM, SEQ), jb, prev is None)
    in_specs = [pl.BlockSpec((SEQ, NA_WIDTH), lambda b, c=c: (b, c)) for c in (P_QA, P_KA, P_VA)]
    args = [proj, proj, proj]
    aliases = {}
    if prev is not None:
        in_specs += [pl.BlockSpec(memory_space=pl.ANY)] * 2
        args += list(prev)
        aliases = {3: 1, 4: 2}
    blocks = 4 * (4 * SEQ * NA_WIDTH + 2 * N_AB * NA_WIDTH * SEQ)
    return pl.pallas_call(
        functools.partial(_ctx_attn_kernel, jb=jb, fills_stack=prev is None),
        grid=(BATCH,),
        in_specs=in_specs,
        out_specs=[pl.BlockSpec((SEQ, NA_WIDTH), lambda b: (b, 0)), cache_spec, cache_spec],
        out_shape=[jax.ShapeDtypeStruct((N_PROMPT, NA_WIDTH), F32), cache, cache],
        input_output_aliases=aliases,
        compiler_params=_params(("parallel",), blocks),
        name=f"ctx_attention_{jb}",
    )(*args)


def _na_bias_kernel(rpb_ref, p_ref):
    base = (pl.program_id(0) * NA_HEADS + pl.program_id(1)) * (N_DR * N_DC)
    shape = (GRID_W, LANES)
    lane = lax.broadcasted_iota(jnp.int32, shape, 1)
    qc = lax.broadcasted_iota(jnp.int32, shape, 0)
    kc = lane & (GRID_W - 1)
    upper = lane >= GRID_W
    cs = jnp.clip(qc - NA_WIN_W // 2, 0, GRID_W - NA_WIN_W)
    col_in = (kc >= cs) & (kc < cs + NA_WIN_W)
    dc = jnp.clip(kc - qc, -(NA_WIN_W - 1), NA_WIN_W - 1) + (NA_WIN_W - 1)
    for pair in range(N_PAIR):
        acc = jnp.zeros(shape, F32)
        for d in range(N_DC):
            lo = rpb_ref[base + pair * N_DC + d]
            hi = rpb_ref[base + (pair + 1) * N_DC + d]
            acc = jnp.where(dc == d, jnp.where(upper, hi, lo), acc)
        p_ref[pair] = jnp.where(col_in, acc, -jnp.inf)


def _na_bias(na_rpb):
    return pl.pallas_call(
        _na_bias_kernel,
        grid=(N_AB, NA_HEADS),
        in_specs=[pl.BlockSpec(memory_space=pltpu.SMEM)],
        out_specs=pl.BlockSpec((None, None, N_PAIR, GRID_W, LANES), lambda j, h: (j, h, 0, 0, 0)),
        out_shape=jax.ShapeDtypeStruct((N_AB, NA_HEADS, N_PAIR, GRID_W, LANES), F32),
        compiler_params=_params(("parallel", "parallel"), 4 * N_PAIR * GRID_W * LANES),
        name="na_bias",
    )(na_rpb.reshape(-1))


def _na_window_start(r):
    return min(max(r - NA_KH // 2, 0), GRID_ROWS - NA_KH)


def _na_bias_block(p_ref, hh, qr, kr):
    rs = _na_window_start(qr)
    lo_ok = rs <= kr < rs + NA_KH
    hi_ok = rs <= kr + 1 < rs + NA_KH
    if not (lo_ok or hi_ok):
        return jnp.full((GRID_W, LANES), -jnp.inf, F32)
    dr = kr - qr + (NA_WIN_H - 1)
    assert 0 <= dr < N_PAIR
    blk = p_ref[hh, dr]
    if lo_ok and hi_ok:
        return blk
    lane = lax.broadcasted_iota(jnp.int32, (GRID_W, LANES), 1)
    keep = (lane < GRID_W) if lo_ok else (lane >= GRID_W)
    return jnp.where(keep, blk, -jnp.inf)


def _na_kernel(q_ref, k_ref, v_ref, kc_ref, vc_ref, p_ref, o_ref):
    scale = NA_HEAD_DIM ** -0.5
    for hh in range(LANES // NA_HEAD_DIM):
        sl = slice(hh * NA_HEAD_DIM, (hh + 1) * NA_HEAD_DIM)
        k_ctx_t = kc_ref[hh]
        v_ctx_t = vc_ref[hh]
        for q0 in range(0, GRID_ROWS, NA_QROWS):
            k_lo = _na_window_start(q0) // 2 * 2
            k_hi = -(-(_na_window_start(q0 + NA_QROWS - 1) + NA_KH) // 2) * 2
            q_rows = slice(q0 * GRID_W, (q0 + NA_QROWS) * GRID_W)
            k_rows = slice(k_lo * GRID_W, k_hi * GRID_W)
            bias = jnp.concatenate(
                [jnp.concatenate([_na_bias_block(p_ref, hh, qr, kr) for kr in range(k_lo, k_hi, 2)], axis=1)
                 for qr in range(q0, q0 + NA_QROWS)], axis=0)
            q = q_ref[q_rows, sl]
            s_loc = _dot_nt(q, k_ref[k_rows, sl]) * scale + bias
            s_ctx = _dot(q, k_ctx_t) * scale
            m = jnp.maximum(jnp.max(s_loc, axis=-1, keepdims=True), jnp.max(s_ctx, axis=-1, keepdims=True))
            p_loc = jnp.exp(s_loc - m)
            p_ctx = jnp.exp(s_ctx - m)
            denom = jnp.sum(p_loc, axis=-1, keepdims=True) + jnp.sum(p_ctx, axis=-1, keepdims=True)
            o_ref[q_rows, sl] = (_dot(p_loc, v_ref[k_rows, sl]) + _dot_nt(p_ctx, v_ctx_t)) / denom


def _na_attention(proj, cache_k_t, cache_v_t, bias_pairs, jb):
    hp = LANES // NA_HEAD_DIM
    s0 = N_PROMPT // DEC_SEQ
    nb = NA_WIDTH // LANES
    ctx_spec = pl.BlockSpec((None, None, hp, NA_HEAD_DIM, PAST_LEN), lambda b, c: (b, jb, c, 0, 0))
    blocks = 4 * (4 * DEC_SEQ * LANES + 2 * hp * NA_HEAD_DIM * PAST_LEN + hp * N_PAIR * GRID_W * LANES)
    return pl.pallas_call(
        _na_kernel,
        grid=(DEC_BATCH, nb),
        in_specs=[
            pl.BlockSpec((DEC_SEQ, LANES), lambda b, c: (s0 + b, P_QA * nb + c)),
            pl.BlockSpec((DEC_SEQ, LANES), lambda b, c: (s0 + b, P_KA * nb + c)),
            pl.BlockSpec((DEC_SEQ, LANES), lambda b, c: (s0 + b, P_VA * nb + c)),
            ctx_spec, ctx_spec,
            pl.BlockSpec((None, hp, N_PAIR, GRID_W, LANES), lambda b, c: (jb, c, 0, 0, 0)),
        ],
        out_specs=pl.BlockSpec((DEC_SEQ, LANES), lambda b, c: (b, c)),
        out_shape=jax.ShapeDtypeStruct((N_SAMPLE, NA_WIDTH), F32),
        compiler_params=_params(("parallel", "parallel"), blocks),
        name=f"na_attention_{jb}",
    )(proj, proj, proj, cache_k_t, cache_v_t, bias_pairs)


def _running_max_rows(x, reverse):
    n = x.shape[0]
    ridx = lax.broadcasted_iota(jnp.int32, x.shape, 0)
    sh = 1
    while sh < n:
        if reverse:
            x = jnp.maximum(x, jnp.where(ridx < n - sh, pltpu.roll(x, n - sh, 0), -jnp.inf))
        else:
            x = jnp.maximum(x, jnp.where(ridx >= sh, pltpu.roll(x, sh, 0), -jnp.inf))
        sh *= 2
    return x


def _mlstm_kernel(*refs, n_chunks, has_state, jb, fills_stack):
    if has_state:
        (q_ref, kt_ref, v_ref, o_ref, gt_ref, gb_ref, hn_ref, c0_ref, n0_ref, m0_ref,
         out_ref, cout_ref, nout_ref, mout_ref, hdir_ref, caug_ref) = refs
    else:
        (q_ref, kt_ref, v_ref, o_ref, gt_ref, gb_ref, hn_ref, *_unused,
         out_ref, cout_ref, nout_ref, mout_ref, hdir_ref, caug_ref) = refs
    cout_ref = _stack_entry(cout_ref, jb, fills_stack)
    nout_ref = _stack_entry(nout_ref, jb, fills_stack)
    mout_ref = _stack_entry(mout_ref, jb, fills_stack)
    T = ML_CHUNK
    dh = ML_HEAD_DIM
    b = pl.program_id(0)
    row = lax.broadcasted_iota(jnp.int32, (T, T), 0)
    col = lax.broadcasted_iota(jnp.int32, (T, T), 1)
    lane = lax.broadcasted_iota(jnp.int32, (T, LANES), 1)
    lane_row = lax.broadcasted_iota(jnp.int32, (1, LANES), 1)
    is_forget = ((lane >= ML_HEADS) & (lane < 2 * ML_HEADS)) | ((lane >= 3 * ML_HEADS) & (lane < 4 * ML_HEADS))
    ones = jnp.ones((T, dh), F32)
    states = [(d, h) for d in range(2) for h in range(ML_HEADS)]

    def gate_lanes(d, h):
        gi = 2 * d * ML_HEADS + h
        return gi, gi + ML_HEADS

    m_init = []
    for d in range(2):
        m_row = jnp.zeros((1, LANES), F32)
        for h in range(ML_HEADS):
            if has_state:
                n_bcast = jnp.broadcast_to(n0_ref[d, h:h + 1, :], (dh, dh)).T
                caug_ref[d, h] = jnp.concatenate([c0_ref[d, h], n_bcast], axis=1)
                m_row = jnp.where(lane_row == gate_lanes(d, h)[1], m0_ref[(b * 2 + d) * ML_HEADS + h], m_row)
            else:
                caug_ref[d, h] = jnp.zeros((dh, 2 * dh), F32)
        m_init.append(m_row)

    def chunk_body(ci, m_rows):
        m_next = []
        for d in range(2):
            causal = (col <= row) if d == 0 else (col >= row)
            last = T - 1 if d == 0 else 0
            c = ci if d == 0 else n_chunks - 1 - ci
            rows = pl.ds(pl.multiple_of(c * T, T), T)
            mine = (lane >= (2 * d + 1) * ML_HEADS) & (lane < (2 * d + 2) * ML_HEADS)
            g = gt_ref[rows, :] + gb_ref[...]
            log_sig = jnp.minimum(g, 0.0) - jnp.log1p(jnp.exp(-jnp.abs(g)))
            g = jnp.where(is_forget, log_sig, g)
            cum = jnp.dot(causal.astype(F32), g, precision=lax.Precision.HIGHEST, preferred_element_type=F32)
            i_al = pltpu.roll(g, ML_HEADS, 1)
            cvec = jnp.where(mine, i_al - cum, 0.0)
            cum = jnp.where(mine, cum, 0.0)
            m_run = jnp.maximum(m_rows[d], _running_max_rows(cvec, reverse=(d == 1)))
            m_t = cum + m_run
            w_inter = jnp.exp(m_rows[d] - m_run)
            e_negm = jnp.exp(-m_t)
            m_run_last = m_run[last:last + 1, :]
            wi_last = w_inter[last:last + 1, :]
            w_last = jnp.exp(cvec - m_run_last)
            packed_t = jnp.where(mine, cvec, pltpu.roll(w_last, LANES - ML_HEADS, 1)).T
            for h in range(ML_HEADS):
                gi, gf = gate_lanes(d, h)
                hl = slice(h * dh, (h + 1) * dh)
                m_run_b = jnp.broadcast_to(m_run[:, gf:gf + 1], (T, T))
                e_negm_b = jnp.broadcast_to(e_negm[:, gf:gf + 1], (T, dh))
                w_d = jnp.exp(jnp.where(causal, packed_t[gf:gf + 1, :] - m_run_b, -jnp.inf))
                w_inter_b = jnp.exp(m_rows[d][:, gf:gf + 1] - m_run_b)
                qh = q_ref[rows, hl]
                kh_t = kt_ref[c, hl, :] * (dh ** -0.5)
                v_aug = jnp.concatenate([v_ref[rows, hl], ones], axis=1)
                s = _dot(qh, kh_t) * w_d
                c_aug = caug_ref[d, h]
                nd = jnp.concatenate([w_inter_b, w_inter_b], axis=1) * _dot(qh, c_aug) + _dot(s, v_aug)
                hdir_ref[d, rows, hl] = nd[:, :dh] / jnp.maximum(jnp.abs(nd[:, dh:]), e_negm_b)
                kw_t = kh_t * packed_t[gi:gi + 1, :]
                caug_ref[d, h] = wi_last[:, gf:gf + 1] * c_aug + _dot(kw_t, v_aug)
            m_next.append(m_t[last:last + 1, :])
        return tuple(m_next)

    m_fin = lax.fori_loop(0, n_chunks, chunk_body, tuple(m_init), unroll=2)
    for d, h in states:
        c_aug = caug_ref[d, h]
        cout_ref[d, h] = c_aug[:, :dh]
        nout_ref[d, h:h + 1, :] = c_aug[:, dh:].T[0:1, :]
    m_heads = [pltpu.roll(m_fin[d], LANES - gate_lanes(d, 0)[1], 1) for d in range(2)]
    mout_ref[...] = jnp.concatenate(m_heads + [jnp.zeros((COND_ROWS - 2, LANES), F32)], axis=0)

    for h in range(ML_HEADS):
        hl = slice(h * dh, (h + 1) * dh)
        h_sum = hdir_ref[0, :, hl] + hdir_ref[1, :, hl]
        out_ref[:, hl] = jax.nn.sigmoid(o_ref[:, hl]) * (_rms_scale(h_sum) * hn_ref[:, hl])


def _mlstm(proj, keys_t, gates, gate_b_row, hnorm_row, seq_len, n_seq, row0, n_stack=1, jb=0, prev=None,
           state=None):
    has_state = state is not None
    r0 = row0 // seq_len
    nb = TN_PROJ // ML_WIDTH
    n_chunks = seq_len // ML_CHUNK
    in_specs = [
        pl.BlockSpec((seq_len, ML_WIDTH), lambda b: (r0 + b, P_QM * nb)),
        pl.BlockSpec((n_chunks, ML_WIDTH, ML_CHUNK), lambda b: (r0 + b, 0, 0)),
        pl.BlockSpec((seq_len, ML_WIDTH), lambda b: (r0 + b, P_VM * nb)),
        pl.BlockSpec((seq_len, ML_WIDTH), lambda b: (r0 + b, P_OM * nb)),
        pl.BlockSpec((seq_len, LANES), lambda b: (r0 + b, 0)),
        pl.BlockSpec((1, LANES), lambda b: (0, 0)),
        pl.BlockSpec((1, ML_WIDTH), lambda b: (0, 0)),
    ]
    args = [proj, keys_t, proj, proj, gates, gate_b_row, hnorm_row]
    aliases = {}
    if has_state:
        st_c, st_n, st_m = state
        in_specs += [
            pl.BlockSpec((None, 2, ML_HEADS, ML_HEAD_DIM, ML_HEAD_DIM), lambda b: (b, 0, 0, 0, 0)),
            pl.BlockSpec((None, 2, ML_HEADS, ML_HEAD_DIM), lambda b: (b, 0, 0, 0)),
            pl.BlockSpec(memory_space=pltpu.SMEM),
        ]
        args += [st_c, st_n, st_m.reshape(-1)]
    elif prev is not None:
        aliases = {len(args) + k: 1 + k for k in range(3)}
        in_specs += [pl.BlockSpec(memory_space=pl.ANY)] * 3
        args += list(prev)
    fills_stack = prev is None
    c_bytes = 2 * ML_HEADS * ML_HEAD_DIM * ML_HEAD_DIM
    blocks = 4 * (5 * seq_len * ML_WIDTH + seq_len * LANES + (1 + n_stack) * c_bytes)
    scratch = 4 * (2 * seq_len * ML_WIDTH + 2 * c_bytes)
    return pl.pallas_call(
        functools.partial(_mlstm_kernel, n_chunks=n_chunks, has_state=has_state, jb=jb, fills_stack=fills_stack),
        grid=(n_seq,),
        in_specs=in_specs,
        out_specs=[
            pl.BlockSpec((seq_len, ML_WIDTH), lambda b: (b, 0)),
            _stack_spec((2, ML_HEADS, ML_HEAD_DIM, ML_HEAD_DIM), jb, fills_stack, n_stack),
            _stack_spec((2, ML_HEADS, ML_HEAD_DIM), jb, fills_stack, n_stack),
            _stack_spec((COND_ROWS, LANES), jb, fills_stack, n_stack),
        ],
        out_shape=[
            jax.ShapeDtypeStruct((n_seq * seq_len, ML_WIDTH), F32),
            jax.ShapeDtypeStruct((n_seq, n_stack, 2, ML_HEADS, ML_HEAD_DIM, ML_HEAD_DIM), F32),
            jax.ShapeDtypeStruct((n_seq, n_stack, 2, ML_HEADS, ML_HEAD_DIM), F32),
            jax.ShapeDtypeStruct((n_seq, n_stack, COND_ROWS, LANES), F32),
        ],
        scratch_shapes=[pltpu.VMEM((2, seq_len, ML_WIDTH), F32),
                        pltpu.VMEM((2, ML_HEADS, ML_HEAD_DIM, 2 * ML_HEAD_DIM), F32)],
        input_output_aliases=aliases,
        compiler_params=_params(("parallel",), blocks, scratch),
        name=f"mlstm_{'latent' if has_state else 'context'}_{jb}",
    )(*args)


def _outproj_kernel(x_ref, mod_ref, g_ref, ap_ref, mp_ref, as_ref, ms_ref, wa_ref, wm_ref, o_ref):
    i = pl.program_id(0)

    def finish(a_ref, m_ref):
        out = _dot(a_ref[...], wa_ref[...]) + _dot(m_ref[...], wm_ref[...])
        for rows, r in _unit_rows(i, TM_OUT):
            o_ref[rows, :] = _gated_residual(x_ref[rows, :], out[rows, :], g_ref, mod_ref, r, 1, 1.0)

    is_prompt = i < N_PROMPT // TM_OUT
    pl.when(is_prompt)(functools.partial(finish, ap_ref, mp_ref))
    pl.when(jnp.logical_not(is_prompt))(functools.partial(finish, as_ref, ms_ref))


def _outproj(y, mod_l, g_l, a_p, m_p, a_s, m_s, ab_w_out, jb):
    tm = TM_OUT
    n_p = N_PROMPT // tm
    n_s = N_SAMPLE // tm
    p_spec = pl.BlockSpec((tm, NA_WIDTH), lambda i: (jnp.minimum(i, n_p - 1), 0))
    s_spec = pl.BlockSpec((tm, NA_WIDTH), lambda i: (jnp.clip(i - n_p, 0, n_s - 1), 0))
    blocks = 4 * (2 * tm * D_MODEL + 4 * tm * NA_WIDTH + D_MODEL * D_MODEL)
    return pl.pallas_call(
        _outproj_kernel,
        grid=(N_TOK // tm,),
        in_specs=[
            pl.BlockSpec((tm, D_MODEL), lambda i: (i, 0)),
            pl.BlockSpec(_MOD_SPEC_SHAPE, lambda i: (0, 0, 0)),
            pl.BlockSpec((6, D_MODEL), lambda i: (0, 0)),
            p_spec, p_spec, s_spec, s_spec,
            pl.BlockSpec((None, NA_WIDTH, D_MODEL), lambda i: (jb, 0, 0)),
            pl.BlockSpec((None, ML_WIDTH, D_MODEL), lambda i: (jb, 1, 0)),
        ],
        out_specs=pl.BlockSpec((tm, D_MODEL), lambda i: (i, 0)),
        out_shape=jax.ShapeDtypeStruct((N_TOK, D_MODEL), F32),
        compiler_params=_params(("parallel",), blocks),
        name=f"outproj_{jb}",
    )(y, mod_l, g_l, a_p, m_p, a_s, m_s, ab_w_out, ab_w_out)


def kernel(x_prompt, x_sample, cache_k, cache_v, state_C, state_n, state_m, c, c_ctx, ada_w, ada_b, norm_g,
           ffn_w13, ffn_w2, ab_w_in, ab_w_out, na_rpb, ml_gate_b, ml_hnorm, sc_w_in, sc_conv_w, sc_conv_b,
           sc_w_out):
    assert NA_WIDTH == ML_WIDTH == TN_PROJ and AB_MAIN % N_GATES == 0
    assert N_PROMPT % TM == 0 and N_SAMPLE % TM == 0 and TM % COND_UNIT == 0
    cond = jnp.concatenate([c_ctx[None, :], c, jnp.zeros((COND_ROWS - 1 - DEC_BATCH, D_MODEL), F32)], axis=0)
    mod = _modulation(cond, ada_w, ada_b)
    bias_pairs = _na_bias(na_rpb)
    y = jnp.concatenate([x_prompt.reshape(N_PROMPT, D_MODEL), x_sample.reshape(N_SAMPLE, D_MODEL)], axis=0)
    w_in_t = jnp.swapaxes(ab_w_in, 1, 2)
    cache_k_t = jnp.swapaxes(cache_k, 3, 4)
    cache_v_t = jnp.swapaxes(cache_v, 3, 4)

    kv_new = None
    st_new = None
    for l in range(DEPTH):
        mod_l = mod[l]
        g_l = norm_g[l].reshape(6, D_MODEL)
        y = _ffn_sublayer(y, mod_l, g_l, ffn_w13, ffn_w2, l, 0)
        j = l // 2
        if l % 2 == 0:
            proj, keys_t, gates = _inproj(y, mod_l, g_l, w_in_t, j)
            gate_b_row = jnp.pad(ml_gate_b[j].reshape(1, N_GATES), ((0, 0), (0, LANES - N_GATES)))
            hnorm_row = ml_hnorm[j].reshape(1, ML_WIDTH)
            a_p, *kv_new = _ctx_attention(proj, j, kv_new)
            m_p, *st_new = _mlstm(proj, keys_t, gates, gate_b_row, hnorm_row, SEQ, BATCH, 0,
                                  n_stack=N_AB, jb=j, prev=st_new)
            a_s = _na_attention(proj, cache_k_t, cache_v_t, bias_pairs, j)
            m_s, _, _, _ = _mlstm(proj, keys_t, gates, gate_b_row, hnorm_row, DEC_SEQ, DEC_BATCH, N_PROMPT,
                                  state=(state_C[:, j], state_n[:, j], state_m[:, j]))
            y = _outproj(y, mod_l, g_l, a_p, m_p, a_s, m_s, ab_w_out, j)
        else:
            y = _sconv_sublayer(y, mod_l, g_l, sc_w_in, sc_conv_w, sc_conv_b, sc_w_out, j)
        y = _ffn_sublayer(y, mod_l, g_l, ffn_w13, ffn_w2, l, 1)

    y_p = y[:N_PROMPT].reshape(BATCH, SEQ, D_MODEL)
    y_s = y[N_PROMPT:].reshape(DEC_BATCH, DEC_SEQ, D_MODEL)
    new_c, new_n, m_rows = st_new
    new_m = m_rows[:, :, :2, :ML_HEADS]
    return (y_p, y_s, jnp.swapaxes(kv_new[0], 3, 4), jnp.swapaxes(kv_new[1], 3, 4), new_c, new_n, new_m)
```

```python
import functools

import jax
import jax.numpy as jnp
from jax import lax
from jax.experimental import pallas as pl
from jax.experimental.pallas import tpu as pltpu

D_MODEL = 1024
BATCH = 16
SEQ = 256
DEPTH = 4
DEC_BATCH = 2
DEC_SEQ = 1024
PAST_LEN = 256
GRID_W = 64
NA_HEADS = 8
NA_HEAD_DIM = 64
NA_WIN_H = 8
NA_WIN_W = 16
ML_HEADS = 4
ML_HEAD_DIM = 128
ML_CHUNK = 128
CONV_WIDTH = 3
D_FF = 2816
N_MOD = 9
EPS = 1e-6
N_AB = (DEPTH + 1) // 2
N_C = DEPTH // 2
NA_WIDTH = NA_HEADS * NA_HEAD_DIM
ML_WIDTH = ML_HEADS * ML_HEAD_DIM
AB_MAIN = 3 * NA_WIDTH + 4 * ML_WIDTH
N_GATES = 4 * ML_HEADS
AB_IN = AB_MAIN + N_GATES
SC_WIDTH = D_MODEL

N_PROMPT = BATCH * SEQ
N_SAMPLE = DEC_BATCH * DEC_SEQ
N_TOK = N_PROMPT + N_SAMPLE
GRID_ROWS = DEC_SEQ // GRID_W
NA_KH = min(NA_WIN_H, GRID_ROWS)
N_DR = 2 * NA_WIN_H - 1
N_DC = 2 * NA_WIN_W - 1
N_PAIR = N_DR - 1

LANES = 128
COND_ROWS = 8
VMEM_CAP = 60 * 1024 * 1024
TEMP_BYTES = 16 * 1024 * 1024

COND_UNIT = DEC_SEQ
TM = 2048
SUB_ROWS = 512
TM_OUT = 1024
TM_CONV = COND_UNIT
TF_CONV = 512
TF = 256
TN_PROJ = 512
TN_MOD = 1536
NA_QROWS = 4

PROJ_BLOCKS = (0, 1, 2, 3, 5, 6)
KM_BLOCK = 4
P_QA, P_KA, P_VA, P_QM, P_VM, P_OM = range(6)

F32 = jnp.float32
BF16 = jnp.bfloat16


def _vmem_limit(block_bytes, scratch_bytes):
    return int(min(VMEM_CAP, 2 * block_bytes + scratch_bytes + TEMP_BYTES))


def _params(semantics, block_bytes, scratch_bytes=0):
    return pltpu.CompilerParams(dimension_semantics=semantics,
                                vmem_limit_bytes=_vmem_limit(block_bytes, scratch_bytes))


def _unit_rows(i, tm=TM):
    first_latent = N_PROMPT // COND_UNIT
    units = tm // COND_UNIT
    return [(slice(u * COND_UNIT, (u + 1) * COND_UNIT), jnp.maximum(i * units + u - (first_latent - 1), 0))
            for u in range(units)]


def _sub_rows(i, tm=TM):
    per_unit = COND_UNIT // SUB_ROWS
    return [(slice(rows.start + s * SUB_ROWS, rows.start + (s + 1) * SUB_ROWS), r)
            for rows, r in _unit_rows(i, tm) for s in range(per_unit)]


def _rms_scale(x):
    return x * lax.rsqrt(jnp.mean(x * x, axis=-1, keepdims=True) + EPS)


def _modulate_in(x, g_ref, mod_ref, r, j):
    gain = g_ref[2 * j:2 * j + 1, :] * (1.0 + mod_ref[r, 3 * j + 1:3 * j + 2, :])
    return _rms_scale(x) * gain + mod_ref[r, 3 * j:3 * j + 1, :]


def _gated_residual(x, out, g_ref, mod_ref, r, j, weight):
    gain = (weight * mod_ref[r, 3 * j + 2:3 * j + 3, :]) * g_ref[2 * j + 1:2 * j + 2, :]
    return x + _rms_scale(out) * gain


_MOD_SPEC_SHAPE = (COND_ROWS, N_MOD, D_MODEL)


def _dot(a, b):
    return jnp.dot(a.astype(BF16), b.astype(BF16), preferred_element_type=F32)


def _dot_nt(a, b):
    return lax.dot_general(a.astype(BF16), b.astype(BF16), (((1,), (1,)), ((), ())),
                           preferred_element_type=F32)


def _mod_kernel(c_ref, w_ref, b_ref, o_ref):
    c = c_ref[...]
    s = c * jax.nn.sigmoid(c)
    o_ref[...] = _dot(s, w_ref[...]) + b_ref[...]


def _modulation(cond, ada_w, ada_b):
    n = N_MOD * D_MODEL
    out = pl.pallas_call(
        _mod_kernel,
        grid=(DEPTH, n // TN_MOD),
        in_specs=[
            pl.BlockSpec((COND_ROWS, D_MODEL), lambda l, t: (0, 0)),
            pl.BlockSpec((None, D_MODEL, TN_MOD), lambda l, t: (l, 0, t)),
            pl.BlockSpec((None, 1, TN_MOD), lambda l, t: (l, 0, t)),
        ],
        out_specs=pl.BlockSpec((None, COND_ROWS, TN_MOD), lambda l, t: (l, 0, t)),
        out_shape=jax.ShapeDtypeStruct((DEPTH, COND_ROWS, n), F32),
        compiler_params=_params(("parallel", "parallel"), 4 * D_MODEL * TN_MOD),
        name="modulation",
    )(cond, ada_w, ada_b.reshape(DEPTH, 1, n))
    return out.reshape(DEPTH, COND_ROWS, N_MOD, D_MODEL)


def _ffn_kernel(x_ref, mod_ref, g_ref, wa_ref, wg_ref, w2_ref, o_ref, h_ref, *, j):
    i = pl.program_id(0)
    k = pl.program_id(1)
    last = pl.num_programs(1) - 1

    def hidden_step(rows, init):
        h = h_ref[rows, :]
        a = _dot(h, wa_ref[...])
        g = _dot(h, wg_ref[...])
        d = _dot((g * jax.nn.sigmoid(g)) * a, w2_ref[...])
        if init:
            o_ref[rows, :] = d
        else:
            o_ref[rows, :] += d

    @pl.when(k == 0)
    def _():
        for rows, r in _sub_rows(i):
            h_ref[rows, :] = _modulate_in(x_ref[rows, :], g_ref, mod_ref, r, j).astype(BF16)
            hidden_step(rows, init=True)

    @pl.when((k > 0) & (k < last))
    def _():
        hidden_step(slice(None), init=False)

    @pl.when(k == last)
    def _():
        for rows, r in _sub_rows(i):
            hidden_step(rows, init=False)
            o_ref[rows, :] = _gated_residual(x_ref[rows, :], o_ref[rows, :], g_ref, mod_ref, r, j, 0.5)


def _ffn_sublayer(y, mod_l, g_l, ffn_w13, ffn_w2, l, s):
    j = 2 * s
    nk = D_FF // TF
    blocks = 4 * (2 * TM * D_MODEL + 3 * D_MODEL * TF)
    scratch = 2 * TM * D_MODEL
    return pl.pallas_call(
        functools.partial(_ffn_kernel, j=j),
        grid=(N_TOK // TM, nk),
        in_specs=[
            pl.BlockSpec((TM, D_MODEL), lambda i, k: (i, 0)),
            pl.BlockSpec(_MOD_SPEC_SHAPE, lambda i, k: (0, 0, 0)),
            pl.BlockSpec((6, D_MODEL), lambda i, k: (0, 0)),
            pl.BlockSpec((None, None, D_MODEL, TF), lambda i, k: (l, s, 0, k)),
            pl.BlockSpec((None, None, D_MODEL, TF), lambda i, k: (l, s, 0, k + nk)),
            pl.BlockSpec((None, None, TF, D_MODEL), lambda i, k: (l, s, k, 0)),
        ],
        out_specs=pl.BlockSpec((TM, D_MODEL), lambda i, k: (i, 0)),
        out_shape=jax.ShapeDtypeStruct((N_TOK, D_MODEL), F32),
        scratch_shapes=[pltpu.VMEM((TM, D_MODEL), BF16)],
        compiler_params=_params(("parallel", "arbitrary"), blocks, scratch),
        name=f"ffn_l{l}_s{s}",
    )(y, mod_l, g_l, ffn_w13, ffn_w13, ffn_w2)


def _sconv_kernel(x_ref, mod_ref, g_ref, wb_ref, wc_ref, wx_ref, cw_ref, cb_ref, wo_ref, o_ref, h_ref):
    i = pl.program_id(0)
    k = pl.program_id(1)
    ((_, r),) = _unit_rows(i, TM_CONV)
    seq = jnp.where(i < N_PROMPT // TM_CONV, SEQ, DEC_SEQ)
    blk = SEQ
    n_blk = TM_CONV // blk
    row = lax.broadcasted_iota(jnp.int32, (blk, TF_CONV), 0)
    zero_row = jnp.zeros((1, TF_CONV), F32)

    def step(first, last):
        wb = wb_ref[...].astype(BF16)
        wc = wc_ref[...].astype(BF16)
        wx = wx_ref[...].astype(BF16)
        wo = wo_ref[...].astype(BF16)
        gbs, us = [], []
        for bi in range(n_blk):
            rows = slice(bi * blk, (bi + 1) * blk)
            if first:
                h_ref[rows, :] = _modulate_in(x_ref[rows, :], g_ref, mod_ref, r, 1).astype(BF16)
            h = h_ref[rows, :]
            gbs.append(_dot(h, wb))
            us.append(_dot(h, wc) * _dot(h, wx))
        for bi in range(n_blk):
            rows = slice(bi * blk, (bi + 1) * blk)
            ub = us[bi]
            t = (row + bi * blk) & (seq - 1)
            above = us[bi - 1][blk - 1:blk, :] if bi > 0 else zero_row
            below = us[bi + 1][0:1, :] if bi < n_blk - 1 else zero_row
            u_prev = jnp.where(t == 0, 0.0, jnp.where(row == 0, above, pltpu.roll(ub, 1, 0)))
            u_next = jnp.where(t == seq - 1, 0.0, jnp.where(row == blk - 1, below, pltpu.roll(ub, blk - 1, 0)))
            y = cb_ref[...] + u_prev * cw_ref[0:1, :] + ub * cw_ref[1:2, :] + u_next * cw_ref[2:3, :]
            d = _dot(gbs[bi] * y, wo)
            acc = d if first else o_ref[rows, :] + d
            o_ref[rows, :] = _gated_residual(x_ref[rows, :], acc, g_ref, mod_ref, r, 1, 1.0) if last else acc

    n_steps = SC_WIDTH // TF_CONV
    for kk in range(n_steps):
        pl.when(k == kk)(functools.partial(step, kk == 0, kk == n_steps - 1))


def _sconv_sublayer(y, mod_l, g_l, sc_w_in, sc_conv_w, sc_conv_b, sc_w_out, jc):
    tm, tf = TM_CONV, TF_CONV
    nk = SC_WIDTH // tf
    blocks = 4 * (2 * tm * D_MODEL + 4 * D_MODEL * tf)
    scratch = 2 * tm * D_MODEL
    return pl.pallas_call(
        _sconv_kernel,
        grid=(N_TOK // tm, nk),
        in_specs=[
            pl.BlockSpec((tm, D_MODEL), lambda i, k: (i, 0)),
            pl.BlockSpec(_MOD_SPEC_SHAPE, lambda i, k: (0, 0, 0)),
            pl.BlockSpec((6, D_MODEL), lambda i, k: (0, 0)),
            pl.BlockSpec((None, D_MODEL, tf), lambda i, k: (jc, 0, k)),
            pl.BlockSpec((None, D_MODEL, tf), lambda i, k: (jc, 0, k + nk)),
            pl.BlockSpec((None, D_MODEL, tf), lambda i, k: (jc, 0, k + 2 * nk)),
            pl.BlockSpec((None, CONV_WIDTH, tf), lambda i, k: (jc, 0, k)),
            pl.BlockSpec((None, 1, tf), lambda i, k: (jc, 0, k)),
            pl.BlockSpec((None, tf, D_MODEL), lambda i, k: (jc, k, 0)),
        ],
        out_specs=pl.BlockSpec((tm, D_MODEL), lambda i, k: (i, 0)),
        out_shape=jax.ShapeDtypeStruct((N_TOK, D_MODEL), F32),
        scratch_shapes=[pltpu.VMEM((tm, D_MODEL), BF16)],
        compiler_params=_params(("parallel", "arbitrary"), blocks, scratch),
        name=f"sconv_{jc}",
    )(y, mod_l, g_l, sc_w_in, sc_w_in, sc_w_in, sc_conv_w, sc_conv_b.reshape(N_C, 1, SC_WIDTH), sc_w_out)


def _inproj_kernel(x_ref, mod_ref, g_ref, w_ref, wk_ref, wg_ref, o_ref, okt_ref, og_ref, h_ref):
    i = pl.program_id(0)
    n = pl.program_id(1)
    n_main = len(PROJ_BLOCKS)

    @pl.when(n == 0)
    def _():
        for rows, r in _sub_rows(i):
            h_ref[rows, :] = _modulate_in(x_ref[rows, :], g_ref, mod_ref, r, 1).astype(BF16)
            o_ref[rows, :] = _dot_nt(h_ref[rows, :], w_ref[...])

    @pl.when((n > 0) & (n < n_main))
    def _():
        o_ref[...] = _dot_nt(h_ref[...], w_ref[...])

    @pl.when(n == n_main)
    def _():
        kt = _dot_nt(wk_ref[...], h_ref[...])
        for c in range(TM // ML_CHUNK):
            okt_ref[c] = kt[:, c * ML_CHUNK:(c + 1) * ML_CHUNK]
        w_gates = jnp.concatenate([wg_ref[...], jnp.zeros((LANES - N_GATES, D_MODEL), F32)], axis=0)
        og_ref[...] = _dot_nt(h_ref[...], w_gates)


def _inproj(y, mod_l, g_l, w_in_t, jb):
    n_main = len(PROJ_BLOCKS)
    assert PROJ_BLOCKS == tuple(b for b in range(n_main + 1) if b != KM_BLOCK)

    def w_block(i, n):
        m = jnp.minimum(n, n_main - 1)
        return (jb, m + (m >= KM_BLOCK).astype(jnp.int32), 0)

    cpt = TM // ML_CHUNK
    blocks = 4 * (TM * D_MODEL + 2 * TN_PROJ * D_MODEL + N_GATES * D_MODEL + TM * TN_PROJ
                  + ML_WIDTH * TM + TM * LANES)
    return pl.pallas_call(
        _inproj_kernel,
        grid=(N_TOK // TM, n_main + 1),
        in_specs=[
            pl.BlockSpec((TM, D_MODEL), lambda i, n: (i, 0)),
            pl.BlockSpec(_MOD_SPEC_SHAPE, lambda i, n: (0, 0, 0)),
            pl.BlockSpec((6, D_MODEL), lambda i, n: (0, 0)),
            pl.BlockSpec((None, TN_PROJ, D_MODEL), w_block),
            pl.BlockSpec((None, ML_WIDTH, D_MODEL), lambda i, n: (jb, KM_BLOCK, 0)),
            pl.BlockSpec((None, N_GATES, D_MODEL), lambda i, n: (jb, AB_MAIN // N_GATES, 0)),
        ],
        out_specs=[
            pl.BlockSpec((TM, TN_PROJ), lambda i, n: (i, jnp.minimum(n, n_main - 1))),
            pl.BlockSpec((cpt, ML_WIDTH, ML_CHUNK), lambda i, n: (i, 0, 0)),
            pl.BlockSpec((TM, LANES), lambda i, n: (i, 0)),
        ],
        out_shape=[jax.ShapeDtypeStruct((N_TOK, n_main * TN_PROJ), F32),
                   jax.ShapeDtypeStruct((N_TOK // ML_CHUNK, ML_WIDTH, ML_CHUNK), F32),
                   jax.ShapeDtypeStruct((N_TOK, LANES), F32)],
        scratch_shapes=[pltpu.VMEM((TM, D_MODEL), BF16)],
        compiler_params=_params(("parallel", "arbitrary"), blocks, 2 * TM * D_MODEL),
        name=f"inproj_{jb}",
    )(y, mod_l, g_l, w_in_t, w_in_t, w_in_t)


def _stack_entry(ref, jb, fills_stack):
    if not fills_stack:
        return ref
    for j in range(ref.shape[0]):
        if j != jb:
            ref[j] = jnp.zeros(ref.shape[1:], ref.dtype)
    return ref.at[jb]


def _stack_spec(entry_shape, jb, fills_stack, n_stack=N_AB):
    zeros = (0,) * len(entry_shape)
    if fills_stack:
        return pl.BlockSpec((None, n_stack) + entry_shape, lambda b: (b, 0) + zeros)
    return pl.BlockSpec((None, None) + entry_shape, lambda b: (b, jb) + zeros)


def _ctx_attn_kernel(q_ref, k_ref, v_ref, *rest, jb, fills_stack):
    o_ref = rest[-3]
    kc_ref = _stack_entry(rest[-2], jb, fills_stack)
    vc_ref = _stack_entry(rest[-1], jb, fills_stack)
    scale = NA_HEAD_DIM ** -0.5
    k_t = k_ref[...].T
    v_t = v_ref[...].T
    for h in range(NA_HEADS):
        sl = slice(h * NA_HEAD_DIM, (h + 1) * NA_HEAD_DIM)
        kc_ref[h] = k_t[sl, :]
        vc_ref[h] = v_t[sl, :]
        s = _dot(q_ref[:, sl], k_t[sl, :]) * scale
        p = jnp.exp(s - jnp.max(s, axis=-1, keepdims=True))
        p = p * (1.0 / jnp.sum(p, axis=-1, keepdims=True))
        o_ref[:, sl] = _dot(p, v_ref[:, sl])


def _ctx_attention(proj, jb, prev):
    cache = jax.ShapeDtypeStruct((BATCH, N_AB, NA_HEADS, NA_HEAD_DIM, SEQ), F32)
    cache_spec = _stack_spec((NA_HEADS, NA_HEAD_DIM, SEQ), jb, prev is None)
    in_specs = [pl.BlockSpec((SEQ, NA_WIDTH), lambda b, c=c: (b, c)) for c in (P_QA, P_KA, P_VA)]
    args = [proj, proj, proj]
    aliases = {}
    if prev is not None:
        in_specs += [pl.BlockSpec(memory_space=pl.ANY)] * 2
        args += list(prev)
        aliases = {3: 1, 4: 2}
    blocks = 4 * (4 * SEQ * NA_WIDTH + 2 * N_AB * NA_WIDTH * SEQ)
    return pl.pallas_call(
        functools.partial(_ctx_attn_kernel, jb=jb, fills_stack=prev is None),
        grid=(BATCH,),
        in_specs=in_specs,
        out_specs=[pl.BlockSpec((SEQ, NA_WIDTH), lambda b: (b, 0)), cache_spec, cache_spec],
        out_shape=[jax.ShapeDtypeStruct((N_PROMPT, NA_WIDTH), F32), cache, cache],
        input_output_aliases=aliases,
        compiler_params=_params(("parallel",), blocks),
        name=f"ctx_attention_{jb}",
    )(*args)


def _na_bias_kernel(rpb_ref, p_ref):
    base = (pl.program_id(0) * NA_HEADS + pl.program_id(1)) * (N_DR * N_DC)
    shape = (GRID_W, LANES)
    lane = lax.broadcasted_iota(jnp.int32, shape, 1)
    qc = lax.broadcasted_iota(jnp.int32, shape, 0)
    kc = lane & (GRID_W - 1)
    upper = lane >= GRID_W
    cs = jnp.clip(qc - NA_WIN_W // 2, 0, GRID_W - NA_WIN_W)
    col_in = (kc >= cs) & (kc < cs + NA_WIN_W)
    dc = jnp.clip(kc - qc, -(NA_WIN_W - 1), NA_WIN_W - 1) + (NA_WIN_W - 1)
    for pair in range(N_PAIR):
        acc = jnp.zeros(shape, F32)
        for d in range(N_DC):
            lo = rpb_ref[base + pair * N_DC + d]
            hi = rpb_ref[base + (pair + 1) * N_DC + d]
            acc = jnp.where(dc == d, jnp.where(upper, hi, lo), acc)
        p_ref[pair] = jnp.where(col_in, acc, -jnp.inf)


def _na_bias(na_rpb):
    return pl.pallas_call(
        _na_bias_kernel,
        grid=(N_AB, NA_HEADS),
        in_specs=[pl.BlockSpec(memory_space=pltpu.SMEM)],
        out_specs=pl.BlockSpec((None, None, N_PAIR, GRID_W, LANES), lambda j, h: (j, h, 0, 0, 0)),
        out_shape=jax.ShapeDtypeStruct((N_AB, NA_HEADS, N_PAIR, GRID_W, LANES), F32),
        compiler_params=_params(("parallel", "parallel"), 4 * N_PAIR * GRID_W * LANES),
        name="na_bias",
    )(na_rpb.reshape(-1))


def _na_window_start(r):
    return min(max(r - NA_KH // 2, 0), GRID_ROWS - NA_KH)


def _na_bias_block(p_ref, hh, qr, kr):
    rs = _na_window_start(qr)
    lo_ok = rs <= kr < rs + NA_KH
    hi_ok = rs <= kr + 1 < rs + NA_KH
    if not (lo_ok or hi_ok):
        return jnp.full((GRID_W, LANES), -jnp.inf, F32)
    dr = kr - qr + (NA_WIN_H - 1)
    assert 0 <= dr < N_PAIR
    blk = p_ref[hh, dr]
    if lo_ok and hi_ok:
        return blk
    lane = lax.broadcasted_iota(jnp.int32, (GRID_W, LANES), 1)
    keep = (lane < GRID_W) if lo_ok else (lane >= GRID_W)
    return jnp.where(keep, blk, -jnp.inf)


def _na_kernel(q_ref, k_ref, v_ref, kc_ref, vc_ref, p_ref, o_ref):
    scale = NA_HEAD_DIM ** -0.5
    for hh in range(LANES // NA_HEAD_DIM):
        sl = slice(hh * NA_HEAD_DIM, (hh + 1) * NA_HEAD_DIM)
        k_ctx_t = kc_ref[hh]
        v_ctx_t = vc_ref[hh]
        for q0 in range(0, GRID_ROWS, NA_QROWS):
            k_lo = _na_window_start(q0) // 2 * 2
            k_hi = -(-(_na_window_start(q0 + NA_QROWS - 1) + NA_KH) // 2) * 2
            q_rows = slice(q0 * GRID_W, (q0 + NA_QROWS) * GRID_W)
            k_rows = slice(k_lo * GRID_W, k_hi * GRID_W)
            bias = jnp.concatenate(
                [jnp.concatenate([_na_bias_block(p_ref, hh, qr, kr) for kr in range(k_lo, k_hi, 2)], axis=1)
                 for qr in range(q0, q0 + NA_QROWS)], axis=0)
            q = q_ref[q_rows, sl]
            s_loc = _dot_nt(q, k_ref[k_rows, sl]) * scale + bias
            s_ctx = _dot(q, k_ctx_t) * scale
            m = jnp.maximum(jnp.max(s_loc, axis=-1, keepdims=True), jnp.max(s_ctx, axis=-1, keepdims=True))
            p_loc = jnp.exp(s_loc - m)
            p_ctx = jnp.exp(s_ctx - m)
            denom = jnp.sum(p_loc, axis=-1, keepdims=True) + jnp.sum(p_ctx, axis=-1, keepdims=True)
            o_ref[q_rows, sl] = (_dot(p_loc, v_ref[k_rows, sl]) + _dot_nt(p_ctx, v_ctx_t)) / denom


def _na_attention(proj, cache_k_t, cache_v_t, bias_pairs, jb):
    hp = LANES // NA_HEAD_DIM
    s0 = N_PROMPT // DEC_SEQ
    nb = NA_WIDTH // LANES
    ctx_spec = pl.BlockSpec((None, None, hp, NA_HEAD_DIM, PAST_LEN), lambda b, c: (b, jb, c, 0, 0))
    blocks = 4 * (4 * DEC_SEQ * LANES + 2 * hp * NA_HEAD_DIM * PAST_LEN + hp * N_PAIR * GRID_W * LANES)
    return pl.pallas_call(
        _na_kernel,
        grid=(DEC_BATCH, nb),
        in_specs=[
            pl.BlockSpec((DEC_SEQ, LANES), lambda b, c: (s0 + b, P_QA * nb + c)),
            pl.BlockSpec((DEC_SEQ, LANES), lambda b, c: (s0 + b, P_KA * nb + c)),
            pl.BlockSpec((DEC_SEQ, LANES), lambda b, c: (s0 + b, P_VA * nb + c)),
            ctx_spec, ctx_spec,
            pl.BlockSpec((None, hp, N_PAIR, GRID_W, LANES), lambda b, c: (jb, c, 0, 0, 0)),
        ],
        out_specs=pl.BlockSpec((DEC_SEQ, LANES), lambda b, c: (b, c)),
        out_shape=jax.ShapeDtypeStruct((N_SAMPLE, NA_WIDTH), F32),
        compiler_params=_params(("parallel", "parallel"), blocks),
        name=f"na_attention_{jb}",
    )(proj, proj, proj, cache_k_t, cache_v_t, bias_pairs)


def _running_max_rows(x, reverse):
    n = x.shape[0]
    ridx = lax.broadcasted_iota(jnp.int32, x.shape, 0)
    sh = 1
    while sh < n:
        if reverse:
            x = jnp.maximum(x, jnp.where(ridx < n - sh, pltpu.roll(x, n - sh, 0), -jnp.inf))
        else:
            x = jnp.maximum(x, jnp.where(ridx >= sh, pltpu.roll(x, sh, 0), -jnp.inf))
        sh *= 2
    return x


def _mlstm_kernel(*refs, n_chunks, has_state, jb, fills_stack):
    if has_state:
        (q_ref, kt_ref, v_ref, o_ref, gt_ref, gb_ref, hn_ref, c0_ref, n0_ref, m0_ref,
         out_ref, cout_ref, nout_ref, mout_ref, hdir_ref, caug_ref) = refs
    else:
        (q_ref, kt_ref, v_ref, o_ref, gt_ref, gb_ref, hn_ref, *_unused,
         out_ref, cout_ref, nout_ref, mout_ref, hdir_ref, caug_ref) = refs
    cout_ref = _stack_entry(cout_ref, jb, fills_stack)
    nout_ref = _stack_entry(nout_ref, jb, fills_stack)
    mout_ref = _stack_entry(mout_ref, jb, fills_stack)
    T = ML_CHUNK
    dh = ML_HEAD_DIM
    b = pl.program_id(0)
    row = lax.broadcasted_iota(jnp.int32, (T, T), 0)
    col = lax.broadcasted_iota(jnp.int32, (T, T), 1)
    lane = lax.broadcasted_iota(jnp.int32, (T, LANES), 1)
    lane_row = lax.broadcasted_iota(jnp.int32, (1, LANES), 1)
    is_forget = ((lane >= ML_HEADS) & (lane < 2 * ML_HEADS)) | ((lane >= 3 * ML_HEADS) & (lane < 4 * ML_HEADS))
    ones = jnp.ones((T, dh), F32)
    states = [(d, h) for d in range(2) for h in range(ML_HEADS)]

    def gate_lanes(d, h):
        gi = 2 * d * ML_HEADS + h
        return gi, gi + ML_HEADS

    m_init = []
    for d in range(2):
        m_row = jnp.zeros((1, LANES), F32)
        for h in range(ML_HEADS):
            if has_state:
                n_bcast = jnp.broadcast_to(n0_ref[d, h:h + 1, :], (dh, dh)).T
                caug_ref[d, h] = jnp.concatenate([c0_ref[d, h], n_bcast], axis=1)
                m_row = jnp.where(lane_row == gate_lanes(d, h)[1], m0_ref[(b * 2 + d) * ML_HEADS + h], m_row)
            else:
                caug_ref[d, h] = jnp.zeros((dh, 2 * dh), F32)
        m_init.append(m_row)

    def chunk_body(ci, m_rows):
        m_next = []
        for d in range(2):
            causal = (col <= row) if d == 0 else (col >= row)
            last = T - 1 if d == 0 else 0
            c = ci if d == 0 else n_chunks - 1 - ci
            rows = pl.ds(pl.multiple_of(c * T, T), T)
            mine = (lane >= (2 * d + 1) * ML_HEADS) & (lane < (2 * d + 2) * ML_HEADS)
            g = gt_ref[rows, :] + gb_ref[...]
            log_sig = jnp.minimum(g, 0.0) - jnp.log1p(jnp.exp(-jnp.abs(g)))
            g = jnp.where(is_forget, log_sig, g)
            cum = jnp.dot(causal.astype(F32), g, precision=lax.Precision.HIGHEST, preferred_element_type=F32)
            i_al = pltpu.roll(g, ML_HEADS, 1)
            cvec = jnp.where(mine, i_al - cum, 0.0)
            cum = jnp.where(mine, cum, 0.0)
            m_run = jnp.maximum(m_rows[d], _running_max_rows(cvec, reverse=(d == 1)))
            m_t = cum + m_run
            w_inter = jnp.exp(m_rows[d] - m_run)
            e_negm = jnp.exp(-m_t)
            m_run_last = m_run[last:last + 1, :]
            wi_last = w_inter[last:last + 1, :]
            w_last = jnp.exp(cvec - m_run_last)
            packed_t = jnp.where(mine, cvec, pltpu.roll(w_last, LANES - ML_HEADS, 1)).T
            for h in range(ML_HEADS):
                gi, gf = gate_lanes(d, h)
                hl = slice(h * dh, (h + 1) * dh)
                m_run_b = jnp.broadcast_to(m_run[:, gf:gf + 1], (T, T))
                e_negm_b = jnp.broadcast_to(e_negm[:, gf:gf + 1], (T, dh))
                w_d = jnp.exp(jnp.where(causal, packed_t[gf:gf + 1, :] - m_run_b, -jnp.inf))
                w_inter_b = jnp.exp(m_rows[d][:, gf:gf + 1] - m_run_b)
                qh = q_ref[rows, hl]
                kh_t = kt_ref[c, hl, :] * (dh ** -0.5)
                v_aug = jnp.concatenate([v_ref[rows, hl], ones], axis=1)
                s = _dot(qh, kh_t) * w_d
                c_aug = caug_ref[d, h]
                nd = jnp.concatenate([w_inter_b, w_inter_b], axis=1) * _dot(qh, c_aug) + _dot(s, v_aug)
                hdir_ref[d, rows, hl] = nd[:, :dh] / jnp.maximum(jnp.abs(nd[:, dh:]), e_negm_b)
                kw_t = kh_t * packed_t[gi:gi + 1, :]
                caug_ref[d, h] = wi_last[:, gf:gf + 1] * c_aug + _dot(kw_t, v_aug)
            m_next.append(m_t[last:last + 1, :])
        return tuple(m_next)

    m_fin = lax.fori_loop(0, n_chunks, chunk_body, tuple(m_init), unroll=2)
    for d, h in states:
        c_aug = caug_ref[d, h]
        cout_ref[d, h] = c_aug[:, :dh]
        nout_ref[d, h:h + 1, :] = c_aug[:, dh:].T[0:1, :]
    m_heads = [pltpu.roll(m_fin[d], LANES - gate_lanes(d, 0)[1], 1) for d in range(2)]
    mout_ref[...] = jnp.concatenate(m_heads + [jnp.zeros((COND_ROWS - 2, LANES), F32)], axis=0)

    for h in range(ML_HEADS):
        hl = slice(h * dh, (h + 1) * dh)
        h_sum = hdir_ref[0, :, hl] + hdir_ref[1, :, hl]
        out_ref[:, hl] = jax.nn.sigmoid(o_ref[:, hl]) * (_rms_scale(h_sum) * hn_ref[:, hl])


def _mlstm(proj, keys_t, gates, gate_b_row, hnorm_row, seq_len, n_seq, row0, n_stack=1, jb=0, prev=None,
           state=None):
    has_state = state is not None
    r0 = row0 // seq_len
    nb = TN_PROJ // ML_WIDTH
    n_chunks = seq_len // ML_CHUNK
    in_specs = [
        pl.BlockSpec((seq_len, ML_WIDTH), lambda b: (r0 + b, P_QM * nb)),
        pl.BlockSpec((n_chunks, ML_WIDTH, ML_CHUNK), lambda b: (r0 + b, 0, 0)),
        pl.BlockSpec((seq_len, ML_WIDTH), lambda b: (r0 + b, P_VM * nb)),
        pl.BlockSpec((seq_len, ML_WIDTH), lambda b: (r0 + b, P_OM * nb)),
        pl.BlockSpec((seq_len, LANES), lambda b: (r0 + b, 0)),
        pl.BlockSpec((1, LANES), lambda b: (0, 0)),
        pl.BlockSpec((1, ML_WIDTH), lambda b: (0, 0)),
    ]
    args = [proj, keys_t, proj, proj, gates, gate_b_row, hnorm_row]
    aliases = {}
    if has_state:
        st_c, st_n, st_m = state
        in_specs += [
            pl.BlockSpec((None, 2, ML_HEADS, ML_HEAD_DIM, ML_HEAD_DIM), lambda b: (b, 0, 0, 0, 0)),
            pl.BlockSpec((None, 2, ML_HEADS, ML_HEAD_DIM), lambda b: (b, 0, 0, 0)),
            pl.BlockSpec(memory_space=pltpu.SMEM),
        ]
        args += [st_c, st_n, st_m.reshape(-1)]
    elif prev is not None:
        aliases = {len(args) + k: 1 + k for k in range(3)}
        in_specs += [pl.BlockSpec(memory_space=pl.ANY)] * 3
        args += list(prev)
    fills_stack = prev is None
    c_bytes = 2 * ML_HEADS * ML_HEAD_DIM * ML_HEAD_DIM
    blocks = 4 * (5 * seq_len * ML_WIDTH + seq_len * LANES + (1 + n_stack) * c_bytes)
    scratch = 4 * (2 * seq_len * ML_WIDTH + 2 * c_bytes)
    return pl.pallas_call(
        functools.partial(_mlstm_kernel, n_chunks=n_chunks, has_state=has_state, jb=jb, fills_stack=fills_stack),
        grid=(n_seq,),
        in_specs=in_specs,
        out_specs=[
            pl.BlockSpec((seq_len, ML_WIDTH), lambda b: (b, 0)),
            _stack_spec((2, ML_HEADS, ML_HEAD_DIM, ML_HEAD_DIM), jb, fills_stack, n_stack),
            _stack_spec((2, ML_HEADS, ML_HEAD_DIM), jb, fills_stack, n_stack),
            _stack_spec((COND_ROWS, LANES), jb, fills_stack, n_stack),
        ],
        out_shape=[
            jax.ShapeDtypeStruct((n_seq * seq_len, ML_WIDTH), F32),
            jax.ShapeDtypeStruct((n_seq, n_stack, 2, ML_HEADS, ML_HEAD_DIM, ML_HEAD_DIM), F32),
            jax.ShapeDtypeStruct((n_seq, n_stack, 2, ML_HEADS, ML_HEAD_DIM), F32),
            jax.ShapeDtypeStruct((n_seq, n_stack, COND_ROWS, LANES), F32),
        ],
        scratch_shapes=[pltpu.VMEM((2, seq_len, ML_WIDTH), F32),
                        pltpu.VMEM((2, ML_HEADS, ML_HEAD_DIM, 2 * ML_HEAD_DIM), F32)],
        input_output_aliases=aliases,
        compiler_params=_params(("parallel",), blocks, scratch),
        name=f"mlstm_{'latent' if has_state else 'context'}_{jb}",
    )(*args)


def _outproj_kernel(x_ref, mod_ref, g_ref, ap_ref, mp_ref, as_ref, ms_ref, wa_ref, wm_ref, o_ref):
    i = pl.program_id(0)

    def finish(a_ref, m_ref):
        wa = wa_ref[...].astype(BF16)
        wm = wm_ref[...].astype(BF16)
        for rows, r in _sub_rows(i, TM_OUT):
            out = _dot(a_ref[rows, :], wa) + _dot(m_ref[rows, :], wm)
            o_ref[rows, :] = _gated_residual(x_ref[rows, :], out, g_ref, mod_ref, r, 1, 1.0)

    is_prompt = i < N_PROMPT // TM_OUT
    pl.when(is_prompt)(functools.partial(finish, ap_ref, mp_ref))
    pl.when(jnp.logical_not(is_prompt))(functools.partial(finish, as_ref, ms_ref))


def _outproj(y, mod_l, g_l, a_p, m_p, a_s, m_s, ab_w_out, jb):
    tm = TM_OUT
    n_p = N_PROMPT // tm
    n_s = N_SAMPLE // tm
    p_spec = pl.BlockSpec((tm, NA_WIDTH), lambda i: (jnp.minimum(i, n_p - 1), 0))
    s_spec = pl.BlockSpec((tm, NA_WIDTH), lambda i: (jnp.clip(i - n_p, 0, n_s - 1), 0))
    blocks = 4 * (2 * tm * D_MODEL + 4 * tm * NA_WIDTH + D_MODEL * D_MODEL)
    return pl.pallas_call(
        _outproj_kernel,
        grid=(N_TOK // tm,),
        in_specs=[
            pl.BlockSpec((tm, D_MODEL), lambda i: (i, 0)),
            pl.BlockSpec(_MOD_SPEC_SHAPE, lambda i: (0, 0, 0)),
            pl.BlockSpec((6, D_MODEL), lambda i: (0, 0)),
            p_spec, p_spec, s_spec, s_spec,
            pl.BlockSpec((None, NA_WIDTH, D_MODEL), lambda i: (jb, 0, 0)),
            pl.BlockSpec((None, ML_WIDTH, D_MODEL), lambda i: (jb, 1, 0)),
        ],
        out_specs=pl.BlockSpec((tm, D_MODEL), lambda i: (i, 0)),
        out_shape=jax.ShapeDtypeStruct((N_TOK, D_MODEL), F32),
        compiler_params=_params(("parallel",), blocks),
        name=f"outproj_{jb}",
    )(y, mod_l, g_l, a_p, m_p, a_s, m_s, ab_w_out, ab_w_out)


def kernel(x_prompt, x_sample, cache_k, cache_v, state_C, state_n, state_m, c, c_ctx, ada_w, ada_b, norm_g,
           ffn_w13, ffn_w2, ab_w_in, ab_w_out, na_rpb, ml_gate_b, ml_hnorm, sc_w_in, sc_conv_w, sc_conv_b,
           sc_w_out):
    assert NA_WIDTH == ML_WIDTH == TN_PROJ and AB_MAIN % N_GATES == 0
    assert N_PROMPT % TM == 0 and N_SAMPLE % TM == 0 and TM % COND_UNIT == 0
    cond = jnp.concatenate([c_ctx[None, :], c, jnp.zeros((COND_ROWS - 1 - DEC_BATCH, D_MODEL), F32)], axis=0)
    mod = _modulation(cond, ada_w, ada_b)
    bias_pairs = _na_bias(na_rpb)
    y = jnp.concatenate([x_prompt.reshape(N_PROMPT, D_MODEL), x_sample.reshape(N_SAMPLE, D_MODEL)], axis=0)
    w_in_t = jnp.swapaxes(ab_w_in, 1, 2)
    cache_k_t = jnp.swapaxes(cache_k, 3, 4)
    cache_v_t = jnp.swapaxes(cache_v, 3, 4)

    kv_new = None
    st_new = None
    for l in range(DEPTH):
        mod_l = mod[l]
        g_l = norm_g[l].reshape(6, D_MODEL)
        y = _ffn_sublayer(y, mod_l, g_l, ffn_w13, ffn_w2, l, 0)
        j = l // 2
        if l % 2 == 0:
            proj, keys_t, gates = _inproj(y, mod_l, g_l, w_in_t, j)
            gate_b_row = jnp.pad(ml_gate_b[j].reshape(1, N_GATES), ((0, 0), (0, LANES - N_GATES)))
            hnorm_row = ml_hnorm[j].reshape(1, ML_WIDTH)
            a_p, *kv_new = _ctx_attention(proj, j, kv_new)
            m_p, *st_new = _mlstm(proj, keys_t, gates, gate_b_row, hnorm_row, SEQ, BATCH, 0,
                                  n_stack=N_AB, jb=j, prev=st_new)
            a_s = _na_attention(proj, cache_k_t, cache_v_t, bias_pairs, j)
            m_s, _, _, _ = _mlstm(proj, keys_t, gates, gate_b_row, hnorm_row, DEC_SEQ, DEC_BATCH, N_PROMPT,
                                  state=(state_C[:, j], state_n[:, j], state_m[:, j]))
            y = _outproj(y, mod_l, g_l, a_p, m_p, a_s, m_s, ab_w_out, j)
        else:
            y = _sconv_sublayer(y, mod_l, g_l, sc_w_in, sc_conv_w, sc_conv_b, sc_w_out, j)
        y = _ffn_sublayer(y, mod_l, g_l, ffn_w13, ffn_w2, l, 1)

    y_p = y[:N_PROMPT].reshape(BATCH, SEQ, D_MODEL)
    y_s = y[N_PROMPT:].reshape(DEC_BATCH, DEC_SEQ, D_MODEL)
    new_c, new_n, m_rows = st_new
    new_m = m_rows[:, :, :2, :ML_HEADS]
    return (y_p, y_s, jnp.swapaxes(kv_new[0], 3, 4), jnp.swapaxes(kv_new[1], 3, 4), new_c, new_n, new_m)
```

```python
import functools
import math

import jax
import jax.numpy as jnp
from jax import lax
from jax.experimental import pallas as pl
from jax.experimental.pallas import tpu as pltpu

D_MODEL = 1024
BATCH = 16
SEQ = 256
DEPTH = 4
DEC_BATCH = 2
DEC_SEQ = 1024
PAST_LEN = 256
GRID_W = 64
NA_HEADS = 8
NA_HEAD_DIM = 64
NA_WIN_H = 8
NA_WIN_W = 16
ML_HEADS = 4
ML_HEAD_DIM = 128
ML_CHUNK = 128
CONV_WIDTH = 3
D_FF = 2816
N_MOD = 9
EPS = 1e-6
N_AB = (DEPTH + 1) // 2
N_C = DEPTH // 2
NA_WIDTH = NA_HEADS * NA_HEAD_DIM
ML_WIDTH = ML_HEADS * ML_HEAD_DIM
AB_MAIN = 3 * NA_WIDTH + 4 * ML_WIDTH
N_GATES = 4 * ML_HEADS
AB_IN = AB_MAIN + N_GATES
SC_WIDTH = D_MODEL

N_PROMPT = BATCH * SEQ
N_SAMPLE = DEC_BATCH * DEC_SEQ
N_TOK = N_PROMPT + N_SAMPLE
GRID_ROWS = DEC_SEQ // GRID_W
NA_KH = min(NA_WIN_H, GRID_ROWS)
N_DR = 2 * NA_WIN_H - 1
N_DC = 2 * NA_WIN_W - 1
N_PAIR = N_DR - 1

LANES = 128
COND_ROWS = 8
VMEM_CAP = 60 * 1024 * 1024
TEMP_BYTES = 16 * 1024 * 1024

COND_UNIT = DEC_SEQ
TM = 2048
SUB_ROWS = 512
TM_OUT = 1024
TM_CONV = COND_UNIT
TF_CONV = 512
TF = 256
TN_PROJ = 512
TN_MOD = 1536
NA_QROWS = 4

PROJ_BLOCKS = (0, 1, 2, 3, 5, 6)
KM_BLOCK = 4
P_QA, P_KA, P_VA, P_QM, P_VM, P_OM = range(6)

F32 = jnp.float32
BF16 = jnp.bfloat16


def _vmem_limit(block_bytes, scratch_bytes):
    assert 2 * block_bytes + scratch_bytes <= VMEM_CAP
    return VMEM_CAP


def _params(semantics, block_bytes, scratch_bytes=0):
    return pltpu.CompilerParams(dimension_semantics=semantics,
                                vmem_limit_bytes=_vmem_limit(block_bytes, scratch_bytes))


def _unit_rows(i, tm=TM):
    first_latent = N_PROMPT // COND_UNIT
    units = tm // COND_UNIT
    return [(slice(u * COND_UNIT, (u + 1) * COND_UNIT), jnp.maximum(i * units + u - (first_latent - 1), 0))
            for u in range(units)]


def _sub_rows(i, tm=TM):
    per_unit = COND_UNIT // SUB_ROWS
    return [(slice(rows.start + s * SUB_ROWS, rows.start + (s + 1) * SUB_ROWS), r)
            for rows, r in _unit_rows(i, tm) for s in range(per_unit)]


def _rms_scale(x):
    return x * lax.rsqrt(jnp.mean(x * x, axis=-1, keepdims=True) + EPS)


def _modulate_in(x, g_ref, mod_ref, r, j):
    gain = g_ref[2 * j:2 * j + 1, :] * (1.0 + mod_ref[r, 3 * j + 1:3 * j + 2, :])
    return _rms_scale(x) * gain + mod_ref[r, 3 * j:3 * j + 1, :]


def _gated_residual(x, out, g_ref, mod_ref, r, j, weight):
    gain = (weight * mod_ref[r, 3 * j + 2:3 * j + 3, :]) * g_ref[2 * j + 1:2 * j + 2, :]
    return x + _rms_scale(out) * gain


_MOD_SPEC_SHAPE = (COND_ROWS, N_MOD, D_MODEL)


def _dot(a, b):
    return jnp.dot(a.astype(BF16), b.astype(BF16), preferred_element_type=F32)


def _dot_nt(a, b):
    return lax.dot_general(a.astype(BF16), b.astype(BF16), (((1,), (1,)), ((), ())),
                           preferred_element_type=F32)


def _mod_kernel(c_ref, w_ref, b_ref, o_ref):
    c = c_ref[...]
    s = c * jax.nn.sigmoid(c)
    o_ref[...] = _dot(s, w_ref[...]) + b_ref[...]


def _modulation(cond, ada_w, ada_b):
    n = N_MOD * D_MODEL
    out = pl.pallas_call(
        _mod_kernel,
        grid=(DEPTH, n // TN_MOD),
        in_specs=[
            pl.BlockSpec((COND_ROWS, D_MODEL), lambda l, t: (0, 0)),
            pl.BlockSpec((None, D_MODEL, TN_MOD), lambda l, t: (l, 0, t)),
            pl.BlockSpec((None, 1, TN_MOD), lambda l, t: (l, 0, t)),
        ],
        out_specs=pl.BlockSpec((None, COND_ROWS, TN_MOD), lambda l, t: (l, 0, t)),
        out_shape=jax.ShapeDtypeStruct((DEPTH, COND_ROWS, n), F32),
        compiler_params=_params(("parallel", "parallel"), 4 * D_MODEL * TN_MOD),
        name="modulation",
    )(cond, ada_w, ada_b.reshape(DEPTH, 1, n))
    return out.reshape(DEPTH, COND_ROWS, N_MOD, D_MODEL)


def _ffn_kernel(x_ref, mod_ref, g_ref, wa_ref, wg_ref, w2_ref, o_ref, h_ref, *, j, tile0):
    i = pl.program_id(0) + tile0
    k = pl.program_id(1)
    last = pl.num_programs(1) - 1

    def hidden_step(rows, init):
        h = h_ref[rows, :]
        a = _dot(h, wa_ref[...])
        g = _dot(h, wg_ref[...])
        d = _dot((g * jax.nn.sigmoid(g)) * a, w2_ref[...])
        if init:
            o_ref[rows, :] = d
        else:
            o_ref[rows, :] += d

    @pl.when(k == 0)
    def _():
        for rows, r in _sub_rows(i):
            h_ref[rows, :] = _modulate_in(x_ref[rows, :], g_ref, mod_ref, r, j).astype(BF16)
            hidden_step(rows, init=True)

    @pl.when((k > 0) & (k < last))
    def _():
        hidden_step(slice(None), init=False)

    @pl.when(k == last)
    def _():
        for rows, r in _sub_rows(i):
            hidden_step(rows, init=False)
            o_ref[rows, :] = _gated_residual(x_ref[rows, :], o_ref[rows, :], g_ref, mod_ref, r, j, 0.5)


def _ffn_sublayer(y, mod_l, g_l, ffn_w13, ffn_w2, l, s, tile0=0, n_tiles=N_TOK // TM):
    j = 2 * s
    nk = D_FF // TF
    blocks = 4 * (2 * TM * D_MODEL + 3 * D_MODEL * TF)
    scratch = 2 * TM * D_MODEL
    return pl.pallas_call(
        functools.partial(_ffn_kernel, j=j, tile0=tile0),
        grid=(n_tiles, nk),
        in_specs=[
            pl.BlockSpec((TM, D_MODEL), lambda i, k: (tile0 + i, 0)),
            pl.BlockSpec(_MOD_SPEC_SHAPE, lambda i, k: (0, 0, 0)),
            pl.BlockSpec((6, D_MODEL), lambda i, k: (0, 0)),
            pl.BlockSpec((None, None, D_MODEL, TF), lambda i, k: (l, s, 0, k)),
            pl.BlockSpec((None, None, D_MODEL, TF), lambda i, k: (l, s, 0, k + nk)),
            pl.BlockSpec((None, None, TF, D_MODEL), lambda i, k: (l, s, k, 0)),
        ],
        out_specs=pl.BlockSpec((TM, D_MODEL), lambda i, k: (i, 0)),
        out_shape=jax.ShapeDtypeStruct((n_tiles * TM, D_MODEL), F32),
        scratch_shapes=[pltpu.VMEM((TM, D_MODEL), BF16)],
        compiler_params=_params(("parallel", "arbitrary"), blocks, scratch),
        name=f"ffn_l{l}_s{s}_t{tile0}",
    )(y, mod_l, g_l, ffn_w13, ffn_w13, ffn_w2)


def _sconv_kernel(x_ref, mod_ref, g_ref, wb_ref, wc_ref, wx_ref, cw_ref, cb_ref, wo_ref, o_ref, h_ref):
    i = pl.program_id(0)
    k = pl.program_id(1)
    ((_, r),) = _unit_rows(i, TM_CONV)
    seq = jnp.where(i < N_PROMPT // TM_CONV, SEQ, DEC_SEQ)
    blk = SEQ
    n_blk = TM_CONV // blk
    row = lax.broadcasted_iota(jnp.int32, (blk, TF_CONV), 0)
    zero_row = jnp.zeros((1, TF_CONV), F32)

    def step(first, last):
        wb = wb_ref[...].astype(BF16)
        wc = wc_ref[...].astype(BF16)
        wx = wx_ref[...].astype(BF16)
        wo = wo_ref[...].astype(BF16)
        gbs, us = [], []
        for bi in range(n_blk):
            rows = slice(bi * blk, (bi + 1) * blk)
            if first:
                h_ref[rows, :] = _modulate_in(x_ref[rows, :], g_ref, mod_ref, r, 1).astype(BF16)
            h = h_ref[rows, :]
            gbs.append(_dot(h, wb))
            us.append(_dot(h, wc) * _dot(h, wx))
        for bi in range(n_blk):
            rows = slice(bi * blk, (bi + 1) * blk)
            ub = us[bi]
            t = (row + bi * blk) & (seq - 1)
            above = us[bi - 1][blk - 1:blk, :] if bi > 0 else zero_row
            below = us[bi + 1][0:1, :] if bi < n_blk - 1 else zero_row
            u_prev = jnp.where(t == 0, 0.0, jnp.where(row == 0, above, pltpu.roll(ub, 1, 0)))
            u_next = jnp.where(t == seq - 1, 0.0, jnp.where(row == blk - 1, below, pltpu.roll(ub, blk - 1, 0)))
            y = cb_ref[...] + u_prev * cw_ref[0:1, :] + ub * cw_ref[1:2, :] + u_next * cw_ref[2:3, :]
            d = _dot(gbs[bi] * y, wo)
            acc = d if first else o_ref[rows, :] + d
            o_ref[rows, :] = _gated_residual(x_ref[rows, :], acc, g_ref, mod_ref, r, 1, 1.0) if last else acc

    n_steps = SC_WIDTH // TF_CONV
    for kk in range(n_steps):
        pl.when(k == kk)(functools.partial(step, kk == 0, kk == n_steps - 1))


def _sconv_sublayer(y, mod_l, g_l, sc_w_in, sc_conv_w, sc_conv_b, sc_w_out, jc):
    tm, tf = TM_CONV, TF_CONV
    nk = SC_WIDTH // tf
    blocks = 4 * (2 * tm * D_MODEL + 4 * D_MODEL * tf)
    scratch = 2 * tm * D_MODEL
    return pl.pallas_call(
        _sconv_kernel,
        grid=(N_TOK // tm, nk),
        in_specs=[
            pl.BlockSpec((tm, D_MODEL), lambda i, k: (i, 0)),
            pl.BlockSpec(_MOD_SPEC_SHAPE, lambda i, k: (0, 0, 0)),
            pl.BlockSpec((6, D_MODEL), lambda i, k: (0, 0)),
            pl.BlockSpec((None, D_MODEL, tf), lambda i, k: (jc, 0, k)),
            pl.BlockSpec((None, D_MODEL, tf), lambda i, k: (jc, 0, k + nk)),
            pl.BlockSpec((None, D_MODEL, tf), lambda i, k: (jc, 0, k + 2 * nk)),
            pl.BlockSpec((None, CONV_WIDTH, tf), lambda i, k: (jc, 0, k)),
            pl.BlockSpec((None, 1, tf), lambda i, k: (jc, 0, k)),
            pl.BlockSpec((None, tf, D_MODEL), lambda i, k: (jc, k, 0)),
        ],
        out_specs=pl.BlockSpec((tm, D_MODEL), lambda i, k: (i, 0)),
        out_shape=jax.ShapeDtypeStruct((N_TOK, D_MODEL), F32),
        scratch_shapes=[pltpu.VMEM((tm, D_MODEL), BF16)],
        compiler_params=_params(("parallel", "arbitrary"), blocks, scratch),
        name=f"sconv_{jc}",
    )(y, mod_l, g_l, sc_w_in, sc_w_in, sc_w_in, sc_conv_w, sc_conv_b.reshape(N_C, 1, SC_WIDTH), sc_w_out)


def _inproj_kernel(x_ref, mod_ref, g_ref, w_ref, wk_ref, wg_ref, o_ref, okt_ref, og_ref, h_ref):
    i = pl.program_id(0)
    n = pl.program_id(1)
    n_main = len(PROJ_BLOCKS)

    @pl.when(n == 0)
    def _():
        for rows, r in _sub_rows(i):
            h_ref[rows, :] = _modulate_in(x_ref[rows, :], g_ref, mod_ref, r, 1).astype(BF16)
            o_ref[rows, :] = _dot_nt(h_ref[rows, :], w_ref[...])

    @pl.when((n > 0) & (n < n_main))
    def _():
        o_ref[...] = _dot_nt(h_ref[...], w_ref[...])

    @pl.when(n == n_main)
    def _():
        kt = _dot_nt(wk_ref[...], h_ref[...])
        for c in range(TM // ML_CHUNK):
            okt_ref[c] = kt[:, c * ML_CHUNK:(c + 1) * ML_CHUNK]
        w_gates = jnp.concatenate([wg_ref[...], jnp.zeros((LANES - N_GATES, D_MODEL), F32)], axis=0)
        og_ref[...] = _dot_nt(h_ref[...], w_gates)


def _inproj(y, mod_l, g_l, w_in_t, jb):
    n_main = len(PROJ_BLOCKS)
    assert PROJ_BLOCKS == tuple(b for b in range(n_main + 1) if b != KM_BLOCK)

    def w_block(i, n):
        m = jnp.minimum(n, n_main - 1)
        return (jb, m + (m >= KM_BLOCK).astype(jnp.int32), 0)

    cpt = TM // ML_CHUNK
    blocks = 4 * (TM * D_MODEL + 2 * TN_PROJ * D_MODEL + N_GATES * D_MODEL + TM * TN_PROJ
                  + ML_WIDTH * TM + TM * LANES)
    return pl.pallas_call(
        _inproj_kernel,
        grid=(N_TOK // TM, n_main + 1),
        in_specs=[
            pl.BlockSpec((TM, D_MODEL), lambda i, n: (i, 0)),
            pl.BlockSpec(_MOD_SPEC_SHAPE, lambda i, n: (0, 0, 0)),
            pl.BlockSpec((6, D_MODEL), lambda i, n: (0, 0)),
            pl.BlockSpec((None, TN_PROJ, D_MODEL), w_block),
            pl.BlockSpec((None, ML_WIDTH, D_MODEL), lambda i, n: (jb, KM_BLOCK, 0)),
            pl.BlockSpec((None, N_GATES, D_MODEL), lambda i, n: (jb, AB_MAIN // N_GATES, 0)),
        ],
        out_specs=[
            pl.BlockSpec((TM, TN_PROJ), lambda i, n: (i, jnp.minimum(n, n_main - 1))),
            pl.BlockSpec((cpt, ML_WIDTH, ML_CHUNK), lambda i, n: (i, 0, 0)),
            pl.BlockSpec((TM, LANES), lambda i, n: (i, 0)),
        ],
        out_shape=[jax.ShapeDtypeStruct((N_TOK, n_main * TN_PROJ), F32),
                   jax.ShapeDtypeStruct((N_TOK // ML_CHUNK, ML_WIDTH, ML_CHUNK), F32),
                   jax.ShapeDtypeStruct((N_TOK, LANES), F32)],
        scratch_shapes=[pltpu.VMEM((TM, D_MODEL), BF16)],
        compiler_params=_params(("parallel", "arbitrary"), blocks, 2 * TM * D_MODEL),
        name=f"inproj_{jb}",
    )(y, mod_l, g_l, w_in_t, w_in_t, w_in_t)


def _stack_entry(ref, jb, fills_stack):
    if not fills_stack:
        return ref
    for j in range(ref.shape[0]):
        if j != jb:
            ref[j] = jnp.zeros(ref.shape[1:], ref.dtype)
    return ref.at[jb]


def _stack_spec(entry_shape, jb, fills_stack, n_stack=N_AB):
    zeros = (0,) * len(entry_shape)
    if fills_stack:
        return pl.BlockSpec((None, n_stack) + entry_shape, lambda b: (b, 0) + zeros)
    return pl.BlockSpec((None, None) + entry_shape, lambda b: (b, jb) + zeros)


def _ctx_attn_kernel(q_ref, k_ref, v_ref, *rest, jb, fills_stack):
    o_ref = rest[-3]
    kc_ref = _stack_entry(rest[-2], jb, fills_stack)
    vc_ref = _stack_entry(rest[-1], jb, fills_stack)
    scale = NA_HEAD_DIM ** -0.5
    k_t = k_ref[...].T
    v_t = v_ref[...].T
    for h in range(NA_HEADS):
        sl = slice(h * NA_HEAD_DIM, (h + 1) * NA_HEAD_DIM)
        kc_ref[h] = k_t[sl, :]
        vc_ref[h] = v_t[sl, :]
        s = _dot(q_ref[:, sl] * scale, k_t[sl, :])
        p = jnp.exp(s - jnp.max(s, axis=-1, keepdims=True))
        p = p * (1.0 / jnp.sum(p, axis=-1, keepdims=True))
        o_ref[:, sl] = _dot(p, v_ref[:, sl])


def _ctx_attention(proj, jb, prev):
    cache = jax.ShapeDtypeStruct((BATCH, N_AB, NA_HEADS, NA_HEAD_DIM, SEQ), F32)
    cache_spec = _stack_spec((NA_HEADS, NA_HEAD_DIM, SEQ), jb, prev is None)
    in_specs = [pl.BlockSpec((SEQ, NA_WIDTH), lambda b, c=c: (b, c)) for c in (P_QA, P_KA, P_VA)]
    args = [proj, proj, proj]
    aliases = {}
    if prev is not None:
        in_specs += [pl.BlockSpec(memory_space=pl.ANY)] * 2
        args += list(prev)
        aliases = {3: 1, 4: 2}
    blocks = 4 * (4 * SEQ * NA_WIDTH + 2 * N_AB * NA_WIDTH * SEQ)
    return pl.pallas_call(
        functools.partial(_ctx_attn_kernel, jb=jb, fills_stack=prev is None),
        grid=(BATCH,),
        in_specs=in_specs,
        out_specs=[pl.BlockSpec((SEQ, NA_WIDTH), lambda b: (b, 0)), cache_spec, cache_spec],
        out_shape=[jax.ShapeDtypeStruct((N_PROMPT, NA_WIDTH), F32), cache, cache],
        input_output_aliases=aliases,
        compiler_params=_params(("parallel",), blocks),
        name=f"ctx_attention_{jb}",
    )(*args)


def _na_bias_kernel(rpb_ref, p_ref, e_ref):
    base = (pl.program_id(0) * NA_HEADS + pl.program_id(1)) * (N_DR * N_DC)
    shape = (GRID_W, LANES)
    lane = lax.broadcasted_iota(jnp.int32, shape, 1)
    qc = lax.broadcasted_iota(jnp.int32, shape, 0)
    kc = lane & (GRID_W - 1)
    upper = lane >= GRID_W
    cs = jnp.clip(qc - NA_WIN_W // 2, 0, GRID_W - NA_WIN_W)
    col_in = (kc >= cs) & (kc < cs + NA_WIN_W)
    dc = jnp.clip(kc - qc, -(NA_WIN_W - 1), NA_WIN_W - 1) + (NA_WIN_W - 1)
    upper_row = upper[0:1, :]
    n_even = (N_DR + 1) // 2
    group = 4
    for e0 in range(0, n_even, group):
        es = range(e0, min(e0 + group, n_even))
        accs = [jnp.zeros(shape, F32) for _ in es]
        for d in range(N_DC):
            hit = dc == d
            for a, e in enumerate(es):
                lo = rpb_ref[base + 2 * e * N_DC + d]
                hi = rpb_ref[base + (2 * e + 1) * N_DC + d] if 2 * e + 1 < N_DR else 0.0
                accs[a] = jnp.where(hit, jnp.where(upper_row, hi, lo), accs[a])
        for a, e in enumerate(es):
            e_ref[e] = jnp.where(col_in, accs[a], -jnp.inf)
    for pair in range(N_PAIR):
        e = pair // 2
        if pair % 2 == 0:
            p_ref[pair] = e_ref[e]
        else:
            p_ref[pair] = jnp.where(upper, pltpu.roll(e_ref[e + 1], GRID_W, 1), pltpu.roll(e_ref[e], GRID_W, 1))


def _na_bias(na_rpb):
    return pl.pallas_call(
        _na_bias_kernel,
        grid=(N_AB, NA_HEADS),
        in_specs=[pl.BlockSpec(memory_space=pltpu.SMEM)],
        out_specs=pl.BlockSpec((None, None, N_PAIR, GRID_W, LANES), lambda j, h: (j, h, 0, 0, 0)),
        out_shape=jax.ShapeDtypeStruct((N_AB, NA_HEADS, N_PAIR, GRID_W, LANES), F32),
        scratch_shapes=[pltpu.VMEM(((N_DR + 1) // 2, GRID_W, LANES), F32)],
        compiler_params=_params(("parallel", "parallel"), 4 * N_PAIR * GRID_W * LANES,
                                4 * ((N_DR + 1) // 2) * GRID_W * LANES),
        name="na_bias",
    )(na_rpb.reshape(-1))


def _na_window_start(r):
    return min(max(r - NA_KH // 2, 0), GRID_ROWS - NA_KH)


def _na_bias_block(p_ref, hh, qr, kr):
    rs = _na_window_start(qr)
    lo_ok = rs <= kr < rs + NA_KH
    hi_ok = rs <= kr + 1 < rs + NA_KH
    if not (lo_ok or hi_ok):
        return jnp.full((GRID_W, LANES), -jnp.inf, F32)
    dr = kr - qr + (NA_WIN_H - 1)
    assert 0 <= dr < N_PAIR
    blk = p_ref[hh, dr]
    if lo_ok and hi_ok:
        return blk
    lane = lax.broadcasted_iota(jnp.int32, (GRID_W, LANES), 1)
    keep = (lane < GRID_W) if lo_ok else (lane >= GRID_W)
    return jnp.where(keep, blk, -jnp.inf)


def _na_kernel(q_ref, k_ref, v_ref, kc_ref, vc_ref, p_ref, o_ref):
    scale = NA_HEAD_DIM ** -0.5
    for hh in range(LANES // NA_HEAD_DIM):
        sl = slice(hh * NA_HEAD_DIM, (hh + 1) * NA_HEAD_DIM)
        k_ctx_t = kc_ref[hh]
        v_ctx_t = vc_ref[hh]
        for q0 in range(0, GRID_ROWS, NA_QROWS):
            k_lo = _na_window_start(q0) // 2 * 2
            k_hi = -(-(_na_window_start(q0 + NA_QROWS - 1) + NA_KH) // 2) * 2
            q_rows = slice(q0 * GRID_W, (q0 + NA_QROWS) * GRID_W)
            k_rows = slice(k_lo * GRID_W, k_hi * GRID_W)
            bias = jnp.concatenate(
                [jnp.concatenate([_na_bias_block(p_ref, hh, qr, kr) for kr in range(k_lo, k_hi, 2)], axis=1)
                 for qr in range(q0, q0 + NA_QROWS)], axis=0)
            q = q_ref[q_rows, sl] * scale
            s_loc = _dot_nt(q, k_ref[k_rows, sl]) + bias
            s_ctx = _dot(q, k_ctx_t)
            m = jnp.maximum(jnp.max(s_loc, axis=-1, keepdims=True), jnp.max(s_ctx, axis=-1, keepdims=True))
            p_loc = jnp.exp(s_loc - m)
            p_ctx = jnp.exp(s_ctx - m)
            denom = jnp.sum(p_loc, axis=-1, keepdims=True) + jnp.sum(p_ctx, axis=-1, keepdims=True)
            o_ref[q_rows, sl] = (_dot(p_loc, v_ref[k_rows, sl]) + _dot_nt(p_ctx, v_ctx_t)) / denom


def _na_attention(proj, cache_k_t, cache_v_t, bias_pairs, jb):
    hp = LANES // NA_HEAD_DIM
    s0 = N_PROMPT // DEC_SEQ
    nb = NA_WIDTH // LANES
    ctx_spec = pl.BlockSpec((None, None, hp, NA_HEAD_DIM, PAST_LEN), lambda b, c: (b, jb, c, 0, 0))
    blocks = 4 * (4 * DEC_SEQ * LANES + 2 * hp * NA_HEAD_DIM * PAST_LEN + hp * N_PAIR * GRID_W * LANES)
    return pl.pallas_call(
        _na_kernel,
        grid=(DEC_BATCH, nb),
        in_specs=[
            pl.BlockSpec((DEC_SEQ, LANES), lambda b, c: (s0 + b, P_QA * nb + c)),
            pl.BlockSpec((DEC_SEQ, LANES), lambda b, c: (s0 + b, P_KA * nb + c)),
            pl.BlockSpec((DEC_SEQ, LANES), lambda b, c: (s0 + b, P_VA * nb + c)),
            ctx_spec, ctx_spec,
            pl.BlockSpec((None, hp, N_PAIR, GRID_W, LANES), lambda b, c: (jb, c, 0, 0, 0)),
        ],
        out_specs=pl.BlockSpec((DEC_SEQ, LANES), lambda b, c: (b, c)),
        out_shape=jax.ShapeDtypeStruct((N_SAMPLE, NA_WIDTH), F32),
        compiler_params=_params(("parallel", "parallel"), blocks),
        name=f"na_attention_{jb}",
    )(proj, proj, proj, cache_k_t, cache_v_t, bias_pairs)


def _running_max_rows(x, reverse):
    n = x.shape[0]
    ridx = lax.broadcasted_iota(jnp.int32, x.shape, 0)
    sh = 1
    while sh < n:
        if reverse:
            x = jnp.maximum(x, jnp.where(ridx < n - sh, pltpu.roll(x, n - sh, 0), -jnp.inf))
        else:
            x = jnp.maximum(x, jnp.where(ridx >= sh, pltpu.roll(x, sh, 0), -jnp.inf))
        sh *= 2
    return x


def _mlstm_kernel(*refs, n_chunks, has_state, jb, fills_stack):
    if has_state:
        (q_ref, kt_ref, v_ref, o_ref, gt_ref, gb_ref, hn_ref, c0_ref, n0_ref, m0_ref,
         out_ref, cout_ref, nout_ref, mout_ref, hdir_ref, caug_ref) = refs
    else:
        (q_ref, kt_ref, v_ref, o_ref, gt_ref, gb_ref, hn_ref, *_unused,
         out_ref, cout_ref, nout_ref, mout_ref, hdir_ref, caug_ref) = refs
    cout_ref = _stack_entry(cout_ref, jb, fills_stack)
    nout_ref = _stack_entry(nout_ref, jb, fills_stack)
    mout_ref = _stack_entry(mout_ref, jb, fills_stack)
    T = ML_CHUNK
    dh = ML_HEAD_DIM
    b = pl.program_id(0)
    row = lax.broadcasted_iota(jnp.int32, (T, T), 0)
    col = lax.broadcasted_iota(jnp.int32, (T, T), 1)
    lane = lax.broadcasted_iota(jnp.int32, (T, LANES), 1)
    lane_row = lax.broadcasted_iota(jnp.int32, (1, LANES), 1)
    is_forget = ((lane >= ML_HEADS) & (lane < 2 * ML_HEADS)) | ((lane >= 3 * ML_HEADS) & (lane < 4 * ML_HEADS))
    ones = jnp.ones((T, dh), F32)
    states = [(d, h) for d in range(2) for h in range(ML_HEADS)]

    def gate_lanes(d, h):
        gi = 2 * d * ML_HEADS + h
        return gi, gi + ML_HEADS

    m_init = []
    for d in range(2):
        m_row = jnp.zeros((1, LANES), F32)
        for h in range(ML_HEADS):
            if has_state:
                n_bcast = jnp.broadcast_to(n0_ref[d, h:h + 1, :], (dh, dh)).T
                caug_ref[d, h] = jnp.concatenate([c0_ref[d, h], n_bcast], axis=1)
                m_row = jnp.where(lane_row == gate_lanes(d, h)[1], m0_ref[(b * 2 + d) * ML_HEADS + h], m_row)
            else:
                caug_ref[d, h] = jnp.zeros((dh, 2 * dh), F32)
        m_init.append(m_row)

    def chunk_body(ci, m_rows):
        m_next = []
        for d in range(2):
            causal = (col <= row) if d == 0 else (col >= row)
            last = T - 1 if d == 0 else 0
            c = ci if d == 0 else n_chunks - 1 - ci
            rows = pl.ds(pl.multiple_of(c * T, T), T)
            mine = (lane >= (2 * d + 1) * ML_HEADS) & (lane < (2 * d + 2) * ML_HEADS)
            g = gt_ref[rows, :] + gb_ref[...]
            log_sig = jnp.minimum(g, 0.0) - jnp.log1p(jnp.exp(-jnp.abs(g)))
            g = jnp.where(is_forget, log_sig, g)
            cum = jnp.dot(causal.astype(F32), g, precision=lax.Precision.HIGHEST, preferred_element_type=F32)
            i_al = pltpu.roll(g, ML_HEADS, 1)
            cvec = jnp.where(mine, i_al - cum, 0.0)
            cum = jnp.where(mine, cum, 0.0)
            m_run = jnp.maximum(m_rows[d], _running_max_rows(cvec, reverse=(d == 1)))
            m_t = cum + m_run
            w_inter = jnp.exp(m_rows[d] - m_run)
            e_negm = jnp.exp(-m_t)
            m_run_last = m_run[last:last + 1, :]
            wi_last = w_inter[last:last + 1, :]
            w_last = jnp.exp(cvec - m_run_last)
            packed_t = jnp.where(mine, cvec, pltpu.roll(w_last, LANES - ML_HEADS, 1)).T
            for h in range(ML_HEADS):
                gi, gf = gate_lanes(d, h)
                hl = slice(h * dh, (h + 1) * dh)
                m_run_b = jnp.broadcast_to(m_run[:, gf:gf + 1], (T, T))
                e_negm_b = jnp.broadcast_to(e_negm[:, gf:gf + 1], (T, dh))
                w_d = jnp.exp(jnp.where(causal, packed_t[gf:gf + 1, :] - m_run_b, -jnp.inf))
                w_inter_b = jnp.exp(m_rows[d][:, gf:gf + 1] - m_run_b)
                qh = q_ref[rows, hl]
                kh_t = kt_ref[c, hl, :] * (dh ** -0.5)
                v_aug = jnp.concatenate([v_ref[rows, hl], ones], axis=1)
                s = _dot(qh, kh_t) * w_d
                c_aug = caug_ref[d, h]
                nd = jnp.concatenate([w_inter_b, w_inter_b], axis=1) * _dot(qh, c_aug) + _dot(s, v_aug)
                hdir_ref[d, rows, hl] = nd[:, :dh] / jnp.maximum(jnp.abs(nd[:, dh:]), e_negm_b)
                kw_t = kh_t * packed_t[gi:gi + 1, :]
                caug_ref[d, h] = wi_last[:, gf:gf + 1] * c_aug + _dot(kw_t, v_aug)
            m_next.append(m_t[last:last + 1, :])
        return tuple(m_next)

    m_fin = lax.fori_loop(0, n_chunks, chunk_body, tuple(m_init), unroll=2)
    for d, h in states:
        c_aug = caug_ref[d, h]
        cout_ref[d, h] = c_aug[:, :dh]
        nout_ref[d, h:h + 1, :] = c_aug[:, dh:].T[0:1, :]
    m_heads = [pltpu.roll(m_fin[d], LANES - gate_lanes(d, 0)[1], 1) for d in range(2)]
    mout_ref[...] = jnp.concatenate(m_heads + [jnp.zeros((COND_ROWS - 2, LANES), F32)], axis=0)

    for h in range(ML_HEADS):
        hl = slice(h * dh, (h + 1) * dh)
        h_sum = hdir_ref[0, :, hl] + hdir_ref[1, :, hl]
        out_ref[:, hl] = jax.nn.sigmoid(o_ref[:, hl]) * (_rms_scale(h_sum) * hn_ref[:, hl])


def _mlstm(proj, keys_t, gates, gate_b_row, hnorm_row, seq_len, n_seq, row0, n_stack=1, jb=0, prev=None,
           state=None):
    has_state = state is not None
    r0 = row0 // seq_len
    nb = TN_PROJ // ML_WIDTH
    n_chunks = seq_len // ML_CHUNK
    in_specs = [
        pl.BlockSpec((seq_len, ML_WIDTH), lambda b: (r0 + b, P_QM * nb)),
        pl.BlockSpec((n_chunks, ML_WIDTH, ML_CHUNK), lambda b: (r0 + b, 0, 0)),
        pl.BlockSpec((seq_len, ML_WIDTH), lambda b: (r0 + b, P_VM * nb)),
        pl.BlockSpec((seq_len, ML_WIDTH), lambda b: (r0 + b, P_OM * nb)),
        pl.BlockSpec((seq_len, LANES), lambda b: (r0 + b, 0)),
        pl.BlockSpec((1, LANES), lambda b: (0, 0)),
        pl.BlockSpec((1, ML_WIDTH), lambda b: (0, 0)),
    ]
    args = [proj, keys_t, proj, proj, gates, gate_b_row, hnorm_row]
    aliases = {}
    if has_state:
        st_c, st_n, st_m = state
        in_specs += [
            pl.BlockSpec((None, 2, ML_HEADS, ML_HEAD_DIM, ML_HEAD_DIM), lambda b: (b, 0, 0, 0, 0)),
            pl.BlockSpec((None, 2, ML_HEADS, ML_HEAD_DIM), lambda b: (b, 0, 0, 0)),
            pl.BlockSpec(memory_space=pltpu.SMEM),
        ]
        args += [st_c, st_n, st_m.reshape(-1)]
    elif prev is not None:
        aliases = {len(args) + k: 1 + k for k in range(3)}
        in_specs += [pl.BlockSpec(memory_space=pl.ANY)] * 3
        args += list(prev)
    fills_stack = prev is None
    c_bytes = 2 * ML_HEADS * ML_HEAD_DIM * ML_HEAD_DIM
    blocks = 4 * (5 * seq_len * ML_WIDTH + seq_len * LANES + (1 + n_stack) * c_bytes)
    scratch = 4 * (2 * seq_len * ML_WIDTH + 2 * c_bytes)
    return pl.pallas_call(
        functools.partial(_mlstm_kernel, n_chunks=n_chunks, has_state=has_state, jb=jb, fills_stack=fills_stack),
        grid=(n_seq,),
        in_specs=in_specs,
        out_specs=[
            pl.BlockSpec((seq_len, ML_WIDTH), lambda b: (b, 0)),
            _stack_spec((2, ML_HEADS, ML_HEAD_DIM, ML_HEAD_DIM), jb, fills_stack, n_stack),
            _stack_spec((2, ML_HEADS, ML_HEAD_DIM), jb, fills_stack, n_stack),
            _stack_spec((COND_ROWS, LANES), jb, fills_stack, n_stack),
        ],
        out_shape=[
            jax.ShapeDtypeStruct((n_seq * seq_len, ML_WIDTH), F32),
            jax.ShapeDtypeStruct((n_seq, n_stack, 2, ML_HEADS, ML_HEAD_DIM, ML_HEAD_DIM), F32),
            jax.ShapeDtypeStruct((n_seq, n_stack, 2, ML_HEADS, ML_HEAD_DIM), F32),
            jax.ShapeDtypeStruct((n_seq, n_stack, COND_ROWS, LANES), F32),
        ],
        scratch_shapes=[pltpu.VMEM((2, seq_len, ML_WIDTH), F32),
                        pltpu.VMEM((2, ML_HEADS, ML_HEAD_DIM, 2 * ML_HEAD_DIM), F32)],
        input_output_aliases=aliases,
        compiler_params=_params(("parallel",), blocks, scratch),
        name=f"mlstm_{'latent' if has_state else 'context'}_{jb}",
    )(*args)


def _outproj_kernel(x_ref, mod_ref, g_ref, ap_ref, mp_ref, as_ref, ms_ref, wa_ref, wm_ref, o_ref):
    i = pl.program_id(0)

    def finish(a_ref, m_ref):
        wa = wa_ref[...].astype(BF16)
        wm = wm_ref[...].astype(BF16)
        for rows, r in _sub_rows(i, TM_OUT):
            out = _dot(a_ref[rows, :], wa) + _dot(m_ref[rows, :], wm)
            o_ref[rows, :] = _gated_residual(x_ref[rows, :], out, g_ref, mod_ref, r, 1, 1.0)

    is_prompt = i < N_PROMPT // TM_OUT
    pl.when(is_prompt)(functools.partial(finish, ap_ref, mp_ref))
    pl.when(jnp.logical_not(is_prompt))(functools.partial(finish, as_ref, ms_ref))


def _outproj(y, mod_l, g_l, a_p, m_p, a_s, m_s, ab_w_out, jb):
    tm = TM_OUT
    n_p = N_PROMPT // tm
    n_s = N_SAMPLE // tm
    p_spec = pl.BlockSpec((tm, NA_WIDTH), lambda i: (jnp.minimum(i, n_p - 1), 0))
    s_spec = pl.BlockSpec((tm, NA_WIDTH), lambda i: (jnp.clip(i - n_p, 0, n_s - 1), 0))
    blocks = 4 * (2 * tm * D_MODEL + 4 * tm * NA_WIDTH + D_MODEL * D_MODEL)
    return pl.pallas_call(
        _outproj_kernel,
        grid=(N_TOK // tm,),
        in_specs=[
            pl.BlockSpec((tm, D_MODEL), lambda i: (i, 0)),
            pl.BlockSpec(_MOD_SPEC_SHAPE, lambda i: (0, 0, 0)),
            pl.BlockSpec((6, D_MODEL), lambda i: (0, 0)),
            p_spec, p_spec, s_spec, s_spec,
            pl.BlockSpec((None, NA_WIDTH, D_MODEL), lambda i: (jb, 0, 0)),
            pl.BlockSpec((None, ML_WIDTH, D_MODEL), lambda i: (jb, 1, 0)),
        ],
        out_specs=pl.BlockSpec((tm, D_MODEL), lambda i: (i, 0)),
        out_shape=jax.ShapeDtypeStruct((N_TOK, D_MODEL), F32),
        compiler_params=_params(("parallel",), blocks),
        name=f"outproj_{jb}",
    )(y, mod_l, g_l, a_p, m_p, a_s, m_s, ab_w_out, ab_w_out)


def kernel(x_prompt, x_sample, cache_k, cache_v, state_C, state_n, state_m, c, c_ctx, ada_w, ada_b, norm_g,
           ffn_w13, ffn_w2, ab_w_in, ab_w_out, na_rpb, ml_gate_b, ml_hnorm, sc_w_in, sc_conv_w, sc_conv_b,
           sc_w_out):
    assert NA_WIDTH == ML_WIDTH == TN_PROJ and AB_MAIN % N_GATES == 0
    assert math.frexp(NA_HEAD_DIM ** -0.5)[0] == 0.5
    assert N_PROMPT % TM == 0 and N_SAMPLE % TM == 0 and TM % COND_UNIT == 0
    cond = jnp.concatenate([c_ctx[None, :], c, jnp.zeros((COND_ROWS - 1 - DEC_BATCH, D_MODEL), F32)], axis=0)
    mod = _modulation(cond, ada_w, ada_b)
    bias_pairs = _na_bias(na_rpb)
    y = jnp.concatenate([x_prompt.reshape(N_PROMPT, D_MODEL), x_sample.reshape(N_SAMPLE, D_MODEL)], axis=0)
    w_in_t = jnp.swapaxes(ab_w_in, 1, 2)
    cache_k_t = jnp.swapaxes(cache_k, 3, 4)
    cache_v_t = jnp.swapaxes(cache_v, 3, 4)

    kv_new = None
    st_new = None
    for l in range(DEPTH):
        mod_l = mod[l]
        g_l = norm_g[l].reshape(6, D_MODEL)
        y = _ffn_sublayer(y, mod_l, g_l, ffn_w13, ffn_w2, l, 0)
        j = l // 2
        if l % 2 == 0:
            proj, keys_t, gates = _inproj(y, mod_l, g_l, w_in_t, j)
            gate_b_row = jnp.pad(ml_gate_b[j].reshape(1, N_GATES), ((0, 0), (0, LANES - N_GATES)))
            hnorm_row = ml_hnorm[j].reshape(1, ML_WIDTH)
            a_p, *kv_new = _ctx_attention(proj, j, kv_new)
            m_p, *st_new = _mlstm(proj, keys_t, gates, gate_b_row, hnorm_row, SEQ, BATCH, 0,
                                  n_stack=N_AB, jb=j, prev=st_new)
            a_s = _na_attention(proj, cache_k_t, cache_v_t, bias_pairs, j)
            m_s, _, _, _ = _mlstm(proj, keys_t, gates, gate_b_row, hnorm_row, DEC_SEQ, DEC_BATCH, N_PROMPT,
                                  state=(state_C[:, j], state_n[:, j], state_m[:, j]))
            y = _outproj(y, mod_l, g_l, a_p, m_p, a_s, m_s, ab_w_out, j)
        else:
            y = _sconv_sublayer(y, mod_l, g_l, sc_w_in, sc_conv_w, sc_conv_b, sc_w_out, j)
        if l < DEPTH - 1:
            y = _ffn_sublayer(y, mod_l, g_l, ffn_w13, ffn_w2, l, 1)

    p_tiles = N_PROMPT // TM
    y_p = _ffn_sublayer(y, mod_l, g_l, ffn_w13, ffn_w2, DEPTH - 1, 1, 0, p_tiles)
    y_s = _ffn_sublayer(y, mod_l, g_l, ffn_w13, ffn_w2, DEPTH - 1, 1, p_tiles, N_TOK // TM - p_tiles)
    y_p = y_p.reshape(BATCH, SEQ, D_MODEL)
    y_s = y_s.reshape(DEC_BATCH, DEC_SEQ, D_MODEL)
    new_c, new_n, m_rows = st_new
    new_m = m_rows[:, :, :2, :ML_HEADS]
    return (y_p, y_s, jnp.swapaxes(kv_new[0], 3, 4), jnp.swapaxes(kv_new[1], 3, 4), new_c, new_n, new_m)
```

```python
import functools
import math

import jax
import jax.numpy as jnp
from jax import lax
from jax.experimental import pallas as pl
from jax.experimental.pallas import tpu as pltpu

D_MODEL = 1024
BATCH = 16
SEQ = 256
DEPTH = 4
DEC_BATCH = 2
DEC_SEQ = 1024
PAST_LEN = 256
GRID_W = 64
NA_HEADS = 8
NA_HEAD_DIM = 64
NA_WIN_H = 8
NA_WIN_W = 16
ML_HEADS = 4
ML_HEAD_DIM = 128
ML_CHUNK = 128
CONV_WIDTH = 3
D_FF = 2816
N_MOD = 9
EPS = 1e-6
N_AB = (DEPTH + 1) // 2
N_C = DEPTH // 2
NA_WIDTH = NA_HEADS * NA_HEAD_DIM
ML_WIDTH = ML_HEADS * ML_HEAD_DIM
AB_MAIN = 3 * NA_WIDTH + 4 * ML_WIDTH
N_GATES = 4 * ML_HEADS
AB_IN = AB_MAIN + N_GATES
SC_WIDTH = D_MODEL

N_PROMPT = BATCH * SEQ
N_SAMPLE = DEC_BATCH * DEC_SEQ
N_TOK = N_PROMPT + N_SAMPLE
GRID_ROWS = DEC_SEQ // GRID_W
NA_KH = min(NA_WIN_H, GRID_ROWS)
N_DR = 2 * NA_WIN_H - 1
N_DC = 2 * NA_WIN_W - 1
N_PAIR = N_DR - 1

LANES = 128
COND_ROWS = 8
VMEM_CAP = 60 * 1024 * 1024
TEMP_BYTES = 16 * 1024 * 1024

COND_UNIT = DEC_SEQ
TM = 2048
SUB_ROWS = 512
TM_OUT = 1024
TM_CONV = COND_UNIT
TF_CONV = 512
TF = 256
TN_PROJ = 512
TN_MOD = 1536
NA_QROWS = 4
ML_SEQS_PER_STEP = 2

PROJ_BLOCKS = (0, 1, 2, 3, 5, 6)
KM_BLOCK = 4
P_QA, P_KA, P_VA, P_QM, P_VM, P_OM = range(6)

F32 = jnp.float32
BF16 = jnp.bfloat16


def _vmem_limit(block_bytes, scratch_bytes):
    assert 2 * block_bytes + scratch_bytes <= VMEM_CAP
    return VMEM_CAP


def _params(semantics, block_bytes, scratch_bytes=0):
    return pltpu.CompilerParams(dimension_semantics=semantics,
                                vmem_limit_bytes=_vmem_limit(block_bytes, scratch_bytes))


def _unit_rows(i, tm=TM):
    first_latent = N_PROMPT // COND_UNIT
    units = tm // COND_UNIT
    return [(slice(u * COND_UNIT, (u + 1) * COND_UNIT), jnp.maximum(i * units + u - (first_latent - 1), 0))
            for u in range(units)]


def _sub_rows(i, tm=TM):
    per_unit = COND_UNIT // SUB_ROWS
    return [(slice(rows.start + s * SUB_ROWS, rows.start + (s + 1) * SUB_ROWS), r)
            for rows, r in _unit_rows(i, tm) for s in range(per_unit)]


def _rms_scale(x):
    return x * lax.rsqrt(jnp.mean(x * x, axis=-1, keepdims=True) + EPS)


def _modulate_in(x, g_ref, mod_ref, r, j):
    gain = g_ref[2 * j:2 * j + 1, :] * (1.0 + mod_ref[r, 3 * j + 1:3 * j + 2, :])
    return _rms_scale(x) * gain + mod_ref[r, 3 * j:3 * j + 1, :]


def _gated_residual(x, out, g_ref, mod_ref, r, j, weight):
    gain = (weight * mod_ref[r, 3 * j + 2:3 * j + 3, :]) * g_ref[2 * j + 1:2 * j + 2, :]
    return x + _rms_scale(out) * gain


_MOD_SPEC_SHAPE = (COND_ROWS, N_MOD, D_MODEL)


def _dot(a, b):
    return jnp.dot(a.astype(BF16), b.astype(BF16), preferred_element_type=F32)


def _dot_nt(a, b):
    return lax.dot_general(a.astype(BF16), b.astype(BF16), (((1,), (1,)), ((), ())),
                           preferred_element_type=F32)


def _mod_kernel(c_ref, w_ref, b_ref, o_ref):
    c = c_ref[...]
    s = c * jax.nn.sigmoid(c)
    o_ref[...] = _dot(s, w_ref[...]) + b_ref[...]


def _modulation(cond, ada_w, ada_b):
    n = N_MOD * D_MODEL
    out = pl.pallas_call(
        _mod_kernel,
        grid=(DEPTH, n // TN_MOD),
        in_specs=[
            pl.BlockSpec((COND_ROWS, D_MODEL), lambda l, t: (0, 0)),
            pl.BlockSpec((None, D_MODEL, TN_MOD), lambda l, t: (l, 0, t)),
            pl.BlockSpec((None, 1, TN_MOD), lambda l, t: (l, 0, t)),
        ],
        out_specs=pl.BlockSpec((None, COND_ROWS, TN_MOD), lambda l, t: (l, 0, t)),
        out_shape=jax.ShapeDtypeStruct((DEPTH, COND_ROWS, n), F32),
        compiler_params=_params(("parallel", "parallel"), 4 * D_MODEL * TN_MOD),
        name="modulation",
    )(cond, ada_w, ada_b.reshape(DEPTH, 1, n))
    return out.reshape(DEPTH, COND_ROWS, N_MOD, D_MODEL)


def _ffn_kernel(x_ref, mod_ref, g_ref, wa_ref, wg_ref, w2_ref, o_ref, h_ref, *, j, tile0):
    i = pl.program_id(0) + tile0
    k = pl.program_id(1)
    last = pl.num_programs(1) - 1

    def hidden_step(rows, init):
        h = h_ref[rows, :]
        a = _dot(h, wa_ref[...])
        g = _dot(h, wg_ref[...])
        d = _dot((g * jax.nn.sigmoid(g)) * a, w2_ref[...])
        if init:
            o_ref[rows, :] = d
        else:
            o_ref[rows, :] += d

    @pl.when(k == 0)
    def _():
        for rows, r in _sub_rows(i):
            h_ref[rows, :] = _modulate_in(x_ref[rows, :], g_ref, mod_ref, r, j).astype(BF16)
            hidden_step(rows, init=True)

    @pl.when((k > 0) & (k < last))
    def _():
        hidden_step(slice(None), init=False)

    @pl.when(k == last)
    def _():
        for rows, r in _sub_rows(i):
            hidden_step(rows, init=False)
            o_ref[rows, :] = _gated_residual(x_ref[rows, :], o_ref[rows, :], g_ref, mod_ref, r, j, 0.5)


def _ffn_sublayer(y, mod_l, g_l, ffn_w13, ffn_w2, l, s, tile0=0, n_tiles=N_TOK // TM):
    j = 2 * s
    nk = D_FF // TF
    blocks = 4 * (2 * TM * D_MODEL + 3 * D_MODEL * TF)
    scratch = 2 * TM * D_MODEL
    return pl.pallas_call(
        functools.partial(_ffn_kernel, j=j, tile0=tile0),
        grid=(n_tiles, nk),
        in_specs=[
            pl.BlockSpec((TM, D_MODEL), lambda i, k: (tile0 + i, 0)),
            pl.BlockSpec(_MOD_SPEC_SHAPE, lambda i, k: (0, 0, 0)),
            pl.BlockSpec((6, D_MODEL), lambda i, k: (0, 0)),
            pl.BlockSpec((None, None, D_MODEL, TF), lambda i, k: (l, s, 0, k)),
            pl.BlockSpec((None, None, D_MODEL, TF), lambda i, k: (l, s, 0, k + nk)),
            pl.BlockSpec((None, None, TF, D_MODEL), lambda i, k: (l, s, k, 0)),
        ],
        out_specs=pl.BlockSpec((TM, D_MODEL), lambda i, k: (i, 0)),
        out_shape=jax.ShapeDtypeStruct((n_tiles * TM, D_MODEL), F32),
        scratch_shapes=[pltpu.VMEM((TM, D_MODEL), BF16)],
        compiler_params=_params(("parallel", "arbitrary"), blocks, scratch),
        name=f"ffn_l{l}_s{s}_t{tile0}",
    )(y, mod_l, g_l, ffn_w13, ffn_w13, ffn_w2)


def _sconv_kernel(x_ref, mod_ref, g_ref, wb_ref, wc_ref, wx_ref, cw_ref, cb_ref, wo_ref, o_ref, h_ref):
    i = pl.program_id(0)
    k = pl.program_id(1)
    ((_, r),) = _unit_rows(i, TM_CONV)
    seq = jnp.where(i < N_PROMPT // TM_CONV, SEQ, DEC_SEQ)
    blk = SEQ
    n_blk = TM_CONV // blk
    row = lax.broadcasted_iota(jnp.int32, (blk, TF_CONV), 0)
    zero_row = jnp.zeros((1, TF_CONV), F32)

    def step(first, last):
        wb = wb_ref[...].astype(BF16)
        wc = wc_ref[...].astype(BF16)
        wx = wx_ref[...].astype(BF16)
        wo = wo_ref[...].astype(BF16)
        gbs, us = [], []
        for bi in range(n_blk):
            rows = slice(bi * blk, (bi + 1) * blk)
            if first:
                h_ref[rows, :] = _modulate_in(x_ref[rows, :], g_ref, mod_ref, r, 1).astype(BF16)
            h = h_ref[rows, :]
            gbs.append(_dot(h, wb))
            us.append(_dot(h, wc) * _dot(h, wx))
        for bi in range(n_blk):
            rows = slice(bi * blk, (bi + 1) * blk)
            ub = us[bi]
            t = (row + bi * blk) & (seq - 1)
            above = us[bi - 1][blk - 1:blk, :] if bi > 0 else zero_row
            below = us[bi + 1][0:1, :] if bi < n_blk - 1 else zero_row
            u_prev = jnp.where(t == 0, 0.0, jnp.where(row == 0, above, pltpu.roll(ub, 1, 0)))
            u_next = jnp.where(t == seq - 1, 0.0, jnp.where(row == blk - 1, below, pltpu.roll(ub, blk - 1, 0)))
            y = cb_ref[...] + u_prev * cw_ref[0:1, :] + ub * cw_ref[1:2, :] + u_next * cw_ref[2:3, :]
            d = _dot(gbs[bi] * y, wo)
            acc = d if first else o_ref[rows, :] + d
            o_ref[rows, :] = _gated_residual(x_ref[rows, :], acc, g_ref, mod_ref, r, 1, 1.0) if last else acc

    n_steps = SC_WIDTH // TF_CONV
    for kk in range(n_steps):
        pl.when(k == kk)(functools.partial(step, kk == 0, kk == n_steps - 1))


def _sconv_sublayer(y, mod_l, g_l, sc_w_in, sc_conv_w, sc_conv_b, sc_w_out, jc):
    tm, tf = TM_CONV, TF_CONV
    nk = SC_WIDTH // tf
    blocks = 4 * (2 * tm * D_MODEL + 4 * D_MODEL * tf)
    scratch = 2 * tm * D_MODEL
    return pl.pallas_call(
        _sconv_kernel,
        grid=(N_TOK // tm, nk),
        in_specs=[
            pl.BlockSpec((tm, D_MODEL), lambda i, k: (i, 0)),
            pl.BlockSpec(_MOD_SPEC_SHAPE, lambda i, k: (0, 0, 0)),
            pl.BlockSpec((6, D_MODEL), lambda i, k: (0, 0)),
            pl.BlockSpec((None, D_MODEL, tf), lambda i, k: (jc, 0, k)),
            pl.BlockSpec((None, D_MODEL, tf), lambda i, k: (jc, 0, k + nk)),
            pl.BlockSpec((None, D_MODEL, tf), lambda i, k: (jc, 0, k + 2 * nk)),
            pl.BlockSpec((None, CONV_WIDTH, tf), lambda i, k: (jc, 0, k)),
            pl.BlockSpec((None, 1, tf), lambda i, k: (jc, 0, k)),
            pl.BlockSpec((None, tf, D_MODEL), lambda i, k: (jc, k, 0)),
        ],
        out_specs=pl.BlockSpec((tm, D_MODEL), lambda i, k: (i, 0)),
        out_shape=jax.ShapeDtypeStruct((N_TOK, D_MODEL), F32),
        scratch_shapes=[pltpu.VMEM((tm, D_MODEL), BF16)],
        compiler_params=_params(("parallel", "arbitrary"), blocks, scratch),
        name=f"sconv_{jc}",
    )(y, mod_l, g_l, sc_w_in, sc_w_in, sc_w_in, sc_conv_w, sc_conv_b.reshape(N_C, 1, SC_WIDTH), sc_w_out)


def _inproj_kernel(x_ref, mod_ref, g_ref, w_ref, wk_ref, wg_ref, o_ref, okt_ref, og_ref, h_ref):
    i = pl.program_id(0)
    n = pl.program_id(1)
    n_main = len(PROJ_BLOCKS)

    @pl.when(n == 0)
    def _():
        for rows, r in _sub_rows(i):
            h_ref[rows, :] = _modulate_in(x_ref[rows, :], g_ref, mod_ref, r, 1).astype(BF16)
            o_ref[rows, :] = _dot_nt(h_ref[rows, :], w_ref[...])

    @pl.when((n > 0) & (n < n_main))
    def _():
        o_ref[...] = _dot_nt(h_ref[...], w_ref[...])

    @pl.when(n == n_main)
    def _():
        kt = _dot_nt(wk_ref[...], h_ref[...])
        for c in range(TM // ML_CHUNK):
            okt_ref[c] = kt[:, c * ML_CHUNK:(c + 1) * ML_CHUNK]
        w_gates = jnp.concatenate([wg_ref[...], jnp.zeros((LANES - N_GATES, D_MODEL), F32)], axis=0)
        og_ref[...] = _dot_nt(h_ref[...], w_gates)


def _inproj(y, mod_l, g_l, w_in_t, jb):
    n_main = len(PROJ_BLOCKS)
    assert PROJ_BLOCKS == tuple(b for b in range(n_main + 1) if b != KM_BLOCK)

    def w_block(i, n):
        m = jnp.minimum(n, n_main - 1)
        return (jb, m + (m >= KM_BLOCK).astype(jnp.int32), 0)

    cpt = TM // ML_CHUNK
    blocks = 4 * (TM * D_MODEL + 2 * TN_PROJ * D_MODEL + N_GATES * D_MODEL + TM * TN_PROJ
                  + ML_WIDTH * TM + TM * LANES)
    return pl.pallas_call(
        _inproj_kernel,
        grid=(N_TOK // TM, n_main + 1),
        in_specs=[
            pl.BlockSpec((TM, D_MODEL), lambda i, n: (i, 0)),
            pl.BlockSpec(_MOD_SPEC_SHAPE, lambda i, n: (0, 0, 0)),
            pl.BlockSpec((6, D_MODEL), lambda i, n: (0, 0)),
            pl.BlockSpec((None, TN_PROJ, D_MODEL), w_block),
            pl.BlockSpec((None, ML_WIDTH, D_MODEL), lambda i, n: (jb, KM_BLOCK, 0)),
            pl.BlockSpec((None, N_GATES, D_MODEL), lambda i, n: (jb, AB_MAIN // N_GATES, 0)),
        ],
        out_specs=[
            pl.BlockSpec((TM, TN_PROJ), lambda i, n: (i, jnp.minimum(n, n_main - 1))),
            pl.BlockSpec((cpt, ML_WIDTH, ML_CHUNK), lambda i, n: (i, 0, 0)),
            pl.BlockSpec((TM, LANES), lambda i, n: (i, 0)),
        ],
        out_shape=[jax.ShapeDtypeStruct((N_TOK, n_main * TN_PROJ), F32),
                   jax.ShapeDtypeStruct((N_TOK // ML_CHUNK, ML_WIDTH, ML_CHUNK), F32),
                   jax.ShapeDtypeStruct((N_TOK, LANES), F32)],
        scratch_shapes=[pltpu.VMEM((TM, D_MODEL), BF16)],
        compiler_params=_params(("parallel", "arbitrary"), blocks, 2 * TM * D_MODEL),
        name=f"inproj_{jb}",
    )(y, mod_l, g_l, w_in_t, w_in_t, w_in_t)


def _stack_entry(ref, jb, fills_stack):
    if not fills_stack:
        return ref
    for j in range(ref.shape[0]):
        if j != jb:
            ref[j] = jnp.zeros(ref.shape[1:], ref.dtype)
    return ref.at[jb]


def _stack_spec(entry_shape, jb, fills_stack, n_stack=N_AB, lead=None):
    zeros = (0,) * len(entry_shape)
    if fills_stack:
        return pl.BlockSpec((lead, n_stack) + entry_shape, lambda b: (b, 0) + zeros)
    return pl.BlockSpec((lead, None) + entry_shape, lambda b: (b, jb) + zeros)


def _ctx_attn_kernel(q_ref, k_ref, v_ref, *rest, jb, fills_stack):
    o_ref = rest[-3]
    kc_ref = _stack_entry(rest[-2], jb, fills_stack)
    vc_ref = _stack_entry(rest[-1], jb, fills_stack)
    scale = NA_HEAD_DIM ** -0.5
    k_t = k_ref[...].T
    v_t = v_ref[...].T
    for h in range(NA_HEADS):
        sl = slice(h * NA_HEAD_DIM, (h + 1) * NA_HEAD_DIM)
        kc_ref[h] = k_t[sl, :]
        vc_ref[h] = v_t[sl, :]
    lane = lax.broadcasted_iota(jnp.int32, (SEQ, LANES), 1)
    for hp in range(NA_WIDTH // LANES):
        pair = slice(hp * LANES, (hp + 1) * LANES)
        q_pair = q_ref[:, pair] * scale
        outs = []
        for hh in range(LANES // NA_HEAD_DIM):
            own = (lane >= hh * NA_HEAD_DIM) & (lane < (hh + 1) * NA_HEAD_DIM)
            s = _dot(jnp.where(own, q_pair, 0.0), k_t[pair, :])
            p = jnp.exp(s - jnp.max(s, axis=-1, keepdims=True))
            p = p * (1.0 / jnp.sum(p, axis=-1, keepdims=True))
            outs.append(_dot(p, v_ref[:, pair]))
        o_ref[:, pair] = jnp.where(lane < NA_HEAD_DIM, outs[0], outs[1]).astype(o_ref.dtype)


def _ctx_attention(proj, jb, prev):
    cache = jax.ShapeDtypeStruct((BATCH, N_AB, NA_HEADS, NA_HEAD_DIM, SEQ), F32)
    cache_spec = _stack_spec((NA_HEADS, NA_HEAD_DIM, SEQ), jb, prev is None)
    in_specs = [pl.BlockSpec((SEQ, NA_WIDTH), lambda b, c=c: (b, c)) for c in (P_QA, P_KA, P_VA)]
    args = [proj, proj, proj]
    aliases = {}
    if prev is not None:
        in_specs += [pl.BlockSpec(memory_space=pl.ANY)] * 2
        args += list(prev)
        aliases = {3: 1, 4: 2}
    blocks = 4 * (4 * SEQ * NA_WIDTH + 2 * N_AB * NA_WIDTH * SEQ)
    return pl.pallas_call(
        functools.partial(_ctx_attn_kernel, jb=jb, fills_stack=prev is None),
        grid=(BATCH,),
        in_specs=in_specs,
        out_specs=[pl.BlockSpec((SEQ, NA_WIDTH), lambda b: (b, 0)), cache_spec, cache_spec],
        out_shape=[jax.ShapeDtypeStruct((N_PROMPT, NA_WIDTH), BF16), cache, cache],
        input_output_aliases=aliases,
        compiler_params=_params(("parallel",), blocks),
        name=f"ctx_attention_{jb}",
    )(*args)


def _na_bias_kernel(rpb_ref, p_ref, e_ref):
    base = (pl.program_id(0) * NA_HEADS + pl.program_id(1)) * (N_DR * N_DC)
    shape = (GRID_W, LANES)
    lane = lax.broadcasted_iota(jnp.int32, shape, 1)
    qc = lax.broadcasted_iota(jnp.int32, shape, 0)
    kc = lane & (GRID_W - 1)
    upper = lane >= GRID_W
    cs = jnp.clip(qc - NA_WIN_W // 2, 0, GRID_W - NA_WIN_W)
    col_in = (kc >= cs) & (kc < cs + NA_WIN_W)
    dc = jnp.clip(kc - qc, -(NA_WIN_W - 1), NA_WIN_W - 1) + (NA_WIN_W - 1)
    upper_row = upper[0:1, :]
    n_even = (N_DR + 1) // 2
    group = 4
    for e0 in range(0, n_even, group):
        es = range(e0, min(e0 + group, n_even))
        accs = [jnp.zeros(shape, F32) for _ in es]
        for d in range(N_DC):
            hit = dc == d
            for a, e in enumerate(es):
                lo = rpb_ref[base + 2 * e * N_DC + d]
                hi = rpb_ref[base + (2 * e + 1) * N_DC + d] if 2 * e + 1 < N_DR else 0.0
                accs[a] = jnp.where(hit, jnp.where(upper_row, hi, lo), accs[a])
        for a, e in enumerate(es):
            e_ref[e] = jnp.where(col_in, accs[a], -jnp.inf)
    for pair in range(N_PAIR):
        e = pair // 2
        if pair % 2 == 0:
            p_ref[pair] = e_ref[e]
        else:
            p_ref[pair] = jnp.where(upper, pltpu.roll(e_ref[e + 1], GRID_W, 1), pltpu.roll(e_ref[e], GRID_W, 1))


def _na_bias(na_rpb):
    return pl.pallas_call(
        _na_bias_kernel,
        grid=(N_AB, NA_HEADS),
        in_specs=[pl.BlockSpec(memory_space=pltpu.SMEM)],
        out_specs=pl.BlockSpec((None, None, N_PAIR, GRID_W, LANES), lambda j, h: (j, h, 0, 0, 0)),
        out_shape=jax.ShapeDtypeStruct((N_AB, NA_HEADS, N_PAIR, GRID_W, LANES), F32),
        scratch_shapes=[pltpu.VMEM(((N_DR + 1) // 2, GRID_W, LANES), F32)],
        compiler_params=_params(("parallel", "parallel"), 4 * N_PAIR * GRID_W * LANES,
                                4 * ((N_DR + 1) // 2) * GRID_W * LANES),
        name="na_bias",
    )(na_rpb.reshape(-1))


def _na_window_start(r):
    return min(max(r - NA_KH // 2, 0), GRID_ROWS - NA_KH)


def _na_bias_block(p_ref, hh, qr, kr):
    rs = _na_window_start(qr)
    lo_ok = rs <= kr < rs + NA_KH
    hi_ok = rs <= kr + 1 < rs + NA_KH
    if not (lo_ok or hi_ok):
        return jnp.full((GRID_W, LANES), -jnp.inf, F32)
    dr = kr - qr + (NA_WIN_H - 1)
    assert 0 <= dr < N_PAIR
    blk = p_ref[hh, dr]
    if lo_ok and hi_ok:
        return blk
    lane = lax.broadcasted_iota(jnp.int32, (GRID_W, LANES), 1)
    keep = (lane < GRID_W) if lo_ok else (lane >= GRID_W)
    return jnp.where(keep, blk, -jnp.inf)


def _na_kernel(q_ref, k_ref, v_ref, kc_ref, vc_ref, p_ref, o_ref):
    scale = NA_HEAD_DIM ** -0.5
    n_q = NA_QROWS * GRID_W
    lane = lax.broadcasted_iota(jnp.int32, (n_q, LANES), 1)
    k_ctx_t = kc_ref[...].reshape(LANES, PAST_LEN)
    v_ctx_t = vc_ref[...].reshape(LANES, PAST_LEN)
    for q0 in range(0, GRID_ROWS, NA_QROWS):
        k_lo = _na_window_start(q0) // 2 * 2
        k_hi = -(-(_na_window_start(q0 + NA_QROWS - 1) + NA_KH) // 2) * 2
        q_rows = slice(q0 * GRID_W, (q0 + NA_QROWS) * GRID_W)
        k_rows = slice(k_lo * GRID_W, k_hi * GRID_W)
        q_pair = q_ref[q_rows, :] * scale
        outs = []
        for hh in range(LANES // NA_HEAD_DIM):
            bias = jnp.concatenate(
                [jnp.concatenate([_na_bias_block(p_ref, hh, qr, kr) for kr in range(k_lo, k_hi, 2)], axis=1)
                 for qr in range(q0, q0 + NA_QROWS)], axis=0)
            own = (lane >= hh * NA_HEAD_DIM) & (lane < (hh + 1) * NA_HEAD_DIM)
            q = jnp.where(own, q_pair, 0.0)
            s_loc = _dot_nt(q, k_ref[k_rows, :]) + bias
            s_ctx = _dot(q, k_ctx_t)
            m = jnp.maximum(jnp.max(s_loc, axis=-1, keepdims=True), jnp.max(s_ctx, axis=-1, keepdims=True))
            p_loc = jnp.exp(s_loc - m)
            p_ctx = jnp.exp(s_ctx - m)
            denom = jnp.sum(p_loc, axis=-1, keepdims=True) + jnp.sum(p_ctx, axis=-1, keepdims=True)
            outs.append((_dot(p_loc, v_ref[k_rows, :]) + _dot_nt(p_ctx, v_ctx_t)) / denom)
        o_ref[q_rows, :] = jnp.where(lane < NA_HEAD_DIM, outs[0], outs[1]).astype(o_ref.dtype)


def _na_attention(proj, cache_k_t, cache_v_t, bias_pairs, jb):
    hp = LANES // NA_HEAD_DIM
    s0 = N_PROMPT // DEC_SEQ
    nb = NA_WIDTH // LANES
    ctx_spec = pl.BlockSpec((None, None, hp, NA_HEAD_DIM, PAST_LEN), lambda b, c: (b, jb, c, 0, 0))
    blocks = 4 * (4 * DEC_SEQ * LANES + 2 * hp * NA_HEAD_DIM * PAST_LEN + hp * N_PAIR * GRID_W * LANES)
    return pl.pallas_call(
        _na_kernel,
        grid=(DEC_BATCH, nb),
        in_specs=[
            pl.BlockSpec((DEC_SEQ, LANES), lambda b, c: (s0 + b, P_QA * nb + c)),
            pl.BlockSpec((DEC_SEQ, LANES), lambda b, c: (s0 + b, P_KA * nb + c)),
            pl.BlockSpec((DEC_SEQ, LANES), lambda b, c: (s0 + b, P_VA * nb + c)),
            ctx_spec, ctx_spec,
            pl.BlockSpec((None, hp, N_PAIR, GRID_W, LANES), lambda b, c: (jb, c, 0, 0, 0)),
        ],
        out_specs=pl.BlockSpec((DEC_SEQ, LANES), lambda b, c: (b, c)),
        out_shape=jax.ShapeDtypeStruct((N_SAMPLE, NA_WIDTH), BF16),
        compiler_params=_params(("parallel", "parallel"), blocks),
        name=f"na_attention_{jb}",
    )(proj, proj, proj, cache_k_t, cache_v_t, bias_pairs)


def _running_max_rows(x, reverse):
    n = x.shape[0]
    ridx = lax.broadcasted_iota(jnp.int32, x.shape, 0)
    sh = 1
    while sh < n:
        if reverse:
            x = jnp.maximum(x, jnp.where(ridx < n - sh, pltpu.roll(x, n - sh, 0), -jnp.inf))
        else:
            x = jnp.maximum(x, jnp.where(ridx >= sh, pltpu.roll(x, sh, 0), -jnp.inf))
        sh *= 2
    return x


def _mlstm_kernel(*refs, n_chunks, has_state, jb, fills_stack, spb):
    if has_state:
        (q_ref, kt_ref, v_ref, o_ref, gt_ref, gb_ref, hn_ref, c0_ref, n0_ref, m0_ref,
         out_ref, cout_ref, nout_ref, mout_ref, hdir_ref, caug_ref) = refs
    else:
        (q_ref, kt_ref, v_ref, o_ref, gt_ref, gb_ref, hn_ref, *_unused,
         out_ref, cout_ref, nout_ref, mout_ref, hdir_ref, caug_ref) = refs
    cout_refs = [_stack_entry(cout_ref.at[s], jb, fills_stack) for s in range(spb)]
    nout_refs = [_stack_entry(nout_ref.at[s], jb, fills_stack) for s in range(spb)]
    mout_refs = [_stack_entry(mout_ref.at[s], jb, fills_stack) for s in range(spb)]
    T = ML_CHUNK
    dh = ML_HEAD_DIM
    seq_len = n_chunks * T
    b = pl.program_id(0)
    row = lax.broadcasted_iota(jnp.int32, (T, T), 0)
    col = lax.broadcasted_iota(jnp.int32, (T, T), 1)
    lane = lax.broadcasted_iota(jnp.int32, (T, LANES), 1)
    lane_row = lax.broadcasted_iota(jnp.int32, (1, LANES), 1)
    is_forget = ((lane >= ML_HEADS) & (lane < 2 * ML_HEADS)) | ((lane >= 3 * ML_HEADS) & (lane < 4 * ML_HEADS))
    ones = jnp.ones((T, dh), F32)
    states = [(d, h) for d in range(2) for h in range(ML_HEADS)]

    def gate_lanes(d, h):
        gi = 2 * d * ML_HEADS + h
        return gi, gi + ML_HEADS

    scans = [(s, d) for s in range(spb) for d in range(2)]
    m_init = []
    for s, d in scans:
        m_row = jnp.zeros((1, LANES), F32)
        for h in range(ML_HEADS):
            if has_state:
                n_bcast = jnp.broadcast_to(n0_ref[s, d, h:h + 1, :], (dh, dh)).T
                caug_ref[s, d, h] = jnp.concatenate([c0_ref[s, d, h], n_bcast], axis=1)
                m0 = m0_ref[((b * spb + s) * 2 + d) * ML_HEADS + h]
                m_row = jnp.where(lane_row == gate_lanes(d, h)[1], m0, m_row)
            else:
                caug_ref[s, d, h] = jnp.zeros((dh, 2 * dh), F32)
        m_init.append(m_row)

    def chunk_body(ci, m_rows):
        m_next = []
        for scan, (s, d) in enumerate(scans):
            causal = (col <= row) if d == 0 else (col >= row)
            last = T - 1 if d == 0 else 0
            c = ci if d == 0 else n_chunks - 1 - ci
            rows = pl.ds(pl.multiple_of(s * seq_len + c * T, T), T)
            m_prev = m_rows[scan]
            mine = (lane >= (2 * d + 1) * ML_HEADS) & (lane < (2 * d + 2) * ML_HEADS)
            g = gt_ref[rows, :] + gb_ref[...]
            log_sig = jnp.minimum(g, 0.0) - jnp.log1p(jnp.exp(-jnp.abs(g)))
            g = jnp.where(is_forget, log_sig, g)
            cum = jnp.dot(causal.astype(F32), g, precision=lax.Precision.HIGHEST, preferred_element_type=F32)
            i_al = pltpu.roll(g, ML_HEADS, 1)
            cvec = jnp.where(mine, i_al - cum, 0.0)
            cum = jnp.where(mine, cum, 0.0)
            m_run = jnp.maximum(m_prev, _running_max_rows(cvec, reverse=(d == 1)))
            m_t = cum + m_run
            w_inter = jnp.exp(m_prev - m_run)
            e_negm = jnp.exp(-m_t)
            m_run_last = m_run[last:last + 1, :]
            wi_last = w_inter[last:last + 1, :]
            w_last = jnp.exp(cvec - m_run_last)
            packed_t = jnp.where(mine, cvec, pltpu.roll(w_last, LANES - ML_HEADS, 1)).T
            for h in range(ML_HEADS):
                gi, gf = gate_lanes(d, h)
                hl = slice(h * dh, (h + 1) * dh)
                m_run_b = jnp.broadcast_to(m_run[:, gf:gf + 1], (T, T))
                e_negm_b = jnp.broadcast_to(e_negm[:, gf:gf + 1], (T, dh))
                w_d = jnp.exp(jnp.where(causal, packed_t[gf:gf + 1, :] - m_run_b, -jnp.inf))
                w_inter_b = jnp.exp(m_prev[:, gf:gf + 1] - m_run_b)
                qh = q_ref[rows, hl]
                kh_t = kt_ref[s * n_chunks + c, hl, :] * (dh ** -0.5)
                v_aug = jnp.concatenate([v_ref[rows, hl], ones], axis=1)
                sc = _dot(qh, kh_t) * w_d
                c_aug = caug_ref[s, d, h]
                nd = jnp.concatenate([w_inter_b, w_inter_b], axis=1) * _dot(qh, c_aug) + _dot(sc, v_aug)
                hdir_ref[d, rows, hl] = nd[:, :dh] / jnp.maximum(jnp.abs(nd[:, dh:]), e_negm_b)
                kw_t = kh_t * packed_t[gi:gi + 1, :]
                caug_ref[s, d, h] = wi_last[:, gf:gf + 1] * c_aug + _dot(kw_t, v_aug)
            m_next.append(m_t[last:last + 1, :])
        return tuple(m_next)

    m_fin = lax.fori_loop(0, n_chunks, chunk_body, tuple(m_init), unroll=2)
    for scan, (s, d) in enumerate(scans):
        for h in range(ML_HEADS):
            c_aug = caug_ref[s, d, h]
            cout_refs[s][d, h] = c_aug[:, :dh]
            nout_refs[s][d, h:h + 1, :] = c_aug[:, dh:].T[0:1, :]
    for s in range(spb):
        m_heads = [pltpu.roll(m_fin[2 * s + d], LANES - gate_lanes(d, 0)[1], 1) for d in range(2)]
        mout_refs[s][...] = jnp.concatenate(m_heads + [jnp.zeros((COND_ROWS - 2, LANES), F32)], axis=0)

    for h in range(ML_HEADS):
        hl = slice(h * dh, (h + 1) * dh)
        h_sum = hdir_ref[0, :, hl] + hdir_ref[1, :, hl]
        gated = jax.nn.sigmoid(o_ref[:, hl]) * (_rms_scale(h_sum) * hn_ref[:, hl])
        out_ref[:, hl] = gated.astype(out_ref.dtype)


def _mlstm(proj, keys_t, gates, gate_b_row, hnorm_row, seq_len, n_seq, row0, spb, n_stack=1, jb=0, prev=None,
           state=None):
    has_state = state is not None
    assert n_seq % spb == 0 and row0 % (spb * seq_len) == 0
    r0 = row0 // (spb * seq_len)
    rows = spb * seq_len
    nb = TN_PROJ // ML_WIDTH
    n_chunks = seq_len // ML_CHUNK
    in_specs = [
        pl.BlockSpec((rows, ML_WIDTH), lambda b: (r0 + b, P_QM * nb)),
        pl.BlockSpec((spb * n_chunks, ML_WIDTH, ML_CHUNK), lambda b: (r0 + b, 0, 0)),
        pl.BlockSpec((rows, ML_WIDTH), lambda b: (r0 + b, P_VM * nb)),
        pl.BlockSpec((rows, ML_WIDTH), lambda b: (r0 + b, P_OM * nb)),
        pl.BlockSpec((rows, LANES), lambda b: (r0 + b, 0)),
        pl.BlockSpec((1, LANES), lambda b: (0, 0)),
        pl.BlockSpec((1, ML_WIDTH), lambda b: (0, 0)),
    ]
    args = [proj, keys_t, proj, proj, gates, gate_b_row, hnorm_row]
    aliases = {}
    if has_state:
        st_c, st_n, st_m = state
        in_specs += [
            pl.BlockSpec((spb, 2, ML_HEADS, ML_HEAD_DIM, ML_HEAD_DIM), lambda b: (b, 0, 0, 0, 0)),
            pl.BlockSpec((spb, 2, ML_HEADS, ML_HEAD_DIM), lambda b: (b, 0, 0, 0)),
            pl.BlockSpec(memory_space=pltpu.SMEM),
        ]
        args += [st_c, st_n, st_m.reshape(-1)]
    elif prev is not None:
        aliases = {len(args) + k: 1 + k for k in range(3)}
        in_specs += [pl.BlockSpec(memory_space=pl.ANY)] * 3
        args += list(prev)
    fills_stack = prev is None
    c_bytes = 2 * ML_HEADS * ML_HEAD_DIM * ML_HEAD_DIM
    blocks = 4 * (5 * rows * ML_WIDTH + rows * LANES + spb * (1 + n_stack) * c_bytes)
    scratch = 4 * (2 * rows * ML_WIDTH + spb * 2 * c_bytes)
    return pl.pallas_call(
        functools.partial(_mlstm_kernel, n_chunks=n_chunks, has_state=has_state, jb=jb, fills_stack=fills_stack,
                          spb=spb),
        grid=(n_seq // spb,),
        in_specs=in_specs,
        out_specs=[
            pl.BlockSpec((rows, ML_WIDTH), lambda b: (b, 0)),
            _stack_spec((2, ML_HEADS, ML_HEAD_DIM, ML_HEAD_DIM), jb, fills_stack, n_stack, spb),
            _stack_spec((2, ML_HEADS, ML_HEAD_DIM), jb, fills_stack, n_stack, spb),
            _stack_spec((COND_ROWS, LANES), jb, fills_stack, n_stack, spb),
        ],
        out_shape=[
            jax.ShapeDtypeStruct((n_seq * seq_len, ML_WIDTH), BF16),
            jax.ShapeDtypeStruct((n_seq, n_stack, 2, ML_HEADS, ML_HEAD_DIM, ML_HEAD_DIM), F32),
            jax.ShapeDtypeStruct((n_seq, n_stack, 2, ML_HEADS, ML_HEAD_DIM), F32),
            jax.ShapeDtypeStruct((n_seq, n_stack, COND_ROWS, LANES), F32),
        ],
        scratch_shapes=[pltpu.VMEM((2, rows, ML_WIDTH), F32),
                        pltpu.VMEM((spb, 2, ML_HEADS, ML_HEAD_DIM, 2 * ML_HEAD_DIM), F32)],
        input_output_aliases=aliases,
        compiler_params=_params(("parallel",), blocks, scratch),
        name=f"mlstm_{'latent' if has_state else 'context'}_{jb}",
    )(*args)


def _outproj_kernel(x_ref, mod_ref, g_ref, ap_ref, mp_ref, as_ref, ms_ref, wa_ref, wm_ref, o_ref):
    i = pl.program_id(0)

    def finish(a_ref, m_ref):
        wa = wa_ref[...].astype(BF16)
        wm = wm_ref[...].astype(BF16)
        for rows, r in _sub_rows(i, TM_OUT):
            out = _dot(a_ref[rows, :], wa) + _dot(m_ref[rows, :], wm)
            o_ref[rows, :] = _gated_residual(x_ref[rows, :], out, g_ref, mod_ref, r, 1, 1.0)

    is_prompt = i < N_PROMPT // TM_OUT
    pl.when(is_prompt)(functools.partial(finish, ap_ref, mp_ref))
    pl.when(jnp.logical_not(is_prompt))(functools.partial(finish, as_ref, ms_ref))


def _outproj(y, mod_l, g_l, a_p, m_p, a_s, m_s, ab_w_out, jb):
    tm = TM_OUT
    n_p = N_PROMPT // tm
    n_s = N_SAMPLE // tm
    p_spec = pl.BlockSpec((tm, NA_WIDTH), lambda i: (jnp.minimum(i, n_p - 1), 0))
    s_spec = pl.BlockSpec((tm, NA_WIDTH), lambda i: (jnp.clip(i - n_p, 0, n_s - 1), 0))
    blocks = 4 * (2 * tm * D_MODEL + 4 * tm * NA_WIDTH + D_MODEL * D_MODEL)
    return pl.pallas_call(
        _outproj_kernel,
        grid=(N_TOK // tm,),
        in_specs=[
            pl.BlockSpec((tm, D_MODEL), lambda i: (i, 0)),
            pl.BlockSpec(_MOD_SPEC_SHAPE, lambda i: (0, 0, 0)),
            pl.BlockSpec((6, D_MODEL), lambda i: (0, 0)),
            p_spec, p_spec, s_spec, s_spec,
            pl.BlockSpec((None, NA_WIDTH, D_MODEL), lambda i: (jb, 0, 0)),
            pl.BlockSpec((None, ML_WIDTH, D_MODEL), lambda i: (jb, 1, 0)),
        ],
        out_specs=pl.BlockSpec((tm, D_MODEL), lambda i: (i, 0)),
        out_shape=jax.ShapeDtypeStruct((N_TOK, D_MODEL), F32),
        compiler_params=_params(("parallel",), blocks),
        name=f"outproj_{jb}",
    )(y, mod_l, g_l, a_p, m_p, a_s, m_s, ab_w_out, ab_w_out)


def kernel(x_prompt, x_sample, cache_k, cache_v, state_C, state_n, state_m, c, c_ctx, ada_w, ada_b, norm_g,
           ffn_w13, ffn_w2, ab_w_in, ab_w_out, na_rpb, ml_gate_b, ml_hnorm, sc_w_in, sc_conv_w, sc_conv_b,
           sc_w_out):
    assert NA_WIDTH == ML_WIDTH == TN_PROJ and AB_MAIN % N_GATES == 0
    assert math.frexp(NA_HEAD_DIM ** -0.5)[0] == 0.5
    assert LANES == 2 * NA_HEAD_DIM
    assert N_PROMPT % TM == 0 and N_SAMPLE % TM == 0 and TM % COND_UNIT == 0
    cond = jnp.concatenate([c_ctx[None, :], c, jnp.zeros((COND_ROWS - 1 - DEC_BATCH, D_MODEL), F32)], axis=0)
    mod = _modulation(cond, ada_w, ada_b)
    bias_pairs = _na_bias(na_rpb)
    y = jnp.concatenate([x_prompt.reshape(N_PROMPT, D_MODEL), x_sample.reshape(N_SAMPLE, D_MODEL)], axis=0)
    w_in_t = jnp.swapaxes(ab_w_in, 1, 2)
    cache_k_t = jnp.swapaxes(cache_k, 3, 4)
    cache_v_t = jnp.swapaxes(cache_v, 3, 4)

    kv_new = None
    st_new = None
    for l in range(DEPTH):
        mod_l = mod[l]
        g_l = norm_g[l].reshape(6, D_MODEL)
        y = _ffn_sublayer(y, mod_l, g_l, ffn_w13, ffn_w2, l, 0)
        j = l // 2
        if l % 2 == 0:
            proj, keys_t, gates = _inproj(y, mod_l, g_l, w_in_t, j)
            gate_b_row = jnp.pad(ml_gate_b[j].reshape(1, N_GATES), ((0, 0), (0, LANES - N_GATES)))
            hnorm_row = ml_hnorm[j].reshape(1, ML_WIDTH)
            a_p, *kv_new = _ctx_attention(proj, j, kv_new)
            m_p, *st_new = _mlstm(proj, keys_t, gates, gate_b_row, hnorm_row, SEQ, BATCH, 0, ML_SEQS_PER_STEP,
                                  n_stack=N_AB, jb=j, prev=st_new)
            a_s = _na_attention(proj, cache_k_t, cache_v_t, bias_pairs, j)
            m_s, _, _, _ = _mlstm(proj, keys_t, gates, gate_b_row, hnorm_row, DEC_SEQ, DEC_BATCH, N_PROMPT,
                                  ML_SEQS_PER_STEP, state=(state_C[:, j], state_n[:, j], state_m[:, j]))
            y = _outproj(y, mod_l, g_l, a_p, m_p, a_s, m_s, ab_w_out, j)
        else:
            y = _sconv_sublayer(y, mod_l, g_l, sc_w_in, sc_conv_w, sc_conv_b, sc_w_out, j)
        if l < DEPTH - 1:
            y = _ffn_sublayer(y, mod_l, g_l, ffn_w13, ffn_w2, l, 1)

    p_tiles = N_PROMPT // TM
    y_p = _ffn_sublayer(y, mod_l, g_l, ffn_w13, ffn_w2, DEPTH - 1, 1, 0, p_tiles)
    y_s = _ffn_sublayer(y, mod_l, g_l, ffn_w13, ffn_w2, DEPTH - 1, 1, p_tiles, N_TOK // TM - p_tiles)
    y_p = y_p.reshape(BATCH, SEQ, D_MODEL)
    y_s = y_s.reshape(DEC_BATCH, DEC_SEQ, D_MODEL)
    new_c, new_n, m_rows = st_new
    new_m = m_rows[:, :, :2, :ML_HEADS]
    return (y_p, y_s, jnp.swapaxes(kv_new[0], 3, 4), jnp.swapaxes(kv_new[1], 3, 4), new_c, new_n, new_m)
```

```python
import functools
import math

import jax
import jax.numpy as jnp
from jax import lax
from jax.experimental import pallas as pl
from jax.experimental.pallas import tpu as pltpu

D_MODEL = 1024
BATCH = 16
SEQ = 256
DEPTH = 4
DEC_BATCH = 2
DEC_SEQ = 1024
PAST_LEN = 256
GRID_W = 64
NA_HEADS = 8
NA_HEAD_DIM = 64
NA_WIN_H = 8
NA_WIN_W = 16
ML_HEADS = 4
ML_HEAD_DIM = 128
ML_CHUNK = 128
CONV_WIDTH = 3
D_FF = 2816
N_MOD = 9
EPS = 1e-6
N_AB = (DEPTH + 1) // 2
N_C = DEPTH // 2
NA_WIDTH = NA_HEADS * NA_HEAD_DIM
ML_WIDTH = ML_HEADS * ML_HEAD_DIM
AB_MAIN = 3 * NA_WIDTH + 4 * ML_WIDTH
N_GATES = 4 * ML_HEADS
AB_IN = AB_MAIN + N_GATES
SC_WIDTH = D_MODEL

N_PROMPT = BATCH * SEQ
N_SAMPLE = DEC_BATCH * DEC_SEQ
N_TOK = N_PROMPT + N_SAMPLE
GRID_ROWS = DEC_SEQ // GRID_W
NA_KH = min(NA_WIN_H, GRID_ROWS)
N_DR = 2 * NA_WIN_H - 1
N_DC = 2 * NA_WIN_W - 1
N_PAIR = N_DR - 1

LANES = 128
COND_ROWS = 8
VMEM_CAP = 60 * 1024 * 1024
TEMP_BYTES = 16 * 1024 * 1024

COND_UNIT = DEC_SEQ
TM = 2048
SUB_ROWS = 512
TM_OUT = 1024
TM_CONV = COND_UNIT
TF_CONV = 512
TF = 256
TN_PROJ = 512
TN_MOD = 1536
NA_QROWS = 4
ML_SEQS_PER_STEP = 2

PROJ_BLOCKS = (0, 1, 2, 3, 5, 6)
KM_BLOCK = 4
P_QA, P_KA, P_VA, P_QM, P_VM, P_OM = range(6)

F32 = jnp.float32
BF16 = jnp.bfloat16


def _vmem_limit(block_bytes, scratch_bytes):
    assert 2 * block_bytes + scratch_bytes <= VMEM_CAP
    return VMEM_CAP


def _params(semantics, block_bytes, scratch_bytes=0):
    return pltpu.CompilerParams(dimension_semantics=semantics,
                                vmem_limit_bytes=_vmem_limit(block_bytes, scratch_bytes))


def _unit_rows(i, tm=TM):
    first_latent = N_PROMPT // COND_UNIT
    units = tm // COND_UNIT
    return [(slice(u * COND_UNIT, (u + 1) * COND_UNIT), jnp.maximum(i * units + u - (first_latent - 1), 0))
            for u in range(units)]


def _sub_rows(i, tm=TM):
    per_unit = COND_UNIT // SUB_ROWS
    return [(slice(rows.start + s * SUB_ROWS, rows.start + (s + 1) * SUB_ROWS), r)
            for rows, r in _unit_rows(i, tm) for s in range(per_unit)]


def _rms_scale(x):
    return x * lax.rsqrt(jnp.mean(x * x, axis=-1, keepdims=True) + EPS)


def _modulate_in(x, g_ref, mod_ref, r, j):
    gain = g_ref[2 * j:2 * j + 1, :] * (1.0 + mod_ref[r, 3 * j + 1:3 * j + 2, :])
    return _rms_scale(x) * gain + mod_ref[r, 3 * j:3 * j + 1, :]


def _gated_residual(x, out, g_ref, mod_ref, r, j, weight):
    gain = (weight * mod_ref[r, 3 * j + 2:3 * j + 3, :]) * g_ref[2 * j + 1:2 * j + 2, :]
    return x + _rms_scale(out) * gain


def _cond_specs(l):
    return [pl.BlockSpec((None, COND_ROWS, N_MOD, D_MODEL), lambda *_: (l, 0, 0, 0)),
            pl.BlockSpec((None, 6, D_MODEL), lambda *_: (l, 0, 0))]


def _dot(a, b):
    return jnp.dot(a.astype(BF16), b.astype(BF16), preferred_element_type=F32)


def _dot_nt(a, b):
    return lax.dot_general(a.astype(BF16), b.astype(BF16), (((1,), (1,)), ((), ())),
                           preferred_element_type=F32)


def _mod_kernel(c_ref, w_ref, b_ref, o_ref):
    c = c_ref[...]
    s = c * jax.nn.sigmoid(c)
    o_ref[...] = _dot(s, w_ref[...]) + b_ref[...]


def _modulation(cond, ada_w, ada_b):
    n = N_MOD * D_MODEL
    out = pl.pallas_call(
        _mod_kernel,
        grid=(DEPTH, n // TN_MOD),
        in_specs=[
            pl.BlockSpec((COND_ROWS, D_MODEL), lambda l, t: (0, 0)),
            pl.BlockSpec((None, D_MODEL, TN_MOD), lambda l, t: (l, 0, t)),
            pl.BlockSpec((None, 1, TN_MOD), lambda l, t: (l, 0, t)),
        ],
        out_specs=pl.BlockSpec((None, COND_ROWS, TN_MOD), lambda l, t: (l, 0, t)),
        out_shape=jax.ShapeDtypeStruct((DEPTH, COND_ROWS, n), F32),
        compiler_params=_params(("parallel", "parallel"), 4 * D_MODEL * TN_MOD),
        name="modulation",
    )(cond, ada_w, ada_b.reshape(DEPTH, 1, n))
    return out.reshape(DEPTH, COND_ROWS, N_MOD, D_MODEL)


def _ffn_kernel(*refs, j, tile0, n_head):
    i = pl.program_id(0) + tile0
    k = pl.program_id(1)
    last = pl.num_programs(1) - 1
    if n_head is None:
        x_ref, mod_ref, g_ref, wa_ref, wg_ref, w2_ref, o_ref, h_ref = refs
        sources = [(None, x_ref)]
    else:
        x_ref, x_tail_ref, mod_ref, g_ref, wa_ref, wg_ref, w2_ref, o_ref, h_ref = refs
        sources = [(pl.program_id(0) < n_head, x_ref), (pl.program_id(0) >= n_head, x_tail_ref)]

    def per_source(fn):
        for pred, ref in sources:
            if pred is None:
                fn(ref)
            else:
                pl.when(pred)(functools.partial(fn, ref))

    def hidden_step(rows, init):
        h = h_ref[rows, :]
        a = _dot(h, wa_ref[...])
        g = _dot(h, wg_ref[...])
        d = _dot((g * jax.nn.sigmoid(g)) * a, w2_ref[...])
        if init:
            o_ref[rows, :] = d
        else:
            o_ref[rows, :] += d

    def first_step(x_src):
        for rows, r in _sub_rows(i):
            h_ref[rows, :] = _modulate_in(x_src[rows, :], g_ref, mod_ref, r, j).astype(BF16)
            hidden_step(rows, init=True)

    def last_step(x_src):
        for rows, r in _sub_rows(i):
            hidden_step(rows, init=False)
            o_ref[rows, :] = _gated_residual(x_src[rows, :], o_ref[rows, :], g_ref, mod_ref, r, j, 0.5)

    pl.when(k == 0)(functools.partial(per_source, first_step))

    @pl.when((k > 0) & (k < last))
    def _():
        for rows, _ in _unit_rows(i):
            hidden_step(rows, init=False)

    pl.when(k == last)(functools.partial(per_source, last_step))


def _ffn_sublayer(y, mod, gains, ffn_w13, ffn_w2, l, s, tile0=0, n_tiles=N_TOK // TM, y_tail=None):
    j = 2 * s
    nk = D_FF // TF
    blocks = 4 * (2 * TM * D_MODEL + 3 * D_MODEL * TF)
    scratch = 2 * TM * D_MODEL
    if y_tail is None:
        n_head = None
        x_specs = [pl.BlockSpec((TM, D_MODEL), lambda i, k: (tile0 + i, 0))]
        xs = [y]
    else:
        assert tile0 == 0
        n_head = y.shape[0] // TM
        x_specs = [pl.BlockSpec((TM, D_MODEL), lambda i, k: (jnp.minimum(i, n_head - 1), 0)),
                   pl.BlockSpec((TM, D_MODEL), lambda i, k: (jnp.maximum(i - n_head, 0), 0),
                                pipeline_mode=pl.Buffered(1))]
        xs = [y, y_tail]
        scratch += 4 * TM * D_MODEL
    return pl.pallas_call(
        functools.partial(_ffn_kernel, j=j, tile0=tile0, n_head=n_head),
        grid=(n_tiles, nk),
        in_specs=[
            *x_specs,
            *_cond_specs(l),
            pl.BlockSpec((None, None, D_MODEL, TF), lambda i, k: (l, s, 0, k)),
            pl.BlockSpec((None, None, D_MODEL, TF), lambda i, k: (l, s, 0, k + nk)),
            pl.BlockSpec((None, None, TF, D_MODEL), lambda i, k: (l, s, k, 0)),
        ],
        out_specs=pl.BlockSpec((TM, D_MODEL), lambda i, k: (i, 0)),
        out_shape=jax.ShapeDtypeStruct((n_tiles * TM, D_MODEL), F32),
        scratch_shapes=[pltpu.VMEM((TM, D_MODEL), BF16)],
        compiler_params=_params(("parallel", "arbitrary"), blocks, scratch),
        name=f"ffn_l{l}_s{s}_t{tile0}",
    )(*xs, mod, gains, ffn_w13, ffn_w13, ffn_w2)


def _sconv_kernel(x_ref, mod_ref, g_ref, wb_ref, wc_ref, wx_ref, cw_ref, cb_ref, wo_ref, o_ref, h_ref):
    i = pl.program_id(0)
    k = pl.program_id(1)
    ((_, r),) = _unit_rows(i, TM_CONV)
    seq = jnp.where(i < N_PROMPT // TM_CONV, SEQ, DEC_SEQ)
    blk = SEQ
    n_blk = TM_CONV // blk
    row = lax.broadcasted_iota(jnp.int32, (blk, TF_CONV), 0)
    zero_row = jnp.zeros((1, TF_CONV), F32)

    def step(first, last):
        wb = wb_ref[...].astype(BF16)
        wc = wc_ref[...].astype(BF16)
        wx = wx_ref[...].astype(BF16)
        wo = wo_ref[...].astype(BF16)
        gbs, us = [], []
        for bi in range(n_blk):
            rows = slice(bi * blk, (bi + 1) * blk)
            if first:
                h_ref[rows, :] = _modulate_in(x_ref[rows, :], g_ref, mod_ref, r, 1).astype(BF16)
            h = h_ref[rows, :]
            gbs.append(_dot(h, wb))
            us.append(_dot(h, wc) * _dot(h, wx))
        for bi in range(n_blk):
            rows = slice(bi * blk, (bi + 1) * blk)
            ub = us[bi]
            t = (row + bi * blk) & (seq - 1)
            above = us[bi - 1][blk - 1:blk, :] if bi > 0 else zero_row
            below = us[bi + 1][0:1, :] if bi < n_blk - 1 else zero_row
            u_prev = jnp.where(t == 0, 0.0, jnp.where(row == 0, above, pltpu.roll(ub, 1, 0)))
            u_next = jnp.where(t == seq - 1, 0.0, jnp.where(row == blk - 1, below, pltpu.roll(ub, blk - 1, 0)))
            y = cb_ref[...] + u_prev * cw_ref[0:1, :] + ub * cw_ref[1:2, :] + u_next * cw_ref[2:3, :]
            d = _dot(gbs[bi] * y, wo)
            acc = d if first else o_ref[rows, :] + d
            o_ref[rows, :] = _gated_residual(x_ref[rows, :], acc, g_ref, mod_ref, r, 1, 1.0) if last else acc

    n_steps = SC_WIDTH // TF_CONV
    for kk in range(n_steps):
        pl.when(k == kk)(functools.partial(step, kk == 0, kk == n_steps - 1))


def _sconv_sublayer(y, mod_l, g_l, sc_w_in, sc_conv_w, sc_conv_b, sc_w_out, jc, l):
    tm, tf = TM_CONV, TF_CONV
    nk = SC_WIDTH // tf
    blocks = 4 * (2 * tm * D_MODEL + 4 * D_MODEL * tf)
    scratch = 2 * tm * D_MODEL
    return pl.pallas_call(
        _sconv_kernel,
        grid=(N_TOK // tm, nk),
        in_specs=[
            pl.BlockSpec((tm, D_MODEL), lambda i, k: (i, 0)),
            *_cond_specs(l),
            pl.BlockSpec((None, D_MODEL, tf), lambda i, k: (jc, 0, k)),
            pl.BlockSpec((None, D_MODEL, tf), lambda i, k: (jc, 0, k + nk)),
            pl.BlockSpec((None, D_MODEL, tf), lambda i, k: (jc, 0, k + 2 * nk)),
            pl.BlockSpec((None, CONV_WIDTH, tf), lambda i, k: (jc, 0, k)),
            pl.BlockSpec((None, 1, tf), lambda i, k: (jc, 0, k)),
            pl.BlockSpec((None, tf, D_MODEL), lambda i, k: (jc, k, 0)),
        ],
        out_specs=pl.BlockSpec((tm, D_MODEL), lambda i, k: (i, 0)),
        out_shape=jax.ShapeDtypeStruct((N_TOK, D_MODEL), F32),
        scratch_shapes=[pltpu.VMEM((tm, D_MODEL), BF16)],
        compiler_params=_params(("parallel", "arbitrary"), blocks, scratch),
        name=f"sconv_{jc}",
    )(y, mod_l, g_l, sc_w_in, sc_w_in, sc_w_in, sc_conv_w, sc_conv_b.reshape(N_C, 1, SC_WIDTH), sc_w_out)


def _inproj_kernel(x_ref, mod_ref, g_ref, w_ref, wk_ref, wg_ref, o_ref, okt_ref, og_ref, h_ref):
    i = pl.program_id(0)
    n = pl.program_id(1)
    n_main = len(PROJ_BLOCKS)

    @pl.when(n == 0)
    def _():
        for rows, r in _sub_rows(i):
            h_ref[rows, :] = _modulate_in(x_ref[rows, :], g_ref, mod_ref, r, 1).astype(BF16)
            o_ref[rows, :] = _dot_nt(h_ref[rows, :], w_ref[...])

    @pl.when((n > 0) & (n < n_main))
    def _():
        o_ref[...] = _dot_nt(h_ref[...], w_ref[...])

    @pl.when(n == n_main)
    def _():
        w_kg = jnp.concatenate([wk_ref[...], wg_ref[...]], axis=0)
        ktg = _dot_nt(w_kg, h_ref[...])
        for c in range(TM // ML_CHUNK):
            okt_ref[c] = ktg[:ML_WIDTH, c * ML_CHUNK:(c + 1) * ML_CHUNK]
        gates_t = jnp.concatenate([ktg[ML_WIDTH:, :], jnp.zeros((LANES - N_GATES, TM), F32)], axis=0)
        og_ref[...] = gates_t.T


def _inproj(y, mod_l, g_l, w_in_t, jb, l):
    n_main = len(PROJ_BLOCKS)
    assert PROJ_BLOCKS == tuple(b for b in range(n_main + 1) if b != KM_BLOCK)

    def w_block(i, n):
        m = jnp.minimum(n, n_main - 1)
        return (jb, m + (m >= KM_BLOCK).astype(jnp.int32), 0)

    cpt = TM // ML_CHUNK
    blocks = 4 * (TM * D_MODEL + 2 * TN_PROJ * D_MODEL + N_GATES * D_MODEL + TM * TN_PROJ
                  + ML_WIDTH * TM + TM * LANES)
    return pl.pallas_call(
        _inproj_kernel,
        grid=(N_TOK // TM, n_main + 1),
        in_specs=[
            pl.BlockSpec((TM, D_MODEL), lambda i, n: (i, 0)),
            *_cond_specs(l),
            pl.BlockSpec((None, TN_PROJ, D_MODEL), w_block),
            pl.BlockSpec((None, ML_WIDTH, D_MODEL), lambda i, n: (jb, KM_BLOCK, 0)),
            pl.BlockSpec((None, N_GATES, D_MODEL), lambda i, n: (jb, AB_MAIN // N_GATES, 0)),
        ],
        out_specs=[
            pl.BlockSpec((TM, TN_PROJ), lambda i, n: (i, jnp.minimum(n, n_main - 1))),
            pl.BlockSpec((cpt, ML_WIDTH, ML_CHUNK), lambda i, n: (i, 0, 0)),
            pl.BlockSpec((TM, LANES), lambda i, n: (i, 0)),
        ],
        out_shape=[jax.ShapeDtypeStruct((N_TOK, n_main * TN_PROJ), F32),
                   jax.ShapeDtypeStruct((N_TOK // ML_CHUNK, ML_WIDTH, ML_CHUNK), F32),
                   jax.ShapeDtypeStruct((N_TOK, LANES), F32)],
        scratch_shapes=[pltpu.VMEM((TM, D_MODEL), BF16)],
        compiler_params=_params(("parallel", "arbitrary"), blocks, 2 * TM * D_MODEL),
        name=f"inproj_{jb}",
    )(y, mod_l, g_l, w_in_t, w_in_t, w_in_t)


def _stack_entry(ref, jb, fills_stack):
    if not fills_stack:
        return ref
    for j in range(ref.shape[0]):
        if j != jb:
            ref[j] = jnp.zeros(ref.shape[1:], ref.dtype)
    return ref.at[jb]


def _stack_spec(entry_shape, jb, fills_stack, n_stack=N_AB, lead=None):
    zeros = (0,) * len(entry_shape)
    if fills_stack:
        return pl.BlockSpec((lead, n_stack) + entry_shape, lambda b: (b, 0) + zeros)
    return pl.BlockSpec((lead, None) + entry_shape, lambda b: (b, jb) + zeros)


def _ctx_attn_kernel(q_ref, k_ref, v_ref, *rest, jb, fills_stack):
    o_ref = rest[-3]
    kc_ref = _stack_entry(rest[-2], jb, fills_stack)
    vc_ref = _stack_entry(rest[-1], jb, fills_stack)
    scale = NA_HEAD_DIM ** -0.5
    k_t = k_ref[...].T
    v_t = v_ref[...].T
    for h in range(NA_HEADS):
        sl = slice(h * NA_HEAD_DIM, (h + 1) * NA_HEAD_DIM)
        kc_ref[h] = k_t[sl, :]
        vc_ref[h] = v_t[sl, :]
    lane = lax.broadcasted_iota(jnp.int32, (SEQ, LANES), 1)
    for hp in range(NA_WIDTH // LANES):
        pair = slice(hp * LANES, (hp + 1) * LANES)
        q_pair = q_ref[:, pair] * scale
        outs = []
        for hh in range(LANES // NA_HEAD_DIM):
            own = (lane >= hh * NA_HEAD_DIM) & (lane < (hh + 1) * NA_HEAD_DIM)
            s = _dot(jnp.where(own, q_pair, 0.0), k_t[pair, :])
            p = jnp.exp(s - jnp.max(s, axis=-1, keepdims=True))
            p = p * (1.0 / jnp.sum(p, axis=-1, keepdims=True))
            outs.append(_dot(p, v_ref[:, pair]))
        o_ref[:, pair] = jnp.where(lane < NA_HEAD_DIM, outs[0], outs[1]).astype(o_ref.dtype)


def _ctx_attention(proj, jb, prev):
    cache = jax.ShapeDtypeStruct((BATCH, N_AB, NA_HEADS, NA_HEAD_DIM, SEQ), F32)
    cache_spec = _stack_spec((NA_HEADS, NA_HEAD_DIM, SEQ), jb, prev is None)
    in_specs = [pl.BlockSpec((SEQ, NA_WIDTH), lambda b, c=c: (b, c)) for c in (P_QA, P_KA, P_VA)]
    args = [proj, proj, proj]
    aliases = {}
    if prev is not None:
        in_specs += [pl.BlockSpec(memory_space=pl.ANY)] * 2
        args += list(prev)
        aliases = {3: 1, 4: 2}
    blocks = 4 * (4 * SEQ * NA_WIDTH + 2 * N_AB * NA_WIDTH * SEQ)
    return pl.pallas_call(
        functools.partial(_ctx_attn_kernel, jb=jb, fills_stack=prev is None),
        grid=(BATCH,),
        in_specs=in_specs,
        out_specs=[pl.BlockSpec((SEQ, NA_WIDTH), lambda b: (b, 0)), cache_spec, cache_spec],
        out_shape=[jax.ShapeDtypeStruct((N_PROMPT, NA_WIDTH), BF16), cache, cache],
        input_output_aliases=aliases,
        compiler_params=_params(("parallel",), blocks),
        name=f"ctx_attention_{jb}",
    )(*args)


def _na_bias_kernel(rpb_ref, p_ref, e_ref):
    base = (pl.program_id(0) * NA_HEADS + pl.program_id(1)) * (N_DR * N_DC)
    shape = (GRID_W, LANES)
    lane = lax.broadcasted_iota(jnp.int32, shape, 1)
    qc = lax.broadcasted_iota(jnp.int32, shape, 0)
    kc = lane & (GRID_W - 1)
    upper = lane >= GRID_W
    cs = jnp.clip(qc - NA_WIN_W // 2, 0, GRID_W - NA_WIN_W)
    col_in = (kc >= cs) & (kc < cs + NA_WIN_W)
    dc = jnp.clip(kc - qc, -(NA_WIN_W - 1), NA_WIN_W - 1) + (NA_WIN_W - 1)
    upper_row = upper[0:1, :]
    n_even = (N_DR + 1) // 2
    group = 4
    for e0 in range(0, n_even, group):
        es = range(e0, min(e0 + group, n_even))
        accs = [jnp.zeros(shape, F32) for _ in es]
        for d in range(N_DC):
            hit = dc == d
            for a, e in enumerate(es):
                lo = rpb_ref[base + 2 * e * N_DC + d]
                hi = rpb_ref[base + (2 * e + 1) * N_DC + d] if 2 * e + 1 < N_DR else 0.0
                accs[a] = jnp.where(hit, jnp.where(upper_row, hi, lo), accs[a])
        for a, e in enumerate(es):
            e_ref[e] = jnp.where(col_in, accs[a], -jnp.inf)
    for pair in range(N_PAIR):
        e = pair // 2
        if pair % 2 == 0:
            p_ref[pair] = e_ref[e]
        else:
            p_ref[pair] = jnp.where(upper, pltpu.roll(e_ref[e + 1], GRID_W, 1), pltpu.roll(e_ref[e], GRID_W, 1))


def _na_bias(na_rpb):
    return pl.pallas_call(
        _na_bias_kernel,
        grid=(N_AB, NA_HEADS),
        in_specs=[pl.BlockSpec(memory_space=pltpu.SMEM)],
        out_specs=pl.BlockSpec((None, None, N_PAIR, GRID_W, LANES), lambda j, h: (j, h, 0, 0, 0)),
        out_shape=jax.ShapeDtypeStruct((N_AB, NA_HEADS, N_PAIR, GRID_W, LANES), F32),
        scratch_shapes=[pltpu.VMEM(((N_DR + 1) // 2, GRID_W, LANES), F32)],
        compiler_params=_params(("parallel", "parallel"), 4 * N_PAIR * GRID_W * LANES,
                                4 * ((N_DR + 1) // 2) * GRID_W * LANES),
        name="na_bias",
    )(na_rpb.reshape(-1))


def _na_window_start(r):
    return min(max(r - NA_KH // 2, 0), GRID_ROWS - NA_KH)


def _na_bias_block(p_ref, hh, qr, kr):
    rs = _na_window_start(qr)
    lo_ok = rs <= kr < rs + NA_KH
    hi_ok = rs <= kr + 1 < rs + NA_KH
    if not (lo_ok or hi_ok):
        return jnp.full((GRID_W, LANES), -jnp.inf, F32)
    dr = kr - qr + (NA_WIN_H - 1)
    assert 0 <= dr < N_PAIR
    blk = p_ref[hh, dr]
    if lo_ok and hi_ok:
        return blk
    lane = lax.broadcasted_iota(jnp.int32, (GRID_W, LANES), 1)
    keep = (lane < GRID_W) if lo_ok else (lane >= GRID_W)
    return jnp.where(keep, blk, -jnp.inf)


def _na_kernel(q_ref, k_ref, v_ref, kc_ref, vc_ref, p_ref, o_ref):
    scale = NA_HEAD_DIM ** -0.5
    n_q = NA_QROWS * GRID_W
    lane = lax.broadcasted_iota(jnp.int32, (n_q, LANES), 1)
    k_ctx_t = kc_ref[...].reshape(LANES, PAST_LEN)
    v_ctx_t = vc_ref[...].reshape(LANES, PAST_LEN)
    for q0 in range(0, GRID_ROWS, NA_QROWS):
        k_lo = _na_window_start(q0) // 2 * 2
        k_hi = -(-(_na_window_start(q0 + NA_QROWS - 1) + NA_KH) // 2) * 2
        q_rows = slice(q0 * GRID_W, (q0 + NA_QROWS) * GRID_W)
        k_rows = slice(k_lo * GRID_W, k_hi * GRID_W)
        q_pair = q_ref[q_rows, :] * scale
        outs = []
        for hh in range(LANES // NA_HEAD_DIM):
            bias = jnp.concatenate(
                [jnp.concatenate([_na_bias_block(p_ref, hh, qr, kr) for kr in range(k_lo, k_hi, 2)], axis=1)
                 for qr in range(q0, q0 + NA_QROWS)], axis=0)
            own = (lane >= hh * NA_HEAD_DIM) & (lane < (hh + 1) * NA_HEAD_DIM)
            q = jnp.where(own, q_pair, 0.0)
            s_loc = _dot_nt(q, k_ref[k_rows, :]) + bias
            s_ctx = _dot(q, k_ctx_t)
            m = jnp.maximum(jnp.max(s_loc, axis=-1, keepdims=True), jnp.max(s_ctx, axis=-1, keepdims=True))
            p_loc = jnp.exp(s_loc - m)
            p_ctx = jnp.exp(s_ctx - m)
            denom = jnp.sum(p_loc, axis=-1, keepdims=True) + jnp.sum(p_ctx, axis=-1, keepdims=True)
            outs.append((_dot(p_loc, v_ref[k_rows, :]) + _dot_nt(p_ctx, v_ctx_t)) / denom)
        o_ref[q_rows, :] = jnp.where(lane < NA_HEAD_DIM, outs[0], outs[1]).astype(o_ref.dtype)


def _na_attention(proj, cache_k_t, cache_v_t, bias_pairs, jb):
    hp = LANES // NA_HEAD_DIM
    s0 = N_PROMPT // DEC_SEQ
    nb = NA_WIDTH // LANES
    ctx_spec = pl.BlockSpec((None, None, hp, NA_HEAD_DIM, PAST_LEN), lambda b, c: (b, jb, c, 0, 0))
    blocks = 4 * (4 * DEC_SEQ * LANES + 2 * hp * NA_HEAD_DIM * PAST_LEN + hp * N_PAIR * GRID_W * LANES)
    return pl.pallas_call(
        _na_kernel,
        grid=(DEC_BATCH, nb),
        in_specs=[
            pl.BlockSpec((DEC_SEQ, LANES), lambda b, c: (s0 + b, P_QA * nb + c)),
            pl.BlockSpec((DEC_SEQ, LANES), lambda b, c: (s0 + b, P_KA * nb + c)),
            pl.BlockSpec((DEC_SEQ, LANES), lambda b, c: (s0 + b, P_VA * nb + c)),
            ctx_spec, ctx_spec,
            pl.BlockSpec((None, hp, N_PAIR, GRID_W, LANES), lambda b, c: (jb, c, 0, 0, 0)),
        ],
        out_specs=pl.BlockSpec((DEC_SEQ, LANES), lambda b, c: (b, c)),
        out_shape=jax.ShapeDtypeStruct((N_SAMPLE, NA_WIDTH), BF16),
        compiler_params=_params(("parallel", "parallel"), blocks),
        name=f"na_attention_{jb}",
    )(proj, proj, proj, cache_k_t, cache_v_t, bias_pairs)


def _running_max_rows(x, reverse):
    n = x.shape[0]
    ridx = lax.broadcasted_iota(jnp.int32, x.shape, 0)
    sh = 1
    while sh < n:
        if reverse:
            x = jnp.maximum(x, jnp.where(ridx < n - sh, pltpu.roll(x, n - sh, 0), -jnp.inf))
        else:
            x = jnp.maximum(x, jnp.where(ridx >= sh, pltpu.roll(x, sh, 0), -jnp.inf))
        sh *= 2
    return x


def _mlstm_kernel(*refs, n_chunks, has_state, jb, fills_stack, spb, layer, n_layers):
    if has_state:
        (q_ref, kt_ref, v_ref, o_ref, gt_ref, gb_ref, hn_ref, c0_ref, n0_ref, m0_ref,
         out_ref, cout_ref, nout_ref, mout_ref, hdir_ref, caug_ref) = refs
    else:
        (q_ref, kt_ref, v_ref, o_ref, gt_ref, gb_ref, hn_ref, *_unused,
         out_ref, cout_ref, nout_ref, mout_ref, hdir_ref, caug_ref) = refs
    cout_refs = [_stack_entry(cout_ref.at[s], jb, fills_stack) for s in range(spb)]
    nout_refs = [_stack_entry(nout_ref.at[s], jb, fills_stack) for s in range(spb)]
    mout_refs = [_stack_entry(mout_ref.at[s], jb, fills_stack) for s in range(spb)]
    T = ML_CHUNK
    dh = ML_HEAD_DIM
    seq_len = n_chunks * T
    b = pl.program_id(0)
    row = lax.broadcasted_iota(jnp.int32, (T, T), 0)
    col = lax.broadcasted_iota(jnp.int32, (T, T), 1)
    lane = lax.broadcasted_iota(jnp.int32, (T, LANES), 1)
    lane_row = lax.broadcasted_iota(jnp.int32, (1, LANES), 1)
    is_forget = ((lane >= ML_HEADS) & (lane < 2 * ML_HEADS)) | ((lane >= 3 * ML_HEADS) & (lane < 4 * ML_HEADS))
    ones = jnp.ones((T, dh), F32)
    states = [(d, h) for d in range(2) for h in range(ML_HEADS)]

    def gate_lanes(d, h):
        gi = 2 * d * ML_HEADS + h
        return gi, gi + ML_HEADS

    scans = [(s, d) for s in range(spb) for d in range(2)]
    m_init = []
    for s, d in scans:
        m_row = jnp.zeros((1, LANES), F32)
        for h in range(ML_HEADS):
            if has_state:
                n_bcast = jnp.broadcast_to(n0_ref[s, d, h:h + 1, :], (dh, dh)).T
                caug_ref[s, d, h] = jnp.concatenate([c0_ref[s, d, h], n_bcast], axis=1)
                m0 = m0_ref[(((b * spb + s) * n_layers + layer) * 2 + d) * ML_HEADS + h]
                m_row = jnp.where(lane_row == gate_lanes(d, h)[1], m0, m_row)
            else:
                caug_ref[s, d, h] = jnp.zeros((dh, 2 * dh), F32)
        m_init.append(m_row)

    def chunk_body(ci, m_rows):
        m_next = []
        for scan, (s, d) in enumerate(scans):
            causal = (col <= row) if d == 0 else (col >= row)
            last = T - 1 if d == 0 else 0
            c = ci if d == 0 else n_chunks - 1 - ci
            rows = pl.ds(pl.multiple_of(s * seq_len + c * T, T), T)
            m_prev = m_rows[scan]
            mine = (lane >= (2 * d + 1) * ML_HEADS) & (lane < (2 * d + 2) * ML_HEADS)
            g = gt_ref[rows, :] + gb_ref[...]
            log_sig = jnp.minimum(g, 0.0) - jnp.log1p(jnp.exp(-jnp.abs(g)))
            g = jnp.where(is_forget, log_sig, g)
            cum = jnp.dot(causal.astype(F32), g, precision=lax.Precision.HIGHEST, preferred_element_type=F32)
            i_al = pltpu.roll(g, ML_HEADS, 1)
            cvec = jnp.where(mine, i_al - cum, 0.0)
            cum = jnp.where(mine, cum, 0.0)
            m_run = jnp.maximum(m_prev, _running_max_rows(cvec, reverse=(d == 1)))
            m_t = cum + m_run
            w_inter = jnp.exp(m_prev - m_run)
            e_negm = jnp.exp(-m_t)
            m_run_last = m_run[last:last + 1, :]
            wi_last = w_inter[last:last + 1, :]
            w_last = jnp.exp(cvec - m_run_last)
            packed_t = jnp.where(mine, cvec, pltpu.roll(w_last, LANES - ML_HEADS, 1)).T
            for h in range(ML_HEADS):
                gi, gf = gate_lanes(d, h)
                hl = slice(h * dh, (h + 1) * dh)
                m_run_b = jnp.broadcast_to(m_run[:, gf:gf + 1], (T, T))
                e_negm_b = jnp.broadcast_to(e_negm[:, gf:gf + 1], (T, dh))
                w_d = jnp.exp(jnp.where(causal, packed_t[gf:gf + 1, :] - m_run_b, -jnp.inf))
                w_inter_b = jnp.exp(m_prev[:, gf:gf + 1] - m_run_b)
                qh = q_ref[rows, hl]
                kh_t = kt_ref[s * n_chunks + c, hl, :] * (dh ** -0.5)
                v_aug = jnp.concatenate([v_ref[rows, hl], ones], axis=1)
                sc = _dot(qh, kh_t) * w_d
                c_aug = caug_ref[s, d, h]
                nd = jnp.concatenate([w_inter_b, w_inter_b], axis=1) * _dot(qh, c_aug) + _dot(sc, v_aug)
                hdir_ref[d, rows, hl] = nd[:, :dh] / jnp.maximum(jnp.abs(nd[:, dh:]), e_negm_b)
                kw_t = kh_t * packed_t[gi:gi + 1, :]
                caug_ref[s, d, h] = wi_last[:, gf:gf + 1] * c_aug + _dot(kw_t, v_aug)
            m_next.append(m_t[last:last + 1, :])
        return tuple(m_next)

    m_fin = lax.fori_loop(0, n_chunks, chunk_body, tuple(m_init), unroll=2)
    for scan, (s, d) in enumerate(scans):
        for h in range(ML_HEADS):
            c_aug = caug_ref[s, d, h]
            cout_refs[s][d, h] = c_aug[:, :dh]
            nout_refs[s][d, h:h + 1, :] = c_aug[:, dh:].T[0:1, :]
    for s in range(spb):
        m_heads = [pltpu.roll(m_fin[2 * s + d], LANES - gate_lanes(d, 0)[1], 1) for d in range(2)]
        mout_refs[s][...] = jnp.concatenate(m_heads + [jnp.zeros((COND_ROWS - 2, LANES), F32)], axis=0)

    for h in range(ML_HEADS):
        hl = slice(h * dh, (h + 1) * dh)
        h_sum = hdir_ref[0, :, hl] + hdir_ref[1, :, hl]
        gated = jax.nn.sigmoid(o_ref[:, hl]) * (_rms_scale(h_sum) * hn_ref[:, hl])
        out_ref[:, hl] = gated.astype(out_ref.dtype)


def _mlstm(proj, keys_t, gates, gate_b_rows, hnorm_rows, layer, seq_len, n_seq, row0, spb, n_stack=1, jb=0,
           prev=None, state=None):
    has_state = state is not None
    assert n_seq % spb == 0 and row0 % (spb * seq_len) == 0
    r0 = row0 // (spb * seq_len)
    rows = spb * seq_len
    nb = TN_PROJ // ML_WIDTH
    n_chunks = seq_len // ML_CHUNK
    in_specs = [
        pl.BlockSpec((rows, ML_WIDTH), lambda b: (r0 + b, P_QM * nb)),
        pl.BlockSpec((spb * n_chunks, ML_WIDTH, ML_CHUNK), lambda b: (r0 + b, 0, 0)),
        pl.BlockSpec((rows, ML_WIDTH), lambda b: (r0 + b, P_VM * nb)),
        pl.BlockSpec((rows, ML_WIDTH), lambda b: (r0 + b, P_OM * nb)),
        pl.BlockSpec((rows, LANES), lambda b: (r0 + b, 0)),
        pl.BlockSpec((None, 1, LANES), lambda b: (layer, 0, 0)),
        pl.BlockSpec((None, 1, ML_WIDTH), lambda b: (layer, 0, 0)),
    ]
    args = [proj, keys_t, proj, proj, gates, gate_b_rows, hnorm_rows]
    aliases = {}
    if has_state:
        st_c, st_n, st_m = state
        in_specs += [
            pl.BlockSpec((spb, None, 2, ML_HEADS, ML_HEAD_DIM, ML_HEAD_DIM), lambda b: (b, layer, 0, 0, 0, 0)),
            pl.BlockSpec((spb, None, 2, ML_HEADS, ML_HEAD_DIM), lambda b: (b, layer, 0, 0, 0)),
            pl.BlockSpec(memory_space=pltpu.SMEM),
        ]
        args += [st_c, st_n, st_m.reshape(-1)]
    elif prev is not None:
        aliases = {len(args) + k: 1 + k for k in range(3)}
        in_specs += [pl.BlockSpec(memory_space=pl.ANY)] * 3
        args += list(prev)
    fills_stack = prev is None
    c_bytes = 2 * ML_HEADS * ML_HEAD_DIM * ML_HEAD_DIM
    blocks = 4 * (5 * rows * ML_WIDTH + rows * LANES + spb * (1 + n_stack) * c_bytes)
    scratch = 4 * (2 * rows * ML_WIDTH + spb * 2 * c_bytes)
    return pl.pallas_call(
        functools.partial(_mlstm_kernel, n_chunks=n_chunks, has_state=has_state, jb=jb, fills_stack=fills_stack,
                          spb=spb, layer=layer, n_layers=N_AB),
        grid=(n_seq // spb,),
        in_specs=in_specs,
        out_specs=[
            pl.BlockSpec((rows, ML_WIDTH), lambda b: (b, 0)),
            _stack_spec((2, ML_HEADS, ML_HEAD_DIM, ML_HEAD_DIM), jb, fills_stack, n_stack, spb),
            _stack_spec((2, ML_HEADS, ML_HEAD_DIM), jb, fills_stack, n_stack, spb),
            _stack_spec((COND_ROWS, LANES), jb, fills_stack, n_stack, spb),
        ],
        out_shape=[
            jax.ShapeDtypeStruct((n_seq * seq_len, ML_WIDTH), BF16),
            jax.ShapeDtypeStruct((n_seq, n_stack, 2, ML_HEADS, ML_HEAD_DIM, ML_HEAD_DIM), F32),
            jax.ShapeDtypeStruct((n_seq, n_stack, 2, ML_HEADS, ML_HEAD_DIM), F32),
            jax.ShapeDtypeStruct((n_seq, n_stack, COND_ROWS, LANES), F32),
        ],
        scratch_shapes=[pltpu.VMEM((2, rows, ML_WIDTH), F32),
                        pltpu.VMEM((spb, 2, ML_HEADS, ML_HEAD_DIM, 2 * ML_HEAD_DIM), F32)],
        input_output_aliases=aliases,
        compiler_params=_params(("parallel",), blocks, scratch),
        name=f"mlstm_{'latent' if has_state else 'context'}_{jb}",
    )(*args)


def _outproj_kernel(x_ref, mod_ref, g_ref, ap_ref, mp_ref, as_ref, ms_ref, wa_ref, wm_ref, o_ref):
    i = pl.program_id(0)

    def finish(a_ref, m_ref):
        wa = wa_ref[...].astype(BF16)
        wm = wm_ref[...].astype(BF16)
        for rows, r in _sub_rows(i, TM_OUT):
            out = _dot(a_ref[rows, :], wa) + _dot(m_ref[rows, :], wm)
            o_ref[rows, :] = _gated_residual(x_ref[rows, :], out, g_ref, mod_ref, r, 1, 1.0)

    is_prompt = i < N_PROMPT // TM_OUT
    pl.when(is_prompt)(functools.partial(finish, ap_ref, mp_ref))
    pl.when(jnp.logical_not(is_prompt))(functools.partial(finish, as_ref, ms_ref))


def _outproj(y, mod_l, g_l, a_p, m_p, a_s, m_s, ab_w_out, jb, l):
    tm = TM_OUT
    n_p = N_PROMPT // tm
    n_s = N_SAMPLE // tm
    p_spec = pl.BlockSpec((tm, NA_WIDTH), lambda i: (jnp.minimum(i, n_p - 1), 0))
    s_spec = pl.BlockSpec((tm, NA_WIDTH), lambda i: (jnp.clip(i - n_p, 0, n_s - 1), 0))
    blocks = 4 * (2 * tm * D_MODEL + 4 * tm * NA_WIDTH + D_MODEL * D_MODEL)
    return pl.pallas_call(
        _outproj_kernel,
        grid=(N_TOK // tm,),
        in_specs=[
            pl.BlockSpec((tm, D_MODEL), lambda i: (i, 0)),
            *_cond_specs(l),
            p_spec, p_spec, s_spec, s_spec,
            pl.BlockSpec((None, NA_WIDTH, D_MODEL), lambda i: (jb, 0, 0)),
            pl.BlockSpec((None, ML_WIDTH, D_MODEL), lambda i: (jb, 1, 0)),
        ],
        out_specs=pl.BlockSpec((tm, D_MODEL), lambda i: (i, 0)),
        out_shape=jax.ShapeDtypeStruct((N_TOK, D_MODEL), F32),
        compiler_params=_params(("parallel",), blocks),
        name=f"outproj_{jb}",
    )(y, mod_l, g_l, a_p, m_p, a_s, m_s, ab_w_out, ab_w_out)


def kernel(x_prompt, x_sample, cache_k, cache_v, state_C, state_n, state_m, c, c_ctx, ada_w, ada_b, norm_g,
           ffn_w13, ffn_w2, ab_w_in, ab_w_out, na_rpb, ml_gate_b, ml_hnorm, sc_w_in, sc_conv_w, sc_conv_b,
           sc_w_out):
    assert NA_WIDTH == ML_WIDTH == TN_PROJ and AB_MAIN % N_GATES == 0
    assert math.frexp(NA_HEAD_DIM ** -0.5)[0] == 0.5
    assert LANES == 2 * NA_HEAD_DIM
    assert N_PROMPT % TM == 0 and N_SAMPLE % TM == 0 and TM % COND_UNIT == 0
    cond = jnp.concatenate([c_ctx[None, :], c, jnp.zeros((COND_ROWS - 1 - DEC_BATCH, D_MODEL), F32)], axis=0)
    mod = _modulation(cond, ada_w, ada_b)
    gains = norm_g.reshape(DEPTH, 6, D_MODEL)
    bias_pairs = _na_bias(na_rpb)
    w_in_t = jnp.swapaxes(ab_w_in, 1, 2)
    cache_k_t = jnp.swapaxes(cache_k, 3, 4)
    cache_v_t = jnp.swapaxes(cache_v, 3, 4)
    gate_b_rows = jnp.pad(ml_gate_b.reshape(N_AB, 1, N_GATES), ((0, 0), (0, 0), (0, LANES - N_GATES)))
    hnorm_rows = ml_hnorm.reshape(N_AB, 1, ML_WIDTH)
    state = (state_C, state_n, state_m)

    kv_new = None
    st_new = None
    y = None
    for l in range(DEPTH):
        if l == 0:
            y = _ffn_sublayer(x_prompt.reshape(N_PROMPT, D_MODEL), mod, gains, ffn_w13, ffn_w2, l, 0,
                              y_tail=x_sample.reshape(N_SAMPLE, D_MODEL))
        else:
            y = _ffn_sublayer(y, mod, gains, ffn_w13, ffn_w2, l, 0)
        j = l // 2
        if l % 2 == 0:
            proj, keys_t, gates = _inproj(y, mod, gains, w_in_t, j, l)
            a_p, *kv_new = _ctx_attention(proj, j, kv_new)
            m_p, *st_new = _mlstm(proj, keys_t, gates, gate_b_rows, hnorm_rows, j, SEQ, BATCH, 0,
                                  ML_SEQS_PER_STEP, n_stack=N_AB, jb=j, prev=st_new)
            a_s = _na_attention(proj, cache_k_t, cache_v_t, bias_pairs, j)
            m_s, _, _, _ = _mlstm(proj, keys_t, gates, gate_b_rows, hnorm_rows, j, DEC_SEQ, DEC_BATCH, N_PROMPT,
                                  ML_SEQS_PER_STEP, state=state)
            y = _outproj(y, mod, gains, a_p, m_p, a_s, m_s, ab_w_out, j, l)
        else:
            y = _sconv_sublayer(y, mod, gains, sc_w_in, sc_conv_w, sc_conv_b, sc_w_out, j, l)
        if l < DEPTH - 1:
            y = _ffn_sublayer(y, mod, gains, ffn_w13, ffn_w2, l, 1)

    p_tiles = N_PROMPT // TM
    y_p = _ffn_sublayer(y, mod, gains, ffn_w13, ffn_w2, DEPTH - 1, 1, 0, p_tiles)
    y_s = _ffn_sublayer(y, mod, gains, ffn_w13, ffn_w2, DEPTH - 1, 1, p_tiles, N_TOK // TM - p_tiles)
    y_p = y_p.reshape(BATCH, SEQ, D_MODEL)
    y_s = y_s.reshape(DEC_BATCH, DEC_SEQ, D_MODEL)
    new_c, new_n, m_rows = st_new
    new_m = m_rows[:, :, :2, :ML_HEADS]
    return (y_p, y_s, jnp.swapaxes(kv_new[0], 3, 4), jnp.swapaxes(kv_new[1], 3, 4), new_c, new_n, new_m)
```

```python
import functools
import math

import jax
import jax.numpy as jnp
from jax import lax
from jax.experimental import pallas as pl
from jax.experimental.pallas import tpu as pltpu

D_MODEL = 1024
BATCH = 16
SEQ = 256
DEPTH = 4
DEC_BATCH = 2
DEC_SEQ = 1024
PAST_LEN = 256
GRID_W = 64
NA_HEADS = 8
NA_HEAD_DIM = 64
NA_WIN_H = 8
NA_WIN_W = 16
ML_HEADS = 4
ML_HEAD_DIM = 128
ML_CHUNK = 128
CONV_WIDTH = 3
D_FF = 2816
N_MOD = 9
EPS = 1e-6
N_AB = (DEPTH + 1) // 2
N_C = DEPTH // 2
NA_WIDTH = NA_HEADS * NA_HEAD_DIM
ML_WIDTH = ML_HEADS * ML_HEAD_DIM
AB_MAIN = 3 * NA_WIDTH + 4 * ML_WIDTH
N_GATES = 4 * ML_HEADS
AB_IN = AB_MAIN + N_GATES
SC_WIDTH = D_MODEL

N_PROMPT = BATCH * SEQ
N_SAMPLE = DEC_BATCH * DEC_SEQ
N_TOK = N_PROMPT + N_SAMPLE
GRID_ROWS = DEC_SEQ // GRID_W
NA_KH = min(NA_WIN_H, GRID_ROWS)
N_DR = 2 * NA_WIN_H - 1
N_DC = 2 * NA_WIN_W - 1
N_PAIR = N_DR - 1

LANES = 128
COND_ROWS = 8
VMEM_CAP = 60 * 1024 * 1024
TEMP_BYTES = 16 * 1024 * 1024

COND_UNIT = DEC_SEQ
TM = 2048
SUB_ROWS = 512
TM_OUT = 1024
TM_CONV = COND_UNIT
TF_CONV = 512
TF = 256
TN_PROJ = 512
TN_MOD = 1536
NA_QROWS = 8
ML_SEQS_PER_STEP = 2

PROJ_BLOCKS = (0, 1, 2, 3, 5, 6)
KM_BLOCK = 4
P_QA, P_KA, P_VA, P_QM, P_VM, P_OM = range(6)

F32 = jnp.float32
BF16 = jnp.bfloat16


def _vmem_limit(block_bytes, scratch_bytes):
    assert 2 * block_bytes + scratch_bytes <= VMEM_CAP
    return VMEM_CAP


def _params(semantics, block_bytes, scratch_bytes=0):
    return pltpu.CompilerParams(dimension_semantics=semantics,
                                vmem_limit_bytes=_vmem_limit(block_bytes, scratch_bytes))


def _unit_rows(i, tm=TM):
    first_latent = N_PROMPT // COND_UNIT
    units = tm // COND_UNIT
    return [(slice(u * COND_UNIT, (u + 1) * COND_UNIT), jnp.maximum(i * units + u - (first_latent - 1), 0))
            for u in range(units)]


def _sub_rows(i, tm=TM):
    per_unit = COND_UNIT // SUB_ROWS
    return [(slice(rows.start + s * SUB_ROWS, rows.start + (s + 1) * SUB_ROWS), r)
            for rows, r in _unit_rows(i, tm) for s in range(per_unit)]


def _rms_scale(x):
    return x * lax.rsqrt(jnp.mean(x * x, axis=-1, keepdims=True) + EPS)


def _modulate_in(x, g_ref, mod_ref, r, j):
    gain = g_ref[2 * j:2 * j + 1, :] * (1.0 + mod_ref[r, 3 * j + 1:3 * j + 2, :])
    return _rms_scale(x) * gain + mod_ref[r, 3 * j:3 * j + 1, :]


def _gated_residual(x, out, g_ref, mod_ref, r, j, weight):
    gain = (weight * mod_ref[r, 3 * j + 2:3 * j + 3, :]) * g_ref[2 * j + 1:2 * j + 2, :]
    return x + _rms_scale(out) * gain


def _cond_specs(l):
    return [pl.BlockSpec((None, COND_ROWS, N_MOD, D_MODEL), lambda *_: (l, 0, 0, 0)),
            pl.BlockSpec((None, 6, D_MODEL), lambda *_: (l, 0, 0))]


def _dot(a, b):
    return jnp.dot(a.astype(BF16), b.astype(BF16), preferred_element_type=F32)


def _dot_nt(a, b):
    return lax.dot_general(a.astype(BF16), b.astype(BF16), (((1,), (1,)), ((), ())),
                           preferred_element_type=F32)


def _mod_kernel(c_ref, w_ref, b_ref, o_ref):
    c = c_ref[...]
    s = c * jax.nn.sigmoid(c)
    o_ref[...] = _dot(s, w_ref[...]) + b_ref[...]


def _modulation(cond, ada_w, ada_b):
    n = N_MOD * D_MODEL
    out = pl.pallas_call(
        _mod_kernel,
        grid=(DEPTH, n // TN_MOD),
        in_specs=[
            pl.BlockSpec((COND_ROWS, D_MODEL), lambda l, t: (0, 0)),
            pl.BlockSpec((None, D_MODEL, TN_MOD), lambda l, t: (l, 0, t)),
            pl.BlockSpec((None, 1, TN_MOD), lambda l, t: (l, 0, t)),
        ],
        out_specs=pl.BlockSpec((None, COND_ROWS, TN_MOD), lambda l, t: (l, 0, t)),
        out_shape=jax.ShapeDtypeStruct((DEPTH, COND_ROWS, n), F32),
        compiler_params=_params(("parallel", "parallel"), 4 * D_MODEL * TN_MOD),
        name="modulation",
    )(cond, ada_w, ada_b.reshape(DEPTH, 1, n))
    return out.reshape(DEPTH, COND_ROWS, N_MOD, D_MODEL)


def _ffn_kernel(*refs, j, tile0, n_head):
    i = pl.program_id(0) + tile0
    k = pl.program_id(1)
    last = pl.num_programs(1) - 1
    if n_head is None:
        x_ref, mod_ref, g_ref, wa_ref, wg_ref, w2_ref, o_ref, h_ref = refs
        sources = [(None, x_ref)]
    else:
        x_ref, x_tail_ref, mod_ref, g_ref, wa_ref, wg_ref, w2_ref, o_ref, h_ref = refs
        sources = [(pl.program_id(0) < n_head, x_ref), (pl.program_id(0) >= n_head, x_tail_ref)]

    def per_source(fn):
        for pred, ref in sources:
            if pred is None:
                fn(ref)
            else:
                pl.when(pred)(functools.partial(fn, ref))

    def hidden_step(rows, init):
        h = h_ref[rows, :]
        a = _dot(h, wa_ref[...])
        g = _dot(h, wg_ref[...])
        d = _dot((g * jax.nn.sigmoid(g)) * a, w2_ref[...])
        if init:
            o_ref[rows, :] = d
        else:
            o_ref[rows, :] += d

    def first_step(x_src):
        for rows, r in _sub_rows(i):
            h_ref[rows, :] = _modulate_in(x_src[rows, :], g_ref, mod_ref, r, j).astype(BF16)
            hidden_step(rows, init=True)

    def last_step(x_src):
        for rows, r in _sub_rows(i):
            hidden_step(rows, init=False)
            o_ref[rows, :] = _gated_residual(x_src[rows, :], o_ref[rows, :], g_ref, mod_ref, r, j, 0.5)

    pl.when(k == 0)(functools.partial(per_source, first_step))

    @pl.when((k > 0) & (k < last))
    def _():
        for rows, _ in _unit_rows(i):
            hidden_step(rows, init=False)

    pl.when(k == last)(functools.partial(per_source, last_step))


def _ffn_sublayer(y, mod, gains, ffn_w13, ffn_w2, l, s, tile0=0, n_tiles=N_TOK // TM, y_tail=None):
    j = 2 * s
    nk = D_FF // TF
    blocks = 4 * (2 * TM * D_MODEL + 3 * D_MODEL * TF)
    scratch = 2 * TM * D_MODEL
    if y_tail is None:
        n_head = None
        x_specs = [pl.BlockSpec((TM, D_MODEL), lambda i, k: (tile0 + i, 0))]
        xs = [y]
    else:
        assert tile0 == 0
        n_head = y.shape[0] // TM
        x_specs = [pl.BlockSpec((TM, D_MODEL), lambda i, k: (jnp.minimum(i, n_head - 1), 0)),
                   pl.BlockSpec((TM, D_MODEL), lambda i, k: (jnp.maximum(i - n_head, 0), 0),
                                pipeline_mode=pl.Buffered(1))]
        xs = [y, y_tail]
        scratch += 4 * TM * D_MODEL
    return pl.pallas_call(
        functools.partial(_ffn_kernel, j=j, tile0=tile0, n_head=n_head),
        grid=(n_tiles, nk),
        in_specs=[
            *x_specs,
            *_cond_specs(l),
            pl.BlockSpec((None, None, D_MODEL, TF), lambda i, k: (l, s, 0, k)),
            pl.BlockSpec((None, None, D_MODEL, TF), lambda i, k: (l, s, 0, k + nk)),
            pl.BlockSpec((None, None, TF, D_MODEL), lambda i, k: (l, s, k, 0)),
        ],
        out_specs=pl.BlockSpec((TM, D_MODEL), lambda i, k: (i, 0)),
        out_shape=jax.ShapeDtypeStruct((n_tiles * TM, D_MODEL), F32),
        scratch_shapes=[pltpu.VMEM((TM, D_MODEL), BF16)],
        compiler_params=_params(("parallel", "arbitrary"), blocks, scratch),
        name=f"ffn_l{l}_s{s}_t{tile0}",
    )(*xs, mod, gains, ffn_w13, ffn_w13, ffn_w2)


def _sconv_kernel(x_ref, mod_ref, g_ref, wb_ref, wc_ref, wx_ref, cw_ref, cb_ref, wo_ref, o_ref, h_ref):
    i = pl.program_id(0)
    k = pl.program_id(1)
    ((_, r),) = _unit_rows(i, TM_CONV)
    seq = jnp.where(i < N_PROMPT // TM_CONV, SEQ, DEC_SEQ)
    blk = SEQ
    n_blk = TM_CONV // blk
    row = lax.broadcasted_iota(jnp.int32, (blk, TF_CONV), 0)
    zero_row = jnp.zeros((1, TF_CONV), F32)

    def step(first, last):
        wb = wb_ref[...].astype(BF16)
        wc = wc_ref[...].astype(BF16)
        wx = wx_ref[...].astype(BF16)
        wo = wo_ref[...].astype(BF16)
        gbs, us = [], []
        for bi in range(n_blk):
            rows = slice(bi * blk, (bi + 1) * blk)
            if first:
                h_ref[rows, :] = _modulate_in(x_ref[rows, :], g_ref, mod_ref, r, 1).astype(BF16)
            h = h_ref[rows, :]
            gbs.append(_dot(h, wb))
            us.append(_dot(h, wc) * _dot(h, wx))
        for bi in range(n_blk):
            rows = slice(bi * blk, (bi + 1) * blk)
            ub = us[bi]
            t = (row + bi * blk) & (seq - 1)
            above = us[bi - 1][blk - 1:blk, :] if bi > 0 else zero_row
            below = us[bi + 1][0:1, :] if bi < n_blk - 1 else zero_row
            u_prev = jnp.where(t == 0, 0.0, jnp.where(row == 0, above, pltpu.roll(ub, 1, 0)))
            u_next = jnp.where(t == seq - 1, 0.0, jnp.where(row == blk - 1, below, pltpu.roll(ub, blk - 1, 0)))
            y = cb_ref[...] + u_prev * cw_ref[0:1, :] + ub * cw_ref[1:2, :] + u_next * cw_ref[2:3, :]
            d = _dot(gbs[bi] * y, wo)
            acc = d if first else o_ref[rows, :] + d
            o_ref[rows, :] = _gated_residual(x_ref[rows, :], acc, g_ref, mod_ref, r, 1, 1.0) if last else acc

    n_steps = SC_WIDTH // TF_CONV
    for kk in range(n_steps):
        pl.when(k == kk)(functools.partial(step, kk == 0, kk == n_steps - 1))


def _sconv_sublayer(y, mod_l, g_l, sc_w_in, sc_conv_w, sc_conv_b, sc_w_out, jc, l):
    tm, tf = TM_CONV, TF_CONV
    nk = SC_WIDTH // tf
    blocks = 4 * (2 * tm * D_MODEL + 4 * D_MODEL * tf)
    scratch = 2 * tm * D_MODEL
    return pl.pallas_call(
        _sconv_kernel,
        grid=(N_TOK // tm, nk),
        in_specs=[
            pl.BlockSpec((tm, D_MODEL), lambda i, k: (i, 0)),
            *_cond_specs(l),
            pl.BlockSpec((None, D_MODEL, tf), lambda i, k: (jc, 0, k)),
            pl.BlockSpec((None, D_MODEL, tf), lambda i, k: (jc, 0, k + nk)),
            pl.BlockSpec((None, D_MODEL, tf), lambda i, k: (jc, 0, k + 2 * nk)),
            pl.BlockSpec((None, CONV_WIDTH, tf), lambda i, k: (jc, 0, k)),
            pl.BlockSpec((None, 1, tf), lambda i, k: (jc, 0, k)),
            pl.BlockSpec((None, tf, D_MODEL), lambda i, k: (jc, k, 0)),
        ],
        out_specs=pl.BlockSpec((tm, D_MODEL), lambda i, k: (i, 0)),
        out_shape=jax.ShapeDtypeStruct((N_TOK, D_MODEL), F32),
        scratch_shapes=[pltpu.VMEM((tm, D_MODEL), BF16)],
        compiler_params=_params(("parallel", "arbitrary"), blocks, scratch),
        name=f"sconv_{jc}",
    )(y, mod_l, g_l, sc_w_in, sc_w_in, sc_w_in, sc_conv_w, sc_conv_b.reshape(N_C, 1, SC_WIDTH), sc_w_out)


def _inproj_kernel(x_ref, mod_ref, g_ref, w_ref, wk_ref, wg_ref, gb_ref, o_ref, okt_ref, og_ref, h_ref):
    i = pl.program_id(0)
    n = pl.program_id(1)
    n_main = len(PROJ_BLOCKS)

    @pl.when(n == 0)
    def _():
        for rows, r in _sub_rows(i):
            h_ref[rows, :] = _modulate_in(x_ref[rows, :], g_ref, mod_ref, r, 1).astype(BF16)
            o_ref[rows, :] = _dot_nt(h_ref[rows, :], w_ref[...])

    @pl.when((n > 0) & (n < n_main))
    def _():
        o_ref[...] = _dot_nt(h_ref[...], w_ref[...])

    @pl.when(n == n_main)
    def _():
        w_kg = jnp.concatenate([wk_ref[...], wg_ref[...]], axis=0)
        ktg = _dot_nt(w_kg, h_ref[...])
        for c in range(TM // ML_CHUNK):
            okt_ref[c] = ktg[:ML_WIDTH, c * ML_CHUNK:(c + 1) * ML_CHUNK]
        g = ktg[ML_WIDTH:, :] + gb_ref[...]
        gate = lax.broadcasted_iota(jnp.int32, g.shape, 0)
        is_forget = ((gate >= ML_HEADS) & (gate < 2 * ML_HEADS)) | (gate >= 3 * ML_HEADS)
        log_sig = jnp.minimum(g, 0.0) - jnp.log1p(jnp.exp(-jnp.abs(g)))
        g = jnp.where(is_forget, log_sig, g)
        gates_t = jnp.concatenate([g, jnp.zeros((LANES - N_GATES, TM), F32)], axis=0)
        og_ref[...] = gates_t.T


def _inproj(y, mod_l, g_l, w_in_t, gate_b_cols, jb, l):
    n_main = len(PROJ_BLOCKS)
    assert PROJ_BLOCKS == tuple(b for b in range(n_main + 1) if b != KM_BLOCK)

    def w_block(i, n):
        m = jnp.minimum(n, n_main - 1)
        return (jb, m + (m >= KM_BLOCK).astype(jnp.int32), 0)

    cpt = TM // ML_CHUNK
    blocks = 4 * (TM * D_MODEL + 2 * TN_PROJ * D_MODEL + N_GATES * D_MODEL + TM * TN_PROJ
                  + ML_WIDTH * TM + TM * LANES)
    return pl.pallas_call(
        _inproj_kernel,
        grid=(N_TOK // TM, n_main + 1),
        in_specs=[
            pl.BlockSpec((TM, D_MODEL), lambda i, n: (i, 0)),
            *_cond_specs(l),
            pl.BlockSpec((None, TN_PROJ, D_MODEL), w_block),
            pl.BlockSpec((None, ML_WIDTH, D_MODEL), lambda i, n: (jb, KM_BLOCK, 0)),
            pl.BlockSpec((None, N_GATES, D_MODEL), lambda i, n: (jb, AB_MAIN // N_GATES, 0)),
            pl.BlockSpec((None, N_GATES, 1), lambda i, n: (jb, 0, 0)),
        ],
        out_specs=[
            pl.BlockSpec((TM, TN_PROJ), lambda i, n: (i, jnp.minimum(n, n_main - 1))),
            pl.BlockSpec((cpt, ML_WIDTH, ML_CHUNK), lambda i, n: (i, 0, 0)),
            pl.BlockSpec((TM, LANES), lambda i, n: (i, 0)),
        ],
        out_shape=[jax.ShapeDtypeStruct((N_TOK, n_main * TN_PROJ), F32),
                   jax.ShapeDtypeStruct((N_TOK // ML_CHUNK, ML_WIDTH, ML_CHUNK), F32),
                   jax.ShapeDtypeStruct((N_TOK, LANES), F32)],
        scratch_shapes=[pltpu.VMEM((TM, D_MODEL), BF16)],
        compiler_params=_params(("parallel", "arbitrary"), blocks, 2 * TM * D_MODEL),
        name=f"inproj_{jb}",
    )(y, mod_l, g_l, w_in_t, w_in_t, w_in_t, gate_b_cols)


def _stack_entry(ref, jb, fills_stack):
    if not fills_stack:
        return ref
    for j in range(ref.shape[0]):
        if j != jb:
            ref[j] = jnp.zeros(ref.shape[1:], ref.dtype)
    return ref.at[jb]


def _stack_spec(entry_shape, jb, fills_stack, n_stack=N_AB, lead=None):
    zeros = (0,) * len(entry_shape)
    if fills_stack:
        return pl.BlockSpec((lead, n_stack) + entry_shape, lambda b: (b, 0) + zeros)
    return pl.BlockSpec((lead, None) + entry_shape, lambda b: (b, jb) + zeros)


def _ctx_attn_kernel(q_ref, k_ref, v_ref, *rest, jb, fills_stack):
    o_ref = rest[-3]
    kc_ref = _stack_entry(rest[-2], jb, fills_stack)
    vc_ref = _stack_entry(rest[-1], jb, fills_stack)
    scale = NA_HEAD_DIM ** -0.5
    k_t = k_ref[...].T
    v_t = v_ref[...].T
    for h in range(NA_HEADS):
        sl = slice(h * NA_HEAD_DIM, (h + 1) * NA_HEAD_DIM)
        kc_ref[h] = k_t[sl, :]
        vc_ref[h] = v_t[sl, :]
    lane = lax.broadcasted_iota(jnp.int32, (SEQ, LANES), 1)
    for hp in range(NA_WIDTH // LANES):
        pair = slice(hp * LANES, (hp + 1) * LANES)
        q_pair = q_ref[:, pair] * scale
        outs = []
        for hh in range(LANES // NA_HEAD_DIM):
            own = (lane >= hh * NA_HEAD_DIM) & (lane < (hh + 1) * NA_HEAD_DIM)
            s = _dot(jnp.where(own, q_pair, 0.0), k_t[pair, :])
            p = jnp.exp(s - jnp.max(s, axis=-1, keepdims=True))
            p = p * (1.0 / jnp.sum(p, axis=-1, keepdims=True))
            outs.append(_dot(p, v_ref[:, pair]))
        o_ref[:, pair] = jnp.where(lane < NA_HEAD_DIM, outs[0], outs[1]).astype(o_ref.dtype)


def _ctx_attention(proj, jb, prev):
    cache = jax.ShapeDtypeStruct((BATCH, N_AB, NA_HEADS, NA_HEAD_DIM, SEQ), F32)
    cache_spec = _stack_spec((NA_HEADS, NA_HEAD_DIM, SEQ), jb, prev is None)
    in_specs = [pl.BlockSpec((SEQ, NA_WIDTH), lambda b, c=c: (b, c)) for c in (P_QA, P_KA, P_VA)]
    args = [proj, proj, proj]
    aliases = {}
    if prev is not None:
        in_specs += [pl.BlockSpec(memory_space=pl.ANY)] * 2
        args += list(prev)
        aliases = {3: 1, 4: 2}
    blocks = 4 * (4 * SEQ * NA_WIDTH + 2 * N_AB * NA_WIDTH * SEQ)
    return pl.pallas_call(
        functools.partial(_ctx_attn_kernel, jb=jb, fills_stack=prev is None),
        grid=(BATCH,),
        in_specs=in_specs,
        out_specs=[pl.BlockSpec((SEQ, NA_WIDTH), lambda b: (b, 0)), cache_spec, cache_spec],
        out_shape=[jax.ShapeDtypeStruct((N_PROMPT, NA_WIDTH), BF16), cache, cache],
        input_output_aliases=aliases,
        compiler_params=_params(("parallel",), blocks),
        name=f"ctx_attention_{jb}",
    )(*args)


def _na_bias_kernel(rpb_ref, p_ref, e_ref):
    base = (pl.program_id(0) * NA_HEADS + pl.program_id(1)) * (N_DR * N_DC)
    shape = (GRID_W, LANES)
    lane = lax.broadcasted_iota(jnp.int32, shape, 1)
    qc = lax.broadcasted_iota(jnp.int32, shape, 0)
    kc = lane & (GRID_W - 1)
    upper = lane >= GRID_W
    cs = jnp.clip(qc - NA_WIN_W // 2, 0, GRID_W - NA_WIN_W)
    col_in = (kc >= cs) & (kc < cs + NA_WIN_W)
    dc = jnp.clip(kc - qc, -(NA_WIN_W - 1), NA_WIN_W - 1) + (NA_WIN_W - 1)
    upper_row = upper[0:1, :]
    n_even = (N_DR + 1) // 2
    group = 4
    for e0 in range(0, n_even, group):
        es = range(e0, min(e0 + group, n_even))
        accs = [jnp.zeros(shape, F32) for _ in es]
        for d in range(N_DC):
            hit = dc == d
            for a, e in enumerate(es):
                lo = rpb_ref[base + 2 * e * N_DC + d]
                hi = rpb_ref[base + (2 * e + 1) * N_DC + d] if 2 * e + 1 < N_DR else 0.0
                accs[a] = jnp.where(hit, jnp.where(upper_row, hi, lo), accs[a])
        for a, e in enumerate(es):
            e_ref[e] = jnp.where(col_in, accs[a], -jnp.inf)
    for pair in range(N_PAIR):
        e = pair // 2
        if pair % 2 == 0:
            p_ref[pair] = e_ref[e]
        else:
            p_ref[pair] = jnp.where(upper, pltpu.roll(e_ref[e + 1], GRID_W, 1), pltpu.roll(e_ref[e], GRID_W, 1))


def _na_bias(na_rpb):
    return pl.pallas_call(
        _na_bias_kernel,
        grid=(N_AB, NA_HEADS),
        in_specs=[pl.BlockSpec(memory_space=pltpu.SMEM)],
        out_specs=pl.BlockSpec((None, None, N_PAIR, GRID_W, LANES), lambda j, h: (j, h, 0, 0, 0)),
        out_shape=jax.ShapeDtypeStruct((N_AB, NA_HEADS, N_PAIR, GRID_W, LANES), F32),
        scratch_shapes=[pltpu.VMEM(((N_DR + 1) // 2, GRID_W, LANES), F32)],
        compiler_params=_params(("parallel", "parallel"), 4 * N_PAIR * GRID_W * LANES,
                                4 * ((N_DR + 1) // 2) * GRID_W * LANES),
        name="na_bias",
    )(na_rpb.reshape(-1))


def _na_window_start(r):
    return min(max(r - NA_KH // 2, 0), GRID_ROWS - NA_KH)


def _na_bias_block(p_ref, hh, qr, kr):
    rs = _na_window_start(qr)
    lo_ok = rs <= kr < rs + NA_KH
    hi_ok = rs <= kr + 1 < rs + NA_KH
    if not (lo_ok or hi_ok):
        return jnp.full((GRID_W, LANES), -jnp.inf, F32)
    dr = kr - qr + (NA_WIN_H - 1)
    assert 0 <= dr < N_PAIR
    blk = p_ref[hh, dr]
    if lo_ok and hi_ok:
        return blk
    lane = lax.broadcasted_iota(jnp.int32, (GRID_W, LANES), 1)
    keep = (lane < GRID_W) if lo_ok else (lane >= GRID_W)
    return jnp.where(keep, blk, -jnp.inf)


def _na_kernel(q_ref, k_ref, v_ref, kc_ref, vc_ref, p_ref, o_ref):
    scale = NA_HEAD_DIM ** -0.5
    n_q = NA_QROWS * GRID_W
    lane = lax.broadcasted_iota(jnp.int32, (n_q, LANES), 1)
    k_ctx_t = kc_ref[...].reshape(LANES, PAST_LEN)
    v_ctx_t = vc_ref[...].reshape(LANES, PAST_LEN)
    for q0 in range(0, GRID_ROWS, NA_QROWS):
        k_lo = _na_window_start(q0) // 2 * 2
        k_hi = -(-(_na_window_start(q0 + NA_QROWS - 1) + NA_KH) // 2) * 2
        q_rows = slice(q0 * GRID_W, (q0 + NA_QROWS) * GRID_W)
        k_rows = slice(k_lo * GRID_W, k_hi * GRID_W)
        q_pair = q_ref[q_rows, :] * scale
        outs = []
        for hh in range(LANES // NA_HEAD_DIM):
            bias = jnp.concatenate(
                [jnp.concatenate([_na_bias_block(p_ref, hh, qr, kr) for kr in range(k_lo, k_hi, 2)], axis=1)
                 for qr in range(q0, q0 + NA_QROWS)], axis=0)
            own = (lane >= hh * NA_HEAD_DIM) & (lane < (hh + 1) * NA_HEAD_DIM)
            q = jnp.where(own, q_pair, 0.0)
            s_loc = _dot_nt(q, k_ref[k_rows, :]) + bias
            s_ctx = _dot(q, k_ctx_t)
            m = jnp.maximum(jnp.max(s_loc, axis=-1, keepdims=True), jnp.max(s_ctx, axis=-1, keepdims=True))
            p_loc = jnp.exp(s_loc - m)
            p_ctx = jnp.exp(s_ctx - m)
            denom = jnp.sum(p_loc, axis=-1, keepdims=True) + jnp.sum(p_ctx, axis=-1, keepdims=True)
            outs.append((_dot(p_loc, v_ref[k_rows, :]) + _dot_nt(p_ctx, v_ctx_t)) / denom)
        o_ref[q_rows, :] = jnp.where(lane < NA_HEAD_DIM, outs[0], outs[1]).astype(o_ref.dtype)


def _na_attention(proj, cache_k_t, cache_v_t, bias_pairs, jb):
    hp = LANES // NA_HEAD_DIM
    s0 = N_PROMPT // DEC_SEQ
    nb = NA_WIDTH // LANES
    ctx_spec = pl.BlockSpec((None, None, hp, NA_HEAD_DIM, PAST_LEN), lambda b, c: (b, jb, c, 0, 0))
    blocks = 4 * (4 * DEC_SEQ * LANES + 2 * hp * NA_HEAD_DIM * PAST_LEN + hp * N_PAIR * GRID_W * LANES)
    return pl.pallas_call(
        _na_kernel,
        grid=(DEC_BATCH, nb),
        in_specs=[
            pl.BlockSpec((DEC_SEQ, LANES), lambda b, c: (s0 + b, P_QA * nb + c)),
            pl.BlockSpec((DEC_SEQ, LANES), lambda b, c: (s0 + b, P_KA * nb + c)),
            pl.BlockSpec((DEC_SEQ, LANES), lambda b, c: (s0 + b, P_VA * nb + c)),
            ctx_spec, ctx_spec,
            pl.BlockSpec((None, hp, N_PAIR, GRID_W, LANES), lambda b, c: (jb, c, 0, 0, 0)),
        ],
        out_specs=pl.BlockSpec((DEC_SEQ, LANES), lambda b, c: (b, c)),
        out_shape=jax.ShapeDtypeStruct((N_SAMPLE, NA_WIDTH), BF16),
        compiler_params=_params(("parallel", "parallel"), blocks),
        name=f"na_attention_{jb}",
    )(proj, proj, proj, cache_k_t, cache_v_t, bias_pairs)


def _running_rows(x, op, identity, reverse):
    n = x.shape[0]
    ridx = lax.broadcasted_iota(jnp.int32, x.shape, 0)
    sh = 1
    while sh < n:
        if reverse:
            x = op(x, jnp.where(ridx < n - sh, pltpu.roll(x, n - sh, 0), identity))
        else:
            x = op(x, jnp.where(ridx >= sh, pltpu.roll(x, sh, 0), identity))
        sh *= 2
    return x


def _mlstm_kernel(*refs, n_chunks, has_state, jb, fills_stack, spb, layer, n_layers):
    if has_state:
        (q_ref, kt_ref, v_ref, o_ref, gt_ref, hn_ref, c0_ref, n0_ref, m0_ref,
         out_ref, cout_ref, nout_ref, mout_ref, hdir_ref, caug_ref) = refs
    else:
        (q_ref, kt_ref, v_ref, o_ref, gt_ref, hn_ref, *_unused,
         out_ref, cout_ref, nout_ref, mout_ref, hdir_ref, caug_ref) = refs
    cout_refs = [_stack_entry(cout_ref.at[s], jb, fills_stack) for s in range(spb)]
    nout_refs = [_stack_entry(nout_ref.at[s], jb, fills_stack) for s in range(spb)]
    mout_refs = [_stack_entry(mout_ref.at[s], jb, fills_stack) for s in range(spb)]
    T = ML_CHUNK
    dh = ML_HEAD_DIM
    seq_len = n_chunks * T
    b = pl.program_id(0)
    row = lax.broadcasted_iota(jnp.int32, (T, T), 0)
    col = lax.broadcasted_iota(jnp.int32, (T, T), 1)
    lane = lax.broadcasted_iota(jnp.int32, (T, LANES), 1)
    lane_row = lax.broadcasted_iota(jnp.int32, (1, LANES), 1)
    ones = jnp.ones((T, dh), F32)
    states = [(d, h) for d in range(2) for h in range(ML_HEADS)]

    def gate_lanes(d, h):
        gi = 2 * d * ML_HEADS + h
        return gi, gi + ML_HEADS

    scans = [(s, d) for s in range(spb) for d in range(2)]
    m_init = []
    for s, d in scans:
        m_row = jnp.zeros((1, LANES), F32)
        for h in range(ML_HEADS):
            if has_state:
                n_bcast = jnp.broadcast_to(n0_ref[s, d, h:h + 1, :], (dh, dh)).T
                caug_ref[s, d, h] = jnp.concatenate([c0_ref[s, d, h], n_bcast], axis=1)
                m0 = m0_ref[(((b * spb + s) * n_layers + layer) * 2 + d) * ML_HEADS + h]
                m_row = jnp.where(lane_row == gate_lanes(d, h)[1], m0, m_row)
            else:
                caug_ref[s, d, h] = jnp.zeros((dh, 2 * dh), F32)
        m_init.append(m_row)

    def chunk_body(ci, m_rows):
        m_next = []
        for scan, (s, d) in enumerate(scans):
            causal = (col <= row) if d == 0 else (col >= row)
            last = T - 1 if d == 0 else 0
            c = ci if d == 0 else n_chunks - 1 - ci
            rows = pl.ds(pl.multiple_of(s * seq_len + c * T, T), T)
            m_prev = m_rows[scan]
            mine = (lane >= (2 * d + 1) * ML_HEADS) & (lane < (2 * d + 2) * ML_HEADS)
            g = gt_ref[rows, :]
            cum = _running_rows(g, jnp.add, 0.0, reverse=(d == 1))
            i_al = pltpu.roll(g, ML_HEADS, 1)
            cvec = jnp.where(mine, i_al - cum, 0.0)
            cum = jnp.where(mine, cum, 0.0)
            m_run = jnp.maximum(m_prev, _running_rows(cvec, jnp.maximum, -jnp.inf, reverse=(d == 1)))
            m_t = cum + m_run
            w_inter = jnp.exp(m_prev - m_run)
            e_negm = jnp.exp(-m_t)
            m_run_last = m_run[last:last + 1, :]
            wi_last = w_inter[last:last + 1, :]
            w_last = jnp.exp(cvec - m_run_last)
            packed_t = jnp.where(mine, cvec, pltpu.roll(w_last, LANES - ML_HEADS, 1)).T
            for h in range(ML_HEADS):
                gi, gf = gate_lanes(d, h)
                hl = slice(h * dh, (h + 1) * dh)
                m_run_b = jnp.broadcast_to(m_run[:, gf:gf + 1], (T, T))
                e_negm_b = jnp.broadcast_to(e_negm[:, gf:gf + 1], (T, dh))
                w_d = jnp.exp(jnp.where(causal, packed_t[gf:gf + 1, :] - m_run_b, -jnp.inf))
                w_inter_b = jnp.exp(m_prev[:, gf:gf + 1] - m_run_b)
                qh = q_ref[rows, hl]
                kh_t = kt_ref[s * n_chunks + c, hl, :] * (dh ** -0.5)
                v_aug = jnp.concatenate([v_ref[rows, hl], ones], axis=1)
                sc = _dot(qh, kh_t) * w_d
                c_aug = caug_ref[s, d, h]
                nd = jnp.concatenate([w_inter_b, w_inter_b], axis=1) * _dot(qh, c_aug) + _dot(sc, v_aug)
                hdir_ref[d, rows, hl] = nd[:, :dh] / jnp.maximum(jnp.abs(nd[:, dh:]), e_negm_b)
                kw_t = kh_t * packed_t[gi:gi + 1, :]
                caug_ref[s, d, h] = wi_last[:, gf:gf + 1] * c_aug + _dot(kw_t, v_aug)
            m_next.append(m_t[last:last + 1, :])
        return tuple(m_next)

    m_fin = lax.fori_loop(0, n_chunks, chunk_body, tuple(m_init), unroll=2)
    for scan, (s, d) in enumerate(scans):
        for h in range(ML_HEADS):
            c_aug = caug_ref[s, d, h]
            cout_refs[s][d, h] = c_aug[:, :dh]
            nout_refs[s][d, h:h + 1, :] = c_aug[:, dh:].T[0:1, :]
    for s in range(spb):
        m_heads = [pltpu.roll(m_fin[2 * s + d], LANES - gate_lanes(d, 0)[1], 1) for d in range(2)]
        mout_refs[s][...] = jnp.concatenate(m_heads + [jnp.zeros((COND_ROWS - 2, LANES), F32)], axis=0)

    for h in range(ML_HEADS):
        hl = slice(h * dh, (h + 1) * dh)
        h_sum = hdir_ref[0, :, hl] + hdir_ref[1, :, hl]
        gated = jax.nn.sigmoid(o_ref[:, hl]) * (_rms_scale(h_sum) * hn_ref[:, hl])
        out_ref[:, hl] = gated.astype(out_ref.dtype)


def _mlstm(proj, keys_t, gates, hnorm_rows, layer, seq_len, n_seq, row0, spb, n_stack=1, jb=0,
           prev=None, state=None):
    has_state = state is not None
    assert n_seq % spb == 0 and row0 % (spb * seq_len) == 0
    r0 = row0 // (spb * seq_len)
    rows = spb * seq_len
    nb = TN_PROJ // ML_WIDTH
    n_chunks = seq_len // ML_CHUNK
    in_specs = [
        pl.BlockSpec((rows, ML_WIDTH), lambda b: (r0 + b, P_QM * nb)),
        pl.BlockSpec((spb * n_chunks, ML_WIDTH, ML_CHUNK), lambda b: (r0 + b, 0, 0)),
        pl.BlockSpec((rows, ML_WIDTH), lambda b: (r0 + b, P_VM * nb)),
        pl.BlockSpec((rows, ML_WIDTH), lambda b: (r0 + b, P_OM * nb)),
        pl.BlockSpec((rows, LANES), lambda b: (r0 + b, 0)),
        pl.BlockSpec((None, 1, ML_WIDTH), lambda b: (layer, 0, 0)),
    ]
    args = [proj, keys_t, proj, proj, gates, hnorm_rows]
    aliases = {}
    if has_state:
        st_c, st_n, st_m = state
        in_specs += [
            pl.BlockSpec((spb, None, 2, ML_HEADS, ML_HEAD_DIM, ML_HEAD_DIM), lambda b: (b, layer, 0, 0, 0, 0)),
            pl.BlockSpec((spb, None, 2, ML_HEADS, ML_HEAD_DIM), lambda b: (b, layer, 0, 0, 0)),
            pl.BlockSpec(memory_space=pltpu.SMEM),
        ]
        args += [st_c, st_n, st_m.reshape(-1)]
    elif prev is not None:
        aliases = {len(args) + k: 1 + k for k in range(3)}
        in_specs += [pl.BlockSpec(memory_space=pl.ANY)] * 3
        args += list(prev)
    fills_stack = prev is None
    c_bytes = 2 * ML_HEADS * ML_HEAD_DIM * ML_HEAD_DIM
    blocks = 4 * (5 * rows * ML_WIDTH + rows * LANES + spb * (1 + n_stack) * c_bytes)
    scratch = 4 * (2 * rows * ML_WIDTH + spb * 2 * c_bytes)
    return pl.pallas_call(
        functools.partial(_mlstm_kernel, n_chunks=n_chunks, has_state=has_state, jb=jb, fills_stack=fills_stack,
                          spb=spb, layer=layer, n_layers=N_AB),
        grid=(n_seq // spb,),
        in_specs=in_specs,
        out_specs=[
            pl.BlockSpec((rows, ML_WIDTH), lambda b: (b, 0)),
            _stack_spec((2, ML_HEADS, ML_HEAD_DIM, ML_HEAD_DIM), jb, fills_stack, n_stack, spb),
            _stack_spec((2, ML_HEADS, ML_HEAD_DIM), jb, fills_stack, n_stack, spb),
            _stack_spec((COND_ROWS, LANES), jb, fills_stack, n_stack, spb),
        ],
        out_shape=[
            jax.ShapeDtypeStruct((n_seq * seq_len, ML_WIDTH), BF16),
            jax.ShapeDtypeStruct((n_seq, n_stack, 2, ML_HEADS, ML_HEAD_DIM, ML_HEAD_DIM), F32),
            jax.ShapeDtypeStruct((n_seq, n_stack, 2, ML_HEADS, ML_HEAD_DIM), F32),
            jax.ShapeDtypeStruct((n_seq, n_stack, COND_ROWS, LANES), F32),
        ],
        scratch_shapes=[pltpu.VMEM((2, rows, ML_WIDTH), F32),
                        pltpu.VMEM((spb, 2, ML_HEADS, ML_HEAD_DIM, 2 * ML_HEAD_DIM), F32)],
        input_output_aliases=aliases,
        compiler_params=_params(("parallel",), blocks, scratch),
        name=f"mlstm_{'latent' if has_state else 'context'}_{jb}",
    )(*args)


def _outproj_kernel(x_ref, mod_ref, g_ref, ap_ref, mp_ref, as_ref, ms_ref, wa_ref, wm_ref, o_ref):
    i = pl.program_id(0)

    def finish(a_ref, m_ref):
        wa = wa_ref[...].astype(BF16)
        wm = wm_ref[...].astype(BF16)
        for rows, r in _sub_rows(i, TM_OUT):
            out = _dot(a_ref[rows, :], wa) + _dot(m_ref[rows, :], wm)
            o_ref[rows, :] = _gated_residual(x_ref[rows, :], out, g_ref, mod_ref, r, 1, 1.0)

    is_prompt = i < N_PROMPT // TM_OUT
    pl.when(is_prompt)(functools.partial(finish, ap_ref, mp_ref))
    pl.when(jnp.logical_not(is_prompt))(functools.partial(finish, as_ref, ms_ref))


def _outproj(y, mod_l, g_l, a_p, m_p, a_s, m_s, ab_w_out, jb, l):
    tm = TM_OUT
    n_p = N_PROMPT // tm
    n_s = N_SAMPLE // tm
    p_spec = pl.BlockSpec((tm, NA_WIDTH), lambda i: (jnp.minimum(i, n_p - 1), 0))
    s_spec = pl.BlockSpec((tm, NA_WIDTH), lambda i: (jnp.clip(i - n_p, 0, n_s - 1), 0))
    blocks = 4 * (2 * tm * D_MODEL + 4 * tm * NA_WIDTH + D_MODEL * D_MODEL)
    return pl.pallas_call(
        _outproj_kernel,
        grid=(N_TOK // tm,),
        in_specs=[
            pl.BlockSpec((tm, D_MODEL), lambda i: (i, 0)),
            *_cond_specs(l),
            p_spec, p_spec, s_spec, s_spec,
            pl.BlockSpec((None, NA_WIDTH, D_MODEL), lambda i: (jb, 0, 0)),
            pl.BlockSpec((None, ML_WIDTH, D_MODEL), lambda i: (jb, 1, 0)),
        ],
        out_specs=pl.BlockSpec((tm, D_MODEL), lambda i: (i, 0)),
        out_shape=jax.ShapeDtypeStruct((N_TOK, D_MODEL), F32),
        compiler_params=_params(("parallel",), blocks),
        name=f"outproj_{jb}",
    )(y, mod_l, g_l, a_p, m_p, a_s, m_s, ab_w_out, ab_w_out)


def kernel(x_prompt, x_sample, cache_k, cache_v, state_C, state_n, state_m, c, c_ctx, ada_w, ada_b, norm_g,
           ffn_w13, ffn_w2, ab_w_in, ab_w_out, na_rpb, ml_gate_b, ml_hnorm, sc_w_in, sc_conv_w, sc_conv_b,
           sc_w_out):
    assert NA_WIDTH == ML_WIDTH == TN_PROJ and AB_MAIN % N_GATES == 0
    assert math.frexp(NA_HEAD_DIM ** -0.5)[0] == 0.5
    assert LANES == 2 * NA_HEAD_DIM
    assert N_PROMPT % TM == 0 and N_SAMPLE % TM == 0 and TM % COND_UNIT == 0
    cond = jnp.concatenate([c_ctx[None, :], c, jnp.zeros((COND_ROWS - 1 - DEC_BATCH, D_MODEL), F32)], axis=0)
    mod = _modulation(cond, ada_w, ada_b)
    gains = norm_g.reshape(DEPTH, 6, D_MODEL)
    bias_pairs = _na_bias(na_rpb)
    w_in_t = jnp.swapaxes(ab_w_in, 1, 2)
    cache_k_t = jnp.swapaxes(cache_k, 3, 4)
    cache_v_t = jnp.swapaxes(cache_v, 3, 4)
    gate_b_cols = ml_gate_b.reshape(N_AB, N_GATES, 1)
    hnorm_rows = ml_hnorm.reshape(N_AB, 1, ML_WIDTH)
    state = (state_C, state_n, state_m)

    kv_new = None
    st_new = None
    y = None
    for l in range(DEPTH):
        if l == 0:
            y = _ffn_sublayer(x_prompt.reshape(N_PROMPT, D_MODEL), mod, gains, ffn_w13, ffn_w2, l, 0,
                              y_tail=x_sample.reshape(N_SAMPLE, D_MODEL))
        else:
            y = _ffn_sublayer(y, mod, gains, ffn_w13, ffn_w2, l, 0)
        j = l // 2
        if l % 2 == 0:
            proj, keys_t, gates = _inproj(y, mod, gains, w_in_t, gate_b_cols, j, l)
            a_p, *kv_new = _ctx_attention(proj, j, kv_new)
            m_p, *st_new = _mlstm(proj, keys_t, gates, hnorm_rows, j, SEQ, BATCH, 0,
                                  ML_SEQS_PER_STEP, n_stack=N_AB, jb=j, prev=st_new)
            a_s = _na_attention(proj, cache_k_t, cache_v_t, bias_pairs, j)
            m_s, _, _, _ = _mlstm(proj, keys_t, gates, hnorm_rows, j, DEC_SEQ, DEC_BATCH, N_PROMPT,
                                  ML_SEQS_PER_STEP, state=state)
            y = _outproj(y, mod, gains, a_p, m_p, a_s, m_s, ab_w_out, j, l)
        else:
            y = _sconv_sublayer(y, mod, gains, sc_w_in, sc_conv_w, sc_conv_b, sc_w_out, j, l)
        if l < DEPTH - 1:
            y = _ffn_sublayer(y, mod, gains, ffn_w13, ffn_w2, l, 1)

    p_tiles = N_PROMPT // TM
    y_p = _ffn_sublayer(y, mod, gains, ffn_w13, ffn_w2, DEPTH - 1, 1, 0, p_tiles)
    y_s = _ffn_sublayer(y, mod, gains, ffn_w13, ffn_w2, DEPTH - 1, 1, p_tiles, N_TOK // TM - p_tiles)
    y_p = y_p.reshape(BATCH, SEQ, D_MODEL)
    y_s = y_s.reshape(DEC_BATCH, DEC_SEQ, D_MODEL)
    new_c, new_n, m_rows = st_new
    new_m = m_rows[:, :, :2, :ML_HEADS]
    return (y_p, y_s, jnp.swapaxes(kv_new[0], 3, 4), jnp.swapaxes(kv_new[1], 3, 4), new_c, new_n, new_m)
```

```python
import functools
import math

import jax
import jax.numpy as jnp
from jax import lax
from jax.experimental import pallas as pl
from jax.experimental.pallas import tpu as pltpu

D_MODEL = 1024
BATCH = 16
SEQ = 256
DEPTH = 4
DEC_BATCH = 2
DEC_SEQ = 1024
PAST_LEN = 256
GRID_W = 64
NA_HEADS = 8
NA_HEAD_DIM = 64
NA_WIN_H = 8
NA_WIN_W = 16
ML_HEADS = 4
ML_HEAD_DIM = 128
ML_CHUNK = 128
CONV_WIDTH = 3
D_FF = 2816
N_MOD = 9
EPS = 1e-6
N_AB = (DEPTH + 1) // 2
N_C = DEPTH // 2
NA_WIDTH = NA_HEADS * NA_HEAD_DIM
ML_WIDTH = ML_HEADS * ML_HEAD_DIM
AB_MAIN = 3 * NA_WIDTH + 4 * ML_WIDTH
N_GATES = 4 * ML_HEADS
AB_IN = AB_MAIN + N_GATES
SC_WIDTH = D_MODEL

N_PROMPT = BATCH * SEQ
N_SAMPLE = DEC_BATCH * DEC_SEQ
N_TOK = N_PROMPT + N_SAMPLE
GRID_ROWS = DEC_SEQ // GRID_W
NA_KH = min(NA_WIN_H, GRID_ROWS)
N_DR = 2 * NA_WIN_H - 1
N_DC = 2 * NA_WIN_W - 1
N_PAIR = N_DR - 1

LANES = 128
COND_ROWS = 8
VMEM_CAP = 60 * 1024 * 1024
TEMP_BYTES = 16 * 1024 * 1024

COND_UNIT = DEC_SEQ
TM = 2048
SUB_ROWS = 512
TM_OUT = 1024
TM_CONV = COND_UNIT
TF_CONV = 512
TF = 256
TN_PROJ = 512
TN_MOD = 1536
NA_QROWS = 8
ML_SEQS_PER_STEP = 2

PROJ_BLOCKS = (0, 1, 2, 3, 5, 6)
KM_BLOCK = 4
P_QA, P_KA, P_VA, P_QM, P_VM, P_OM = range(6)

F32 = jnp.float32
BF16 = jnp.bfloat16


def _vmem_limit(block_bytes, scratch_bytes):
    assert 2 * block_bytes + scratch_bytes <= VMEM_CAP
    return VMEM_CAP


def _params(semantics, block_bytes, scratch_bytes=0):
    return pltpu.CompilerParams(dimension_semantics=semantics,
                                vmem_limit_bytes=_vmem_limit(block_bytes, scratch_bytes))


def _unit_rows(i, tm=TM):
    first_latent = N_PROMPT // COND_UNIT
    units = tm // COND_UNIT
    return [(slice(u * COND_UNIT, (u + 1) * COND_UNIT), jnp.maximum(i * units + u - (first_latent - 1), 0))
            for u in range(units)]


def _sub_rows(i, tm=TM):
    per_unit = COND_UNIT // SUB_ROWS
    return [(slice(rows.start + s * SUB_ROWS, rows.start + (s + 1) * SUB_ROWS), r)
            for rows, r in _unit_rows(i, tm) for s in range(per_unit)]


def _rms_scale(x):
    return x * lax.rsqrt(jnp.mean(x * x, axis=-1, keepdims=True) + EPS)


def _modulate_in(x, g_ref, mod_ref, r, j):
    gain = g_ref[2 * j:2 * j + 1, :] * (1.0 + mod_ref[r, 3 * j + 1:3 * j + 2, :])
    return _rms_scale(x) * gain + mod_ref[r, 3 * j:3 * j + 1, :]


def _gated_residual(x, out, g_ref, mod_ref, r, j, weight):
    gain = (weight * mod_ref[r, 3 * j + 2:3 * j + 3, :]) * g_ref[2 * j + 1:2 * j + 2, :]
    return x + _rms_scale(out) * gain


def _cond_specs(l):
    return [pl.BlockSpec((None, COND_ROWS, N_MOD, D_MODEL), lambda *_: (l, 0, 0, 0)),
            pl.BlockSpec((None, 6, D_MODEL), lambda *_: (l, 0, 0))]


def _dot(a, b):
    return jnp.dot(a.astype(BF16), b.astype(BF16), preferred_element_type=F32)


def _dot_nt(a, b):
    return lax.dot_general(a.astype(BF16), b.astype(BF16), (((1,), (1,)), ((), ())),
                           preferred_element_type=F32)


def _mod_kernel(c_ref, w_ref, b_ref, o_ref):
    c = c_ref[...]
    s = c * jax.nn.sigmoid(c)
    o_ref[...] = _dot(s, w_ref[...]) + b_ref[...]


def _modulation(cond, ada_w, ada_b):
    n = N_MOD * D_MODEL
    out = pl.pallas_call(
        _mod_kernel,
        grid=(DEPTH, n // TN_MOD),
        in_specs=[
            pl.BlockSpec((COND_ROWS, D_MODEL), lambda l, t: (0, 0)),
            pl.BlockSpec((None, D_MODEL, TN_MOD), lambda l, t: (l, 0, t)),
            pl.BlockSpec((None, 1, TN_MOD), lambda l, t: (l, 0, t)),
        ],
        out_specs=pl.BlockSpec((None, COND_ROWS, TN_MOD), lambda l, t: (l, 0, t)),
        out_shape=jax.ShapeDtypeStruct((DEPTH, COND_ROWS, n), F32),
        compiler_params=_params(("parallel", "parallel"), 4 * D_MODEL * TN_MOD),
        name="modulation",
    )(cond, ada_w, ada_b.reshape(DEPTH, 1, n))
    return out.reshape(DEPTH, COND_ROWS, N_MOD, D_MODEL)


def _ffn_kernel(*refs, j, tile0, n_head):
    i = pl.program_id(0) + tile0
    k = pl.program_id(1)
    last = pl.num_programs(1) - 1
    if n_head is None:
        x_ref, mod_ref, g_ref, wa_ref, wg_ref, w2_ref, o_ref, h_ref = refs
        sources = [(None, x_ref)]
    else:
        x_ref, x_tail_ref, mod_ref, g_ref, wa_ref, wg_ref, w2_ref, o_ref, h_ref = refs
        sources = [(pl.program_id(0) < n_head, x_ref), (pl.program_id(0) >= n_head, x_tail_ref)]

    def per_source(fn):
        for pred, ref in sources:
            if pred is None:
                fn(ref)
            else:
                pl.when(pred)(functools.partial(fn, ref))

    def hidden_step(rows, init):
        h = h_ref[rows, :]
        a = _dot(h, wa_ref[...])
        g = _dot(h, wg_ref[...])
        d = _dot((g * jax.nn.sigmoid(g)) * a, w2_ref[...])
        if init:
            o_ref[rows, :] = d
        else:
            o_ref[rows, :] += d

    def first_step(x_src):
        for rows, r in _sub_rows(i):
            h_ref[rows, :] = _modulate_in(x_src[rows, :], g_ref, mod_ref, r, j).astype(BF16)
            hidden_step(rows, init=True)

    def last_step(x_src):
        for rows, r in _sub_rows(i):
            hidden_step(rows, init=False)
            o_ref[rows, :] = _gated_residual(x_src[rows, :], o_ref[rows, :], g_ref, mod_ref, r, j, 0.5)

    pl.when(k == 0)(functools.partial(per_source, first_step))

    @pl.when((k > 0) & (k < last))
    def _():
        for rows, _ in _unit_rows(i):
            hidden_step(rows, init=False)

    pl.when(k == last)(functools.partial(per_source, last_step))


def _ffn_sublayer(y, mod, gains, ffn_w13, ffn_w2, l, s, tile0=0, n_tiles=N_TOK // TM, y_tail=None):
    j = 2 * s
    nk = D_FF // TF
    blocks = 4 * (2 * TM * D_MODEL + 3 * D_MODEL * TF)
    scratch = 2 * TM * D_MODEL
    if y_tail is None:
        n_head = None
        x_specs = [pl.BlockSpec((TM, D_MODEL), lambda i, k: (tile0 + i, 0))]
        xs = [y]
    else:
        assert tile0 == 0
        n_head = y.shape[0] // TM
        x_specs = [pl.BlockSpec((TM, D_MODEL), lambda i, k: (jnp.minimum(i, n_head - 1), 0)),
                   pl.BlockSpec((TM, D_MODEL), lambda i, k: (jnp.maximum(i - n_head, 0), 0),
                                pipeline_mode=pl.Buffered(1))]
        xs = [y, y_tail]
        scratch += 4 * TM * D_MODEL
    return pl.pallas_call(
        functools.partial(_ffn_kernel, j=j, tile0=tile0, n_head=n_head),
        grid=(n_tiles, nk),
        in_specs=[
            *x_specs,
            *_cond_specs(l),
            pl.BlockSpec((None, None, D_MODEL, TF), lambda i, k: (l, s, 0, k)),
            pl.BlockSpec((None, None, D_MODEL, TF), lambda i, k: (l, s, 0, k + nk)),
            pl.BlockSpec((None, None, TF, D_MODEL), lambda i, k: (l, s, k, 0)),
        ],
        out_specs=pl.BlockSpec((TM, D_MODEL), lambda i, k: (i, 0)),
        out_shape=jax.ShapeDtypeStruct((n_tiles * TM, D_MODEL), F32),
        scratch_shapes=[pltpu.VMEM((TM, D_MODEL), BF16)],
        compiler_params=_params(("parallel", "arbitrary"), blocks, scratch),
        name=f"ffn_l{l}_s{s}_t{tile0}",
    )(*xs, mod, gains, ffn_w13, ffn_w13, ffn_w2)


def _sconv_kernel(x_ref, mod_ref, g_ref, wb_ref, wc_ref, wx_ref, cw_ref, cb_ref, wo_ref, o_ref, h_ref):
    i = pl.program_id(0)
    k = pl.program_id(1)
    ((_, r),) = _unit_rows(i, TM_CONV)
    seq = jnp.where(i < N_PROMPT // TM_CONV, SEQ, DEC_SEQ)
    blk = SEQ
    n_blk = TM_CONV // blk
    row = lax.broadcasted_iota(jnp.int32, (blk, TF_CONV), 0)
    zero_row = jnp.zeros((1, TF_CONV), F32)

    def step(first, last):
        wb = wb_ref[...].astype(BF16)
        wc = wc_ref[...].astype(BF16)
        wx = wx_ref[...].astype(BF16)
        wo = wo_ref[...].astype(BF16)
        gbs, us = [], []
        for bi in range(n_blk):
            rows = slice(bi * blk, (bi + 1) * blk)
            if first:
                h_ref[rows, :] = _modulate_in(x_ref[rows, :], g_ref, mod_ref, r, 1).astype(BF16)
            h = h_ref[rows, :]
            gbs.append(_dot(h, wb))
            us.append(_dot(h, wc) * _dot(h, wx))
        for bi in range(n_blk):
            rows = slice(bi * blk, (bi + 1) * blk)
            ub = us[bi]
            t = (row + bi * blk) & (seq - 1)
            above = us[bi - 1][blk - 1:blk, :] if bi > 0 else zero_row
            below = us[bi + 1][0:1, :] if bi < n_blk - 1 else zero_row
            u_prev = jnp.where(t == 0, 0.0, jnp.where(row == 0, above, pltpu.roll(ub, 1, 0)))
            u_next = jnp.where(t == seq - 1, 0.0, jnp.where(row == blk - 1, below, pltpu.roll(ub, blk - 1, 0)))
            y = cb_ref[...] + u_prev * cw_ref[0:1, :] + ub * cw_ref[1:2, :] + u_next * cw_ref[2:3, :]
            d = _dot(gbs[bi] * y, wo)
            acc = d if first else o_ref[rows, :] + d
            o_ref[rows, :] = _gated_residual(x_ref[rows, :], acc, g_ref, mod_ref, r, 1, 1.0) if last else acc

    n_steps = SC_WIDTH // TF_CONV
    for kk in range(n_steps):
        pl.when(k == kk)(functools.partial(step, kk == 0, kk == n_steps - 1))


def _sconv_sublayer(y, mod_l, g_l, sc_w_in, sc_conv_w, sc_conv_b, sc_w_out, jc, l):
    tm, tf = TM_CONV, TF_CONV
    nk = SC_WIDTH // tf
    blocks = 4 * (2 * tm * D_MODEL + 4 * D_MODEL * tf)
    scratch = 2 * tm * D_MODEL
    return pl.pallas_call(
        _sconv_kernel,
        grid=(N_TOK // tm, nk),
        in_specs=[
            pl.BlockSpec((tm, D_MODEL), lambda i, k: (i, 0)),
            *_cond_specs(l),
            pl.BlockSpec((None, D_MODEL, tf), lambda i, k: (jc, 0, k)),
            pl.BlockSpec((None, D_MODEL, tf), lambda i, k: (jc, 0, k + nk)),
            pl.BlockSpec((None, D_MODEL, tf), lambda i, k: (jc, 0, k + 2 * nk)),
            pl.BlockSpec((None, CONV_WIDTH, tf), lambda i, k: (jc, 0, k)),
            pl.BlockSpec((None, 1, tf), lambda i, k: (jc, 0, k)),
            pl.BlockSpec((None, tf, D_MODEL), lambda i, k: (jc, k, 0)),
        ],
        out_specs=pl.BlockSpec((tm, D_MODEL), lambda i, k: (i, 0)),
        out_shape=jax.ShapeDtypeStruct((N_TOK, D_MODEL), F32),
        scratch_shapes=[pltpu.VMEM((tm, D_MODEL), BF16)],
        compiler_params=_params(("parallel", "arbitrary"), blocks, scratch),
        name=f"sconv_{jc}",
    )(y, mod_l, g_l, sc_w_in, sc_w_in, sc_w_in, sc_conv_w, sc_conv_b.reshape(N_C, 1, SC_WIDTH), sc_w_out)


def _inproj_kernel(x_ref, mod_ref, g_ref, w_ref, wk_ref, wg_ref, gb_ref, o_ref, okt_ref, og_ref, h_ref):
    i = pl.program_id(0)
    n = pl.program_id(1)
    n_main = len(PROJ_BLOCKS)

    @pl.when(n == 0)
    def _():
        for rows, r in _sub_rows(i):
            h_ref[rows, :] = _modulate_in(x_ref[rows, :], g_ref, mod_ref, r, 1).astype(BF16)
            o_ref[rows, :] = _dot_nt(h_ref[rows, :], w_ref[...])

    @pl.when((n > 0) & (n < n_main))
    def _():
        o_ref[...] = _dot_nt(h_ref[...], w_ref[...])

    @pl.when(n == n_main)
    def _():
        w_kg = jnp.concatenate([wk_ref[...], wg_ref[...]], axis=0)
        ktg = _dot_nt(w_kg, h_ref[...])
        k_scale = ML_HEAD_DIM ** -0.5
        for c in range(TM // ML_CHUNK):
            okt_ref[c] = ktg[:ML_WIDTH, c * ML_CHUNK:(c + 1) * ML_CHUNK] * k_scale
        g = ktg[ML_WIDTH:, :] + gb_ref[...]
        gate = lax.broadcasted_iota(jnp.int32, g.shape, 0)
        is_forget = ((gate >= ML_HEADS) & (gate < 2 * ML_HEADS)) | (gate >= 3 * ML_HEADS)
        log_sig = jnp.minimum(g, 0.0) - jnp.log1p(jnp.exp(-jnp.abs(g)))
        g = jnp.where(is_forget, log_sig, g)
        gates_t = jnp.concatenate([g, jnp.zeros((LANES - N_GATES, TM), F32)], axis=0)
        og_ref[...] = gates_t.T


def _inproj(y, mod_l, g_l, w_in_t, gate_b_cols, jb, l):
    n_main = len(PROJ_BLOCKS)
    assert PROJ_BLOCKS == tuple(b for b in range(n_main + 1) if b != KM_BLOCK)

    def w_block(i, n):
        m = jnp.minimum(n, n_main - 1)
        return (jb, m + (m >= KM_BLOCK).astype(jnp.int32), 0)

    cpt = TM // ML_CHUNK
    blocks = 4 * (TM * D_MODEL + 2 * TN_PROJ * D_MODEL + N_GATES * D_MODEL + TM * TN_PROJ
                  + ML_WIDTH * TM + TM * LANES)
    return pl.pallas_call(
        _inproj_kernel,
        grid=(N_TOK // TM, n_main + 1),
        in_specs=[
            pl.BlockSpec((TM, D_MODEL), lambda i, n: (i, 0)),
            *_cond_specs(l),
            pl.BlockSpec((None, TN_PROJ, D_MODEL), w_block),
            pl.BlockSpec((None, ML_WIDTH, D_MODEL), lambda i, n: (jb, KM_BLOCK, 0)),
            pl.BlockSpec((None, N_GATES, D_MODEL), lambda i, n: (jb, AB_MAIN // N_GATES, 0)),
            pl.BlockSpec((None, N_GATES, 1), lambda i, n: (jb, 0, 0)),
        ],
        out_specs=[
            pl.BlockSpec((TM, TN_PROJ), lambda i, n: (i, jnp.minimum(n, n_main - 1))),
            pl.BlockSpec((cpt, ML_WIDTH, ML_CHUNK), lambda i, n: (i, 0, 0)),
            pl.BlockSpec((TM, LANES), lambda i, n: (i, 0)),
        ],
        out_shape=[jax.ShapeDtypeStruct((N_TOK, n_main * TN_PROJ), F32),
                   jax.ShapeDtypeStruct((N_TOK // ML_CHUNK, ML_WIDTH, ML_CHUNK), F32),
                   jax.ShapeDtypeStruct((N_TOK, LANES), F32)],
        scratch_shapes=[pltpu.VMEM((TM, D_MODEL), BF16)],
        compiler_params=_params(("parallel", "arbitrary"), blocks, 2 * TM * D_MODEL),
        name=f"inproj_{jb}",
    )(y, mod_l, g_l, w_in_t, w_in_t, w_in_t, gate_b_cols)


def _stack_entry(ref, jb, fills_stack):
    if not fills_stack:
        return ref
    for j in range(ref.shape[0]):
        if j != jb:
            ref[j] = jnp.zeros(ref.shape[1:], ref.dtype)
    return ref.at[jb]


def _stack_spec(entry_shape, jb, fills_stack, n_stack=N_AB, lead=None):
    zeros = (0,) * len(entry_shape)
    if fills_stack:
        return pl.BlockSpec((lead, n_stack) + entry_shape, lambda b: (b, 0) + zeros)
    return pl.BlockSpec((lead, None) + entry_shape, lambda b: (b, jb) + zeros)


def _ctx_attn_kernel(q_ref, k_ref, v_ref, *rest, jb, fills_stack):
    o_ref = rest[-3]
    kc_ref = _stack_entry(rest[-2], jb, fills_stack)
    vc_ref = _stack_entry(rest[-1], jb, fills_stack)
    scale = NA_HEAD_DIM ** -0.5
    k_t = k_ref[...].T
    v_t = v_ref[...].T
    for h in range(NA_HEADS):
        sl = slice(h * NA_HEAD_DIM, (h + 1) * NA_HEAD_DIM)
        kc_ref[h] = k_t[sl, :]
        vc_ref[h] = v_t[sl, :]
    lane = lax.broadcasted_iota(jnp.int32, (SEQ, LANES), 1)
    ones = jnp.ones((SEQ, LANES), BF16)
    for hp in range(NA_WIDTH // LANES):
        pair = slice(hp * LANES, (hp + 1) * LANES)
        q_pair = q_ref[:, pair] * scale
        v_aug = jnp.concatenate([v_ref[:, pair].astype(BF16), ones], axis=1)
        outs = []
        for hh in range(LANES // NA_HEAD_DIM):
            own = (lane >= hh * NA_HEAD_DIM) & (lane < (hh + 1) * NA_HEAD_DIM)
            s = _dot(jnp.where(own, q_pair, 0.0), k_t[pair, :])
            r = _dot(jnp.exp(s - jnp.max(s, axis=-1, keepdims=True)), v_aug)
            outs.append(r[:, :LANES] / r[:, LANES:])
        o_ref[:, pair] = jnp.where(lane < NA_HEAD_DIM, outs[0], outs[1]).astype(o_ref.dtype)


def _ctx_attention(proj, jb, prev):
    cache = jax.ShapeDtypeStruct((BATCH, N_AB, NA_HEADS, NA_HEAD_DIM, SEQ), F32)
    cache_spec = _stack_spec((NA_HEADS, NA_HEAD_DIM, SEQ), jb, prev is None)
    in_specs = [pl.BlockSpec((SEQ, NA_WIDTH), lambda b, c=c: (b, c)) for c in (P_QA, P_KA, P_VA)]
    args = [proj, proj, proj]
    aliases = {}
    if prev is not None:
        in_specs += [pl.BlockSpec(memory_space=pl.ANY)] * 2
        args += list(prev)
        aliases = {3: 1, 4: 2}
    blocks = 4 * (4 * SEQ * NA_WIDTH + 2 * N_AB * NA_WIDTH * SEQ)
    return pl.pallas_call(
        functools.partial(_ctx_attn_kernel, jb=jb, fills_stack=prev is None),
        grid=(BATCH,),
        in_specs=in_specs,
        out_specs=[pl.BlockSpec((SEQ, NA_WIDTH), lambda b: (b, 0)), cache_spec, cache_spec],
        out_shape=[jax.ShapeDtypeStruct((N_PROMPT, NA_WIDTH), BF16), cache, cache],
        input_output_aliases=aliases,
        compiler_params=_params(("parallel",), blocks),
        name=f"ctx_attention_{jb}",
    )(*args)


def _na_bias_kernel(rpb_ref, p_ref, e_ref):
    base = (pl.program_id(0) * NA_HEADS + pl.program_id(1)) * (N_DR * N_DC)
    shape = (GRID_W, LANES)
    lane = lax.broadcasted_iota(jnp.int32, shape, 1)
    qc = lax.broadcasted_iota(jnp.int32, shape, 0)
    kc = lane & (GRID_W - 1)
    upper = lane >= GRID_W
    cs = jnp.clip(qc - NA_WIN_W // 2, 0, GRID_W - NA_WIN_W)
    col_in = (kc >= cs) & (kc < cs + NA_WIN_W)
    dc = jnp.clip(kc - qc, -(NA_WIN_W - 1), NA_WIN_W - 1) + (NA_WIN_W - 1)
    upper_row = upper[0:1, :]
    n_even = (N_DR + 1) // 2
    group = 4
    for e0 in range(0, n_even, group):
        es = range(e0, min(e0 + group, n_even))
        accs = [jnp.zeros(shape, F32) for _ in es]
        for d in range(N_DC):
            hit = dc == d
            for a, e in enumerate(es):
                lo = rpb_ref[base + 2 * e * N_DC + d]
                hi = rpb_ref[base + (2 * e + 1) * N_DC + d] if 2 * e + 1 < N_DR else 0.0
                accs[a] = jnp.where(hit, jnp.where(upper_row, hi, lo), accs[a])
        for a, e in enumerate(es):
            e_ref[e] = jnp.where(col_in, accs[a], -jnp.inf)
    for pair in range(N_PAIR):
        e = pair // 2
        if pair % 2 == 0:
            p_ref[pair] = e_ref[e]
        else:
            p_ref[pair] = jnp.where(upper, pltpu.roll(e_ref[e + 1], GRID_W, 1), pltpu.roll(e_ref[e], GRID_W, 1))


def _na_bias(na_rpb):
    return pl.pallas_call(
        _na_bias_kernel,
        grid=(N_AB, NA_HEADS),
        in_specs=[pl.BlockSpec(memory_space=pltpu.SMEM)],
        out_specs=pl.BlockSpec((None, None, N_PAIR, GRID_W, LANES), lambda j, h: (j, h, 0, 0, 0)),
        out_shape=jax.ShapeDtypeStruct((N_AB, NA_HEADS, N_PAIR, GRID_W, LANES), F32),
        scratch_shapes=[pltpu.VMEM(((N_DR + 1) // 2, GRID_W, LANES), F32)],
        compiler_params=_params(("parallel", "parallel"), 4 * N_PAIR * GRID_W * LANES,
                                4 * ((N_DR + 1) // 2) * GRID_W * LANES),
        name="na_bias",
    )(na_rpb.reshape(-1))


def _na_window_start(r):
    return min(max(r - NA_KH // 2, 0), GRID_ROWS - NA_KH)


def _na_bias_block(p_ref, hh, qr, kr):
    rs = _na_window_start(qr)
    lo_ok = rs <= kr < rs + NA_KH
    hi_ok = rs <= kr + 1 < rs + NA_KH
    if not (lo_ok or hi_ok):
        return jnp.full((GRID_W, LANES), -jnp.inf, F32)
    dr = kr - qr + (NA_WIN_H - 1)
    assert 0 <= dr < N_PAIR
    blk = p_ref[hh, dr]
    if lo_ok and hi_ok:
        return blk
    lane = lax.broadcasted_iota(jnp.int32, (GRID_W, LANES), 1)
    keep = (lane < GRID_W) if lo_ok else (lane >= GRID_W)
    return jnp.where(keep, blk, -jnp.inf)


def _na_kernel(q_ref, k_ref, v_ref, kc_ref, vc_ref, p_ref, o_ref):
    scale = NA_HEAD_DIM ** -0.5
    n_q = NA_QROWS * GRID_W
    lane = lax.broadcasted_iota(jnp.int32, (n_q, LANES), 1)
    k_ctx_t = kc_ref[...].reshape(LANES, PAST_LEN)
    v_ctx_t = vc_ref[...].reshape(LANES, PAST_LEN)
    v_aug_ctx_t = jnp.concatenate([v_ctx_t.astype(BF16), jnp.ones((LANES, PAST_LEN), BF16)], axis=0)
    for q0 in range(0, GRID_ROWS, NA_QROWS):
        k_lo = _na_window_start(q0) // 2 * 2
        k_hi = -(-(_na_window_start(q0 + NA_QROWS - 1) + NA_KH) // 2) * 2
        q_rows = slice(q0 * GRID_W, (q0 + NA_QROWS) * GRID_W)
        k_rows = slice(k_lo * GRID_W, k_hi * GRID_W)
        q_pair = q_ref[q_rows, :] * scale
        n_k = (k_hi - k_lo) * GRID_W
        v_aug_loc = jnp.concatenate([v_ref[k_rows, :].astype(BF16), jnp.ones((n_k, LANES), BF16)], axis=1)
        outs = []
        for hh in range(LANES // NA_HEAD_DIM):
            bias = jnp.concatenate(
                [jnp.concatenate([_na_bias_block(p_ref, hh, qr, kr) for kr in range(k_lo, k_hi, 2)], axis=1)
                 for qr in range(q0, q0 + NA_QROWS)], axis=0)
            own = (lane >= hh * NA_HEAD_DIM) & (lane < (hh + 1) * NA_HEAD_DIM)
            q = jnp.where(own, q_pair, 0.0)
            s_loc = _dot_nt(q, k_ref[k_rows, :]) + bias
            s_ctx = _dot(q, k_ctx_t)
            m = jnp.maximum(jnp.max(s_loc, axis=-1, keepdims=True), jnp.max(s_ctx, axis=-1, keepdims=True))
            r = _dot(jnp.exp(s_loc - m), v_aug_loc) + _dot_nt(jnp.exp(s_ctx - m), v_aug_ctx_t)
            outs.append(r[:, :LANES] / r[:, LANES:])
        o_ref[q_rows, :] = jnp.where(lane < NA_HEAD_DIM, outs[0], outs[1]).astype(o_ref.dtype)


def _na_attention(proj, cache_k_t, cache_v_t, bias_pairs, jb):
    hp = LANES // NA_HEAD_DIM
    s0 = N_PROMPT // DEC_SEQ
    nb = NA_WIDTH // LANES
    ctx_spec = pl.BlockSpec((None, None, hp, NA_HEAD_DIM, PAST_LEN), lambda b, c: (b, jb, c, 0, 0))
    blocks = 4 * (4 * DEC_SEQ * LANES + 2 * hp * NA_HEAD_DIM * PAST_LEN + hp * N_PAIR * GRID_W * LANES)
    return pl.pallas_call(
        _na_kernel,
        grid=(DEC_BATCH, nb),
        in_specs=[
            pl.BlockSpec((DEC_SEQ, LANES), lambda b, c: (s0 + b, P_QA * nb + c)),
            pl.BlockSpec((DEC_SEQ, LANES), lambda b, c: (s0 + b, P_KA * nb + c)),
            pl.BlockSpec((DEC_SEQ, LANES), lambda b, c: (s0 + b, P_VA * nb + c)),
            ctx_spec, ctx_spec,
            pl.BlockSpec((None, hp, N_PAIR, GRID_W, LANES), lambda b, c: (jb, c, 0, 0, 0)),
        ],
        out_specs=pl.BlockSpec((DEC_SEQ, LANES), lambda b, c: (b, c)),
        out_shape=jax.ShapeDtypeStruct((N_SAMPLE, NA_WIDTH), BF16),
        compiler_params=_params(("parallel", "parallel"), blocks),
        name=f"na_attention_{jb}",
    )(proj, proj, proj, cache_k_t, cache_v_t, bias_pairs)


def _running_rows(x, op, identity, reverse):
    n = x.shape[0]
    ridx = lax.broadcasted_iota(jnp.int32, x.shape, 0)
    sh = 1
    while sh < n:
        if reverse:
            x = op(x, jnp.where(ridx < n - sh, pltpu.roll(x, n - sh, 0), identity))
        else:
            x = op(x, jnp.where(ridx >= sh, pltpu.roll(x, sh, 0), identity))
        sh *= 2
    return x


def _mlstm_kernel(*refs, n_chunks, has_state, jb, fills_stack, spb, layer, n_layers):
    if has_state:
        (q_ref, kt_ref, v_ref, o_ref, gt_ref, hn_ref, c0_ref, n0_ref, m0_ref,
         out_ref, cout_ref, nout_ref, mout_ref, hdir_ref, caug_ref) = refs
    else:
        (q_ref, kt_ref, v_ref, o_ref, gt_ref, hn_ref, *_unused,
         out_ref, cout_ref, nout_ref, mout_ref, hdir_ref, caug_ref) = refs
    cout_refs = [_stack_entry(cout_ref.at[s], jb, fills_stack) for s in range(spb)]
    nout_refs = [_stack_entry(nout_ref.at[s], jb, fills_stack) for s in range(spb)]
    mout_refs = [_stack_entry(mout_ref.at[s], jb, fills_stack) for s in range(spb)]
    T = ML_CHUNK
    dh = ML_HEAD_DIM
    seq_len = n_chunks * T
    b = pl.program_id(0)
    row = lax.broadcasted_iota(jnp.int32, (T, T), 0)
    col = lax.broadcasted_iota(jnp.int32, (T, T), 1)
    lane = lax.broadcasted_iota(jnp.int32, (T, LANES), 1)
    lane_row = lax.broadcasted_iota(jnp.int32, (1, LANES), 1)
    ones = jnp.ones((T, dh), F32)
    states = [(d, h) for d in range(2) for h in range(ML_HEADS)]

    def gate_lanes(d, h):
        gi = 2 * d * ML_HEADS + h
        return gi, gi + ML_HEADS

    scans = [(s, d) for s in range(spb) for d in range(2)]
    m_init = []
    for s, d in scans:
        m_row = jnp.zeros((1, LANES), F32)
        for h in range(ML_HEADS):
            if has_state:
                n_bcast = jnp.broadcast_to(n0_ref[s, d, h:h + 1, :], (dh, dh)).T
                caug_ref[s, d, h] = jnp.concatenate([c0_ref[s, d, h], n_bcast], axis=1)
                m0 = m0_ref[(((b * spb + s) * n_layers + layer) * 2 + d) * ML_HEADS + h]
                m_row = jnp.where(lane_row == gate_lanes(d, h)[1], m0, m_row)
            else:
                caug_ref[s, d, h] = jnp.zeros((dh, 2 * dh), F32)
        m_init.append(m_row)

    def chunk_body(ci, m_rows):
        m_next = []
        for scan, (s, d) in enumerate(scans):
            causal = (col <= row) if d == 0 else (col >= row)
            last = T - 1 if d == 0 else 0
            c = ci if d == 0 else n_chunks - 1 - ci
            rows = pl.ds(pl.multiple_of(s * seq_len + c * T, T), T)
            m_prev = m_rows[scan]
            mine = (lane >= (2 * d + 1) * ML_HEADS) & (lane < (2 * d + 2) * ML_HEADS)
            g = gt_ref[rows, :]
            cum = _running_rows(g, jnp.add, 0.0, reverse=(d == 1))
            i_al = pltpu.roll(g, ML_HEADS, 1)
            cvec = jnp.where(mine, i_al - cum, 0.0)
            cum = jnp.where(mine, cum, 0.0)
            m_run = jnp.maximum(m_prev, _running_rows(cvec, jnp.maximum, -jnp.inf, reverse=(d == 1)))
            m_t = cum + m_run
            e_negm = jnp.exp(-m_t)
            m_run_last = m_run[last:last + 1, :]
            wi_last = jnp.exp(m_prev - m_run_last)
            cvec_t = cvec.T
            for h in range(ML_HEADS):
                _, gf = gate_lanes(d, h)
                hl = slice(h * dh, (h + 1) * dh)
                m_run_b = jnp.broadcast_to(m_run[:, gf:gf + 1], (T, T))
                e_negm_b = jnp.broadcast_to(e_negm[:, gf:gf + 1], (T, dh))
                c_row = cvec_t[gf:gf + 1, :]
                w_d = jnp.exp(jnp.where(causal, c_row - m_run_b, -jnp.inf))
                w_inter_b = jnp.exp(m_prev[:, gf:gf + 1] - m_run_b)
                qh = q_ref[rows, hl]
                kh_t = kt_ref[s * n_chunks + c, hl, :]
                v_aug = jnp.concatenate([v_ref[rows, hl], ones], axis=1)
                sc = _dot(qh, kh_t) * w_d
                c_aug = caug_ref[s, d, h]
                nd = jnp.concatenate([w_inter_b, w_inter_b], axis=1) * _dot(qh, c_aug) + _dot(sc, v_aug)
                hdir_ref[d, rows, hl] = nd[:, :dh] / jnp.maximum(jnp.abs(nd[:, dh:]), e_negm_b)
                kw_t = kh_t * jnp.exp(c_row - m_run_last[:, gf:gf + 1])
                caug_ref[s, d, h] = wi_last[:, gf:gf + 1] * c_aug + _dot(kw_t, v_aug)
            m_next.append(m_t[last:last + 1, :])
        return tuple(m_next)

    m_fin = lax.fori_loop(0, n_chunks, chunk_body, tuple(m_init), unroll=2)
    for scan, (s, d) in enumerate(scans):
        for h in range(ML_HEADS):
            c_aug = caug_ref[s, d, h]
            cout_refs[s][d, h] = c_aug[:, :dh]
            nout_refs[s][d, h:h + 1, :] = c_aug[:, dh:].T[0:1, :]
    for s in range(spb):
        m_heads = [pltpu.roll(m_fin[2 * s + d], LANES - gate_lanes(d, 0)[1], 1) for d in range(2)]
        mout_refs[s][...] = jnp.concatenate(m_heads + [jnp.zeros((COND_ROWS - 2, LANES), F32)], axis=0)

    for h in range(ML_HEADS):
        hl = slice(h * dh, (h + 1) * dh)
        h_sum = hdir_ref[0, :, hl] + hdir_ref[1, :, hl]
        gated = jax.nn.sigmoid(o_ref[:, hl]) * (_rms_scale(h_sum) * hn_ref[:, hl])
        out_ref[:, hl] = gated.astype(out_ref.dtype)


def _mlstm(proj, keys_t, gates, hnorm_rows, layer, seq_len, n_seq, row0, spb, n_stack=1, jb=0,
           prev=None, state=None):
    has_state = state is not None
    assert n_seq % spb == 0 and row0 % (spb * seq_len) == 0
    r0 = row0 // (spb * seq_len)
    rows = spb * seq_len
    nb = TN_PROJ // ML_WIDTH
    n_chunks = seq_len // ML_CHUNK
    in_specs = [
        pl.BlockSpec((rows, ML_WIDTH), lambda b: (r0 + b, P_QM * nb)),
        pl.BlockSpec((spb * n_chunks, ML_WIDTH, ML_CHUNK), lambda b: (r0 + b, 0, 0)),
        pl.BlockSpec((rows, ML_WIDTH), lambda b: (r0 + b, P_VM * nb)),
        pl.BlockSpec((rows, ML_WIDTH), lambda b: (r0 + b, P_OM * nb)),
        pl.BlockSpec((rows, LANES), lambda b: (r0 + b, 0)),
        pl.BlockSpec((None, 1, ML_WIDTH), lambda b: (layer, 0, 0)),
    ]
    args = [proj, keys_t, proj, proj, gates, hnorm_rows]
    aliases = {}
    if has_state:
        st_c, st_n, st_m = state
        in_specs += [
            pl.BlockSpec((spb, None, 2, ML_HEADS, ML_HEAD_DIM, ML_HEAD_DIM), lambda b: (b, layer, 0, 0, 0, 0)),
            pl.BlockSpec((spb, None, 2, ML_HEADS, ML_HEAD_DIM), lambda b: (b, layer, 0, 0, 0)),
            pl.BlockSpec(memory_space=pltpu.SMEM),
        ]
        args += [st_c, st_n, st_m.reshape(-1)]
    elif prev is not None:
        aliases = {len(args) + k: 1 + k for k in range(3)}
        in_specs += [pl.BlockSpec(memory_space=pl.ANY)] * 3
        args += list(prev)
    fills_stack = prev is None
    c_bytes = 2 * ML_HEADS * ML_HEAD_DIM * ML_HEAD_DIM
    blocks = 4 * (5 * rows * ML_WIDTH + rows * LANES + spb * (1 + n_stack) * c_bytes)
    scratch = 4 * (2 * rows * ML_WIDTH + spb * 2 * c_bytes)
    return pl.pallas_call(
        functools.partial(_mlstm_kernel, n_chunks=n_chunks, has_state=has_state, jb=jb, fills_stack=fills_stack,
                          spb=spb, layer=layer, n_layers=N_AB),
        grid=(n_seq // spb,),
        in_specs=in_specs,
        out_specs=[
            pl.BlockSpec((rows, ML_WIDTH), lambda b: (b, 0)),
            _stack_spec((2, ML_HEADS, ML_HEAD_DIM, ML_HEAD_DIM), jb, fills_stack, n_stack, spb),
            _stack_spec((2, ML_HEADS, ML_HEAD_DIM), jb, fills_stack, n_stack, spb),
            _stack_spec((COND_ROWS, LANES), jb, fills_stack, n_stack, spb),
        ],
        out_shape=[
            jax.ShapeDtypeStruct((n_seq * seq_len, ML_WIDTH), BF16),
            jax.ShapeDtypeStruct((n_seq, n_stack, 2, ML_HEADS, ML_HEAD_DIM, ML_HEAD_DIM), F32),
            jax.ShapeDtypeStruct((n_seq, n_stack, 2, ML_HEADS, ML_HEAD_DIM), F32),
            jax.ShapeDtypeStruct((n_seq, n_stack, COND_ROWS, LANES), F32),
        ],
        scratch_shapes=[pltpu.VMEM((2, rows, ML_WIDTH), F32),
                        pltpu.VMEM((spb, 2, ML_HEADS, ML_HEAD_DIM, 2 * ML_HEAD_DIM), F32)],
        input_output_aliases=aliases,
        compiler_params=_params(("parallel",), blocks, scratch),
        name=f"mlstm_{'latent' if has_state else 'context'}_{jb}",
    )(*args)


def _outproj_kernel(x_ref, mod_ref, g_ref, ap_ref, mp_ref, as_ref, ms_ref, wa_ref, wm_ref, o_ref):
    i = pl.program_id(0)

    def finish(a_ref, m_ref):
        wa = wa_ref[...].astype(BF16)
        wm = wm_ref[...].astype(BF16)
        for rows, r in _sub_rows(i, TM_OUT):
            out = _dot(a_ref[rows, :], wa) + _dot(m_ref[rows, :], wm)
            o_ref[rows, :] = _gated_residual(x_ref[rows, :], out, g_ref, mod_ref, r, 1, 1.0)

    is_prompt = i < N_PROMPT // TM_OUT
    pl.when(is_prompt)(functools.partial(finish, ap_ref, mp_ref))
    pl.when(jnp.logical_not(is_prompt))(functools.partial(finish, as_ref, ms_ref))


def _outproj(y, mod_l, g_l, a_p, m_p, a_s, m_s, ab_w_out, jb, l):
    tm = TM_OUT
    n_p = N_PROMPT // tm
    n_s = N_SAMPLE // tm
    p_spec = pl.BlockSpec((tm, NA_WIDTH), lambda i: (jnp.minimum(i, n_p - 1), 0))
    s_spec = pl.BlockSpec((tm, NA_WIDTH), lambda i: (jnp.clip(i - n_p, 0, n_s - 1), 0))
    blocks = 4 * (2 * tm * D_MODEL + 4 * tm * NA_WIDTH + D_MODEL * D_MODEL)
    return pl.pallas_call(
        _outproj_kernel,
        grid=(N_TOK // tm,),
        in_specs=[
            pl.BlockSpec((tm, D_MODEL), lambda i: (i, 0)),
            *_cond_specs(l),
            p_spec, p_spec, s_spec, s_spec,
            pl.BlockSpec((None, NA_WIDTH, D_MODEL), lambda i: (jb, 0, 0)),
            pl.BlockSpec((None, ML_WIDTH, D_MODEL), lambda i: (jb, 1, 0)),
        ],
        out_specs=pl.BlockSpec((tm, D_MODEL), lambda i: (i, 0)),
        out_shape=jax.ShapeDtypeStruct((N_TOK, D_MODEL), F32),
        compiler_params=_params(("parallel",), blocks),
        name=f"outproj_{jb}",
    )(y, mod_l, g_l, a_p, m_p, a_s, m_s, ab_w_out, ab_w_out)


def kernel(x_prompt, x_sample, cache_k, cache_v, state_C, state_n, state_m, c, c_ctx, ada_w, ada_b, norm_g,
           ffn_w13, ffn_w2, ab_w_in, ab_w_out, na_rpb, ml_gate_b, ml_hnorm, sc_w_in, sc_conv_w, sc_conv_b,
           sc_w_out):
    assert NA_WIDTH == ML_WIDTH == TN_PROJ and AB_MAIN % N_GATES == 0
    assert math.frexp(NA_HEAD_DIM ** -0.5)[0] == 0.5
    assert LANES == 2 * NA_HEAD_DIM
    assert N_PROMPT % TM == 0 and N_SAMPLE % TM == 0 and TM % COND_UNIT == 0
    cond = jnp.concatenate([c_ctx[None, :], c, jnp.zeros((COND_ROWS - 1 - DEC_BATCH, D_MODEL), F32)], axis=0)
    mod = _modulation(cond, ada_w, ada_b)
    gains = norm_g.reshape(DEPTH, 6, D_MODEL)
    bias_pairs = _na_bias(na_rpb)
    w_in_t = jnp.swapaxes(ab_w_in, 1, 2)
    cache_k_t = jnp.swapaxes(cache_k, 3, 4)
    cache_v_t = jnp.swapaxes(cache_v, 3, 4)
    gate_b_cols = ml_gate_b.reshape(N_AB, N_GATES, 1)
    hnorm_rows = ml_hnorm.reshape(N_AB, 1, ML_WIDTH)
    state = (state_C, state_n, state_m)

    kv_new = None
    st_new = None
    y = None
    for l in range(DEPTH):
        if l == 0:
            y = _ffn_sublayer(x_prompt.reshape(N_PROMPT, D_MODEL), mod, gains, ffn_w13, ffn_w2, l, 0,
                              y_tail=x_sample.reshape(N_SAMPLE, D_MODEL))
        else:
            y = _ffn_sublayer(y, mod, gains, ffn_w13, ffn_w2, l, 0)
        j = l // 2
        if l % 2 == 0:
            proj, keys_t, gates = _inproj(y, mod, gains, w_in_t, gate_b_cols, j, l)
            a_p, *kv_new = _ctx_attention(proj, j, kv_new)
            m_p, *st_new = _mlstm(proj, keys_t, gates, hnorm_rows, j, SEQ, BATCH, 0,
                                  ML_SEQS_PER_STEP, n_stack=N_AB, jb=j, prev=st_new)
            a_s = _na_attention(proj, cache_k_t, cache_v_t, bias_pairs, j)
            m_s, _, _, _ = _mlstm(proj, keys_t, gates, hnorm_rows, j, DEC_SEQ, DEC_BATCH, N_PROMPT,
                                  ML_SEQS_PER_STEP, state=state)
            y = _outproj(y, mod, gains, a_p, m_p, a_s, m_s, ab_w_out, j, l)
        else:
            y = _sconv_sublayer(y, mod, gains, sc_w_in, sc_conv_w, sc_conv_b, sc_w_out, j, l)
        if l < DEPTH - 1:
            y = _ffn_sublayer(y, mod, gains, ffn_w13, ffn_w2, l, 1)

    p_tiles = N_PROMPT // TM
    y_p = _ffn_sublayer(y, mod, gains, ffn_w13, ffn_w2, DEPTH - 1, 1, 0, p_tiles)
    y_s = _ffn_sublayer(y, mod, gains, ffn_w13, ffn_w2, DEPTH - 1, 1, p_tiles, N_TOK // TM - p_tiles)
    y_p = y_p.reshape(BATCH, SEQ, D_MODEL)
    y_s = y_s.reshape(DEC_BATCH, DEC_SEQ, D_MODEL)
    new_c, new_n, m_rows = st_new
    new_m = m_rows[:, :, :2, :ML_HEADS]
    return (y_p, y_s, jnp.swapaxes(kv_new[0], 3, 4), jnp.swapaxes(kv_new[1], 3, 4), new_c, new_n, new_m)
```

```python
import functools
import math

import jax
import jax.numpy as jnp
from jax import lax
from jax.experimental import pallas as pl
from jax.experimental.pallas import tpu as pltpu

D_MODEL = 1024
BATCH = 16
SEQ = 256
DEPTH = 4
DEC_BATCH = 2
DEC_SEQ = 1024
PAST_LEN = 256
GRID_W = 64
NA_HEADS = 8
NA_HEAD_DIM = 64
NA_WIN_H = 8
NA_WIN_W = 16
ML_HEADS = 4
ML_HEAD_DIM = 128
ML_CHUNK = 128
CONV_WIDTH = 3
D_FF = 2816
N_MOD = 9
EPS = 1e-6
N_AB = (DEPTH + 1) // 2
N_C = DEPTH // 2
NA_WIDTH = NA_HEADS * NA_HEAD_DIM
ML_WIDTH = ML_HEADS * ML_HEAD_DIM
AB_MAIN = 3 * NA_WIDTH + 4 * ML_WIDTH
N_GATES = 4 * ML_HEADS
AB_IN = AB_MAIN + N_GATES
SC_WIDTH = D_MODEL

N_PROMPT = BATCH * SEQ
N_SAMPLE = DEC_BATCH * DEC_SEQ
N_TOK = N_PROMPT + N_SAMPLE
GRID_ROWS = DEC_SEQ // GRID_W
NA_KH = min(NA_WIN_H, GRID_ROWS)
N_DR = 2 * NA_WIN_H - 1
N_DC = 2 * NA_WIN_W - 1
N_PAIR = N_DR - 1

LANES = 128
COND_ROWS = 8
VMEM_CAP = 60 * 1024 * 1024
TEMP_BYTES = 16 * 1024 * 1024

COND_UNIT = DEC_SEQ
TM = 2048
SUB_ROWS = 512
TM_OUT = 1024
TM_CONV = COND_UNIT
TF_CONV = 512
TF = 256
TN_PROJ = 512
TN_MOD = 1536
NA_QROWS = 8
ML_SEQS_PER_STEP = 2

PROJ_BLOCKS = (0, 1, 2, 3, 5, 6)
KM_BLOCK = 4
P_QA, P_KA, P_VA, P_QM, P_VM, P_OM = range(6)

F32 = jnp.float32
BF16 = jnp.bfloat16


def _vmem_limit(block_bytes, scratch_bytes):
    assert 2 * block_bytes + scratch_bytes <= VMEM_CAP
    return VMEM_CAP


def _params(semantics, block_bytes, scratch_bytes=0):
    return pltpu.CompilerParams(dimension_semantics=semantics,
                                vmem_limit_bytes=_vmem_limit(block_bytes, scratch_bytes))


def _unit_rows(i, tm=TM):
    first_latent = N_PROMPT // COND_UNIT
    units = tm // COND_UNIT
    return [(slice(u * COND_UNIT, (u + 1) * COND_UNIT), jnp.maximum(i * units + u - (first_latent - 1), 0))
            for u in range(units)]


def _sub_rows(i, tm=TM):
    per_unit = COND_UNIT // SUB_ROWS
    return [(slice(rows.start + s * SUB_ROWS, rows.start + (s + 1) * SUB_ROWS), r)
            for rows, r in _unit_rows(i, tm) for s in range(per_unit)]


def _rms_scale(x):
    return x * lax.rsqrt(jnp.mean(x * x, axis=-1, keepdims=True) + EPS)


def _modulate_in(x, g_ref, mod_ref, r, j):
    gain = g_ref[2 * j:2 * j + 1, :] * (1.0 + mod_ref[r, 3 * j + 1:3 * j + 2, :])
    return _rms_scale(x) * gain + mod_ref[r, 3 * j:3 * j + 1, :]


def _gated_residual(x, out, g_ref, mod_ref, r, j, weight):
    gain = (weight * mod_ref[r, 3 * j + 2:3 * j + 3, :]) * g_ref[2 * j + 1:2 * j + 2, :]
    return x + _rms_scale(out) * gain


def _cond_specs(l):
    return [pl.BlockSpec((None, COND_ROWS, N_MOD, D_MODEL), lambda *_: (l, 0, 0, 0)),
            pl.BlockSpec((None, 6, D_MODEL), lambda *_: (l, 0, 0))]


def _dot(a, b):
    return jnp.dot(a.astype(BF16), b.astype(BF16), preferred_element_type=F32)


def _dot_nt(a, b):
    return lax.dot_general(a.astype(BF16), b.astype(BF16), (((1,), (1,)), ((), ())),
                           preferred_element_type=F32)


def _mod_kernel(c_ref, w_ref, b_ref, o_ref):
    c = c_ref[...]
    s = c * jax.nn.sigmoid(c)
    o_ref[...] = _dot(s, w_ref[...]) + b_ref[...]


def _modulation(cond, ada_w, ada_b):
    n = N_MOD * D_MODEL
    out = pl.pallas_call(
        _mod_kernel,
        grid=(DEPTH, n // TN_MOD),
        in_specs=[
            pl.BlockSpec((COND_ROWS, D_MODEL), lambda l, t: (0, 0)),
            pl.BlockSpec((None, D_MODEL, TN_MOD), lambda l, t: (l, 0, t)),
            pl.BlockSpec((None, 1, TN_MOD), lambda l, t: (l, 0, t)),
        ],
        out_specs=pl.BlockSpec((None, COND_ROWS, TN_MOD), lambda l, t: (l, 0, t)),
        out_shape=jax.ShapeDtypeStruct((DEPTH, COND_ROWS, n), F32),
        compiler_params=_params(("parallel", "parallel"), 4 * D_MODEL * TN_MOD),
        name="modulation",
    )(cond, ada_w, ada_b.reshape(DEPTH, 1, n))
    return out.reshape(DEPTH, COND_ROWS, N_MOD, D_MODEL)


def _ffn_kernel(*refs, j, tile0, n_head):
    i = pl.program_id(0) + tile0
    k = pl.program_id(1)
    last = pl.num_programs(1) - 1
    if n_head is None:
        x_ref, mod_ref, g_ref, wa_ref, wg_ref, w2_ref, o_ref, h_ref = refs
        sources = [(None, x_ref)]
    else:
        x_ref, x_tail_ref, mod_ref, g_ref, wa_ref, wg_ref, w2_ref, o_ref, h_ref = refs
        sources = [(pl.program_id(0) < n_head, x_ref), (pl.program_id(0) >= n_head, x_tail_ref)]

    def per_source(fn):
        for pred, ref in sources:
            if pred is None:
                fn(ref)
            else:
                pl.when(pred)(functools.partial(fn, ref))

    def hidden_step(rows, init):
        h = h_ref[rows, :]
        a = _dot(h, wa_ref[...])
        g = _dot(h, wg_ref[...])
        d = _dot((g * jax.nn.sigmoid(g)) * a, w2_ref[...])
        if init:
            o_ref[rows, :] = d
        else:
            o_ref[rows, :] += d

    def first_step(x_src):
        for rows, r in _sub_rows(i):
            h_ref[rows, :] = _modulate_in(x_src[rows, :], g_ref, mod_ref, r, j).astype(BF16)
            hidden_step(rows, init=True)

    def last_step(x_src):
        for rows, r in _sub_rows(i):
            hidden_step(rows, init=False)
            o_ref[rows, :] = _gated_residual(x_src[rows, :], o_ref[rows, :], g_ref, mod_ref, r, j, 0.5)

    pl.when(k == 0)(functools.partial(per_source, first_step))

    @pl.when((k > 0) & (k < last))
    def _():
        for rows, _ in _unit_rows(i):
            hidden_step(rows, init=False)

    pl.when(k == last)(functools.partial(per_source, last_step))


def _ffn_sublayer(y, mod, gains, ffn_w13, ffn_w2, l, s, tile0=0, n_tiles=N_TOK // TM, y_tail=None):
    j = 2 * s
    nk = D_FF // TF
    blocks = 4 * (2 * TM * D_MODEL + 3 * D_MODEL * TF)
    scratch = 2 * TM * D_MODEL
    if y_tail is None:
        n_head = None
        x_specs = [pl.BlockSpec((TM, D_MODEL), lambda i, k: (tile0 + i, 0))]
        xs = [y]
    else:
        assert tile0 == 0
        n_head = y.shape[0] // TM
        x_specs = [pl.BlockSpec((TM, D_MODEL), lambda i, k: (jnp.minimum(i, n_head - 1), 0)),
                   pl.BlockSpec((TM, D_MODEL), lambda i, k: (jnp.maximum(i - n_head, 0), 0))]
        xs = [y, y_tail]
        scratch += 2 * 4 * TM * D_MODEL
    return pl.pallas_call(
        functools.partial(_ffn_kernel, j=j, tile0=tile0, n_head=n_head),
        grid=(n_tiles, nk),
        in_specs=[
            *x_specs,
            *_cond_specs(l),
            pl.BlockSpec((None, None, D_MODEL, TF), lambda i, k: (l, s, 0, k)),
            pl.BlockSpec((None, None, D_MODEL, TF), lambda i, k: (l, s, 0, k + nk)),
            pl.BlockSpec((None, None, TF, D_MODEL), lambda i, k: (l, s, k, 0)),
        ],
        out_specs=pl.BlockSpec((TM, D_MODEL), lambda i, k: (i, 0)),
        out_shape=jax.ShapeDtypeStruct((n_tiles * TM, D_MODEL), F32),
        scratch_shapes=[pltpu.VMEM((TM, D_MODEL), BF16)],
        compiler_params=_params(("parallel", "arbitrary"), blocks, scratch),
        name=f"ffn_l{l}_s{s}_t{tile0}",
    )(*xs, mod, gains, ffn_w13, ffn_w13, ffn_w2)


def _sconv_kernel(x_ref, mod_ref, g_ref, wb_ref, wc_ref, wx_ref, cw_ref, cb_ref, wo_ref, o_ref, h_ref):
    i = pl.program_id(0)
    k = pl.program_id(1)
    ((_, r),) = _unit_rows(i, TM_CONV)
    seq = jnp.where(i < N_PROMPT // TM_CONV, SEQ, DEC_SEQ)
    blk = SEQ
    n_blk = TM_CONV // blk
    row = lax.broadcasted_iota(jnp.int32, (blk, TF_CONV), 0)
    zero_row = jnp.zeros((1, TF_CONV), F32)

    def step(first, last):
        wb = wb_ref[...].astype(BF16)
        wc = wc_ref[...].astype(BF16)
        wx = wx_ref[...].astype(BF16)
        wo = wo_ref[...].astype(BF16)
        gbs, us = [], []
        for bi in range(n_blk):
            rows = slice(bi * blk, (bi + 1) * blk)
            if first:
                h_ref[rows, :] = _modulate_in(x_ref[rows, :], g_ref, mod_ref, r, 1).astype(BF16)
            h = h_ref[rows, :]
            gbs.append(_dot(h, wb))
            us.append(_dot(h, wc) * _dot(h, wx))
        for bi in range(n_blk):
            rows = slice(bi * blk, (bi + 1) * blk)
            ub = us[bi]
            t = (row + bi * blk) & (seq - 1)
            above = us[bi - 1][blk - 1:blk, :] if bi > 0 else zero_row
            below = us[bi + 1][0:1, :] if bi < n_blk - 1 else zero_row
            u_prev = jnp.where(t == 0, 0.0, jnp.where(row == 0, above, pltpu.roll(ub, 1, 0)))
            u_next = jnp.where(t == seq - 1, 0.0, jnp.where(row == blk - 1, below, pltpu.roll(ub, blk - 1, 0)))
            y = cb_ref[...] + u_prev * cw_ref[0:1, :] + ub * cw_ref[1:2, :] + u_next * cw_ref[2:3, :]
            d = _dot(gbs[bi] * y, wo)
            acc = d if first else o_ref[rows, :] + d
            o_ref[rows, :] = _gated_residual(x_ref[rows, :], acc, g_ref, mod_ref, r, 1, 1.0) if last else acc

    n_steps = SC_WIDTH // TF_CONV
    for kk in range(n_steps):
        pl.when(k == kk)(functools.partial(step, kk == 0, kk == n_steps - 1))


def _sconv_sublayer(y, mod_l, g_l, sc_w_in, sc_conv_w, sc_conv_b, sc_w_out, jc, l):
    tm, tf = TM_CONV, TF_CONV
    nk = SC_WIDTH // tf
    blocks = 4 * (2 * tm * D_MODEL + 4 * D_MODEL * tf)
    scratch = 2 * tm * D_MODEL
    return pl.pallas_call(
        _sconv_kernel,
        grid=(N_TOK // tm, nk),
        in_specs=[
            pl.BlockSpec((tm, D_MODEL), lambda i, k: (i, 0)),
            *_cond_specs(l),
            pl.BlockSpec((None, D_MODEL, tf), lambda i, k: (jc, 0, k)),
            pl.BlockSpec((None, D_MODEL, tf), lambda i, k: (jc, 0, k + nk)),
            pl.BlockSpec((None, D_MODEL, tf), lambda i, k: (jc, 0, k + 2 * nk)),
            pl.BlockSpec((None, CONV_WIDTH, tf), lambda i, k: (jc, 0, k)),
            pl.BlockSpec((None, 1, tf), lambda i, k: (jc, 0, k)),
            pl.BlockSpec((None, tf, D_MODEL), lambda i, k: (jc, k, 0)),
        ],
        out_specs=pl.BlockSpec((tm, D_MODEL), lambda i, k: (i, 0)),
        out_shape=jax.ShapeDtypeStruct((N_TOK, D_MODEL), F32),
        scratch_shapes=[pltpu.VMEM((tm, D_MODEL), BF16)],
        compiler_params=_params(("parallel", "arbitrary"), blocks, scratch),
        name=f"sconv_{jc}",
    )(y, mod_l, g_l, sc_w_in, sc_w_in, sc_w_in, sc_conv_w, sc_conv_b.reshape(N_C, 1, SC_WIDTH), sc_w_out)


def _inproj_kernel(x_ref, mod_ref, g_ref, w_ref, wk_ref, wg_ref, gb_ref, o_ref, okt_ref, og_ref, h_ref):
    i = pl.program_id(0)
    n = pl.program_id(1)
    n_main = len(PROJ_BLOCKS)

    @pl.when(n == 0)
    def _():
        for rows, r in _sub_rows(i):
            h_ref[rows, :] = _modulate_in(x_ref[rows, :], g_ref, mod_ref, r, 1).astype(BF16)
            o_ref[rows, :] = _dot_nt(h_ref[rows, :], w_ref[...])

    @pl.when((n > 0) & (n < n_main))
    def _():
        o_ref[...] = _dot_nt(h_ref[...], w_ref[...])

    @pl.when(n == n_main)
    def _():
        w_kg = jnp.concatenate([wk_ref[...], wg_ref[...]], axis=0)
        ktg = _dot_nt(w_kg, h_ref[...])
        k_scale = ML_HEAD_DIM ** -0.5
        for c in range(TM // ML_CHUNK):
            okt_ref[c] = ktg[:ML_WIDTH, c * ML_CHUNK:(c + 1) * ML_CHUNK] * k_scale
        g = ktg[ML_WIDTH:, :] + gb_ref[...]
        gate = lax.broadcasted_iota(jnp.int32, g.shape, 0)
        is_forget = ((gate >= ML_HEADS) & (gate < 2 * ML_HEADS)) | (gate >= 3 * ML_HEADS)
        log_sig = jnp.minimum(g, 0.0) - jnp.log1p(jnp.exp(-jnp.abs(g)))
        g = jnp.where(is_forget, log_sig, g)
        gates_t = jnp.concatenate([g, jnp.zeros((LANES - N_GATES, TM), F32)], axis=0)
        og_ref[...] = gates_t.T


def _inproj(y, mod_l, g_l, w_in_t, gate_b_cols, jb, l):
    n_main = len(PROJ_BLOCKS)
    assert PROJ_BLOCKS == tuple(b for b in range(n_main + 1) if b != KM_BLOCK)

    def w_block(i, n):
        m = jnp.minimum(n, n_main - 1)
        return (jb, m + (m >= KM_BLOCK).astype(jnp.int32), 0)

    cpt = TM // ML_CHUNK
    blocks = 4 * (TM * D_MODEL + 2 * TN_PROJ * D_MODEL + N_GATES * D_MODEL + TM * TN_PROJ
                  + ML_WIDTH * TM + TM * LANES)
    return pl.pallas_call(
        _inproj_kernel,
        grid=(N_TOK // TM, n_main + 1),
        in_specs=[
            pl.BlockSpec((TM, D_MODEL), lambda i, n: (i, 0)),
            *_cond_specs(l),
            pl.BlockSpec((None, TN_PROJ, D_MODEL), w_block),
            pl.BlockSpec((None, ML_WIDTH, D_MODEL), lambda i, n: (jb, KM_BLOCK, 0)),
            pl.BlockSpec((None, N_GATES, D_MODEL), lambda i, n: (jb, AB_MAIN // N_GATES, 0)),
            pl.BlockSpec((None, N_GATES, 1), lambda i, n: (jb, 0, 0)),
        ],
        out_specs=[
            pl.BlockSpec((TM, TN_PROJ), lambda i, n: (i, jnp.minimum(n, n_main - 1))),
            pl.BlockSpec((cpt, ML_WIDTH, ML_CHUNK), lambda i, n: (i, 0, 0)),
            pl.BlockSpec((TM, LANES), lambda i, n: (i, 0)),
        ],
        out_shape=[jax.ShapeDtypeStruct((N_TOK, n_main * TN_PROJ), F32),
                   jax.ShapeDtypeStruct((N_TOK // ML_CHUNK, ML_WIDTH, ML_CHUNK), F32),
                   jax.ShapeDtypeStruct((N_TOK, LANES), F32)],
        scratch_shapes=[pltpu.VMEM((TM, D_MODEL), BF16)],
        compiler_params=_params(("parallel", "arbitrary"), blocks, 2 * TM * D_MODEL),
        name=f"inproj_{jb}",
    )(y, mod_l, g_l, w_in_t, w_in_t, w_in_t, gate_b_cols)


def _stack_entry(ref, jb, fills_stack):
    if not fills_stack:
        return ref
    for j in range(ref.shape[0]):
        if j != jb:
            ref[j] = jnp.zeros(ref.shape[1:], ref.dtype)
    return ref.at[jb]


def _stack_spec(entry_shape, jb, fills_stack, n_stack=N_AB, lead=None):
    zeros = (0,) * len(entry_shape)
    if fills_stack:
        return pl.BlockSpec((lead, n_stack) + entry_shape, lambda b: (b, 0) + zeros)
    return pl.BlockSpec((lead, None) + entry_shape, lambda b: (b, jb) + zeros)


def _ctx_attn_kernel(q_ref, k_ref, v_ref, *rest, jb, fills_stack):
    o_ref = rest[-3]
    kc_ref = _stack_entry(rest[-2], jb, fills_stack)
    vc_ref = _stack_entry(rest[-1], jb, fills_stack)
    scale = NA_HEAD_DIM ** -0.5
    k_t = k_ref[...].T
    v_t = v_ref[...].T
    for h in range(NA_HEADS):
        sl = slice(h * NA_HEAD_DIM, (h + 1) * NA_HEAD_DIM)
        kc_ref[h] = k_t[sl, :]
        vc_ref[h] = v_t[sl, :]
    lane = lax.broadcasted_iota(jnp.int32, (SEQ, LANES), 1)
    ones = jnp.ones((SEQ, LANES), BF16)
    for hp in range(NA_WIDTH // LANES):
        pair = slice(hp * LANES, (hp + 1) * LANES)
        q_pair = q_ref[:, pair] * scale
        v_aug = jnp.concatenate([v_ref[:, pair].astype(BF16), ones], axis=1)
        outs = []
        for hh in range(LANES // NA_HEAD_DIM):
            own = (lane >= hh * NA_HEAD_DIM) & (lane < (hh + 1) * NA_HEAD_DIM)
            s = _dot(jnp.where(own, q_pair, 0.0), k_t[pair, :])
            r = _dot(jnp.exp(s - jnp.max(s, axis=-1, keepdims=True)), v_aug)
            outs.append(r[:, :LANES] / r[:, LANES:])
        o_ref[:, pair] = jnp.where(lane < NA_HEAD_DIM, outs[0], outs[1]).astype(o_ref.dtype)


def _ctx_attention(proj, jb, prev):
    cache = jax.ShapeDtypeStruct((BATCH, N_AB, NA_HEADS, NA_HEAD_DIM, SEQ), F32)
    cache_spec = _stack_spec((NA_HEADS, NA_HEAD_DIM, SEQ), jb, prev is None)
    in_specs = [pl.BlockSpec((SEQ, NA_WIDTH), lambda b, c=c: (b, c)) for c in (P_QA, P_KA, P_VA)]
    args = [proj, proj, proj]
    aliases = {}
    if prev is not None:
        in_specs += [pl.BlockSpec(memory_space=pl.ANY)] * 2
        args += list(prev)
        aliases = {3: 1, 4: 2}
    blocks = 4 * (4 * SEQ * NA_WIDTH + 2 * N_AB * NA_WIDTH * SEQ)
    return pl.pallas_call(
        functools.partial(_ctx_attn_kernel, jb=jb, fills_stack=prev is None),
        grid=(BATCH,),
        in_specs=in_specs,
        out_specs=[pl.BlockSpec((SEQ, NA_WIDTH), lambda b: (b, 0)), cache_spec, cache_spec],
        out_shape=[jax.ShapeDtypeStruct((N_PROMPT, NA_WIDTH), BF16), cache, cache],
        input_output_aliases=aliases,
        compiler_params=_params(("parallel",), blocks),
        name=f"ctx_attention_{jb}",
    )(*args)


def _na_bias_kernel(rpb_ref, p_ref):
    base = (pl.program_id(0) * NA_HEADS + pl.program_id(1)) * (N_DR * N_DC)
    shape = (GRID_W, LANES)
    lane = lax.broadcasted_iota(jnp.int32, shape, 1)
    qc = lax.broadcasted_iota(jnp.int32, shape, 0)
    kc = lane & (GRID_W - 1)
    upper = lane >= GRID_W
    cs = jnp.clip(qc - NA_WIN_W // 2, 0, GRID_W - NA_WIN_W)
    col_in = (kc >= cs) & (kc < cs + NA_WIN_W)
    j = lax.broadcasted_iota(jnp.int32, (1, LANES), 1)
    dcol = jnp.clip(j - (GRID_W - 1), -(NA_WIN_W - 1), NA_WIN_W - 1) + (NA_WIN_W - 1)

    def toeplitz(dr, half):
        b = jnp.zeros((1, LANES), F32)
        for d in range(N_DC):
            b = jnp.where(dcol == d, rpb_ref[base + dr * N_DC + d], b)
        shift = (LANES - (GRID_W - 1) + half * GRID_W) % LANES
        return pltpu.roll(jnp.broadcast_to(b, shape), shift, 1, stride=1, stride_axis=0)

    blocks = [[toeplitz(dr, half) for half in range(2)] for dr in range(N_DR)]
    for pair in range(N_PAIR):
        both = jnp.where(upper, blocks[pair + 1][1], blocks[pair][0])
        p_ref[pair] = jnp.where(col_in, both, -jnp.inf)


def _na_bias(na_rpb):
    return pl.pallas_call(
        _na_bias_kernel,
        grid=(N_AB, NA_HEADS),
        in_specs=[pl.BlockSpec(memory_space=pltpu.SMEM)],
        out_specs=pl.BlockSpec((None, None, N_PAIR, GRID_W, LANES), lambda j, h: (j, h, 0, 0, 0)),
        out_shape=jax.ShapeDtypeStruct((N_AB, NA_HEADS, N_PAIR, GRID_W, LANES), F32),
        compiler_params=_params(("parallel", "parallel"), 4 * N_PAIR * GRID_W * LANES),
        name="na_bias",
    )(na_rpb.reshape(-1))


def _na_window_start(r):
    return min(max(r - NA_KH // 2, 0), GRID_ROWS - NA_KH)


def _na_bias_block(p_ref, hh, qr, kr):
    rs = _na_window_start(qr)
    lo_ok = rs <= kr < rs + NA_KH
    hi_ok = rs <= kr + 1 < rs + NA_KH
    if not (lo_ok or hi_ok):
        return jnp.full((GRID_W, LANES), -jnp.inf, F32)
    dr = kr - qr + (NA_WIN_H - 1)
    assert 0 <= dr < N_PAIR
    blk = p_ref[hh, dr]
    if lo_ok and hi_ok:
        return blk
    lane = lax.broadcasted_iota(jnp.int32, (GRID_W, LANES), 1)
    keep = (lane < GRID_W) if lo_ok else (lane >= GRID_W)
    return jnp.where(keep, blk, -jnp.inf)


def _na_kernel(q_ref, k_ref, v_ref, kc_ref, vc_ref, p_ref, o_ref):
    scale = NA_HEAD_DIM ** -0.5
    n_q = NA_QROWS * GRID_W
    lane = lax.broadcasted_iota(jnp.int32, (n_q, LANES), 1)
    k_ctx_t = kc_ref[...].reshape(LANES, PAST_LEN)
    v_ctx_t = vc_ref[...].reshape(LANES, PAST_LEN)
    v_aug_ctx_t = jnp.concatenate([v_ctx_t.astype(BF16), jnp.ones((LANES, PAST_LEN), BF16)], axis=0)
    for q0 in range(0, GRID_ROWS, NA_QROWS):
        k_lo = _na_window_start(q0) // 2 * 2
        k_hi = -(-(_na_window_start(q0 + NA_QROWS - 1) + NA_KH) // 2) * 2
        q_rows = slice(q0 * GRID_W, (q0 + NA_QROWS) * GRID_W)
        k_rows = slice(k_lo * GRID_W, k_hi * GRID_W)
        q_pair = q_ref[q_rows, :] * scale
        n_k = (k_hi - k_lo) * GRID_W
        v_aug_loc = jnp.concatenate([v_ref[k_rows, :].astype(BF16), jnp.ones((n_k, LANES), BF16)], axis=1)
        outs = []
        for hh in range(LANES // NA_HEAD_DIM):
            bias = jnp.concatenate(
                [jnp.concatenate([_na_bias_block(p_ref, hh, qr, kr) for kr in range(k_lo, k_hi, 2)], axis=1)
                 for qr in range(q0, q0 + NA_QROWS)], axis=0)
            own = (lane >= hh * NA_HEAD_DIM) & (lane < (hh + 1) * NA_HEAD_DIM)
            q = jnp.where(own, q_pair, 0.0)
            s_loc = _dot_nt(q, k_ref[k_rows, :]) + bias
            s_ctx = _dot(q, k_ctx_t)
            m = jnp.maximum(jnp.max(s_loc, axis=-1, keepdims=True), jnp.max(s_ctx, axis=-1, keepdims=True))
            r = _dot(jnp.exp(s_loc - m), v_aug_loc) + _dot_nt(jnp.exp(s_ctx - m), v_aug_ctx_t)
            outs.append(r[:, :LANES] / r[:, LANES:])
        o_ref[q_rows, :] = jnp.where(lane < NA_HEAD_DIM, outs[0], outs[1]).astype(o_ref.dtype)


def _na_attention(proj, cache_k_t, cache_v_t, bias_pairs, jb):
    hp = LANES // NA_HEAD_DIM
    s0 = N_PROMPT // DEC_SEQ
    nb = NA_WIDTH // LANES
    ctx_spec = pl.BlockSpec((None, None, hp, NA_HEAD_DIM, PAST_LEN), lambda b, c: (b, jb, c, 0, 0))
    blocks = 4 * (4 * DEC_SEQ * LANES + 2 * hp * NA_HEAD_DIM * PAST_LEN + hp * N_PAIR * GRID_W * LANES)
    return pl.pallas_call(
        _na_kernel,
        grid=(DEC_BATCH, nb),
        in_specs=[
            pl.BlockSpec((DEC_SEQ, LANES), lambda b, c: (s0 + b, P_QA * nb + c)),
            pl.BlockSpec((DEC_SEQ, LANES), lambda b, c: (s0 + b, P_KA * nb + c)),
            pl.BlockSpec((DEC_SEQ, LANES), lambda b, c: (s0 + b, P_VA * nb + c)),
            ctx_spec, ctx_spec,
            pl.BlockSpec((None, hp, N_PAIR, GRID_W, LANES), lambda b, c: (jb, c, 0, 0, 0)),
        ],
        out_specs=pl.BlockSpec((DEC_SEQ, LANES), lambda b, c: (b, c)),
        out_shape=jax.ShapeDtypeStruct((N_SAMPLE, NA_WIDTH), BF16),
        compiler_params=_params(("parallel", "parallel"), blocks),
        name=f"na_attention_{jb}",
    )(proj, proj, proj, cache_k_t, cache_v_t, bias_pairs)


def _running_rows(x, op, identity, reverse):
    n = x.shape[0]
    ridx = lax.broadcasted_iota(jnp.int32, x.shape, 0)
    sh = 1
    while sh < n:
        if reverse:
            x = op(x, jnp.where(ridx < n - sh, pltpu.roll(x, n - sh, 0), identity))
        else:
            x = op(x, jnp.where(ridx >= sh, pltpu.roll(x, sh, 0), identity))
        sh *= 2
    return x


def _mlstm_kernel(*refs, n_chunks, has_state, jb, fills_stack, spb, layer, n_layers):
    if has_state:
        (q_ref, kt_ref, v_ref, o_ref, gt_ref, hn_ref, c0_ref, n0_ref, m0_ref,
         out_ref, cout_ref, nout_ref, mout_ref, hdir_ref, caug_ref) = refs
    else:
        (q_ref, kt_ref, v_ref, o_ref, gt_ref, hn_ref, *_unused,
         out_ref, cout_ref, nout_ref, mout_ref, hdir_ref, caug_ref) = refs
    cout_refs = [_stack_entry(cout_ref.at[s], jb, fills_stack) for s in range(spb)]
    nout_refs = [_stack_entry(nout_ref.at[s], jb, fills_stack) for s in range(spb)]
    mout_refs = [_stack_entry(mout_ref.at[s], jb, fills_stack) for s in range(spb)]
    T = ML_CHUNK
    dh = ML_HEAD_DIM
    seq_len = n_chunks * T
    b = pl.program_id(0)
    row = lax.broadcasted_iota(jnp.int32, (T, T), 0)
    col = lax.broadcasted_iota(jnp.int32, (T, T), 1)
    lane = lax.broadcasted_iota(jnp.int32, (T, LANES), 1)
    lane_row = lax.broadcasted_iota(jnp.int32, (1, LANES), 1)
    ones = jnp.ones((T, dh), F32)
    states = [(d, h) for d in range(2) for h in range(ML_HEADS)]

    def gate_lanes(d, h):
        gi = 2 * d * ML_HEADS + h
        return gi, gi + ML_HEADS

    scans = [(s, d) for s in range(spb) for d in range(2)]
    m_init = []
    for s, d in scans:
        m_row = jnp.zeros((1, LANES), F32)
        for h in range(ML_HEADS):
            if has_state:
                n_bcast = jnp.broadcast_to(n0_ref[s, d, h:h + 1, :], (dh, dh)).T
                caug_ref[s, d, h] = jnp.concatenate([c0_ref[s, d, h], n_bcast], axis=1)
                m0 = m0_ref[(((b * spb + s) * n_layers + layer) * 2 + d) * ML_HEADS + h]
                m_row = jnp.where(lane_row == gate_lanes(d, h)[1], m0, m_row)
            else:
                caug_ref[s, d, h] = jnp.zeros((dh, 2 * dh), F32)
        m_init.append(m_row)

    def chunk_body(ci, m_rows):
        m_next = []
        for scan, (s, d) in enumerate(scans):
            causal = (col <= row) if d == 0 else (col >= row)
            last = T - 1 if d == 0 else 0
            c = ci if d == 0 else n_chunks - 1 - ci
            rows = pl.ds(pl.multiple_of(s * seq_len + c * T, T), T)
            m_prev = m_rows[scan]
            mine = (lane >= (2 * d + 1) * ML_HEADS) & (lane < (2 * d + 2) * ML_HEADS)
            g = gt_ref[rows, :]
            cum = _running_rows(g, jnp.add, 0.0, reverse=(d == 1))
            i_al = pltpu.roll(g, ML_HEADS, 1)
            cvec = jnp.where(mine, i_al - cum, 0.0)
            cum = jnp.where(mine, cum, 0.0)
            m_run = jnp.maximum(m_prev, _running_rows(cvec, jnp.maximum, -jnp.inf, reverse=(d == 1)))
            m_t = cum + m_run
            e_negm = jnp.exp(-m_t)
            m_run_last = m_run[last:last + 1, :]
            wi_last = jnp.exp(m_prev - m_run_last)
            cvec_t = cvec.T
            for h in range(ML_HEADS):
                _, gf = gate_lanes(d, h)
                hl = slice(h * dh, (h + 1) * dh)
                m_run_b = jnp.broadcast_to(m_run[:, gf:gf + 1], (T, T))
                e_negm_b = jnp.broadcast_to(e_negm[:, gf:gf + 1], (T, dh))
                c_row = cvec_t[gf:gf + 1, :]
                w_d = jnp.exp(jnp.where(causal, c_row - m_run_b, -jnp.inf))
                w_inter_b = jnp.exp(m_prev[:, gf:gf + 1] - m_run_b)
                qh = q_ref[rows, hl]
                kh_t = kt_ref[s * n_chunks + c, hl, :]
                v_aug = jnp.concatenate([v_ref[rows, hl], ones], axis=1)
                sc = _dot(qh, kh_t) * w_d
                c_aug = caug_ref[s, d, h]
                nd = jnp.concatenate([w_inter_b, w_inter_b], axis=1) * _dot(qh, c_aug) + _dot(sc, v_aug)
                hdir_ref[d, rows, hl] = nd[:, :dh] / jnp.maximum(jnp.abs(nd[:, dh:]), e_negm_b)
                kw_t = kh_t * jnp.exp(c_row - m_run_last[:, gf:gf + 1])
                caug_ref[s, d, h] = wi_last[:, gf:gf + 1] * c_aug + _dot(kw_t, v_aug)
            m_next.append(m_t[last:last + 1, :])
        return tuple(m_next)

    m_fin = lax.fori_loop(0, n_chunks, chunk_body, tuple(m_init), unroll=2)
    for scan, (s, d) in enumerate(scans):
        for h in range(ML_HEADS):
            c_aug = caug_ref[s, d, h]
            cout_refs[s][d, h] = c_aug[:, :dh]
            nout_refs[s][d, h:h + 1, :] = c_aug[:, dh:].T[0:1, :]
    for s in range(spb):
        m_heads = [pltpu.roll(m_fin[2 * s + d], LANES - gate_lanes(d, 0)[1], 1) for d in range(2)]
        mout_refs[s][...] = jnp.concatenate(m_heads + [jnp.zeros((COND_ROWS - 2, LANES), F32)], axis=0)

    for h in range(ML_HEADS):
        hl = slice(h * dh, (h + 1) * dh)
        h_sum = hdir_ref[0, :, hl] + hdir_ref[1, :, hl]
        gated = jax.nn.sigmoid(o_ref[:, hl]) * (_rms_scale(h_sum) * hn_ref[:, hl])
        out_ref[:, hl] = gated.astype(out_ref.dtype)


def _mlstm(proj, keys_t, gates, hnorm_rows, layer, seq_len, n_seq, row0, spb, n_stack=1, jb=0,
           prev=None, state=None):
    has_state = state is not None
    assert n_seq % spb == 0 and row0 % (spb * seq_len) == 0
    r0 = row0 // (spb * seq_len)
    rows = spb * seq_len
    nb = TN_PROJ // ML_WIDTH
    n_chunks = seq_len // ML_CHUNK
    in_specs = [
        pl.BlockSpec((rows, ML_WIDTH), lambda b: (r0 + b, P_QM * nb)),
        pl.BlockSpec((spb * n_chunks, ML_WIDTH, ML_CHUNK), lambda b: (r0 + b, 0, 0)),
        pl.BlockSpec((rows, ML_WIDTH), lambda b: (r0 + b, P_VM * nb)),
        pl.BlockSpec((rows, ML_WIDTH), lambda b: (r0 + b, P_OM * nb)),
        pl.BlockSpec((rows, LANES), lambda b: (r0 + b, 0)),
        pl.BlockSpec((None, 1, ML_WIDTH), lambda b: (layer, 0, 0)),
    ]
    args = [proj, keys_t, proj, proj, gates, hnorm_rows]
    aliases = {}
    if has_state:
        st_c, st_n, st_m = state
        in_specs += [
            pl.BlockSpec((spb, None, 2, ML_HEADS, ML_HEAD_DIM, ML_HEAD_DIM), lambda b: (b, layer, 0, 0, 0, 0)),
            pl.BlockSpec((spb, None, 2, ML_HEADS, ML_HEAD_DIM), lambda b: (b, layer, 0, 0, 0)),
            pl.BlockSpec(memory_space=pltpu.SMEM),
        ]
        args += [st_c, st_n, st_m.reshape(-1)]
    elif prev is not None:
        aliases = {len(args) + k: 1 + k for k in range(3)}
        in_specs += [pl.BlockSpec(memory_space=pl.ANY)] * 3
        args += list(prev)
    fills_stack = prev is None
    c_bytes = 2 * ML_HEADS * ML_HEAD_DIM * ML_HEAD_DIM
    blocks = 4 * (5 * rows * ML_WIDTH + rows * LANES + spb * (1 + n_stack) * c_bytes)
    scratch = 4 * (2 * rows * ML_WIDTH + spb * 2 * c_bytes)
    return pl.pallas_call(
        functools.partial(_mlstm_kernel, n_chunks=n_chunks, has_state=has_state, jb=jb, fills_stack=fills_stack,
                          spb=spb, layer=layer, n_layers=N_AB),
        grid=(n_seq // spb,),
        in_specs=in_specs,
        out_specs=[
            pl.BlockSpec((rows, ML_WIDTH), lambda b: (b, 0)),
            _stack_spec((2, ML_HEADS, ML_HEAD_DIM, ML_HEAD_DIM), jb, fills_stack, n_stack, spb),
            _stack_spec((2, ML_HEADS, ML_HEAD_DIM), jb, fills_stack, n_stack, spb),
            _stack_spec((COND_ROWS, LANES), jb, fills_stack, n_stack, spb),
        ],
        out_shape=[
            jax.ShapeDtypeStruct((n_seq * seq_len, ML_WIDTH), BF16),
            jax.ShapeDtypeStruct((n_seq, n_stack, 2, ML_HEADS, ML_HEAD_DIM, ML_HEAD_DIM), F32),
            jax.ShapeDtypeStruct((n_seq, n_stack, 2, ML_HEADS, ML_HEAD_DIM), F32),
            jax.ShapeDtypeStruct((n_seq, n_stack, COND_ROWS, LANES), F32),
        ],
        scratch_shapes=[pltpu.VMEM((2, rows, ML_WIDTH), F32),
                        pltpu.VMEM((spb, 2, ML_HEADS, ML_HEAD_DIM, 2 * ML_HEAD_DIM), F32)],
        input_output_aliases=aliases,
        compiler_params=_params(("parallel",), blocks, scratch),
        name=f"mlstm_{'latent' if has_state else 'context'}_{jb}",
    )(*args)


def _outproj_kernel(x_ref, mod_ref, g_ref, ap_ref, mp_ref, as_ref, ms_ref, wa_ref, wm_ref, o_ref):
    i = pl.program_id(0)

    def finish(a_ref, m_ref):
        wa = wa_ref[...].astype(BF16)
        wm = wm_ref[...].astype(BF16)
        for rows, r in _sub_rows(i, TM_OUT):
            out = _dot(a_ref[rows, :], wa) + _dot(m_ref[rows, :], wm)
            o_ref[rows, :] = _gated_residual(x_ref[rows, :], out, g_ref, mod_ref, r, 1, 1.0)

    is_prompt = i < N_PROMPT // TM_OUT
    pl.when(is_prompt)(functools.partial(finish, ap_ref, mp_ref))
    pl.when(jnp.logical_not(is_prompt))(functools.partial(finish, as_ref, ms_ref))


def _outproj(y, mod_l, g_l, a_p, m_p, a_s, m_s, ab_w_out, jb, l):
    tm = TM_OUT
    n_p = N_PROMPT // tm
    n_s = N_SAMPLE // tm
    p_spec = pl.BlockSpec((tm, NA_WIDTH), lambda i: (jnp.minimum(i, n_p - 1), 0))
    s_spec = pl.BlockSpec((tm, NA_WIDTH), lambda i: (jnp.clip(i - n_p, 0, n_s - 1), 0))
    blocks = 4 * (2 * tm * D_MODEL + 4 * tm * NA_WIDTH + D_MODEL * D_MODEL)
    return pl.pallas_call(
        _outproj_kernel,
        grid=(N_TOK // tm,),
        in_specs=[
            pl.BlockSpec((tm, D_MODEL), lambda i: (i, 0)),
            *_cond_specs(l),
            p_spec, p_spec, s_spec, s_spec,
            pl.BlockSpec((None, NA_WIDTH, D_MODEL), lambda i: (jb, 0, 0)),
            pl.BlockSpec((None, ML_WIDTH, D_MODEL), lambda i: (jb, 1, 0)),
        ],
        out_specs=pl.BlockSpec((tm, D_MODEL), lambda i: (i, 0)),
        out_shape=jax.ShapeDtypeStruct((N_TOK, D_MODEL), F32),
        compiler_params=_params(("parallel",), blocks),
        name=f"outproj_{jb}",
    )(y, mod_l, g_l, a_p, m_p, a_s, m_s, ab_w_out, ab_w_out)


def kernel(x_prompt, x_sample, cache_k, cache_v, state_C, state_n, state_m, c, c_ctx, ada_w, ada_b, norm_g,
           ffn_w13, ffn_w2, ab_w_in, ab_w_out, na_rpb, ml_gate_b, ml_hnorm, sc_w_in, sc_conv_w, sc_conv_b,
           sc_w_out):
    assert NA_WIDTH == ML_WIDTH == TN_PROJ and AB_MAIN % N_GATES == 0
    assert math.frexp(NA_HEAD_DIM ** -0.5)[0] == 0.5
    assert LANES == 2 * NA_HEAD_DIM
    assert N_PROMPT % TM == 0 and N_SAMPLE % TM == 0 and TM % COND_UNIT == 0
    cond = jnp.concatenate([c_ctx[None, :], c, jnp.zeros((COND_ROWS - 1 - DEC_BATCH, D_MODEL), F32)], axis=0)
    mod = _modulation(cond, ada_w, ada_b)
    gains = norm_g.reshape(DEPTH, 6, D_MODEL)
    bias_pairs = _na_bias(na_rpb)
    w_in_t = jnp.swapaxes(ab_w_in, 1, 2)
    cache_k_t = jnp.swapaxes(cache_k, 3, 4)
    cache_v_t = jnp.swapaxes(cache_v, 3, 4)
    gate_b_cols = ml_gate_b.reshape(N_AB, N_GATES, 1)
    hnorm_rows = ml_hnorm.reshape(N_AB, 1, ML_WIDTH)
    state = (state_C, state_n, state_m)

    kv_new = None
    st_new = None
    y = None
    for l in range(DEPTH):
        if l == 0:
            y = _ffn_sublayer(x_prompt.reshape(N_PROMPT, D_MODEL), mod, gains, ffn_w13, ffn_w2, l, 0,
                              y_tail=x_sample.reshape(N_SAMPLE, D_MODEL))
        else:
            y = _ffn_sublayer(y, mod, gains, ffn_w13, ffn_w2, l, 0)
        j = l // 2
        if l % 2 == 0:
            proj, keys_t, gates = _inproj(y, mod, gains, w_in_t, gate_b_cols, j, l)
            a_p, *kv_new = _ctx_attention(proj, j, kv_new)
            m_p, *st_new = _mlstm(proj, keys_t, gates, hnorm_rows, j, SEQ, BATCH, 0,
                                  ML_SEQS_PER_STEP, n_stack=N_AB, jb=j, prev=st_new)
            a_s = _na_attention(proj, cache_k_t, cache_v_t, bias_pairs, j)
            m_s, _, _, _ = _mlstm(proj, keys_t, gates, hnorm_rows, j, DEC_SEQ, DEC_BATCH, N_PROMPT,
                                  ML_SEQS_PER_STEP, state=state)
            y = _outproj(y, mod, gains, a_p, m_p, a_s, m_s, ab_w_out, j, l)
        else:
            y = _sconv_sublayer(y, mod, gains, sc_w_in, sc_conv_w, sc_conv_b, sc_w_out, j, l)
        if l < DEPTH - 1:
            y = _ffn_sublayer(y, mod, gains, ffn_w13, ffn_w2, l, 1)

    p_tiles = N_PROMPT // TM
    y_p = _ffn_sublayer(y, mod, gains, ffn_w13, ffn_w2, DEPTH - 1, 1, 0, p_tiles)
    y_s = _ffn_sublayer(y, mod, gains, ffn_w13, ffn_w2, DEPTH - 1, 1, p_tiles, N_TOK // TM - p_tiles)
    y_p = y_p.reshape(BATCH, SEQ, D_MODEL)
    y_s = y_s.reshape(DEC_BATCH, DEC_SEQ, D_MODEL)
    new_c, new_n, m_rows = st_new
    new_m = m_rows[:, :, :2, :ML_HEADS]
    return (y_p, y_s, jnp.swapaxes(kv_new[0], 3, 4), jnp.swapaxes(kv_new[1], 3, 4), new_c, new_n, new_m)
```

```python
import functools
import math

import jax
import jax.numpy as jnp
from jax import lax
from jax.experimental import pallas as pl
from jax.experimental.pallas import tpu as pltpu

D_MODEL = 1024
BATCH = 16
SEQ = 256
DEPTH = 4
DEC_BATCH = 2
DEC_SEQ = 1024
PAST_LEN = 256
GRID_W = 64
NA_HEADS = 8
NA_HEAD_DIM = 64
NA_WIN_H = 8
NA_WIN_W = 16
ML_HEADS = 4
ML_HEAD_DIM = 128
ML_CHUNK = 128
CONV_WIDTH = 3
D_FF = 2816
N_MOD = 9
EPS = 1e-6
N_AB = (DEPTH + 1) // 2
N_C = DEPTH // 2
NA_WIDTH = NA_HEADS * NA_HEAD_DIM
ML_WIDTH = ML_HEADS * ML_HEAD_DIM
AB_MAIN = 3 * NA_WIDTH + 4 * ML_WIDTH
N_GATES = 4 * ML_HEADS
AB_IN = AB_MAIN + N_GATES
SC_WIDTH = D_MODEL

N_PROMPT = BATCH * SEQ
N_SAMPLE = DEC_BATCH * DEC_SEQ
N_TOK = N_PROMPT + N_SAMPLE
GRID_ROWS = DEC_SEQ // GRID_W
NA_KH = min(NA_WIN_H, GRID_ROWS)
N_DR = 2 * NA_WIN_H - 1
N_DC = 2 * NA_WIN_W - 1
N_PAIR = N_DR - 1

LANES = 128
COND_ROWS = 8
VMEM_CAP = 60 * 1024 * 1024

COND_UNIT = DEC_SEQ
TM = 2048
SUB_ROWS = 512
TM_OUT = 1024
TM_CONV = COND_UNIT
TF_CONV = 512
TF = 256
TN_PROJ = 512
TN_MOD = 1536
NA_QROWS = 8
ML_SEQS_PER_STEP = 2

PROJ_BLOCKS = (0, 1, 2, 3, 5, 6)
KM_BLOCK = 4
P_QA, P_KA, P_VA, P_QM, P_VM, P_OM = range(6)

F32 = jnp.float32
BF16 = jnp.bfloat16


def _vmem_limit(block_bytes, scratch_bytes):
    assert 2 * block_bytes + scratch_bytes <= VMEM_CAP
    return VMEM_CAP


def _params(semantics, block_bytes, scratch_bytes=0):
    return pltpu.CompilerParams(dimension_semantics=semantics,
                                vmem_limit_bytes=_vmem_limit(block_bytes, scratch_bytes))


def _unit_rows(i, tm=TM):
    first_latent = N_PROMPT // COND_UNIT
    units = tm // COND_UNIT
    return [(slice(u * COND_UNIT, (u + 1) * COND_UNIT), jnp.maximum(i * units + u - (first_latent - 1), 0))
            for u in range(units)]


def _sub_rows(i, tm=TM):
    per_unit = COND_UNIT // SUB_ROWS
    return [(slice(rows.start + s * SUB_ROWS, rows.start + (s + 1) * SUB_ROWS), r)
            for rows, r in _unit_rows(i, tm) for s in range(per_unit)]


def _rms_scale(x):
    return x * lax.rsqrt(jnp.mean(x * x, axis=-1, keepdims=True) + EPS)


def _modulate_in(x, g_ref, mod_ref, r, j):
    gain = g_ref[2 * j:2 * j + 1, :] * (1.0 + mod_ref[r, 3 * j + 1:3 * j + 2, :])
    return _rms_scale(x) * gain + mod_ref[r, 3 * j:3 * j + 1, :]


def _gated_residual(x, out, g_ref, mod_ref, r, j, weight):
    gain = (weight * mod_ref[r, 3 * j + 2:3 * j + 3, :]) * g_ref[2 * j + 1:2 * j + 2, :]
    return x + _rms_scale(out) * gain


def _cond_specs(l):
    return [pl.BlockSpec((None, COND_ROWS, N_MOD, D_MODEL), lambda *_: (l, 0, 0, 0)),
            pl.BlockSpec((None, 6, D_MODEL), lambda *_: (l, 0, 0))]


def _dot(a, b):
    return jnp.dot(a.astype(BF16), b.astype(BF16), preferred_element_type=F32)


def _dot_nt(a, b):
    return lax.dot_general(a.astype(BF16), b.astype(BF16), (((1,), (1,)), ((), ())),
                           preferred_element_type=F32)


def _mod_kernel(c_ref, w_ref, b_ref, o_ref):
    c = c_ref[...]
    s = c * jax.nn.sigmoid(c)
    o_ref[...] = _dot(s, w_ref[...]) + b_ref[...]


def _modulation(cond, ada_w, ada_b):
    n = N_MOD * D_MODEL
    out = pl.pallas_call(
        _mod_kernel,
        grid=(DEPTH, n // TN_MOD),
        in_specs=[
            pl.BlockSpec((COND_ROWS, D_MODEL), lambda l, t: (0, 0)),
            pl.BlockSpec((None, D_MODEL, TN_MOD), lambda l, t: (l, 0, t)),
            pl.BlockSpec((None, 1, TN_MOD), lambda l, t: (l, 0, t)),
        ],
        out_specs=pl.BlockSpec((None, COND_ROWS, TN_MOD), lambda l, t: (l, 0, t)),
        out_shape=jax.ShapeDtypeStruct((DEPTH, COND_ROWS, n), F32),
        compiler_params=_params(("parallel", "parallel"), 4 * D_MODEL * TN_MOD),
        name="modulation",
    )(cond, ada_w, ada_b.reshape(DEPTH, 1, n))
    return out.reshape(DEPTH, COND_ROWS, N_MOD, D_MODEL)


def _ffn_kernel(*refs, j, tile0, n_head):
    i = pl.program_id(0) + tile0
    k = pl.program_id(1)
    last = pl.num_programs(1) - 1
    if n_head is None:
        x_ref, mod_ref, g_ref, wa_ref, wg_ref, w2_ref, o_ref, h_ref = refs
        sources = [(None, x_ref)]
    else:
        x_ref, x_tail_ref, mod_ref, g_ref, wa_ref, wg_ref, w2_ref, o_ref, h_ref = refs
        sources = [(pl.program_id(0) < n_head, x_ref), (pl.program_id(0) >= n_head, x_tail_ref)]

    def per_source(fn):
        for pred, ref in sources:
            if pred is None:
                fn(ref)
            else:
                pl.when(pred)(functools.partial(fn, ref))

    def hidden_step(rows, init):
        h = h_ref[rows, :]
        a = _dot(h, wa_ref[...])
        g = _dot(h, wg_ref[...])
        d = _dot((g * jax.nn.sigmoid(g)) * a, w2_ref[...])
        if init:
            o_ref[rows, :] = d
        else:
            o_ref[rows, :] += d

    def first_step(x_src):
        for rows, r in _sub_rows(i):
            h_ref[rows, :] = _modulate_in(x_src[rows, :], g_ref, mod_ref, r, j).astype(BF16)
            hidden_step(rows, init=True)

    def last_step(x_src):
        for rows, r in _sub_rows(i):
            hidden_step(rows, init=False)
            o_ref[rows, :] = _gated_residual(x_src[rows, :], o_ref[rows, :], g_ref, mod_ref, r, j, 0.5)

    pl.when(k == 0)(functools.partial(per_source, first_step))

    @pl.when((k > 0) & (k < last))
    def _():
        for rows, _ in _unit_rows(i):
            hidden_step(rows, init=False)

    pl.when(k == last)(functools.partial(per_source, last_step))


def _ffn_sublayer(y, mod, gains, ffn_w13, ffn_w2, l, s, tile0=0, n_tiles=N_TOK // TM, y_tail=None):
    j = 2 * s
    nk = D_FF // TF
    blocks = 4 * (2 * TM * D_MODEL + 3 * D_MODEL * TF)
    scratch = 2 * TM * D_MODEL
    if y_tail is None:
        n_head = None
        x_specs = [pl.BlockSpec((TM, D_MODEL), lambda i, k: (tile0 + i, 0))]
        xs = [y]
    else:
        assert tile0 == 0
        n_head = y.shape[0] // TM
        x_specs = [pl.BlockSpec((TM, D_MODEL), lambda i, k: (jnp.minimum(i, n_head - 1), 0)),
                   pl.BlockSpec((TM, D_MODEL), lambda i, k: (jnp.maximum(i - n_head, 0), 0))]
        xs = [y, y_tail]
        scratch += 2 * 4 * TM * D_MODEL
    return pl.pallas_call(
        functools.partial(_ffn_kernel, j=j, tile0=tile0, n_head=n_head),
        grid=(n_tiles, nk),
        in_specs=[
            *x_specs,
            *_cond_specs(l),
            pl.BlockSpec((None, None, D_MODEL, TF), lambda i, k: (l, s, 0, k)),
            pl.BlockSpec((None, None, D_MODEL, TF), lambda i, k: (l, s, 0, k + nk)),
            pl.BlockSpec((None, None, TF, D_MODEL), lambda i, k: (l, s, k, 0)),
        ],
        out_specs=pl.BlockSpec((TM, D_MODEL), lambda i, k: (i, 0)),
        out_shape=jax.ShapeDtypeStruct((n_tiles * TM, D_MODEL), F32),
        scratch_shapes=[pltpu.VMEM((TM, D_MODEL), BF16)],
        compiler_params=_params(("parallel", "arbitrary"), blocks, scratch),
        name=f"ffn_l{l}_s{s}_t{tile0}",
    )(*xs, mod, gains, ffn_w13, ffn_w13, ffn_w2)


def _sconv_kernel(x_ref, mod_ref, g_ref, wb_ref, wc_ref, wx_ref, cw_ref, cb_ref, wo_ref, o_ref, h_ref, *, jc):
    i = pl.program_id(0)
    k = pl.program_id(1)
    ((_, r),) = _unit_rows(i, TM_CONV)
    seq = jnp.where(i < N_PROMPT // TM_CONV, SEQ, DEC_SEQ)
    blk = SEQ
    n_blk = TM_CONV // blk
    row = lax.broadcasted_iota(jnp.int32, (blk, TF_CONV), 0)
    zero_row = jnp.zeros((1, TF_CONV), F32)

    def step(first, last):
        wb = wb_ref[...].astype(BF16)
        wc = wc_ref[...].astype(BF16)
        wx = wx_ref[...].astype(BF16)
        wo = wo_ref[...].astype(BF16)
        gbs, us = [], []

        def project(bi):
            rows = slice(bi * blk, (bi + 1) * blk)
            if first:
                h_ref[rows, :] = _modulate_in(x_ref[rows, :], g_ref, mod_ref, r, 1).astype(BF16)
            h = h_ref[rows, :]
            gbs.append(_dot(h, wb))
            us.append(_dot(h, wc) * _dot(h, wx))

        project(0)
        for bi in range(n_blk):
            if bi + 1 < n_blk:
                project(bi + 1)
            rows = slice(bi * blk, (bi + 1) * blk)
            ub = us[bi]
            t = (row + bi * blk) & (seq - 1)
            above = us[bi - 1][blk - 1:blk, :] if bi > 0 else zero_row
            below = us[bi + 1][0:1, :] if bi < n_blk - 1 else zero_row
            u_prev = jnp.where(t == 0, 0.0, jnp.where(row == 0, above, pltpu.roll(ub, 1, 0)))
            u_next = jnp.where(t == seq - 1, 0.0, jnp.where(row == blk - 1, below, pltpu.roll(ub, blk - 1, 0)))
            y = cb_ref[jc:jc + 1, :] + u_prev * cw_ref[0:1, :] + ub * cw_ref[1:2, :] + u_next * cw_ref[2:3, :]
            d = _dot(gbs[bi] * y, wo)
            acc = d if first else o_ref[rows, :] + d
            o_ref[rows, :] = _gated_residual(x_ref[rows, :], acc, g_ref, mod_ref, r, 1, 1.0) if last else acc

    n_steps = SC_WIDTH // TF_CONV
    for kk in range(n_steps):
        pl.when(k == kk)(functools.partial(step, kk == 0, kk == n_steps - 1))


def _sconv_sublayer(y, mod, gains, sc_w_in, sc_conv_w, sc_conv_b, sc_w_out, jc, l):
    tm, tf = TM_CONV, TF_CONV
    nk = SC_WIDTH // tf
    blocks = 4 * (2 * tm * D_MODEL + 4 * D_MODEL * tf)
    scratch = 2 * tm * D_MODEL
    return pl.pallas_call(
        functools.partial(_sconv_kernel, jc=jc),
        grid=(N_TOK // tm, nk),
        in_specs=[
            pl.BlockSpec((tm, D_MODEL), lambda i, k: (i, 0)),
            *_cond_specs(l),
            pl.BlockSpec((None, D_MODEL, tf), lambda i, k: (jc, 0, k)),
            pl.BlockSpec((None, D_MODEL, tf), lambda i, k: (jc, 0, k + nk)),
            pl.BlockSpec((None, D_MODEL, tf), lambda i, k: (jc, 0, k + 2 * nk)),
            pl.BlockSpec((None, CONV_WIDTH, tf), lambda i, k: (jc, 0, k)),
            pl.BlockSpec((N_C, tf), lambda i, k: (0, k)),
            pl.BlockSpec((None, tf, D_MODEL), lambda i, k: (jc, k, 0)),
        ],
        out_specs=pl.BlockSpec((tm, D_MODEL), lambda i, k: (i, 0)),
        out_shape=jax.ShapeDtypeStruct((N_TOK, D_MODEL), F32),
        scratch_shapes=[pltpu.VMEM((tm, D_MODEL), BF16)],
        compiler_params=_params(("parallel", "arbitrary"), blocks, scratch),
        name=f"sconv_{jc}",
    )(y, mod, gains, sc_w_in, sc_w_in, sc_w_in, sc_conv_w, sc_conv_b, sc_w_out)


def _inproj_kernel(x_ref, mod_ref, g_ref, w_ref, wk_ref, wg_ref, gb_ref, o_ref, okt_ref, og_ref, h_ref):
    i = pl.program_id(0)
    n = pl.program_id(1)
    n_main = len(PROJ_BLOCKS)

    @pl.when(n == 0)
    def _():
        for rows, r in _sub_rows(i):
            h_ref[rows, :] = _modulate_in(x_ref[rows, :], g_ref, mod_ref, r, 1).astype(BF16)
            o_ref[rows, :] = _dot_nt(h_ref[rows, :], w_ref[...])

    @pl.when((n > 0) & (n < n_main))
    def _():
        o_ref[...] = _dot_nt(h_ref[...], w_ref[...])

    @pl.when(n == n_main)
    def _():
        w_kg = jnp.concatenate([wk_ref[...], wg_ref[...]], axis=0)
        ktg = _dot_nt(w_kg, h_ref[...])
        k_scale = ML_HEAD_DIM ** -0.5
        for c in range(TM // ML_CHUNK):
            okt_ref[c] = ktg[:ML_WIDTH, c * ML_CHUNK:(c + 1) * ML_CHUNK] * k_scale
        g = ktg[ML_WIDTH:, :] + gb_ref[...]
        gate = lax.broadcasted_iota(jnp.int32, g.shape, 0)
        is_forget = ((gate >= ML_HEADS) & (gate < 2 * ML_HEADS)) | (gate >= 3 * ML_HEADS)
        log_sig = jnp.minimum(g, 0.0) - jnp.log1p(jnp.exp(-jnp.abs(g)))
        g = jnp.where(is_forget, log_sig, g)
        gates_t = jnp.concatenate([g, jnp.zeros((LANES - N_GATES, TM), F32)], axis=0)
        og_ref[...] = gates_t.T


def _inproj(y, mod, gains, w_in_t, gate_b_cols, jb, l):
    n_main = len(PROJ_BLOCKS)
    assert PROJ_BLOCKS == tuple(b for b in range(n_main + 1) if b != KM_BLOCK)

    def w_block(i, n):
        m = jnp.minimum(n, n_main - 1)
        return (jb, m + (m >= KM_BLOCK).astype(jnp.int32), 0)

    cpt = TM // ML_CHUNK
    blocks = 4 * (TM * D_MODEL + 2 * TN_PROJ * D_MODEL + N_GATES * D_MODEL + TM * TN_PROJ
                  + ML_WIDTH * TM + TM * LANES)
    return pl.pallas_call(
        _inproj_kernel,
        grid=(N_TOK // TM, n_main + 1),
        in_specs=[
            pl.BlockSpec((TM, D_MODEL), lambda i, n: (i, 0)),
            *_cond_specs(l),
            pl.BlockSpec((None, TN_PROJ, D_MODEL), w_block),
            pl.BlockSpec((None, ML_WIDTH, D_MODEL), lambda i, n: (jb, KM_BLOCK, 0)),
            pl.BlockSpec((None, N_GATES, D_MODEL), lambda i, n: (jb, AB_MAIN // N_GATES, 0)),
            pl.BlockSpec((None, N_GATES, 1), lambda i, n: (jb, 0, 0)),
        ],
        out_specs=[
            pl.BlockSpec((TM, TN_PROJ), lambda i, n: (i, jnp.minimum(n, n_main - 1))),
            pl.BlockSpec((cpt, ML_WIDTH, ML_CHUNK), lambda i, n: (i, 0, 0)),
            pl.BlockSpec((TM, LANES), lambda i, n: (i, 0)),
        ],
        out_shape=[jax.ShapeDtypeStruct((N_TOK, n_main * TN_PROJ), F32),
                   jax.ShapeDtypeStruct((N_TOK // ML_CHUNK, ML_WIDTH, ML_CHUNK), F32),
                   jax.ShapeDtypeStruct((N_TOK, LANES), F32)],
        scratch_shapes=[pltpu.VMEM((TM, D_MODEL), BF16)],
        compiler_params=_params(("parallel", "arbitrary"), blocks, 2 * TM * D_MODEL),
        name=f"inproj_{jb}",
    )(y, mod, gains, w_in_t, w_in_t, w_in_t, gate_b_cols)


def _stack_entry(ref, jb, fills_stack):
    if not fills_stack:
        return ref
    for j in range(ref.shape[0]):
        if j != jb:
            ref[j] = jnp.zeros(ref.shape[1:], ref.dtype)
    return ref.at[jb]


def _stack_spec(entry_shape, jb, fills_stack, n_stack=N_AB, lead=None):
    zeros = (0,) * len(entry_shape)
    if fills_stack:
        return pl.BlockSpec((lead, n_stack) + entry_shape, lambda b: (b, 0) + zeros)
    return pl.BlockSpec((lead, None) + entry_shape, lambda b: (b, jb) + zeros)


def _ctx_attn_kernel(q_ref, k_ref, v_ref, *rest, jb, fills_stack):
    o_ref = rest[-3]
    kc_ref = _stack_entry(rest[-2], jb, fills_stack)
    vc_ref = _stack_entry(rest[-1], jb, fills_stack)
    scale = NA_HEAD_DIM ** -0.5
    k_t = k_ref[...].T
    v_t = v_ref[...].T
    for h in range(NA_HEADS):
        sl = slice(h * NA_HEAD_DIM, (h + 1) * NA_HEAD_DIM)
        kc_ref[h] = k_t[sl, :]
        vc_ref[h] = v_t[sl, :]
    lane = lax.broadcasted_iota(jnp.int32, (SEQ, LANES), 1)
    ones = jnp.ones((SEQ, LANES), BF16)
    for hp in range(NA_WIDTH // LANES):
        pair = slice(hp * LANES, (hp + 1) * LANES)
        q_pair = q_ref[:, pair] * scale
        v_aug = jnp.concatenate([v_ref[:, pair].astype(BF16), ones], axis=1)
        outs = []
        for hh in range(LANES // NA_HEAD_DIM):
            own = (lane >= hh * NA_HEAD_DIM) & (lane < (hh + 1) * NA_HEAD_DIM)
            s = _dot(jnp.where(own, q_pair, 0.0), k_t[pair, :])
            r = _dot(jnp.exp(s - jnp.max(s, axis=-1, keepdims=True)), v_aug)
            outs.append(r[:, :LANES] / r[:, LANES:])
        o_ref[:, pair] = jnp.where(lane < NA_HEAD_DIM, outs[0], outs[1]).astype(o_ref.dtype)


def _ctx_attention(proj, jb, prev):
    cache = jax.ShapeDtypeStruct((BATCH, N_AB, NA_HEADS, NA_HEAD_DIM, SEQ), F32)
    cache_spec = _stack_spec((NA_HEADS, NA_HEAD_DIM, SEQ), jb, prev is None)
    in_specs = [pl.BlockSpec((SEQ, NA_WIDTH), lambda b, c=c: (b, c)) for c in (P_QA, P_KA, P_VA)]
    args = [proj, proj, proj]
    aliases = {}
    if prev is not None:
        in_specs += [pl.BlockSpec(memory_space=pl.ANY)] * 2
        args += list(prev)
        aliases = {3: 1, 4: 2}
    blocks = 4 * (4 * SEQ * NA_WIDTH + 2 * N_AB * NA_WIDTH * SEQ)
    return pl.pallas_call(
        functools.partial(_ctx_attn_kernel, jb=jb, fills_stack=prev is None),
        grid=(BATCH,),
        in_specs=in_specs,
        out_specs=[pl.BlockSpec((SEQ, NA_WIDTH), lambda b: (b, 0)), cache_spec, cache_spec],
        out_shape=[jax.ShapeDtypeStruct((N_PROMPT, NA_WIDTH), BF16), cache, cache],
        input_output_aliases=aliases,
        compiler_params=_params(("parallel",), blocks),
        name=f"ctx_attention_{jb}",
    )(*args)


def _na_bias_kernel(rpb_ref, p_ref):
    layer = pl.program_id(0)
    head = pl.program_id(1)
    shape = (GRID_W, LANES)
    lane = lax.broadcasted_iota(jnp.int32, shape, 1)
    qc = lax.broadcasted_iota(jnp.int32, shape, 0)
    kc = lane & (GRID_W - 1)
    upper = lane >= GRID_W
    cs = jnp.clip(qc - NA_WIN_W // 2, 0, GRID_W - NA_WIN_W)
    col_in = (kc >= cs) & (kc < cs + NA_WIN_W)
    j = lax.broadcasted_iota(jnp.int32, (1, LANES), 1)
    dcol = jnp.clip(j - (GRID_W - 1), -(NA_WIN_W - 1), NA_WIN_W - 1) + (NA_WIN_W - 1)

    def toeplitz(dr, half):
        b = jnp.zeros((1, LANES), F32)
        for d in range(N_DC):
            b = jnp.where(dcol == d, rpb_ref[layer, head, dr, d], b)
        shift = (LANES - (GRID_W - 1) + half * GRID_W) % LANES
        return pltpu.roll(jnp.broadcast_to(b, shape), shift, 1, stride=1, stride_axis=0)

    blocks = [[toeplitz(dr, half) for half in range(2)] for dr in range(N_DR)]
    for pair in range(N_PAIR):
        both = jnp.where(upper, blocks[pair + 1][1], blocks[pair][0])
        p_ref[pair] = jnp.where(col_in, both, -jnp.inf)


def _na_bias(na_rpb):
    return pl.pallas_call(
        _na_bias_kernel,
        grid=(N_AB, NA_HEADS),
        in_specs=[pl.BlockSpec(memory_space=pltpu.SMEM)],
        out_specs=pl.BlockSpec((None, None, N_PAIR, GRID_W, LANES), lambda j, h: (j, h, 0, 0, 0)),
        out_shape=jax.ShapeDtypeStruct((N_AB, NA_HEADS, N_PAIR, GRID_W, LANES), F32),
        compiler_params=_params(("parallel", "parallel"), 4 * N_PAIR * GRID_W * LANES),
        name="na_bias",
    )(na_rpb)


def _na_window_start(r):
    return min(max(r - NA_KH // 2, 0), GRID_ROWS - NA_KH)


def _na_bias_block(p_ref, hh, qr, kr):
    rs = _na_window_start(qr)
    lo_ok = rs <= kr < rs + NA_KH
    hi_ok = rs <= kr + 1 < rs + NA_KH
    if not (lo_ok or hi_ok):
        return jnp.full((GRID_W, LANES), -jnp.inf, F32)
    dr = kr - qr + (NA_WIN_H - 1)
    assert 0 <= dr < N_PAIR
    blk = p_ref[hh, dr]
    if lo_ok and hi_ok:
        return blk
    lane = lax.broadcasted_iota(jnp.int32, (GRID_W, LANES), 1)
    keep = (lane < GRID_W) if lo_ok else (lane >= GRID_W)
    return jnp.where(keep, blk, -jnp.inf)


def _na_kernel(q_ref, k_ref, v_ref, kc_ref, vc_ref, p_ref, o_ref):
    scale = NA_HEAD_DIM ** -0.5
    n_q = NA_QROWS * GRID_W
    lane = lax.broadcasted_iota(jnp.int32, (n_q, LANES), 1)
    k_ctx_t = kc_ref[...].reshape(LANES, PAST_LEN)
    v_ctx_t = vc_ref[...].reshape(LANES, PAST_LEN)
    v_aug_ctx_t = jnp.concatenate([v_ctx_t.astype(BF16), jnp.ones((LANES, PAST_LEN), BF16)], axis=0)
    for q0 in range(0, GRID_ROWS, NA_QROWS):
        k_lo = _na_window_start(q0) // 2 * 2
        k_hi = -(-(_na_window_start(q0 + NA_QROWS - 1) + NA_KH) // 2) * 2
        q_rows = slice(q0 * GRID_W, (q0 + NA_QROWS) * GRID_W)
        k_rows = slice(k_lo * GRID_W, k_hi * GRID_W)
        q_pair = q_ref[q_rows, :] * scale
        n_k = (k_hi - k_lo) * GRID_W
        v_aug_loc = jnp.concatenate([v_ref[k_rows, :].astype(BF16), jnp.ones((n_k, LANES), BF16)], axis=1)
        outs = []
        for hh in range(LANES // NA_HEAD_DIM):
            bias = jnp.concatenate(
                [jnp.concatenate([_na_bias_block(p_ref, hh, qr, kr) for kr in range(k_lo, k_hi, 2)], axis=1)
                 for qr in range(q0, q0 + NA_QROWS)], axis=0)
            own = (lane >= hh * NA_HEAD_DIM) & (lane < (hh + 1) * NA_HEAD_DIM)
            q = jnp.where(own, q_pair, 0.0)
            s_loc = _dot_nt(q, k_ref[k_rows, :]) + bias
            s_ctx = _dot(q, k_ctx_t)
            m = jnp.maximum(jnp.max(s_loc, axis=-1, keepdims=True), jnp.max(s_ctx, axis=-1, keepdims=True))
            r = _dot(jnp.exp(s_loc - m), v_aug_loc) + _dot_nt(jnp.exp(s_ctx - m), v_aug_ctx_t)
            outs.append(r[:, :LANES] / r[:, LANES:])
        o_ref[q_rows, :] = jnp.where(lane < NA_HEAD_DIM, outs[0], outs[1]).astype(o_ref.dtype)


def _na_attention(proj, cache_k_t, cache_v_t, bias_pairs, jb):
    hp = LANES // NA_HEAD_DIM
    s0 = N_PROMPT // DEC_SEQ
    nb = NA_WIDTH // LANES
    ctx_spec = pl.BlockSpec((None, None, hp, NA_HEAD_DIM, PAST_LEN), lambda b, c: (b, jb, c, 0, 0))
    blocks = 4 * (4 * DEC_SEQ * LANES + 2 * hp * NA_HEAD_DIM * PAST_LEN + hp * N_PAIR * GRID_W * LANES)
    return pl.pallas_call(
        _na_kernel,
        grid=(DEC_BATCH, nb),
        in_specs=[
            pl.BlockSpec((DEC_SEQ, LANES), lambda b, c: (s0 + b, P_QA * nb + c)),
            pl.BlockSpec((DEC_SEQ, LANES), lambda b, c: (s0 + b, P_KA * nb + c)),
            pl.BlockSpec((DEC_SEQ, LANES), lambda b, c: (s0 + b, P_VA * nb + c)),
            ctx_spec, ctx_spec,
            pl.BlockSpec((None, hp, N_PAIR, GRID_W, LANES), lambda b, c: (jb, c, 0, 0, 0)),
        ],
        out_specs=pl.BlockSpec((DEC_SEQ, LANES), lambda b, c: (b, c)),
        out_shape=jax.ShapeDtypeStruct((N_SAMPLE, NA_WIDTH), BF16),
        compiler_params=_params(("parallel", "parallel"), blocks),
        name=f"na_attention_{jb}",
    )(proj, proj, proj, cache_k_t, cache_v_t, bias_pairs)


def _running_rows(x, op, identity, reverse):
    n = x.shape[0]
    ridx = lax.broadcasted_iota(jnp.int32, x.shape, 0)
    sh = 1
    while sh < n:
        if reverse:
            x = op(x, jnp.where(ridx < n - sh, pltpu.roll(x, n - sh, 0), identity))
        else:
            x = op(x, jnp.where(ridx >= sh, pltpu.roll(x, sh, 0), identity))
        sh *= 2
    return x


def _mlstm_kernel(*refs, n_chunks, has_state, jb, fills_stack, spb, layer):
    if has_state:
        (q_ref, kt_ref, v_ref, o_ref, gt_ref, hn_ref, c0_ref, n0_ref, m0_ref,
         out_ref, cout_ref, nout_ref, mout_ref, hdir_ref, caug_ref) = refs
    else:
        (q_ref, kt_ref, v_ref, o_ref, gt_ref, hn_ref, *_unused,
         out_ref, cout_ref, nout_ref, mout_ref, hdir_ref, caug_ref) = refs
    cout_refs = [_stack_entry(cout_ref.at[s], jb, fills_stack) for s in range(spb)]
    nout_refs = [_stack_entry(nout_ref.at[s], jb, fills_stack) for s in range(spb)]
    mout_refs = [_stack_entry(mout_ref.at[s], jb, fills_stack) for s in range(spb)]
    T = ML_CHUNK
    dh = ML_HEAD_DIM
    seq_len = n_chunks * T
    b = pl.program_id(0)
    row = lax.broadcasted_iota(jnp.int32, (T, T), 0)
    col = lax.broadcasted_iota(jnp.int32, (T, T), 1)
    lane = lax.broadcasted_iota(jnp.int32, (T, LANES), 1)
    lane_row = lax.broadcasted_iota(jnp.int32, (1, LANES), 1)
    ones = jnp.ones((T, dh), F32)

    def gate_lanes(d, h):
        gi = 2 * d * ML_HEADS + h
        return gi, gi + ML_HEADS

    scans = [(s, d) for s in range(spb) for d in range(2)]
    m_init = []
    for s, d in scans:
        m_row = jnp.zeros((1, LANES), F32)
        for h in range(ML_HEADS):
            if has_state:
                n_bcast = jnp.broadcast_to(n0_ref[s, d, h:h + 1, :], (dh, dh)).T
                caug_ref[s, d, h] = jnp.concatenate([c0_ref[s, d, h], n_bcast], axis=1)
                m0 = m0_ref[b * spb + s, layer, d, h]
                m_row = jnp.where(lane_row == gate_lanes(d, h)[1], m0, m_row)
            else:
                caug_ref[s, d, h] = jnp.zeros((dh, 2 * dh), F32)
        m_init.append(m_row)

    def chunk_body(ci, m_rows):
        m_next = []
        for scan, (s, d) in enumerate(scans):
            causal = (col <= row) if d == 0 else (col >= row)
            last = T - 1 if d == 0 else 0
            c = ci if d == 0 else n_chunks - 1 - ci
            rows = pl.ds(pl.multiple_of(s * seq_len + c * T, T), T)
            m_prev = m_rows[scan]
            mine = (lane >= (2 * d + 1) * ML_HEADS) & (lane < (2 * d + 2) * ML_HEADS)
            g = gt_ref[rows, :]
            cum = _running_rows(g, jnp.add, 0.0, reverse=(d == 1))
            i_al = pltpu.roll(g, ML_HEADS, 1)
            cvec = jnp.where(mine, i_al - cum, 0.0)
            cum = jnp.where(mine, cum, 0.0)
            m_run = jnp.maximum(m_prev, _running_rows(cvec, jnp.maximum, -jnp.inf, reverse=(d == 1)))
            m_t = cum + m_run
            e_negm = jnp.exp(-m_t)
            m_run_last = m_run[last:last + 1, :]
            wi_last = jnp.exp(m_prev - m_run_last)
            cvec_t = cvec.T
            for h in range(ML_HEADS):
                _, gf = gate_lanes(d, h)
                hl = slice(h * dh, (h + 1) * dh)
                m_run_b = jnp.broadcast_to(m_run[:, gf:gf + 1], (T, T))
                e_negm_b = jnp.broadcast_to(e_negm[:, gf:gf + 1], (T, dh))
                c_row = cvec_t[gf:gf + 1, :]
                w_d = jnp.exp(jnp.where(causal, c_row - m_run_b, -jnp.inf))
                w_inter_b = jnp.exp(m_prev[:, gf:gf + 1] - m_run_b)
                qh = q_ref[rows, hl]
                kh_t = kt_ref[s * n_chunks + c, hl, :]
                v_aug = jnp.concatenate([v_ref[rows, hl], ones], axis=1)
                sc = _dot(qh, kh_t) * w_d
                c_aug = caug_ref[s, d, h]
                nd = jnp.concatenate([w_inter_b, w_inter_b], axis=1) * _dot(qh, c_aug) + _dot(sc, v_aug)
                hdir_ref[d, rows, hl] = nd[:, :dh] / jnp.maximum(jnp.abs(nd[:, dh:]), e_negm_b)
                kw_t = kh_t * jnp.exp(c_row - m_run_last[:, gf:gf + 1])
                caug_ref[s, d, h] = wi_last[:, gf:gf + 1] * c_aug + _dot(kw_t, v_aug)
            m_next.append(m_t[last:last + 1, :])
        return tuple(m_next)

    m_fin = lax.fori_loop(0, n_chunks, chunk_body, tuple(m_init), unroll=2)
    for scan, (s, d) in enumerate(scans):
        for h in range(ML_HEADS):
            c_aug = caug_ref[s, d, h]
            cout_refs[s][d, h] = c_aug[:, :dh]
            nout_refs[s][d, h:h + 1, :] = c_aug[:, dh:].T[0:1, :]
    for s in range(spb):
        m_heads = [pltpu.roll(m_fin[2 * s + d], LANES - gate_lanes(d, 0)[1], 1) for d in range(2)]
        mout_refs[s][...] = jnp.concatenate(m_heads + [jnp.zeros((COND_ROWS - 2, LANES), F32)], axis=0)

    for h in range(ML_HEADS):
        hl = slice(h * dh, (h + 1) * dh)
        h_sum = hdir_ref[0, :, hl] + hdir_ref[1, :, hl]
        gated = jax.nn.sigmoid(o_ref[:, hl]) * (_rms_scale(h_sum) * hn_ref[layer:layer + 1, hl])
        out_ref[:, hl] = gated.astype(out_ref.dtype)


def _mlstm(proj, keys_t, gates, hnorm_rows, layer, seq_len, n_seq, row0, spb, n_stack=1, jb=0,
           prev=None, state=None):
    has_state = state is not None
    assert n_seq % spb == 0 and row0 % (spb * seq_len) == 0
    r0 = row0 // (spb * seq_len)
    rows = spb * seq_len
    nb = TN_PROJ // ML_WIDTH
    n_chunks = seq_len // ML_CHUNK
    in_specs = [
        pl.BlockSpec((rows, ML_WIDTH), lambda b: (r0 + b, P_QM * nb)),
        pl.BlockSpec((spb * n_chunks, ML_WIDTH, ML_CHUNK), lambda b: (r0 + b, 0, 0)),
        pl.BlockSpec((rows, ML_WIDTH), lambda b: (r0 + b, P_VM * nb)),
        pl.BlockSpec((rows, ML_WIDTH), lambda b: (r0 + b, P_OM * nb)),
        pl.BlockSpec((rows, LANES), lambda b: (r0 + b, 0)),
        pl.BlockSpec((N_AB, ML_WIDTH), lambda b: (0, 0)),
    ]
    args = [proj, keys_t, proj, proj, gates, hnorm_rows]
    aliases = {}
    if has_state:
        st_c, st_n, st_m = state
        in_specs += [
            pl.BlockSpec((spb, None, 2, ML_HEADS, ML_HEAD_DIM, ML_HEAD_DIM), lambda b: (b, layer, 0, 0, 0, 0)),
            pl.BlockSpec((spb, None, 2, ML_HEADS, ML_HEAD_DIM), lambda b: (b, layer, 0, 0, 0)),
            pl.BlockSpec(memory_space=pltpu.SMEM),
        ]
        args += [st_c, st_n, st_m]
    elif prev is not None:
        aliases = {len(args) + k: 1 + k for k in range(3)}
        in_specs += [pl.BlockSpec(memory_space=pl.ANY)] * 3
        args += list(prev)
    fills_stack = prev is None
    c_bytes = 2 * ML_HEADS * ML_HEAD_DIM * ML_HEAD_DIM
    blocks = 4 * (5 * rows * ML_WIDTH + rows * LANES + spb * (1 + n_stack) * c_bytes)
    scratch = 4 * (2 * rows * ML_WIDTH + spb * 2 * c_bytes)
    return pl.pallas_call(
        functools.partial(_mlstm_kernel, n_chunks=n_chunks, has_state=has_state, jb=jb, fills_stack=fills_stack,
                          spb=spb, layer=layer),
        grid=(n_seq // spb,),
        in_specs=in_specs,
        out_specs=[
            pl.BlockSpec((rows, ML_WIDTH), lambda b: (b, 0)),
            _stack_spec((2, ML_HEADS, ML_HEAD_DIM, ML_HEAD_DIM), jb, fills_stack, n_stack, spb),
            _stack_spec((2, ML_HEADS, ML_HEAD_DIM), jb, fills_stack, n_stack, spb),
            _stack_spec((COND_ROWS, LANES), jb, fills_stack, n_stack, spb),
        ],
        out_shape=[
            jax.ShapeDtypeStruct((n_seq * seq_len, ML_WIDTH), BF16),
            jax.ShapeDtypeStruct((n_seq, n_stack, 2, ML_HEADS, ML_HEAD_DIM, ML_HEAD_DIM), F32),
            jax.ShapeDtypeStruct((n_seq, n_stack, 2, ML_HEADS, ML_HEAD_DIM), F32),
            jax.ShapeDtypeStruct((n_seq, n_stack, COND_ROWS, LANES), F32),
        ],
        scratch_shapes=[pltpu.VMEM((2, rows, ML_WIDTH), F32),
                        pltpu.VMEM((spb, 2, ML_HEADS, ML_HEAD_DIM, 2 * ML_HEAD_DIM), F32)],
        input_output_aliases=aliases,
        compiler_params=_params(("parallel",), blocks, scratch),
        name=f"mlstm_{'latent' if has_state else 'context'}_{jb}",
    )(*args)


def _outproj_kernel(x_ref, mod_ref, g_ref, ap_ref, mp_ref, as_ref, ms_ref, wa_ref, wm_ref, o_ref):
    i = pl.program_id(0)

    def finish(a_ref, m_ref):
        wa = wa_ref[...].astype(BF16)
        wm = wm_ref[...].astype(BF16)
        for rows, r in _sub_rows(i, TM_OUT):
            out = _dot(a_ref[rows, :], wa) + _dot(m_ref[rows, :], wm)
            o_ref[rows, :] = _gated_residual(x_ref[rows, :], out, g_ref, mod_ref, r, 1, 1.0)

    is_prompt = i < N_PROMPT // TM_OUT
    pl.when(is_prompt)(functools.partial(finish, ap_ref, mp_ref))
    pl.when(jnp.logical_not(is_prompt))(functools.partial(finish, as_ref, ms_ref))


def _outproj(y, mod, gains, a_p, m_p, a_s, m_s, ab_w_out, jb, l):
    tm = TM_OUT
    n_p = N_PROMPT // tm
    n_s = N_SAMPLE // tm
    p_spec = pl.BlockSpec((tm, NA_WIDTH), lambda i: (jnp.minimum(i, n_p - 1), 0))
    s_spec = pl.BlockSpec((tm, NA_WIDTH), lambda i: (jnp.clip(i - n_p, 0, n_s - 1), 0))
    blocks = 4 * (2 * tm * D_MODEL + 4 * tm * NA_WIDTH + D_MODEL * D_MODEL)
    return pl.pallas_call(
        _outproj_kernel,
        grid=(N_TOK // tm,),
        in_specs=[
            pl.BlockSpec((tm, D_MODEL), lambda i: (i, 0)),
            *_cond_specs(l),
            p_spec, p_spec, s_spec, s_spec,
            pl.BlockSpec((None, NA_WIDTH, D_MODEL), lambda i: (jb, 0, 0)),
            pl.BlockSpec((None, ML_WIDTH, D_MODEL), lambda i: (jb, 1, 0)),
        ],
        out_specs=pl.BlockSpec((tm, D_MODEL), lambda i: (i, 0)),
        out_shape=jax.ShapeDtypeStruct((N_TOK, D_MODEL), F32),
        compiler_params=_params(("parallel",), blocks),
        name=f"outproj_{jb}",
    )(y, mod, gains, a_p, m_p, a_s, m_s, ab_w_out, ab_w_out)


def kernel(x_prompt, x_sample, cache_k, cache_v, state_C, state_n, state_m, c, c_ctx, ada_w, ada_b, norm_g,
           ffn_w13, ffn_w2, ab_w_in, ab_w_out, na_rpb, ml_gate_b, ml_hnorm, sc_w_in, sc_conv_w, sc_conv_b,
           sc_w_out):
    assert NA_WIDTH == ML_WIDTH == TN_PROJ and AB_MAIN % N_GATES == 0
    assert math.frexp(NA_HEAD_DIM ** -0.5)[0] == 0.5
    assert LANES == 2 * NA_HEAD_DIM
    assert N_PROMPT % TM == 0 and N_SAMPLE % TM == 0 and TM % COND_UNIT == 0
    cond = jnp.concatenate([c_ctx[None, :], c, jnp.zeros((COND_ROWS - 1 - DEC_BATCH, D_MODEL), F32)], axis=0)
    mod = _modulation(cond, ada_w, ada_b)
    gains = norm_g.reshape(DEPTH, 6, D_MODEL)
    bias_pairs = _na_bias(na_rpb)
    w_in_t = jnp.swapaxes(ab_w_in, 1, 2)
    cache_k_t = jnp.swapaxes(cache_k, 3, 4)
    cache_v_t = jnp.swapaxes(cache_v, 3, 4)
    gate_b_cols = ml_gate_b.reshape(N_AB, N_GATES, 1)
    hnorm_rows = ml_hnorm
    state = (state_C, state_n, state_m)

    kv_new = None
    st_new = None
    y = None
    for l in range(DEPTH):
        if l == 0:
            y = _ffn_sublayer(x_prompt.reshape(N_PROMPT, D_MODEL), mod, gains, ffn_w13, ffn_w2, l, 0,
                              y_tail=x_sample.reshape(N_SAMPLE, D_MODEL))
        else:
            y = _ffn_sublayer(y, mod, gains, ffn_w13, ffn_w2, l, 0)
        j = l // 2
        if l % 2 == 0:
            proj, keys_t, gates = _inproj(y, mod, gains, w_in_t, gate_b_cols, j, l)
            a_p, *kv_new = _ctx_attention(proj, j, kv_new)
            m_p, *st_new = _mlstm(proj, keys_t, gates, hnorm_rows, j, SEQ, BATCH, 0,
                                  ML_SEQS_PER_STEP, n_stack=N_AB, jb=j, prev=st_new)
            a_s = _na_attention(proj, cache_k_t, cache_v_t, bias_pairs, j)
            m_s, _, _, _ = _mlstm(proj, keys_t, gates, hnorm_rows, j, DEC_SEQ, DEC_BATCH, N_PROMPT,
                                  ML_SEQS_PER_STEP, state=state)
            y = _outproj(y, mod, gains, a_p, m_p, a_s, m_s, ab_w_out, j, l)
        else:
            y = _sconv_sublayer(y, mod, gains, sc_w_in, sc_conv_w, sc_conv_b, sc_w_out, j, l)
        if l < DEPTH - 1:
            y = _ffn_sublayer(y, mod, gains, ffn_w13, ffn_w2, l, 1)

    p_tiles = N_PROMPT // TM
    y_p = _ffn_sublayer(y, mod, gains, ffn_w13, ffn_w2, DEPTH - 1, 1, 0, p_tiles)
    y_s = _ffn_sublayer(y, mod, gains, ffn_w13, ffn_w2, DEPTH - 1, 1, p_tiles, N_TOK // TM - p_tiles)
    y_p = y_p.reshape(BATCH, SEQ, D_MODEL)
    y_s = y_s.reshape(DEC_BATCH, DEC_SEQ, D_MODEL)
    new_c, new_n, m_rows = st_new
    new_m = m_rows[:, :, :2, :ML_HEADS]
    return (y_p, y_s, jnp.swapaxes(kv_new[0], 3, 4), jnp.swapaxes(kv_new[1], 3, 4), new_c, new_n, new_m)
```

```python
import functools
import math

import jax
import jax.numpy as jnp
from jax import lax
from jax.experimental import pallas as pl
from jax.experimental.pallas import tpu as pltpu

D_MODEL = 1024
BATCH = 16
SEQ = 256
DEPTH = 4
DEC_BATCH = 2
DEC_SEQ = 1024
PAST_LEN = 256
GRID_W = 64
NA_HEADS = 8
NA_HEAD_DIM = 64
NA_WIN_H = 8
NA_WIN_W = 16
ML_HEADS = 4
ML_HEAD_DIM = 128
ML_CHUNK = 128
CONV_WIDTH = 3
D_FF = 2816
N_MOD = 9
EPS = 1e-6
N_AB = (DEPTH + 1) // 2
N_C = DEPTH // 2
NA_WIDTH = NA_HEADS * NA_HEAD_DIM
ML_WIDTH = ML_HEADS * ML_HEAD_DIM
AB_MAIN = 3 * NA_WIDTH + 4 * ML_WIDTH
N_GATES = 4 * ML_HEADS
SC_WIDTH = D_MODEL
N_GAINS = 3 * 2

N_PROMPT = BATCH * SEQ
N_SAMPLE = DEC_BATCH * DEC_SEQ
N_TOK = N_PROMPT + N_SAMPLE
GRID_ROWS = DEC_SEQ // GRID_W
NA_KH = min(NA_WIN_H, GRID_ROWS)
N_DR = 2 * NA_WIN_H - 1
N_DC = 2 * NA_WIN_W - 1
N_PAIR = N_DR - 1

LANES = 128
COND_ROWS = 8
VMEM_CAP = 60 * 1024 * 1024

COND_UNIT = DEC_SEQ
TM = 2048
SUB_ROWS = 512
TM_OUT = 1024
TM_CONV = COND_UNIT
TF_CONV = 512
TF = 256
TN_PROJ = 512
TN_MOD = 1536
NA_QROWS = 8
CTX_SEQS_PER_STEP = 2
ML_SEQS_PER_STEP = 2

PROJ_BLOCKS = (0, 1, 2, 3, 5, 6)
KM_BLOCK = 4
P_QA, P_KA, P_VA, P_QM, P_VM, P_OM = range(6)

F32 = jnp.float32
BF16 = jnp.bfloat16


def _vmem_limit(block_bytes, scratch_bytes):
    assert 2 * block_bytes + scratch_bytes <= VMEM_CAP
    return VMEM_CAP


def _params(semantics, block_bytes, scratch_bytes=0):
    return pltpu.CompilerParams(dimension_semantics=semantics,
                                vmem_limit_bytes=_vmem_limit(block_bytes, scratch_bytes))


def _unit_rows(i, tm=TM):
    first_latent = N_PROMPT // COND_UNIT
    units = tm // COND_UNIT
    return [(slice(u * COND_UNIT, (u + 1) * COND_UNIT), jnp.maximum(i * units + u - (first_latent - 1), 0))
            for u in range(units)]


def _sub_rows(i, tm=TM):
    per_unit = COND_UNIT // SUB_ROWS
    return [(slice(rows.start + s * SUB_ROWS, rows.start + (s + 1) * SUB_ROWS), r)
            for rows, r in _unit_rows(i, tm) for s in range(per_unit)]


def _rms_scale(x):
    return x * lax.rsqrt(jnp.mean(x * x, axis=-1, keepdims=True) + EPS)


def _modulate_in(x, g_ref, mod_ref, r, j):
    gain = g_ref[2 * j:2 * j + 1, :] * (1.0 + mod_ref[r, 3 * j + 1:3 * j + 2, :])
    return _rms_scale(x) * gain + mod_ref[r, 3 * j:3 * j + 1, :]


def _gated_residual(x, out, g_ref, mod_ref, r, j, weight):
    gain = (weight * mod_ref[r, 3 * j + 2:3 * j + 3, :]) * g_ref[2 * j + 1:2 * j + 2, :]
    return x + _rms_scale(out) * gain


def _cond_specs(l):
    return [pl.BlockSpec((None, COND_ROWS, N_MOD, D_MODEL), lambda *_: (l, 0, 0, 0)),
            pl.BlockSpec((None, N_GAINS, D_MODEL), lambda *_: (l, 0, 0))]


def _dot(a, b):
    return jnp.dot(a.astype(BF16), b.astype(BF16), preferred_element_type=F32)


def _dot_nt(a, b):
    return lax.dot_general(a.astype(BF16), b.astype(BF16), (((1,), (1,)), ((), ())),
                           preferred_element_type=F32)


def _mod_kernel(c_ref, w_ref, b_ref, o_ref):
    c = c_ref[...]
    s = c * jax.nn.sigmoid(c)
    o_ref[...] = _dot(s, w_ref[...]) + b_ref[...]


def _modulation(cond, ada_w, ada_b):
    n = N_MOD * D_MODEL
    out = pl.pallas_call(
        _mod_kernel,
        grid=(DEPTH, n // TN_MOD),
        in_specs=[
            pl.BlockSpec((COND_ROWS, D_MODEL), lambda l, t: (0, 0)),
            pl.BlockSpec((None, D_MODEL, TN_MOD), lambda l, t: (l, 0, t)),
            pl.BlockSpec((None, 1, TN_MOD), lambda l, t: (l, 0, t)),
        ],
        out_specs=pl.BlockSpec((None, COND_ROWS, TN_MOD), lambda l, t: (l, 0, t)),
        out_shape=jax.ShapeDtypeStruct((DEPTH, COND_ROWS, n), F32),
        compiler_params=_params(("parallel", "parallel"), 4 * D_MODEL * TN_MOD),
        name="modulation",
    )(cond, ada_w, ada_b.reshape(DEPTH, 1, n))
    return out.reshape(DEPTH, COND_ROWS, N_MOD, D_MODEL)


def _ffn_kernel(*refs, j, tile0, n_head):
    i = pl.program_id(0) + tile0
    k = pl.program_id(1)
    last = pl.num_programs(1) - 1
    if n_head is None:
        x_ref, mod_ref, g_ref, wa_ref, wg_ref, w2_ref, o_ref, h_ref = refs
        sources = [(None, x_ref)]
    else:
        x_ref, x_tail_ref, mod_ref, g_ref, wa_ref, wg_ref, w2_ref, o_ref, h_ref = refs
        sources = [(pl.program_id(0) < n_head, x_ref), (pl.program_id(0) >= n_head, x_tail_ref)]

    def per_source(fn):
        for pred, ref in sources:
            if pred is None:
                fn(ref)
            else:
                pl.when(pred)(functools.partial(fn, ref))

    def hidden_step(rows, init):
        h = h_ref[rows, :]
        a = _dot(h, wa_ref[...])
        g = _dot(h, wg_ref[...])
        d = _dot((g * jax.nn.sigmoid(g)) * a, w2_ref[...])
        if init:
            o_ref[rows, :] = d
        else:
            o_ref[rows, :] += d

    def first_step(x_src):
        for rows, r in _sub_rows(i):
            h_ref[rows, :] = _modulate_in(x_src[rows, :], g_ref, mod_ref, r, j).astype(BF16)
            hidden_step(rows, init=True)

    def last_step(x_src):
        for rows, r in _sub_rows(i):
            hidden_step(rows, init=False)
            o_ref[rows, :] = _gated_residual(x_src[rows, :], o_ref[rows, :], g_ref, mod_ref, r, j, 0.5)

    pl.when(k == 0)(functools.partial(per_source, first_step))

    @pl.when((k > 0) & (k < last))
    def _():
        for rows, _ in _unit_rows(i):
            hidden_step(rows, init=False)

    pl.when(k == last)(functools.partial(per_source, last_step))


def _ffn_sublayer(y, mod, gains, ffn_w13, ffn_w2, l, s, tile0=0, n_tiles=N_TOK // TM, y_tail=None):
    j = 2 * s
    nk = D_FF // TF
    blocks = 4 * (2 * TM * D_MODEL + 3 * D_MODEL * TF)
    scratch = 2 * TM * D_MODEL
    if y_tail is None:
        n_head = None
        x_specs = [pl.BlockSpec((TM, D_MODEL), lambda i, k: (tile0 + i, 0))]
        xs = [y]
    else:
        assert tile0 == 0
        n_head = y.shape[0] // TM
        x_specs = [pl.BlockSpec((TM, D_MODEL), lambda i, k: (jnp.minimum(i, n_head - 1), 0)),
                   pl.BlockSpec((TM, D_MODEL), lambda i, k: (jnp.maximum(i - n_head, 0), 0))]
        xs = [y, y_tail]
        blocks += 4 * TM * D_MODEL
    return pl.pallas_call(
        functools.partial(_ffn_kernel, j=j, tile0=tile0, n_head=n_head),
        grid=(n_tiles, nk),
        in_specs=[
            *x_specs,
            *_cond_specs(l),
            pl.BlockSpec((None, None, D_MODEL, TF), lambda i, k: (l, s, 0, k)),
            pl.BlockSpec((None, None, D_MODEL, TF), lambda i, k: (l, s, 0, k + nk)),
            pl.BlockSpec((None, None, TF, D_MODEL), lambda i, k: (l, s, k, 0)),
        ],
        out_specs=pl.BlockSpec((TM, D_MODEL), lambda i, k: (i, 0)),
        out_shape=jax.ShapeDtypeStruct((n_tiles * TM, D_MODEL), F32),
        scratch_shapes=[pltpu.VMEM((TM, D_MODEL), BF16)],
        compiler_params=_params(("parallel", "arbitrary"), blocks, scratch),
        name=f"ffn_l{l}_s{s}_t{tile0}",
    )(*xs, mod, gains, ffn_w13, ffn_w13, ffn_w2)


def _sconv_kernel(x_ref, mod_ref, g_ref, wb_ref, wc_ref, wx_ref, cw_ref, cb_ref, wo_ref, o_ref, h_ref, *, jc):
    i = pl.program_id(0)
    k = pl.program_id(1)
    ((_, r),) = _unit_rows(i, TM_CONV)
    seq = jnp.where(i < N_PROMPT // TM_CONV, SEQ, DEC_SEQ)
    blk = SEQ
    n_blk = TM_CONV // blk
    row = lax.broadcasted_iota(jnp.int32, (blk, TF_CONV), 0)
    zero_row = jnp.zeros((1, TF_CONV), F32)

    def step(first, last):
        wb = wb_ref[...].astype(BF16)
        wc = wc_ref[...].astype(BF16)
        wx = wx_ref[...].astype(BF16)
        wo = wo_ref[...].astype(BF16)
        gbs, us = [], []

        def project(bi):
            rows = slice(bi * blk, (bi + 1) * blk)
            if first:
                h_ref[rows, :] = _modulate_in(x_ref[rows, :], g_ref, mod_ref, r, 1).astype(BF16)
            h = h_ref[rows, :]
            gbs.append(_dot(h, wb))
            us.append(_dot(h, wc) * _dot(h, wx))

        project(0)
        for bi in range(n_blk):
            if bi + 1 < n_blk:
                project(bi + 1)
            rows = slice(bi * blk, (bi + 1) * blk)
            ub = us[bi]
            t = (row + bi * blk) & (seq - 1)
            above = us[bi - 1][blk - 1:blk, :] if bi > 0 else zero_row
            below = us[bi + 1][0:1, :] if bi < n_blk - 1 else zero_row
            u_prev = jnp.where(t == 0, 0.0, jnp.where(row == 0, above, pltpu.roll(ub, 1, 0)))
            u_next = jnp.where(t == seq - 1, 0.0, jnp.where(row == blk - 1, below, pltpu.roll(ub, blk - 1, 0)))
            y = cb_ref[jc:jc + 1, :] + u_prev * cw_ref[0:1, :] + ub * cw_ref[1:2, :] + u_next * cw_ref[2:3, :]
            d = _dot(gbs[bi] * y, wo)
            acc = d if first else o_ref[rows, :] + d
            o_ref[rows, :] = _gated_residual(x_ref[rows, :], acc, g_ref, mod_ref, r, 1, 1.0) if last else acc

    n_steps = SC_WIDTH // TF_CONV
    for kk in range(n_steps):
        pl.when(k == kk)(functools.partial(step, kk == 0, kk == n_steps - 1))


def _sconv_sublayer(y, mod, gains, sc_w_in, sc_conv_w, sc_conv_b, sc_w_out, jc, l):
    tm, tf = TM_CONV, TF_CONV
    nk = SC_WIDTH // tf
    blocks = 4 * (2 * tm * D_MODEL + 4 * D_MODEL * tf)
    scratch = 2 * tm * D_MODEL
    return pl.pallas_call(
        functools.partial(_sconv_kernel, jc=jc),
        grid=(N_TOK // tm, nk),
        in_specs=[
            pl.BlockSpec((tm, D_MODEL), lambda i, k: (i, 0)),
            *_cond_specs(l),
            pl.BlockSpec((None, D_MODEL, tf), lambda i, k: (jc, 0, k)),
            pl.BlockSpec((None, D_MODEL, tf), lambda i, k: (jc, 0, k + nk)),
            pl.BlockSpec((None, D_MODEL, tf), lambda i, k: (jc, 0, k + 2 * nk)),
            pl.BlockSpec((None, CONV_WIDTH, tf), lambda i, k: (jc, 0, k)),
            pl.BlockSpec((N_C, tf), lambda i, k: (0, k)),
            pl.BlockSpec((None, tf, D_MODEL), lambda i, k: (jc, k, 0)),
        ],
        out_specs=pl.BlockSpec((tm, D_MODEL), lambda i, k: (i, 0)),
        out_shape=jax.ShapeDtypeStruct((N_TOK, D_MODEL), F32),
        scratch_shapes=[pltpu.VMEM((tm, D_MODEL), BF16)],
        compiler_params=_params(("parallel", "arbitrary"), blocks, scratch),
        name=f"sconv_{jc}",
    )(y, mod, gains, sc_w_in, sc_w_in, sc_w_in, sc_conv_w, sc_conv_b, sc_w_out)


def _inproj_kernel(x_ref, mod_ref, g_ref, w_ref, wk_ref, wg_ref, gb_ref, o_ref, okt_ref, og_ref, h_ref):
    i = pl.program_id(0)
    n = pl.program_id(1)
    n_main = len(PROJ_BLOCKS)

    @pl.when(n == 0)
    def _():
        for rows, r in _sub_rows(i):
            h_ref[rows, :] = _modulate_in(x_ref[rows, :], g_ref, mod_ref, r, 1).astype(BF16)
            o_ref[rows, :] = _dot_nt(h_ref[rows, :], w_ref[...])

    @pl.when((n > 0) & (n < n_main))
    def _():
        o_ref[...] = _dot_nt(h_ref[...], w_ref[...])

    @pl.when(n == n_main)
    def _():
        w_kg = jnp.concatenate([wk_ref[...], wg_ref[...]], axis=0)
        ktg = _dot_nt(w_kg, h_ref[...])
        k_scale = ML_HEAD_DIM ** -0.5
        for c in range(TM // ML_CHUNK):
            okt_ref[c] = ktg[:ML_WIDTH, c * ML_CHUNK:(c + 1) * ML_CHUNK] * k_scale
        g = ktg[ML_WIDTH:, :] + gb_ref[...]
        gate = lax.broadcasted_iota(jnp.int32, g.shape, 0)
        is_forget = ((gate >= ML_HEADS) & (gate < 2 * ML_HEADS)) | (gate >= 3 * ML_HEADS)
        log_sig = jnp.minimum(g, 0.0) - jnp.log1p(jnp.exp(-jnp.abs(g)))
        g = jnp.where(is_forget, log_sig, g)
        gates_t = jnp.concatenate([g, jnp.zeros((LANES - N_GATES, TM), F32)], axis=0)
        og_ref[...] = gates_t.T


def _inproj(y, mod, gains, w_in_t, gate_b_cols, jb, l):
    n_main = len(PROJ_BLOCKS)
    assert PROJ_BLOCKS == tuple(b for b in range(n_main + 1) if b != KM_BLOCK)

    def w_block(i, n):
        m = jnp.minimum(n, n_main - 1)
        return (jb, m + (m >= KM_BLOCK).astype(jnp.int32), 0)

    cpt = TM // ML_CHUNK
    blocks = 4 * (TM * D_MODEL + 2 * TN_PROJ * D_MODEL + N_GATES * D_MODEL + TM * TN_PROJ
                  + ML_WIDTH * TM + TM * LANES)
    return pl.pallas_call(
        _inproj_kernel,
        grid=(N_TOK // TM, n_main + 1),
        in_specs=[
            pl.BlockSpec((TM, D_MODEL), lambda i, n: (i, 0)),
            *_cond_specs(l),
            pl.BlockSpec((None, TN_PROJ, D_MODEL), w_block),
            pl.BlockSpec((None, ML_WIDTH, D_MODEL), lambda i, n: (jb, KM_BLOCK, 0)),
            pl.BlockSpec((None, N_GATES, D_MODEL), lambda i, n: (jb, AB_MAIN // N_GATES, 0)),
            pl.BlockSpec((None, N_GATES, 1), lambda i, n: (jb, 0, 0)),
        ],
        out_specs=[
            pl.BlockSpec((TM, TN_PROJ), lambda i, n: (i, jnp.minimum(n, n_main - 1))),
            pl.BlockSpec((cpt, ML_WIDTH, ML_CHUNK), lambda i, n: (i, 0, 0)),
            pl.BlockSpec((TM, LANES), lambda i, n: (i, 0)),
        ],
        out_shape=[jax.ShapeDtypeStruct((N_TOK, n_main * TN_PROJ), F32),
                   jax.ShapeDtypeStruct((N_TOK // ML_CHUNK, ML_WIDTH, ML_CHUNK), F32),
                   jax.ShapeDtypeStruct((N_TOK, LANES), F32)],
        scratch_shapes=[pltpu.VMEM((TM, D_MODEL), BF16)],
        compiler_params=_params(("parallel", "arbitrary"), blocks, 2 * TM * D_MODEL),
        name=f"inproj_{jb}",
    )(y, mod, gains, w_in_t, w_in_t, w_in_t, gate_b_cols)


def _stack_entry(ref, jb, fills_stack):
    if not fills_stack:
        return ref
    for j in range(ref.shape[0]):
        if j != jb:
            ref[j] = jnp.zeros(ref.shape[1:], ref.dtype)
    return ref.at[jb]


def _stack_spec(entry_shape, jb, fills_stack, n_stack=N_AB, lead=None):
    zeros = (0,) * len(entry_shape)
    if fills_stack:
        return pl.BlockSpec((lead, n_stack) + entry_shape, lambda b: (b, 0) + zeros)
    return pl.BlockSpec((lead, None) + entry_shape, lambda b: (b, jb) + zeros)


def _ctx_attn_kernel(q_ref, k_ref, v_ref, *rest, jb, fills_stack, spb):
    o_ref = rest[-3]
    scale = NA_HEAD_DIM ** -0.5
    lane = lax.broadcasted_iota(jnp.int32, (SEQ, LANES), 1)
    ones = jnp.ones((SEQ, LANES), BF16)
    for s in range(spb):
        rows = slice(s * SEQ, (s + 1) * SEQ)
        kc_ref = _stack_entry(rest[-2].at[s], jb, fills_stack)
        vc_ref = _stack_entry(rest[-1].at[s], jb, fills_stack)
        k_t = k_ref[rows, :].T
        v_t = v_ref[rows, :].T
        for h in range(NA_HEADS):
            sl = slice(h * NA_HEAD_DIM, (h + 1) * NA_HEAD_DIM)
            kc_ref[h] = k_t[sl, :]
            vc_ref[h] = v_t[sl, :]
        for hp in range(NA_WIDTH // LANES):
            pair = slice(hp * LANES, (hp + 1) * LANES)
            q_pair = q_ref[rows, pair] * scale
            v_aug = jnp.concatenate([v_ref[rows, pair].astype(BF16), ones], axis=1)
            outs = []
            for hh in range(LANES // NA_HEAD_DIM):
                own = (lane >= hh * NA_HEAD_DIM) & (lane < (hh + 1) * NA_HEAD_DIM)
                logits = _dot(jnp.where(own, q_pair, 0.0), k_t[pair, :])
                r = _dot(jnp.exp(logits - jnp.max(logits, axis=-1, keepdims=True)), v_aug)
                outs.append(r[:, :LANES] / r[:, LANES:])
            o_ref[rows, pair] = jnp.where(lane < NA_HEAD_DIM, outs[0], outs[1]).astype(o_ref.dtype)


def _ctx_attention(proj, jb, prev):
    spb = CTX_SEQS_PER_STEP
    rows = spb * SEQ
    cache = jax.ShapeDtypeStruct((BATCH, N_AB, NA_HEADS, NA_HEAD_DIM, SEQ), F32)
    cache_spec = _stack_spec((NA_HEADS, NA_HEAD_DIM, SEQ), jb, prev is None, lead=spb)
    in_specs = [pl.BlockSpec((rows, NA_WIDTH), lambda b, c=c: (b, c)) for c in (P_QA, P_KA, P_VA)]
    args = [proj, proj, proj]
    aliases = {}
    if prev is not None:
        in_specs += [pl.BlockSpec(memory_space=pl.ANY)] * 2
        args += list(prev)
        aliases = {3: 1, 4: 2}
    blocks = 4 * (4 * rows * NA_WIDTH + 2 * N_AB * NA_WIDTH * rows)
    return pl.pallas_call(
        functools.partial(_ctx_attn_kernel, jb=jb, fills_stack=prev is None, spb=spb),
        grid=(BATCH // spb,),
        in_specs=in_specs,
        out_specs=[pl.BlockSpec((rows, NA_WIDTH), lambda b: (b, 0)), cache_spec, cache_spec],
        out_shape=[jax.ShapeDtypeStruct((N_PROMPT, NA_WIDTH), BF16), cache, cache],
        input_output_aliases=aliases,
        compiler_params=_params(("parallel",), blocks),
        name=f"ctx_attention_{jb}",
    )(*args)


def _na_bias_kernel(rpb_ref, p_ref):
    layer = pl.program_id(0)
    head = pl.program_id(1)
    shape = (GRID_W, LANES)
    lane = lax.broadcasted_iota(jnp.int32, shape, 1)
    qc = lax.broadcasted_iota(jnp.int32, shape, 0)
    kc = lane & (GRID_W - 1)
    upper = lane >= GRID_W
    cs = jnp.clip(qc - NA_WIN_W // 2, 0, GRID_W - NA_WIN_W)
    col_in = (kc >= cs) & (kc < cs + NA_WIN_W)
    j = lax.broadcasted_iota(jnp.int32, (1, LANES), 1)
    dcol = jnp.clip(j - (GRID_W - 1), -(NA_WIN_W - 1), NA_WIN_W - 1) + (NA_WIN_W - 1)

    def toeplitz(dr, half):
        b = jnp.zeros((1, LANES), F32)
        for d in range(N_DC):
            b = jnp.where(dcol == d, rpb_ref[layer, head, dr, d], b)
        shift = (LANES - (GRID_W - 1) + half * GRID_W) % LANES
        return pltpu.roll(jnp.broadcast_to(b, shape), shift, 1, stride=1, stride_axis=0)

    blocks = [[toeplitz(dr, half) for half in range(2)] for dr in range(N_DR)]
    for pair in range(N_PAIR):
        both = jnp.where(upper, blocks[pair + 1][1], blocks[pair][0])
        p_ref[pair] = jnp.where(col_in, both, -jnp.inf)


def _na_bias(na_rpb):
    return pl.pallas_call(
        _na_bias_kernel,
        grid=(N_AB, NA_HEADS),
        in_specs=[pl.BlockSpec(memory_space=pltpu.SMEM)],
        out_specs=pl.BlockSpec((None, None, N_PAIR, GRID_W, LANES), lambda j, h: (j, h, 0, 0, 0)),
        out_shape=jax.ShapeDtypeStruct((N_AB, NA_HEADS, N_PAIR, GRID_W, LANES), F32),
        compiler_params=_params(("parallel", "parallel"), 4 * N_PAIR * GRID_W * LANES),
        name="na_bias",
    )(na_rpb)


def _na_window_start(r):
    return min(max(r - NA_KH // 2, 0), GRID_ROWS - NA_KH)


def _na_bias_block(p_ref, hh, qr, kr):
    rs = _na_window_start(qr)
    lo_ok = rs <= kr < rs + NA_KH
    hi_ok = rs <= kr + 1 < rs + NA_KH
    if not (lo_ok or hi_ok):
        return jnp.full((GRID_W, LANES), -jnp.inf, F32)
    dr = kr - qr + (NA_WIN_H - 1)
    assert 0 <= dr < N_PAIR
    blk = p_ref[hh, dr]
    if lo_ok and hi_ok:
        return blk
    lane = lax.broadcasted_iota(jnp.int32, (GRID_W, LANES), 1)
    keep = (lane < GRID_W) if lo_ok else (lane >= GRID_W)
    return jnp.where(keep, blk, -jnp.inf)


def _na_kernel(q_ref, k_ref, v_ref, kc_ref, vc_ref, p_ref, o_ref):
    scale = NA_HEAD_DIM ** -0.5
    n_q = NA_QROWS * GRID_W
    lane = lax.broadcasted_iota(jnp.int32, (n_q, LANES), 1)
    k_ctx_t = kc_ref[...].reshape(LANES, PAST_LEN)
    v_ctx_t = vc_ref[...].reshape(LANES, PAST_LEN)
    v_aug_ctx_t = jnp.concatenate([v_ctx_t.astype(BF16), jnp.ones((LANES, PAST_LEN), BF16)], axis=0)
    for q0 in range(0, GRID_ROWS, NA_QROWS):
        k_lo = _na_window_start(q0) // 2 * 2
        k_hi = -(-(_na_window_start(q0 + NA_QROWS - 1) + NA_KH) // 2) * 2
        q_rows = slice(q0 * GRID_W, (q0 + NA_QROWS) * GRID_W)
        k_rows = slice(k_lo * GRID_W, k_hi * GRID_W)
        q_pair = q_ref[q_rows, :] * scale
        n_k = (k_hi - k_lo) * GRID_W
        v_aug_loc = jnp.concatenate([v_ref[k_rows, :].astype(BF16), jnp.ones((n_k, LANES), BF16)], axis=1)
        outs = []
        for hh in range(LANES // NA_HEAD_DIM):
            bias = jnp.concatenate(
                [jnp.concatenate([_na_bias_block(p_ref, hh, qr, kr) for kr in range(k_lo, k_hi, 2)], axis=1)
                 for qr in range(q0, q0 + NA_QROWS)], axis=0)
            own = (lane >= hh * NA_HEAD_DIM) & (lane < (hh + 1) * NA_HEAD_DIM)
            q = jnp.where(own, q_pair, 0.0)
            s_loc = _dot_nt(q, k_ref[k_rows, :]) + bias
            s_ctx = _dot(q, k_ctx_t)
            m = jnp.maximum(jnp.max(s_loc, axis=-1, keepdims=True), jnp.max(s_ctx, axis=-1, keepdims=True))
            r = _dot(jnp.exp(s_loc - m), v_aug_loc) + _dot_nt(jnp.exp(s_ctx - m), v_aug_ctx_t)
            outs.append(r[:, :LANES] / r[:, LANES:])
        o_ref[q_rows, :] = jnp.where(lane < NA_HEAD_DIM, outs[0], outs[1]).astype(o_ref.dtype)


def _na_attention(proj, cache_k_t, cache_v_t, bias_pairs, jb):
    hp = LANES // NA_HEAD_DIM
    s0 = N_PROMPT // DEC_SEQ
    nb = NA_WIDTH // LANES
    ctx_spec = pl.BlockSpec((None, None, hp, NA_HEAD_DIM, PAST_LEN), lambda b, c: (b, jb, c, 0, 0))
    blocks = 4 * (4 * DEC_SEQ * LANES + 2 * hp * NA_HEAD_DIM * PAST_LEN + hp * N_PAIR * GRID_W * LANES)
    return pl.pallas_call(
        _na_kernel,
        grid=(DEC_BATCH, nb),
        in_specs=[
            pl.BlockSpec((DEC_SEQ, LANES), lambda b, c: (s0 + b, P_QA * nb + c)),
            pl.BlockSpec((DEC_SEQ, LANES), lambda b, c: (s0 + b, P_KA * nb + c)),
            pl.BlockSpec((DEC_SEQ, LANES), lambda b, c: (s0 + b, P_VA * nb + c)),
            ctx_spec, ctx_spec,
            pl.BlockSpec((None, hp, N_PAIR, GRID_W, LANES), lambda b, c: (jb, c, 0, 0, 0)),
        ],
        out_specs=pl.BlockSpec((DEC_SEQ, LANES), lambda b, c: (b, c)),
        out_shape=jax.ShapeDtypeStruct((N_SAMPLE, NA_WIDTH), BF16),
        compiler_params=_params(("parallel", "parallel"), blocks),
        name=f"na_attention_{jb}",
    )(proj, proj, proj, cache_k_t, cache_v_t, bias_pairs)


def _running_rows(x, op, identity, reverse):
    n = x.shape[0]
    ridx = lax.broadcasted_iota(jnp.int32, x.shape, 0)
    sh = 1
    while sh < n:
        if reverse:
            x = op(x, jnp.where(ridx < n - sh, pltpu.roll(x, n - sh, 0), identity))
        else:
            x = op(x, jnp.where(ridx >= sh, pltpu.roll(x, sh, 0), identity))
        sh *= 2
    return x


def _mlstm_kernel(*refs, n_chunks, has_state, jb, fills_stack, spb, layer):
    if has_state:
        (q_ref, kt_ref, v_ref, o_ref, gt_ref, hn_ref, c0_ref, n0_ref, m0_ref,
         out_ref, cout_ref, nout_ref, mout_ref, hdir_ref, caug_ref) = refs
    else:
        (q_ref, kt_ref, v_ref, o_ref, gt_ref, hn_ref, *_unused,
         out_ref, cout_ref, nout_ref, mout_ref, hdir_ref, caug_ref) = refs
    cout_refs = [_stack_entry(cout_ref.at[s], jb, fills_stack) for s in range(spb)]
    nout_refs = [_stack_entry(nout_ref.at[s], jb, fills_stack) for s in range(spb)]
    mout_refs = [_stack_entry(mout_ref.at[s], jb, fills_stack) for s in range(spb)]
    T = ML_CHUNK
    dh = ML_HEAD_DIM
    seq_len = n_chunks * T
    b = pl.program_id(0)
    row = lax.broadcasted_iota(jnp.int32, (T, T), 0)
    col = lax.broadcasted_iota(jnp.int32, (T, T), 1)
    lane = lax.broadcasted_iota(jnp.int32, (T, LANES), 1)
    lane_row = lax.broadcasted_iota(jnp.int32, (1, LANES), 1)
    ones = jnp.ones((T, dh), F32)

    def gate_lanes(d, h):
        gi = 2 * d * ML_HEADS + h
        return gi, gi + ML_HEADS

    scans = [(s, d) for s in range(spb) for d in range(2)]
    m_init = []
    for s, d in scans:
        m_row = jnp.zeros((1, LANES), F32)
        for h in range(ML_HEADS):
            if has_state:
                n_bcast = jnp.broadcast_to(n0_ref[s, d, h:h + 1, :], (dh, dh)).T
                caug_ref[s, d, h] = jnp.concatenate([c0_ref[s, d, h], n_bcast], axis=1)
                m0 = m0_ref[b * spb + s, layer, d, h]
                m_row = jnp.where(lane_row == gate_lanes(d, h)[1], m0, m_row)
            else:
                caug_ref[s, d, h] = jnp.zeros((dh, 2 * dh), F32)
        m_init.append(m_row)

    def chunk_body(ci, m_rows):
        m_next = []
        for scan, (s, d) in enumerate(scans):
            causal = (col <= row) if d == 0 else (col >= row)
            last = T - 1 if d == 0 else 0
            c = ci if d == 0 else n_chunks - 1 - ci
            rows = pl.ds(pl.multiple_of(s * seq_len + c * T, T), T)
            m_prev = m_rows[scan]
            mine = (lane >= (2 * d + 1) * ML_HEADS) & (lane < (2 * d + 2) * ML_HEADS)
            g = gt_ref[rows, :]
            cum = _running_rows(g, jnp.add, 0.0, reverse=(d == 1))
            i_al = pltpu.roll(g, ML_HEADS, 1)
            cvec = jnp.where(mine, i_al - cum, 0.0)
            cum = jnp.where(mine, cum, 0.0)
            m_run = jnp.maximum(m_prev, _running_rows(cvec, jnp.maximum, -jnp.inf, reverse=(d == 1)))
            m_t = cum + m_run
            e_negm = jnp.exp(-m_t)
            m_run_last = m_run[last:last + 1, :]
            wi_last = jnp.exp(m_prev - m_run_last)
            cvec_t = cvec.T
            for h in range(ML_HEADS):
                _, gf = gate_lanes(d, h)
                hl = slice(h * dh, (h + 1) * dh)
                m_run_b = jnp.broadcast_to(m_run[:, gf:gf + 1], (T, T))
                e_negm_b = jnp.broadcast_to(e_negm[:, gf:gf + 1], (T, dh))
                c_row = cvec_t[gf:gf + 1, :]
                w_d = jnp.exp(jnp.where(causal, c_row - m_run_b, -jnp.inf))
                w_inter_b = jnp.exp(m_prev[:, gf:gf + 1] - m_run_b)
                qh = q_ref[rows, hl]
                kh_t = kt_ref[s * n_chunks + c, hl, :]
                v_aug = jnp.concatenate([v_ref[rows, hl], ones], axis=1)
                sc = _dot(qh, kh_t) * w_d
                c_aug = caug_ref[s, d, h]
                nd = jnp.concatenate([w_inter_b, w_inter_b], axis=1) * _dot(qh, c_aug) + _dot(sc, v_aug)
                hdir_ref[d, rows, hl] = nd[:, :dh] / jnp.maximum(jnp.abs(nd[:, dh:]), e_negm_b)
                kw_t = kh_t * jnp.exp(c_row - m_run_last[:, gf:gf + 1])
                caug_ref[s, d, h] = wi_last[:, gf:gf + 1] * c_aug + _dot(kw_t, v_aug)
            m_next.append(m_t[last:last + 1, :])
        return tuple(m_next)

    m_fin = lax.fori_loop(0, n_chunks, chunk_body, tuple(m_init), unroll=2)
    for scan, (s, d) in enumerate(scans):
        for h in range(ML_HEADS):
            c_aug = caug_ref[s, d, h]
            cout_refs[s][d, h] = c_aug[:, :dh]
            nout_refs[s][d, h:h + 1, :] = c_aug[:, dh:].T[0:1, :]
    for s in range(spb):
        m_heads = [pltpu.roll(m_fin[2 * s + d], LANES - gate_lanes(d, 0)[1], 1) for d in range(2)]
        mout_refs[s][...] = jnp.concatenate(m_heads + [jnp.zeros((COND_ROWS - 2, LANES), F32)], axis=0)

    for h in range(ML_HEADS):
        hl = slice(h * dh, (h + 1) * dh)
        h_sum = hdir_ref[0, :, hl] + hdir_ref[1, :, hl]
        gated = jax.nn.sigmoid(o_ref[:, hl]) * (_rms_scale(h_sum) * hn_ref[layer:layer + 1, hl])
        out_ref[:, hl] = gated.astype(out_ref.dtype)


def _mlstm(proj, keys_t, gates, hnorm_rows, layer, seq_len, n_seq, row0, spb, n_stack=1, jb=0,
           prev=None, state=None):
    has_state = state is not None
    assert n_seq % spb == 0 and row0 % (spb * seq_len) == 0
    r0 = row0 // (spb * seq_len)
    rows = spb * seq_len
    nb = TN_PROJ // ML_WIDTH
    n_chunks = seq_len // ML_CHUNK
    in_specs = [
        pl.BlockSpec((rows, ML_WIDTH), lambda b: (r0 + b, P_QM * nb)),
        pl.BlockSpec((spb * n_chunks, ML_WIDTH, ML_CHUNK), lambda b: (r0 + b, 0, 0)),
        pl.BlockSpec((rows, ML_WIDTH), lambda b: (r0 + b, P_VM * nb)),
        pl.BlockSpec((rows, ML_WIDTH), lambda b: (r0 + b, P_OM * nb)),
        pl.BlockSpec((rows, LANES), lambda b: (r0 + b, 0)),
        pl.BlockSpec((N_AB, ML_WIDTH), lambda b: (0, 0)),
    ]
    args = [proj, keys_t, proj, proj, gates, hnorm_rows]
    aliases = {}
    if has_state:
        st_c, st_n, st_m = state
        in_specs += [
            pl.BlockSpec((spb, None, 2, ML_HEADS, ML_HEAD_DIM, ML_HEAD_DIM), lambda b: (b, layer, 0, 0, 0, 0)),
            pl.BlockSpec((spb, None, 2, ML_HEADS, ML_HEAD_DIM), lambda b: (b, layer, 0, 0, 0)),
            pl.BlockSpec(memory_space=pltpu.SMEM),
        ]
        args += [st_c, st_n, st_m]
    elif prev is not None:
        aliases = {len(args) + k: 1 + k for k in range(3)}
        in_specs += [pl.BlockSpec(memory_space=pl.ANY)] * 3
        args += list(prev)
    fills_stack = prev is None
    c_bytes = 2 * ML_HEADS * ML_HEAD_DIM * ML_HEAD_DIM
    blocks = 4 * (5 * rows * ML_WIDTH + rows * LANES + spb * (1 + n_stack) * c_bytes)
    scratch = 4 * (2 * rows * ML_WIDTH + spb * 2 * c_bytes)
    return pl.pallas_call(
        functools.partial(_mlstm_kernel, n_chunks=n_chunks, has_state=has_state, jb=jb, fills_stack=fills_stack,
                          spb=spb, layer=layer),
        grid=(n_seq // spb,),
        in_specs=in_specs,
        out_specs=[
            pl.BlockSpec((rows, ML_WIDTH), lambda b: (b, 0)),
            _stack_spec((2, ML_HEADS, ML_HEAD_DIM, ML_HEAD_DIM), jb, fills_stack, n_stack, spb),
            _stack_spec((2, ML_HEADS, ML_HEAD_DIM), jb, fills_stack, n_stack, spb),
            _stack_spec((COND_ROWS, LANES), jb, fills_stack, n_stack, spb),
        ],
        out_shape=[
            jax.ShapeDtypeStruct((n_seq * seq_len, ML_WIDTH), BF16),
            jax.ShapeDtypeStruct((n_seq, n_stack, 2, ML_HEADS, ML_HEAD_DIM, ML_HEAD_DIM), F32),
            jax.ShapeDtypeStruct((n_seq, n_stack, 2, ML_HEADS, ML_HEAD_DIM), F32),
            jax.ShapeDtypeStruct((n_seq, n_stack, COND_ROWS, LANES), F32),
        ],
        scratch_shapes=[pltpu.VMEM((2, rows, ML_WIDTH), F32),
                        pltpu.VMEM((spb, 2, ML_HEADS, ML_HEAD_DIM, 2 * ML_HEAD_DIM), F32)],
        input_output_aliases=aliases,
        compiler_params=_params(("parallel",), blocks, scratch),
        name=f"mlstm_{'latent' if has_state else 'context'}_{jb}",
    )(*args)


def _outproj_kernel(x_ref, mod_ref, g_ref, ap_ref, mp_ref, as_ref, ms_ref, wa_ref, wm_ref, o_ref):
    i = pl.program_id(0)

    def finish(a_ref, m_ref):
        wa = wa_ref[...].astype(BF16)
        wm = wm_ref[...].astype(BF16)
        for rows, r in _sub_rows(i, TM_OUT):
            out = _dot(a_ref[rows, :], wa) + _dot(m_ref[rows, :], wm)
            o_ref[rows, :] = _gated_residual(x_ref[rows, :], out, g_ref, mod_ref, r, 1, 1.0)

    is_prompt = i < N_PROMPT // TM_OUT
    pl.when(is_prompt)(functools.partial(finish, ap_ref, mp_ref))
    pl.when(jnp.logical_not(is_prompt))(functools.partial(finish, as_ref, ms_ref))


def _outproj(y, mod, gains, a_p, m_p, a_s, m_s, ab_w_out, jb, l):
    tm = TM_OUT
    n_p = N_PROMPT // tm
    n_s = N_SAMPLE // tm
    p_spec = pl.BlockSpec((tm, NA_WIDTH), lambda i: (jnp.minimum(i, n_p - 1), 0))
    s_spec = pl.BlockSpec((tm, NA_WIDTH), lambda i: (jnp.clip(i - n_p, 0, n_s - 1), 0))
    blocks = 4 * (2 * tm * D_MODEL + 4 * tm * NA_WIDTH + D_MODEL * D_MODEL)
    return pl.pallas_call(
        _outproj_kernel,
        grid=(N_TOK // tm,),
        in_specs=[
            pl.BlockSpec((tm, D_MODEL), lambda i: (i, 0)),
            *_cond_specs(l),
            p_spec, p_spec, s_spec, s_spec,
            pl.BlockSpec((None, NA_WIDTH, D_MODEL), lambda i: (jb, 0, 0)),
            pl.BlockSpec((None, ML_WIDTH, D_MODEL), lambda i: (jb, 1, 0)),
        ],
        out_specs=pl.BlockSpec((tm, D_MODEL), lambda i: (i, 0)),
        out_shape=jax.ShapeDtypeStruct((N_TOK, D_MODEL), F32),
        compiler_params=_params(("parallel",), blocks),
        name=f"outproj_{jb}",
    )(y, mod, gains, a_p, m_p, a_s, m_s, ab_w_out, ab_w_out)


def kernel(x_prompt, x_sample, cache_k, cache_v, state_C, state_n, state_m, c, c_ctx, ada_w, ada_b, norm_g,
           ffn_w13, ffn_w2, ab_w_in, ab_w_out, na_rpb, ml_gate_b, ml_hnorm, sc_w_in, sc_conv_w, sc_conv_b,
           sc_w_out):
    assert NA_WIDTH == ML_WIDTH == TN_PROJ and AB_MAIN % N_GATES == 0
    assert math.frexp(NA_HEAD_DIM ** -0.5)[0] == 0.5
    assert LANES == 2 * NA_HEAD_DIM
    assert N_PROMPT % TM == 0 and N_SAMPLE % TM == 0 and TM % COND_UNIT == 0
    cond = jnp.concatenate([c_ctx[None, :], c, jnp.zeros((COND_ROWS - 1 - DEC_BATCH, D_MODEL), F32)], axis=0)
    mod = _modulation(cond, ada_w, ada_b)
    gains = norm_g.reshape(DEPTH, N_GAINS, D_MODEL)
    bias_pairs = _na_bias(na_rpb)
    w_in_t = jnp.swapaxes(ab_w_in, 1, 2)
    cache_k_t = jnp.swapaxes(cache_k, 3, 4)
    cache_v_t = jnp.swapaxes(cache_v, 3, 4)
    gate_b_cols = ml_gate_b.reshape(N_AB, N_GATES, 1)
    hnorm_rows = ml_hnorm
    state = (state_C, state_n, state_m)

    kv_new = None
    st_new = None
    y = None
    for l in range(DEPTH):
        if l == 0:
            y = _ffn_sublayer(x_prompt.reshape(N_PROMPT, D_MODEL), mod, gains, ffn_w13, ffn_w2, l, 0,
                              y_tail=x_sample.reshape(N_SAMPLE, D_MODEL))
        else:
            y = _ffn_sublayer(y, mod, gains, ffn_w13, ffn_w2, l, 0)
        j = l // 2
        if l % 2 == 0:
            proj, keys_t, gates = _inproj(y, mod, gains, w_in_t, gate_b_cols, j, l)
            a_p, *kv_new = _ctx_attention(proj, j, kv_new)
            m_p, *st_new = _mlstm(proj, keys_t, gates, hnorm_rows, j, SEQ, BATCH, 0,
                                  ML_SEQS_PER_STEP, n_stack=N_AB, jb=j, prev=st_new)
            a_s = _na_attention(proj, cache_k_t, cache_v_t, bias_pairs, j)
            m_s, _, _, _ = _mlstm(proj, keys_t, gates, hnorm_rows, j, DEC_SEQ, DEC_BATCH, N_PROMPT,
                                  ML_SEQS_PER_STEP, state=state)
            y = _outproj(y, mod, gains, a_p, m_p, a_s, m_s, ab_w_out, j, l)
        else:
            y = _sconv_sublayer(y, mod, gains, sc_w_in, sc_conv_w, sc_conv_b, sc_w_out, j, l)
        if l < DEPTH - 1:
            y = _ffn_sublayer(y, mod, gains, ffn_w13, ffn_w2, l, 1)

    p_tiles = N_PROMPT // TM
    y_p = _ffn_sublayer(y, mod, gains, ffn_w13, ffn_w2, DEPTH - 1, 1, 0, p_tiles)
    y_s = _ffn_sublayer(y, mod, gains, ffn_w13, ffn_w2, DEPTH - 1, 1, p_tiles, N_TOK // TM - p_tiles)
    y_p = y_p.reshape(BATCH, SEQ, D_MODEL)
    y_s = y_s.reshape(DEC_BATCH, DEC_SEQ, D_MODEL)
    new_c, new_n, m_rows = st_new
    new_m = m_rows[:, :, :2, :ML_HEADS]
    return (y_p, y_s, jnp.swapaxes(kv_new[0], 3, 4), jnp.swapaxes(kv_new[1], 3, 4), new_c, new_n, new_m)
```

```python
import functools
import math

import jax
import jax.numpy as jnp
from jax import lax
from jax.experimental import pallas as pl
from jax.experimental.pallas import tpu as pltpu

D_MODEL = 1024
BATCH = 16
SEQ = 256
DEPTH = 4
DEC_BATCH = 2
DEC_SEQ = 1024
PAST_LEN = 256
GRID_W = 64
NA_HEADS = 8
NA_HEAD_DIM = 64
NA_WIN_H = 8
NA_WIN_W = 16
ML_HEADS = 4
ML_HEAD_DIM = 128
ML_CHUNK = 128
CONV_WIDTH = 3
D_FF = 2816
N_MOD = 9
EPS = 1e-6
N_AB = (DEPTH + 1) // 2
N_C = DEPTH // 2
NA_WIDTH = NA_HEADS * NA_HEAD_DIM
ML_WIDTH = ML_HEADS * ML_HEAD_DIM
AB_MAIN = 3 * NA_WIDTH + 4 * ML_WIDTH
N_GATES = 4 * ML_HEADS
SC_WIDTH = D_MODEL
N_GAINS = 3 * 2

N_PROMPT = BATCH * SEQ
N_SAMPLE = DEC_BATCH * DEC_SEQ
N_TOK = N_PROMPT + N_SAMPLE
GRID_ROWS = DEC_SEQ // GRID_W
NA_KH = min(NA_WIN_H, GRID_ROWS)
N_DR = 2 * NA_WIN_H - 1
N_DC = 2 * NA_WIN_W - 1
N_PAIR = N_DR - 1

LANES = 128
COND_ROWS = 8
VMEM_CAP = 60 * 1024 * 1024

COND_UNIT = DEC_SEQ
TM = 2048
SUB_ROWS = 512
TM_OUT = 1024
TM_CONV = COND_UNIT
TF_CONV = 512
TF = 256
TN_PROJ = 512
TN_MOD = 1536
NA_QROWS = 8
CTX_SEQS_PER_STEP = 4
ML_SEQS_PER_STEP = 2

PROJ_BLOCKS = (0, 1, 2, 3, 5, 6)
KM_BLOCK = 4
P_QA, P_KA, P_VA, P_QM, P_VM, P_OM = range(6)

F32 = jnp.float32
BF16 = jnp.bfloat16


def _vmem_limit(block_bytes, scratch_bytes):
    assert 2 * block_bytes + scratch_bytes <= VMEM_CAP
    return VMEM_CAP


def _params(semantics, block_bytes, scratch_bytes=0):
    return pltpu.CompilerParams(dimension_semantics=semantics,
                                vmem_limit_bytes=_vmem_limit(block_bytes, scratch_bytes))


def _unit_rows(i, tm=TM):
    first_latent = N_PROMPT // COND_UNIT
    units = tm // COND_UNIT
    return [(slice(u * COND_UNIT, (u + 1) * COND_UNIT), jnp.maximum(i * units + u - (first_latent - 1), 0))
            for u in range(units)]


def _sub_rows(i, tm=TM):
    per_unit = COND_UNIT // SUB_ROWS
    return [(slice(rows.start + s * SUB_ROWS, rows.start + (s + 1) * SUB_ROWS), r)
            for rows, r in _unit_rows(i, tm) for s in range(per_unit)]


def _rms_scale(x):
    return x * lax.rsqrt(jnp.mean(x * x, axis=-1, keepdims=True) + EPS)


def _modulate_in(x, g_ref, mod_ref, r, j):
    gain = g_ref[2 * j:2 * j + 1, :] * (1.0 + mod_ref[r, 3 * j + 1:3 * j + 2, :])
    return _rms_scale(x) * gain + mod_ref[r, 3 * j:3 * j + 1, :]


def _gated_residual(x, out, g_ref, mod_ref, r, j, weight):
    gain = (weight * mod_ref[r, 3 * j + 2:3 * j + 3, :]) * g_ref[2 * j + 1:2 * j + 2, :]
    return x + _rms_scale(out) * gain


def _cond_specs(l):
    return [pl.BlockSpec((None, COND_ROWS, N_MOD, D_MODEL), lambda *_: (l, 0, 0, 0)),
            pl.BlockSpec((None, N_GAINS, D_MODEL), lambda *_: (l, 0, 0))]


def _dot(a, b):
    return jnp.dot(a.astype(BF16), b.astype(BF16), preferred_element_type=F32)


def _dot_nt(a, b):
    return lax.dot_general(a.astype(BF16), b.astype(BF16), (((1,), (1,)), ((), ())),
                           preferred_element_type=F32)


def _mod_kernel(c_ref, w_ref, b_ref, o_ref):
    c = c_ref[...]
    s = c * jax.nn.sigmoid(c)
    o_ref[...] = _dot(s, w_ref[...]) + b_ref[...]


def _modulation(cond, ada_w, ada_b):
    n = N_MOD * D_MODEL
    out = pl.pallas_call(
        _mod_kernel,
        grid=(DEPTH, n // TN_MOD),
        in_specs=[
            pl.BlockSpec((COND_ROWS, D_MODEL), lambda l, t: (0, 0)),
            pl.BlockSpec((None, D_MODEL, TN_MOD), lambda l, t: (l, 0, t)),
            pl.BlockSpec((None, 1, TN_MOD), lambda l, t: (l, 0, t)),
        ],
        out_specs=pl.BlockSpec((None, COND_ROWS, TN_MOD), lambda l, t: (l, 0, t)),
        out_shape=jax.ShapeDtypeStruct((DEPTH, COND_ROWS, n), F32),
        compiler_params=_params(("parallel", "parallel"), 4 * D_MODEL * TN_MOD),
        name="modulation",
    )(cond, ada_w, ada_b.reshape(DEPTH, 1, n))
    return out.reshape(DEPTH, COND_ROWS, N_MOD, D_MODEL)


def _ffn_kernel(*refs, j, n_x, n_o, n_head):
    i = pl.program_id(0)
    k = pl.program_id(1)
    last = pl.num_programs(1) - 1
    x_refs = refs[:n_x]
    mod_ref, g_ref, wa_ref, wg_ref, w2_ref = refs[n_x:n_x + 5]
    o_refs = refs[n_x + 5:n_x + 5 + n_o]
    h_ref = refs[-1]

    def per_part(fn):
        if n_x == 1 and n_o == 1:
            fn(x_refs[0], o_refs[0])
        else:
            pl.when(i < n_head)(functools.partial(fn, x_refs[0], o_refs[0]))
            pl.when(i >= n_head)(functools.partial(fn, x_refs[-1], o_refs[-1]))

    def hidden_step(o_ref, rows, init):
        h = h_ref[rows, :]
        a = _dot(h, wa_ref[...])
        g = _dot(h, wg_ref[...])
        d = _dot((g * jax.nn.sigmoid(g)) * a, w2_ref[...])
        if init:
            o_ref[rows, :] = d
        else:
            o_ref[rows, :] += d

    def first_step(x_ref, o_ref):
        for rows, r in _sub_rows(i):
            h_ref[rows, :] = _modulate_in(x_ref[rows, :], g_ref, mod_ref, r, j).astype(BF16)
            hidden_step(o_ref, rows, init=True)

    def middle_step(x_ref, o_ref):
        for rows, _ in _unit_rows(i):
            hidden_step(o_ref, rows, init=False)

    def last_step(x_ref, o_ref):
        for rows, r in _sub_rows(i):
            hidden_step(o_ref, rows, init=False)
            o_ref[rows, :] = _gated_residual(x_ref[rows, :], o_ref[rows, :], g_ref, mod_ref, r, j, 0.5)

    pl.when(k == 0)(functools.partial(per_part, first_step))
    pl.when((k > 0) & (k < last))(functools.partial(per_part, middle_step))
    pl.when(k == last)(functools.partial(per_part, last_step))


def _ffn_sublayer(ys, mod, gains, ffn_w13, ffn_w2, l, s, split_out=False):
    j = 2 * s
    nk = D_FF // TF
    n_tiles = N_TOK // TM
    n_head = N_PROMPT // TM
    ys = ys if isinstance(ys, (tuple, list)) else [ys]
    head_spec = pl.BlockSpec((TM, D_MODEL), lambda i, k: (jnp.minimum(i, n_head - 1), 0))
    tail_spec = pl.BlockSpec((TM, D_MODEL), lambda i, k: (jnp.maximum(i - n_head, 0), 0))
    whole_spec = pl.BlockSpec((TM, D_MODEL), lambda i, k: (i, 0))
    x_specs = [whole_spec] if len(ys) == 1 else [head_spec, tail_spec]
    if split_out:
        out_specs = [head_spec, tail_spec]
        out_shape = [jax.ShapeDtypeStruct((n_head * TM, D_MODEL), F32),
                     jax.ShapeDtypeStruct(((n_tiles - n_head) * TM, D_MODEL), F32)]
    else:
        out_specs = [whole_spec]
        out_shape = [jax.ShapeDtypeStruct((N_TOK, D_MODEL), F32)]
    blocks = 4 * ((len(x_specs) + len(out_specs)) * TM * D_MODEL + 3 * D_MODEL * TF)
    scratch = 2 * TM * D_MODEL
    out = pl.pallas_call(
        functools.partial(_ffn_kernel, j=j, n_x=len(x_specs), n_o=len(out_specs), n_head=n_head),
        grid=(n_tiles, nk),
        in_specs=[
            *x_specs,
            *_cond_specs(l),
            pl.BlockSpec((None, None, D_MODEL, TF), lambda i, k: (l, s, 0, k)),
            pl.BlockSpec((None, None, D_MODEL, TF), lambda i, k: (l, s, 0, k + nk)),
            pl.BlockSpec((None, None, TF, D_MODEL), lambda i, k: (l, s, k, 0)),
        ],
        out_specs=out_specs,
        out_shape=out_shape,
        scratch_shapes=[pltpu.VMEM((TM, D_MODEL), BF16)],
        compiler_params=_params(("arbitrary" if split_out else "parallel", "arbitrary"), blocks, scratch),
        name=f"ffn_l{l}_s{s}",
    )(*ys, mod, gains, ffn_w13, ffn_w13, ffn_w2)
    return out if split_out else out[0]


def _sconv_kernel(x_ref, mod_ref, g_ref, wb_ref, wc_ref, wx_ref, cw_ref, cb_ref, wo_ref, o_ref, h_ref, *, jc):
    i = pl.program_id(0)
    k = pl.program_id(1)
    ((_, r),) = _unit_rows(i, TM_CONV)
    seq = jnp.where(i < N_PROMPT // TM_CONV, SEQ, DEC_SEQ)
    blk = SEQ
    n_blk = TM_CONV // blk
    row = lax.broadcasted_iota(jnp.int32, (blk, TF_CONV), 0)
    zero_row = jnp.zeros((1, TF_CONV), F32)

    def step(first, last):
        wb = wb_ref[...].astype(BF16)
        wc = wc_ref[...].astype(BF16)
        wx = wx_ref[...].astype(BF16)
        wo = wo_ref[...].astype(BF16)
        gbs, us = [], []

        def project(bi):
            rows = slice(bi * blk, (bi + 1) * blk)
            if first:
                h_ref[rows, :] = _modulate_in(x_ref[rows, :], g_ref, mod_ref, r, 1).astype(BF16)
            h = h_ref[rows, :]
            gbs.append(_dot(h, wb))
            us.append(_dot(h, wc) * _dot(h, wx))

        project(0)
        for bi in range(n_blk):
            if bi + 1 < n_blk:
                project(bi + 1)
            rows = slice(bi * blk, (bi + 1) * blk)
            ub = us[bi]
            t = (row + bi * blk) & (seq - 1)
            above = us[bi - 1][blk - 1:blk, :] if bi > 0 else zero_row
            below = us[bi + 1][0:1, :] if bi < n_blk - 1 else zero_row
            u_prev = jnp.where(t == 0, 0.0, jnp.where(row == 0, above, pltpu.roll(ub, 1, 0)))
            u_next = jnp.where(t == seq - 1, 0.0, jnp.where(row == blk - 1, below, pltpu.roll(ub, blk - 1, 0)))
            y = cb_ref[jc:jc + 1, :] + u_prev * cw_ref[0:1, :] + ub * cw_ref[1:2, :] + u_next * cw_ref[2:3, :]
            d = _dot(gbs[bi] * y, wo)
            acc = d if first else o_ref[rows, :] + d
            o_ref[rows, :] = _gated_residual(x_ref[rows, :], acc, g_ref, mod_ref, r, 1, 1.0) if last else acc

    n_steps = SC_WIDTH // TF_CONV
    for kk in range(n_steps):
        pl.when(k == kk)(functools.partial(step, kk == 0, kk == n_steps - 1))


def _sconv_sublayer(y, mod, gains, sc_w_in, sc_conv_w, sc_conv_b, sc_w_out, jc, l):
    tm, tf = TM_CONV, TF_CONV
    nk = SC_WIDTH // tf
    blocks = 4 * (2 * tm * D_MODEL + 4 * D_MODEL * tf)
    scratch = 2 * tm * D_MODEL
    return pl.pallas_call(
        functools.partial(_sconv_kernel, jc=jc),
        grid=(N_TOK // tm, nk),
        in_specs=[
            pl.BlockSpec((tm, D_MODEL), lambda i, k: (i, 0)),
            *_cond_specs(l),
            pl.BlockSpec((None, D_MODEL, tf), lambda i, k: (jc, 0, k)),
            pl.BlockSpec((None, D_MODEL, tf), lambda i, k: (jc, 0, k + nk)),
            pl.BlockSpec((None, D_MODEL, tf), lambda i, k: (jc, 0, k + 2 * nk)),
            pl.BlockSpec((None, CONV_WIDTH, tf), lambda i, k: (jc, 0, k)),
            pl.BlockSpec((N_C, tf), lambda i, k: (0, k)),
            pl.BlockSpec((None, tf, D_MODEL), lambda i, k: (jc, k, 0)),
        ],
        out_specs=pl.BlockSpec((tm, D_MODEL), lambda i, k: (i, 0)),
        out_shape=jax.ShapeDtypeStruct((N_TOK, D_MODEL), F32),
        scratch_shapes=[pltpu.VMEM((tm, D_MODEL), BF16)],
        compiler_params=_params(("parallel", "arbitrary"), blocks, scratch),
        name=f"sconv_{jc}",
    )(y, mod, gains, sc_w_in, sc_w_in, sc_w_in, sc_conv_w, sc_conv_b, sc_w_out)


def _inproj_kernel(x_ref, mod_ref, g_ref, w_ref, wk_ref, wg_ref, gb_ref, o_ref, okt_ref, og_ref, h_ref):
    i = pl.program_id(0)
    n = pl.program_id(1)
    n_main = len(PROJ_BLOCKS)

    @pl.when(n == 0)
    def _():
        for rows, r in _sub_rows(i):
            h_ref[rows, :] = _modulate_in(x_ref[rows, :], g_ref, mod_ref, r, 1).astype(BF16)
            o_ref[rows, :] = _dot_nt(h_ref[rows, :], w_ref[...])

    @pl.when((n > 0) & (n < n_main))
    def _():
        o_ref[...] = _dot_nt(h_ref[...], w_ref[...])

    @pl.when(n == n_main)
    def _():
        w_kg = jnp.concatenate([wk_ref[...], wg_ref[...]], axis=0)
        ktg = _dot_nt(w_kg, h_ref[...])
        k_scale = ML_HEAD_DIM ** -0.5
        for c in range(TM // ML_CHUNK):
            okt_ref[c] = ktg[:ML_WIDTH, c * ML_CHUNK:(c + 1) * ML_CHUNK] * k_scale
        g = ktg[ML_WIDTH:, :] + gb_ref[...]
        gate = lax.broadcasted_iota(jnp.int32, g.shape, 0)
        is_forget = ((gate >= ML_HEADS) & (gate < 2 * ML_HEADS)) | (gate >= 3 * ML_HEADS)
        log_sig = jnp.minimum(g, 0.0) - jnp.log1p(jnp.exp(-jnp.abs(g)))
        g = jnp.where(is_forget, log_sig, g)
        gates_t = jnp.concatenate([g, jnp.zeros((LANES - N_GATES, TM), F32)], axis=0)
        og_ref[...] = gates_t.T


def _inproj(y, mod, gains, w_in_t, gate_b_cols, jb, l):
    n_main = len(PROJ_BLOCKS)
    assert PROJ_BLOCKS == tuple(b for b in range(n_main + 1) if b != KM_BLOCK)

    def w_block(i, n):
        m = jnp.minimum(n, n_main - 1)
        return (jb, m + (m >= KM_BLOCK).astype(jnp.int32), 0)

    cpt = TM // ML_CHUNK
    blocks = 4 * (TM * D_MODEL + 2 * TN_PROJ * D_MODEL + N_GATES * D_MODEL + TM * TN_PROJ
                  + ML_WIDTH * TM + TM * LANES)
    return pl.pallas_call(
        _inproj_kernel,
        grid=(N_TOK // TM, n_main + 1),
        in_specs=[
            pl.BlockSpec((TM, D_MODEL), lambda i, n: (i, 0)),
            *_cond_specs(l),
            pl.BlockSpec((None, TN_PROJ, D_MODEL), w_block),
            pl.BlockSpec((None, ML_WIDTH, D_MODEL), lambda i, n: (jb, KM_BLOCK, 0)),
            pl.BlockSpec((None, N_GATES, D_MODEL), lambda i, n: (jb, AB_MAIN // N_GATES, 0)),
            pl.BlockSpec((None, N_GATES, 1), lambda i, n: (jb, 0, 0)),
        ],
        out_specs=[
            pl.BlockSpec((TM, TN_PROJ), lambda i, n: (i, jnp.minimum(n, n_main - 1))),
            pl.BlockSpec((cpt, ML_WIDTH, ML_CHUNK), lambda i, n: (i, 0, 0)),
            pl.BlockSpec((TM, LANES), lambda i, n: (i, 0)),
        ],
        out_shape=[jax.ShapeDtypeStruct((N_TOK, n_main * TN_PROJ), F32),
                   jax.ShapeDtypeStruct((N_TOK // ML_CHUNK, ML_WIDTH, ML_CHUNK), F32),
                   jax.ShapeDtypeStruct((N_TOK, LANES), F32)],
        scratch_shapes=[pltpu.VMEM((TM, D_MODEL), BF16)],
        compiler_params=_params(("parallel", "arbitrary"), blocks, 2 * TM * D_MODEL),
        name=f"inproj_{jb}",
    )(y, mod, gains, w_in_t, w_in_t, w_in_t, gate_b_cols)


def _stack_entry(ref, jb, fills_stack):
    if not fills_stack:
        return ref
    for j in range(ref.shape[0]):
        if j != jb:
            ref[j] = jnp.zeros(ref.shape[1:], ref.dtype)
    return ref.at[jb]


def _stack_spec(entry_shape, jb, fills_stack, n_stack=N_AB, lead=None):
    zeros = (0,) * len(entry_shape)
    if fills_stack:
        return pl.BlockSpec((lead, n_stack) + entry_shape, lambda b: (b, 0) + zeros)
    return pl.BlockSpec((lead, None) + entry_shape, lambda b: (b, jb) + zeros)


def _ctx_attn_kernel(q_ref, k_ref, v_ref, *rest, jb, fills_stack, spb):
    o_ref = rest[-3]
    scale = NA_HEAD_DIM ** -0.5
    lane = lax.broadcasted_iota(jnp.int32, (SEQ, LANES), 1)
    ones = jnp.ones((SEQ, LANES), BF16)
    for s in range(spb):
        rows = slice(s * SEQ, (s + 1) * SEQ)
        kc_ref = _stack_entry(rest[-2].at[s], jb, fills_stack)
        vc_ref = _stack_entry(rest[-1].at[s], jb, fills_stack)
        k_t = k_ref[rows, :].T
        v_t = v_ref[rows, :].T
        for h in range(NA_HEADS):
            sl = slice(h * NA_HEAD_DIM, (h + 1) * NA_HEAD_DIM)
            kc_ref[h] = k_t[sl, :]
            vc_ref[h] = v_t[sl, :]
        for hp in range(NA_WIDTH // LANES):
            pair = slice(hp * LANES, (hp + 1) * LANES)
            q_pair = q_ref[rows, pair] * scale
            v_aug = jnp.concatenate([v_ref[rows, pair].astype(BF16), ones], axis=1)
            outs = []
            for hh in range(LANES // NA_HEAD_DIM):
                own = (lane >= hh * NA_HEAD_DIM) & (lane < (hh + 1) * NA_HEAD_DIM)
                logits = _dot(jnp.where(own, q_pair, 0.0), k_t[pair, :])
                r = _dot(jnp.exp(logits - jnp.max(logits, axis=-1, keepdims=True)), v_aug)
                outs.append(r[:, :LANES] / r[:, LANES:])
            o_ref[rows, pair] = jnp.where(lane < NA_HEAD_DIM, outs[0], outs[1]).astype(o_ref.dtype)


def _ctx_attention(proj, jb, prev):
    spb = CTX_SEQS_PER_STEP
    rows = spb * SEQ
    cache = jax.ShapeDtypeStruct((BATCH, N_AB, NA_HEADS, NA_HEAD_DIM, SEQ), F32)
    cache_spec = _stack_spec((NA_HEADS, NA_HEAD_DIM, SEQ), jb, prev is None, lead=spb)
    in_specs = [pl.BlockSpec((rows, NA_WIDTH), lambda b, c=c: (b, c)) for c in (P_QA, P_KA, P_VA)]
    args = [proj, proj, proj]
    aliases = {}
    if prev is not None:
        in_specs += [pl.BlockSpec(memory_space=pl.ANY)] * 2
        args += list(prev)
        aliases = {3: 1, 4: 2}
    blocks = 4 * (4 * rows * NA_WIDTH + 2 * N_AB * NA_WIDTH * rows)
    return pl.pallas_call(
        functools.partial(_ctx_attn_kernel, jb=jb, fills_stack=prev is None, spb=spb),
        grid=(BATCH // spb,),
        in_specs=in_specs,
        out_specs=[pl.BlockSpec((rows, NA_WIDTH), lambda b: (b, 0)), cache_spec, cache_spec],
        out_shape=[jax.ShapeDtypeStruct((N_PROMPT, NA_WIDTH), BF16), cache, cache],
        input_output_aliases=aliases,
        compiler_params=_params(("parallel",), blocks),
        name=f"ctx_attention_{jb}",
    )(*args)


def _na_bias_kernel(rpb_ref, p_ref):
    layer = pl.program_id(0)
    head = pl.program_id(1)
    shape = (GRID_W, LANES)
    lane = lax.broadcasted_iota(jnp.int32, shape, 1)
    qc = lax.broadcasted_iota(jnp.int32, shape, 0)
    kc = lane & (GRID_W - 1)
    upper = lane >= GRID_W
    cs = jnp.clip(qc - NA_WIN_W // 2, 0, GRID_W - NA_WIN_W)
    col_in = (kc >= cs) & (kc < cs + NA_WIN_W)
    j = lax.broadcasted_iota(jnp.int32, (1, LANES), 1)
    dcol = jnp.clip(j - (GRID_W - 1), -(NA_WIN_W - 1), NA_WIN_W - 1) + (NA_WIN_W - 1)

    def toeplitz(dr, half):
        b = jnp.zeros((1, LANES), F32)
        for d in range(N_DC):
            b = jnp.where(dcol == d, rpb_ref[layer, head, dr, d], b)
        shift = (LANES - (GRID_W - 1) + half * GRID_W) % LANES
        return pltpu.roll(jnp.broadcast_to(b, shape), shift, 1, stride=1, stride_axis=0)

    blocks = [[toeplitz(dr, half) for half in range(2)] for dr in range(N_DR)]
    for pair in range(N_PAIR):
        both = jnp.where(upper, blocks[pair + 1][1], blocks[pair][0])
        p_ref[pair] = jnp.where(col_in, both, -jnp.inf)


def _na_bias(na_rpb):
    return pl.pallas_call(
        _na_bias_kernel,
        grid=(N_AB, NA_HEADS),
        in_specs=[pl.BlockSpec(memory_space=pltpu.SMEM)],
        out_specs=pl.BlockSpec((None, None, N_PAIR, GRID_W, LANES), lambda j, h: (j, h, 0, 0, 0)),
        out_shape=jax.ShapeDtypeStruct((N_AB, NA_HEADS, N_PAIR, GRID_W, LANES), F32),
        compiler_params=_params(("parallel", "parallel"), 4 * N_PAIR * GRID_W * LANES),
        name="na_bias",
    )(na_rpb)


def _na_window_start(r):
    return min(max(r - NA_KH // 2, 0), GRID_ROWS - NA_KH)


def _na_bias_block(p_ref, hh, qr, kr):
    rs = _na_window_start(qr)
    lo_ok = rs <= kr < rs + NA_KH
    hi_ok = rs <= kr + 1 < rs + NA_KH
    if not (lo_ok or hi_ok):
        return jnp.full((GRID_W, LANES), -jnp.inf, F32)
    dr = kr - qr + (NA_WIN_H - 1)
    assert 0 <= dr < N_PAIR
    blk = p_ref[hh, dr]
    if lo_ok and hi_ok:
        return blk
    lane = lax.broadcasted_iota(jnp.int32, (GRID_W, LANES), 1)
    keep = (lane < GRID_W) if lo_ok else (lane >= GRID_W)
    return jnp.where(keep, blk, -jnp.inf)


def _na_kernel(q_ref, k_ref, v_ref, kc_ref, vc_ref, p_ref, o_ref):
    scale = NA_HEAD_DIM ** -0.5
    n_q = NA_QROWS * GRID_W
    lane = lax.broadcasted_iota(jnp.int32, (n_q, LANES), 1)
    k_ctx_t = kc_ref[...].reshape(LANES, PAST_LEN)
    v_ctx_t = vc_ref[...].reshape(LANES, PAST_LEN)
    v_aug_ctx_t = jnp.concatenate([v_ctx_t.astype(BF16), jnp.ones((LANES, PAST_LEN), BF16)], axis=0)
    for q0 in range(0, GRID_ROWS, NA_QROWS):
        k_lo = _na_window_start(q0) // 2 * 2
        k_hi = -(-(_na_window_start(q0 + NA_QROWS - 1) + NA_KH) // 2) * 2
        q_rows = slice(q0 * GRID_W, (q0 + NA_QROWS) * GRID_W)
        k_rows = slice(k_lo * GRID_W, k_hi * GRID_W)
        q_pair = q_ref[q_rows, :] * scale
        n_k = (k_hi - k_lo) * GRID_W
        v_aug_loc = jnp.concatenate([v_ref[k_rows, :].astype(BF16), jnp.ones((n_k, LANES), BF16)], axis=1)
        outs = []
        for hh in range(LANES // NA_HEAD_DIM):
            bias = jnp.concatenate(
                [jnp.concatenate([_na_bias_block(p_ref, hh, qr, kr) for kr in range(k_lo, k_hi, 2)], axis=1)
                 for qr in range(q0, q0 + NA_QROWS)], axis=0)
            own = (lane >= hh * NA_HEAD_DIM) & (lane < (hh + 1) * NA_HEAD_DIM)
            q = jnp.where(own, q_pair, 0.0)
            s_loc = _dot_nt(q, k_ref[k_rows, :]) + bias
            s_ctx = _dot(q, k_ctx_t)
            m = jnp.maximum(jnp.max(s_loc, axis=-1, keepdims=True), jnp.max(s_ctx, axis=-1, keepdims=True))
            r = _dot(jnp.exp(s_loc - m), v_aug_loc) + _dot_nt(jnp.exp(s_ctx - m), v_aug_ctx_t)
            outs.append(r[:, :LANES] / r[:, LANES:])
        o_ref[q_rows, :] = jnp.where(lane < NA_HEAD_DIM, outs[0], outs[1]).astype(o_ref.dtype)


def _na_attention(proj, cache_k_t, cache_v_t, bias_pairs, jb):
    hp = LANES // NA_HEAD_DIM
    s0 = N_PROMPT // DEC_SEQ
    nb = NA_WIDTH // LANES
    ctx_spec = pl.BlockSpec((None, None, hp, NA_HEAD_DIM, PAST_LEN), lambda b, c: (b, jb, c, 0, 0))
    blocks = 4 * (4 * DEC_SEQ * LANES + 2 * hp * NA_HEAD_DIM * PAST_LEN + hp * N_PAIR * GRID_W * LANES)
    return pl.pallas_call(
        _na_kernel,
        grid=(DEC_BATCH, nb),
        in_specs=[
            pl.BlockSpec((DEC_SEQ, LANES), lambda b, c: (s0 + b, P_QA * nb + c)),
            pl.BlockSpec((DEC_SEQ, LANES), lambda b, c: (s0 + b, P_KA * nb + c)),
            pl.BlockSpec((DEC_SEQ, LANES), lambda b, c: (s0 + b, P_VA * nb + c)),
            ctx_spec, ctx_spec,
            pl.BlockSpec((None, hp, N_PAIR, GRID_W, LANES), lambda b, c: (jb, c, 0, 0, 0)),
        ],
        out_specs=pl.BlockSpec((DEC_SEQ, LANES), lambda b, c: (b, c)),
        out_shape=jax.ShapeDtypeStruct((N_SAMPLE, NA_WIDTH), BF16),
        compiler_params=_params(("parallel", "parallel"), blocks),
        name=f"na_attention_{jb}",
    )(proj, proj, proj, cache_k_t, cache_v_t, bias_pairs)


def _running_rows(x, op, identity, reverse):
    n = x.shape[0]
    ridx = lax.broadcasted_iota(jnp.int32, x.shape, 0)
    sh = 1
    while sh < n:
        if reverse:
            x = op(x, jnp.where(ridx < n - sh, pltpu.roll(x, n - sh, 0), identity))
        else:
            x = op(x, jnp.where(ridx >= sh, pltpu.roll(x, sh, 0), identity))
        sh *= 2
    return x


def _mlstm_kernel(*refs, n_chunks, has_state, jb, fills_stack, spb, layer):
    if has_state:
        (q_ref, kt_ref, v_ref, o_ref, gt_ref, hn_ref, c0_ref, n0_ref, m0_ref,
         out_ref, cout_ref, nout_ref, mout_ref, hdir_ref, caug_ref) = refs
    else:
        (q_ref, kt_ref, v_ref, o_ref, gt_ref, hn_ref, *_unused,
         out_ref, cout_ref, nout_ref, mout_ref, hdir_ref, caug_ref) = refs
    cout_refs = [_stack_entry(cout_ref.at[s], jb, fills_stack) for s in range(spb)]
    nout_refs = [_stack_entry(nout_ref.at[s], jb, fills_stack) for s in range(spb)]
    mout_refs = [_stack_entry(mout_ref.at[s], jb, fills_stack) for s in range(spb)]
    T = ML_CHUNK
    dh = ML_HEAD_DIM
    seq_len = n_chunks * T
    b = pl.program_id(0)
    row = lax.broadcasted_iota(jnp.int32, (T, T), 0)
    col = lax.broadcasted_iota(jnp.int32, (T, T), 1)
    lane = lax.broadcasted_iota(jnp.int32, (T, LANES), 1)
    lane_row = lax.broadcasted_iota(jnp.int32, (1, LANES), 1)
    ones = jnp.ones((T, dh), F32)

    def gate_lanes(d, h):
        gi = 2 * d * ML_HEADS + h
        return gi, gi + ML_HEADS

    scans = [(s, d) for s in range(spb) for d in range(2)]
    m_init = []
    for s, d in scans:
        m_row = jnp.zeros((1, LANES), F32)
        for h in range(ML_HEADS):
            if has_state:
                n_bcast = jnp.broadcast_to(n0_ref[s, d, h:h + 1, :], (dh, dh)).T
                caug_ref[s, d, h] = jnp.concatenate([c0_ref[s, d, h], n_bcast], axis=1)
                m0 = m0_ref[b * spb + s, layer, d, h]
                m_row = jnp.where(lane_row == gate_lanes(d, h)[1], m0, m_row)
            else:
                caug_ref[s, d, h] = jnp.zeros((dh, 2 * dh), F32)
        m_init.append(m_row)

    def chunk_body(ci, m_rows):
        m_next = []
        for scan, (s, d) in enumerate(scans):
            causal = (col <= row) if d == 0 else (col >= row)
            last = T - 1 if d == 0 else 0
            c = ci if d == 0 else n_chunks - 1 - ci
            rows = pl.ds(pl.multiple_of(s * seq_len + c * T, T), T)
            m_prev = m_rows[scan]
            mine = (lane >= (2 * d + 1) * ML_HEADS) & (lane < (2 * d + 2) * ML_HEADS)
            g = gt_ref[rows, :]
            cum = _running_rows(g, jnp.add, 0.0, reverse=(d == 1))
            i_al = pltpu.roll(g, ML_HEADS, 1)
            cvec = jnp.where(mine, i_al - cum, 0.0)
            cum = jnp.where(mine, cum, 0.0)
            m_run = jnp.maximum(m_prev, _running_rows(cvec, jnp.maximum, -jnp.inf, reverse=(d == 1)))
            m_t = cum + m_run
            e_negm = jnp.exp(-m_t)
            m_run_last = m_run[last:last + 1, :]
            wi_last = jnp.exp(m_prev - m_run_last)
            cvec_t = cvec.T
            for h in range(ML_HEADS):
                _, gf = gate_lanes(d, h)
                hl = slice(h * dh, (h + 1) * dh)
                m_run_b = jnp.broadcast_to(m_run[:, gf:gf + 1], (T, T))
                e_negm_b = jnp.broadcast_to(e_negm[:, gf:gf + 1], (T, dh))
                c_row = cvec_t[gf:gf + 1, :]
                w_d = jnp.exp(jnp.where(causal, c_row - m_run_b, -jnp.inf))
                w_inter_b = jnp.exp(m_prev[:, gf:gf + 1] - m_run_b)
                qh = q_ref[rows, hl]
                kh_t = kt_ref[s * n_chunks + c, hl, :]
                v_aug = jnp.concatenate([v_ref[rows, hl], ones], axis=1)
                sc = _dot(qh, kh_t) * w_d
                c_aug = caug_ref[s, d, h]
                nd = jnp.concatenate([w_inter_b, w_inter_b], axis=1) * _dot(qh, c_aug) + _dot(sc, v_aug)
                hdir_ref[d, rows, hl] = nd[:, :dh] / jnp.maximum(jnp.abs(nd[:, dh:]), e_negm_b)
                kw_t = kh_t * jnp.exp(c_row - m_run_last[:, gf:gf + 1])
                caug_ref[s, d, h] = wi_last[:, gf:gf + 1] * c_aug + _dot(kw_t, v_aug)
            m_next.append(m_t[last:last + 1, :])
        return tuple(m_next)

    m_fin = lax.fori_loop(0, n_chunks, chunk_body, tuple(m_init), unroll=2)
    for scan, (s, d) in enumerate(scans):
        for h in range(ML_HEADS):
            c_aug = caug_ref[s, d, h]
            cout_refs[s][d, h] = c_aug[:, :dh]
            nout_refs[s][d, h:h + 1, :] = c_aug[:, dh:].T[0:1, :]
    for s in range(spb):
        m_heads = [pltpu.roll(m_fin[2 * s + d], LANES - gate_lanes(d, 0)[1], 1) for d in range(2)]
        mout_refs[s][...] = jnp.concatenate(m_heads + [jnp.zeros((COND_ROWS - 2, LANES), F32)], axis=0)

    for h in range(ML_HEADS):
        hl = slice(h * dh, (h + 1) * dh)
        h_sum = hdir_ref[0, :, hl] + hdir_ref[1, :, hl]
        gated = jax.nn.sigmoid(o_ref[:, hl]) * (_rms_scale(h_sum) * hn_ref[layer:layer + 1, hl])
        out_ref[:, hl] = gated.astype(out_ref.dtype)


def _mlstm(proj, keys_t, gates, hnorm_rows, layer, seq_len, n_seq, row0, spb, n_stack=1, jb=0,
           prev=None, state=None):
    has_state = state is not None
    assert n_seq % spb == 0 and row0 % (spb * seq_len) == 0
    r0 = row0 // (spb * seq_len)
    rows = spb * seq_len
    nb = TN_PROJ // ML_WIDTH
    n_chunks = seq_len // ML_CHUNK
    in_specs = [
        pl.BlockSpec((rows, ML_WIDTH), lambda b: (r0 + b, P_QM * nb)),
        pl.BlockSpec((spb * n_chunks, ML_WIDTH, ML_CHUNK), lambda b: (r0 + b, 0, 0)),
        pl.BlockSpec((rows, ML_WIDTH), lambda b: (r0 + b, P_VM * nb)),
        pl.BlockSpec((rows, ML_WIDTH), lambda b: (r0 + b, P_OM * nb)),
        pl.BlockSpec((rows, LANES), lambda b: (r0 + b, 0)),
        pl.BlockSpec((N_AB, ML_WIDTH), lambda b: (0, 0)),
    ]
    args = [proj, keys_t, proj, proj, gates, hnorm_rows]
    aliases = {}
    if has_state:
        st_c, st_n, st_m = state
        in_specs += [
            pl.BlockSpec((spb, None, 2, ML_HEADS, ML_HEAD_DIM, ML_HEAD_DIM), lambda b: (b, layer, 0, 0, 0, 0)),
            pl.BlockSpec((spb, None, 2, ML_HEADS, ML_HEAD_DIM), lambda b: (b, layer, 0, 0, 0)),
            pl.BlockSpec(memory_space=pltpu.SMEM),
        ]
        args += [st_c, st_n, st_m]
    elif prev is not None:
        aliases = {len(args) + k: 1 + k for k in range(3)}
        in_specs += [pl.BlockSpec(memory_space=pl.ANY)] * 3
        args += list(prev)
    fills_stack = prev is None
    c_bytes = 2 * ML_HEADS * ML_HEAD_DIM * ML_HEAD_DIM
    blocks = 4 * (5 * rows * ML_WIDTH + rows * LANES + spb * (1 + n_stack) * c_bytes)
    scratch = 4 * (2 * rows * ML_WIDTH + spb * 2 * c_bytes)
    return pl.pallas_call(
        functools.partial(_mlstm_kernel, n_chunks=n_chunks, has_state=has_state, jb=jb, fills_stack=fills_stack,
                          spb=spb, layer=layer),
        grid=(n_seq // spb,),
        in_specs=in_specs,
        out_specs=[
            pl.BlockSpec((rows, ML_WIDTH), lambda b: (b, 0)),
            _stack_spec((2, ML_HEADS, ML_HEAD_DIM, ML_HEAD_DIM), jb, fills_stack, n_stack, spb),
            _stack_spec((2, ML_HEADS, ML_HEAD_DIM), jb, fills_stack, n_stack, spb),
            _stack_spec((COND_ROWS, LANES), jb, fills_stack, n_stack, spb),
        ],
        out_shape=[
            jax.ShapeDtypeStruct((n_seq * seq_len, ML_WIDTH), BF16),
            jax.ShapeDtypeStruct((n_seq, n_stack, 2, ML_HEADS, ML_HEAD_DIM, ML_HEAD_DIM), F32),
            jax.ShapeDtypeStruct((n_seq, n_stack, 2, ML_HEADS, ML_HEAD_DIM), F32),
            jax.ShapeDtypeStruct((n_seq, n_stack, COND_ROWS, LANES), F32),
        ],
        scratch_shapes=[pltpu.VMEM((2, rows, ML_WIDTH), F32),
                        pltpu.VMEM((spb, 2, ML_HEADS, ML_HEAD_DIM, 2 * ML_HEAD_DIM), F32)],
        input_output_aliases=aliases,
        compiler_params=_params(("parallel",), blocks, scratch),
        name=f"mlstm_{'latent' if has_state else 'context'}_{jb}",
    )(*args)


def _outproj_kernel(x_ref, mod_ref, g_ref, ap_ref, mp_ref, as_ref, ms_ref, wa_ref, wm_ref, o_ref):
    i = pl.program_id(0)

    def finish(a_ref, m_ref):
        wa = wa_ref[...].astype(BF16)
        wm = wm_ref[...].astype(BF16)
        for rows, r in _sub_rows(i, TM_OUT):
            out = _dot(a_ref[rows, :], wa) + _dot(m_ref[rows, :], wm)
            o_ref[rows, :] = _gated_residual(x_ref[rows, :], out, g_ref, mod_ref, r, 1, 1.0)

    is_prompt = i < N_PROMPT // TM_OUT
    pl.when(is_prompt)(functools.partial(finish, ap_ref, mp_ref))
    pl.when(jnp.logical_not(is_prompt))(functools.partial(finish, as_ref, ms_ref))


def _outproj(y, mod, gains, a_p, m_p, a_s, m_s, ab_w_out, jb, l):
    tm = TM_OUT
    n_p = N_PROMPT // tm
    n_s = N_SAMPLE // tm
    p_spec = pl.BlockSpec((tm, NA_WIDTH), lambda i: (jnp.minimum(i, n_p - 1), 0))
    s_spec = pl.BlockSpec((tm, NA_WIDTH), lambda i: (jnp.clip(i - n_p, 0, n_s - 1), 0))
    blocks = 4 * (2 * tm * D_MODEL + 4 * tm * NA_WIDTH + D_MODEL * D_MODEL)
    return pl.pallas_call(
        _outproj_kernel,
        grid=(N_TOK // tm,),
        in_specs=[
            pl.BlockSpec((tm, D_MODEL), lambda i: (i, 0)),
            *_cond_specs(l),
            p_spec, p_spec, s_spec, s_spec,
            pl.BlockSpec((None, NA_WIDTH, D_MODEL), lambda i: (jb, 0, 0)),
            pl.BlockSpec((None, ML_WIDTH, D_MODEL), lambda i: (jb, 1, 0)),
        ],
        out_specs=pl.BlockSpec((tm, D_MODEL), lambda i: (i, 0)),
        out_shape=jax.ShapeDtypeStruct((N_TOK, D_MODEL), F32),
        compiler_params=_params(("parallel",), blocks),
        name=f"outproj_{jb}",
    )(y, mod, gains, a_p, m_p, a_s, m_s, ab_w_out, ab_w_out)


def kernel(x_prompt, x_sample, cache_k, cache_v, state_C, state_n, state_m, c, c_ctx, ada_w, ada_b, norm_g,
           ffn_w13, ffn_w2, ab_w_in, ab_w_out, na_rpb, ml_gate_b, ml_hnorm, sc_w_in, sc_conv_w, sc_conv_b,
           sc_w_out):
    assert NA_WIDTH == ML_WIDTH == TN_PROJ and AB_MAIN % N_GATES == 0
    assert math.frexp(NA_HEAD_DIM ** -0.5)[0] == 0.5
    assert LANES == 2 * NA_HEAD_DIM
    assert N_PROMPT % TM == 0 and N_SAMPLE % TM == 0 and TM % COND_UNIT == 0
    cond = jnp.concatenate([c_ctx[None, :], c, jnp.zeros((COND_ROWS - 1 - DEC_BATCH, D_MODEL), F32)], axis=0)
    mod = _modulation(cond, ada_w, ada_b)
    gains = norm_g.reshape(DEPTH, N_GAINS, D_MODEL)
    bias_pairs = _na_bias(na_rpb)
    w_in_t = jnp.swapaxes(ab_w_in, 1, 2)
    cache_k_t = jnp.swapaxes(cache_k, 3, 4)
    cache_v_t = jnp.swapaxes(cache_v, 3, 4)
    gate_b_cols = ml_gate_b.reshape(N_AB, N_GATES, 1)
    hnorm_rows = ml_hnorm
    state = (state_C, state_n, state_m)

    kv_new = None
    st_new = None
    y = None
    y = (x_prompt.reshape(N_PROMPT, D_MODEL), x_sample.reshape(N_SAMPLE, D_MODEL))
    for l in range(DEPTH):
        y = _ffn_sublayer(y, mod, gains, ffn_w13, ffn_w2, l, 0)
        j = l // 2
        if l % 2 == 0:
            proj, keys_t, gates = _inproj(y, mod, gains, w_in_t, gate_b_cols, j, l)
            a_p, *kv_new = _ctx_attention(proj, j, kv_new)
            m_p, *st_new = _mlstm(proj, keys_t, gates, hnorm_rows, j, SEQ, BATCH, 0,
                                  ML_SEQS_PER_STEP, n_stack=N_AB, jb=j, prev=st_new)
            a_s = _na_attention(proj, cache_k_t, cache_v_t, bias_pairs, j)
            m_s, _, _, _ = _mlstm(proj, keys_t, gates, hnorm_rows, j, DEC_SEQ, DEC_BATCH, N_PROMPT,
                                  ML_SEQS_PER_STEP, state=state)
            y = _outproj(y, mod, gains, a_p, m_p, a_s, m_s, ab_w_out, j, l)
        else:
            y = _sconv_sublayer(y, mod, gains, sc_w_in, sc_conv_w, sc_conv_b, sc_w_out, j, l)
        if l < DEPTH - 1:
            y = _ffn_sublayer(y, mod, gains, ffn_w13, ffn_w2, l, 1)

    y_p, y_s = _ffn_sublayer(y, mod, gains, ffn_w13, ffn_w2, DEPTH - 1, 1, split_out=True)
    y_p = y_p.reshape(BATCH, SEQ, D_MODEL)
    y_s = y_s.reshape(DEC_BATCH, DEC_SEQ, D_MODEL)
    new_c, new_n, m_rows = st_new
    new_m = m_rows[:, :, :2, :ML_HEADS]
    return (y_p, y_s, jnp.swapaxes(kv_new[0], 3, 4), jnp.swapaxes(kv_new[1], 3, 4), new_c, new_n, new_m)
```

```python
import functools
import math

import jax
import jax.numpy as jnp
from jax import lax
from jax.experimental import pallas as pl
from jax.experimental.pallas import tpu as pltpu

D_MODEL = 1024
BATCH = 16
SEQ = 256
DEPTH = 4
DEC_BATCH = 2
DEC_SEQ = 1024
PAST_LEN = 256
GRID_W = 64
NA_HEADS = 8
NA_HEAD_DIM = 64
NA_WIN_H = 8
NA_WIN_W = 16
ML_HEADS = 4
ML_HEAD_DIM = 128
ML_CHUNK = 128
CONV_WIDTH = 3
D_FF = 2816
N_MOD = 9
EPS = 1e-6
N_AB = (DEPTH + 1) // 2
N_C = DEPTH // 2
NA_WIDTH = NA_HEADS * NA_HEAD_DIM
ML_WIDTH = ML_HEADS * ML_HEAD_DIM
AB_MAIN = 3 * NA_WIDTH + 4 * ML_WIDTH
N_GATES = 4 * ML_HEADS
SC_WIDTH = D_MODEL
N_GAINS = 3 * 2

N_PROMPT = BATCH * SEQ
N_SAMPLE = DEC_BATCH * DEC_SEQ
N_TOK = N_PROMPT + N_SAMPLE
GRID_ROWS = DEC_SEQ // GRID_W
NA_KH = min(NA_WIN_H, GRID_ROWS)
N_DR = 2 * NA_WIN_H - 1
N_DC = 2 * NA_WIN_W - 1
N_PAIR = N_DR - 1

LANES = 128
COND_ROWS = 8
VMEM_CAP = 60 * 1024 * 1024

COND_UNIT = DEC_SEQ
TM = 2048
SUB_ROWS = 512
TM_OUT = 1024
TM_CONV = COND_UNIT
TF_CONV = 512
TF = 256
TN_PROJ = 512
TN_MOD = 1536
NA_QROWS = 8
CTX_SEQS_PER_STEP = 4
ML_CTX_SEQS_PER_STEP = 4
ML_SEQS_PER_STEP = 2

PROJ_BLOCKS = (0, 1, 2, 3, 5, 6)
KM_BLOCK = 4
P_QA, P_KA, P_VA, P_QM, P_VM, P_OM = range(6)

F32 = jnp.float32
BF16 = jnp.bfloat16


def _vmem_limit(block_bytes, scratch_bytes):
    assert 2 * block_bytes + scratch_bytes <= VMEM_CAP
    return VMEM_CAP


def _params(semantics, block_bytes, scratch_bytes=0):
    return pltpu.CompilerParams(dimension_semantics=semantics,
                                vmem_limit_bytes=_vmem_limit(block_bytes, scratch_bytes))


def _unit_rows(i, tm=TM):
    first_latent = N_PROMPT // COND_UNIT
    units = tm // COND_UNIT
    return [(slice(u * COND_UNIT, (u + 1) * COND_UNIT), jnp.maximum(i * units + u - (first_latent - 1), 0))
            for u in range(units)]


def _sub_rows(i, tm=TM):
    per_unit = COND_UNIT // SUB_ROWS
    return [(slice(rows.start + s * SUB_ROWS, rows.start + (s + 1) * SUB_ROWS), r)
            for rows, r in _unit_rows(i, tm) for s in range(per_unit)]


def _rms_scale(x):
    return x * lax.rsqrt(jnp.mean(x * x, axis=-1, keepdims=True) + EPS)


def _modulate_in(x, g_ref, mod_ref, r, j):
    gain = g_ref[2 * j:2 * j + 1, :] * (1.0 + mod_ref[r, 3 * j + 1:3 * j + 2, :])
    return _rms_scale(x) * gain + mod_ref[r, 3 * j:3 * j + 1, :]


def _gated_residual(x, out, g_ref, mod_ref, r, j, weight):
    gain = (weight * mod_ref[r, 3 * j + 2:3 * j + 3, :]) * g_ref[2 * j + 1:2 * j + 2, :]
    return x + _rms_scale(out) * gain


def _cond_specs(l):
    return [pl.BlockSpec((None, COND_ROWS, N_MOD, D_MODEL), lambda *_: (l, 0, 0, 0)),
            pl.BlockSpec((None, N_GAINS, D_MODEL), lambda *_: (l, 0, 0))]


def _dot(a, b):
    return jnp.dot(a.astype(BF16), b.astype(BF16), preferred_element_type=F32)


def _dot_nt(a, b):
    return lax.dot_general(a.astype(BF16), b.astype(BF16), (((1,), (1,)), ((), ())),
                           preferred_element_type=F32)


def _mod_kernel(c_ref, w_ref, b_ref, o_ref):
    c = c_ref[...]
    s = c * jax.nn.sigmoid(c)
    o_ref[...] = _dot(s, w_ref[...]) + b_ref[...]


def _modulation(cond, ada_w, ada_b):
    n = N_MOD * D_MODEL
    out = pl.pallas_call(
        _mod_kernel,
        grid=(DEPTH, n // TN_MOD),
        in_specs=[
            pl.BlockSpec((COND_ROWS, D_MODEL), lambda l, t: (0, 0)),
            pl.BlockSpec((None, D_MODEL, TN_MOD), lambda l, t: (l, 0, t)),
            pl.BlockSpec((None, 1, TN_MOD), lambda l, t: (l, 0, t)),
        ],
        out_specs=pl.BlockSpec((None, COND_ROWS, TN_MOD), lambda l, t: (l, 0, t)),
        out_shape=jax.ShapeDtypeStruct((DEPTH, COND_ROWS, n), F32),
        compiler_params=_params(("parallel", "parallel"), 4 * D_MODEL * TN_MOD),
        name="modulation",
    )(cond, ada_w, ada_b.reshape(DEPTH, 1, n))
    return out.reshape(DEPTH, COND_ROWS, N_MOD, D_MODEL)


def _ffn_kernel(*refs, j, n_x, n_o, n_head):
    i = pl.program_id(0)
    k = pl.program_id(1)
    last = pl.num_programs(1) - 1
    x_refs = refs[:n_x]
    mod_ref, g_ref, wa_ref, wg_ref, w2_ref = refs[n_x:n_x + 5]
    o_refs = refs[n_x + 5:n_x + 5 + n_o]
    h_ref = refs[-1]

    def per_part(fn):
        if n_x == 1 and n_o == 1:
            fn(x_refs[0], o_refs[0])
        else:
            pl.when(i < n_head)(functools.partial(fn, x_refs[0], o_refs[0]))
            pl.when(i >= n_head)(functools.partial(fn, x_refs[-1], o_refs[-1]))

    def hidden_step(o_ref, rows, init):
        h = h_ref[rows, :]
        a = _dot(h, wa_ref[...])
        g = _dot(h, wg_ref[...])
        d = _dot((g * jax.nn.sigmoid(g)) * a, w2_ref[...])
        if init:
            o_ref[rows, :] = d
        else:
            o_ref[rows, :] += d

    def first_step(x_ref, o_ref):
        for rows, r in _sub_rows(i):
            h_ref[rows, :] = _modulate_in(x_ref[rows, :], g_ref, mod_ref, r, j).astype(BF16)
            hidden_step(o_ref, rows, init=True)

    def middle_step(x_ref, o_ref):
        for rows, _ in _unit_rows(i):
            hidden_step(o_ref, rows, init=False)

    def last_step(x_ref, o_ref):
        for rows, r in _sub_rows(i):
            hidden_step(o_ref, rows, init=False)
            o_ref[rows, :] = _gated_residual(x_ref[rows, :], o_ref[rows, :], g_ref, mod_ref, r, j, 0.5)

    pl.when(k == 0)(functools.partial(per_part, first_step))
    pl.when((k > 0) & (k < last))(functools.partial(per_part, middle_step))
    pl.when(k == last)(functools.partial(per_part, last_step))


def _ffn_sublayer(ys, mod, gains, ffn_w13, ffn_w2, l, s, split_out=False):
    j = 2 * s
    nk = D_FF // TF
    n_tiles = N_TOK // TM
    n_head = N_PROMPT // TM
    ys = ys if isinstance(ys, (tuple, list)) else [ys]
    head_spec = pl.BlockSpec((TM, D_MODEL), lambda i, k: (jnp.minimum(i, n_head - 1), 0))
    tail_spec = pl.BlockSpec((TM, D_MODEL), lambda i, k: (jnp.maximum(i - n_head, 0), 0))
    whole_spec = pl.BlockSpec((TM, D_MODEL), lambda i, k: (i, 0))
    x_specs = [whole_spec] if len(ys) == 1 else [head_spec, tail_spec]
    if split_out:
        out_specs = [head_spec, tail_spec]
        out_shape = [jax.ShapeDtypeStruct((n_head * TM, D_MODEL), F32),
                     jax.ShapeDtypeStruct(((n_tiles - n_head) * TM, D_MODEL), F32)]
    else:
        out_specs = [whole_spec]
        out_shape = [jax.ShapeDtypeStruct((N_TOK, D_MODEL), F32)]
    blocks = 4 * ((len(x_specs) + len(out_specs)) * TM * D_MODEL + 3 * D_MODEL * TF)
    scratch = 2 * TM * D_MODEL
    out = pl.pallas_call(
        functools.partial(_ffn_kernel, j=j, n_x=len(x_specs), n_o=len(out_specs), n_head=n_head),
        grid=(n_tiles, nk),
        in_specs=[
            *x_specs,
            *_cond_specs(l),
            pl.BlockSpec((None, None, D_MODEL, TF), lambda i, k: (l, s, 0, k)),
            pl.BlockSpec((None, None, D_MODEL, TF), lambda i, k: (l, s, 0, k + nk)),
            pl.BlockSpec((None, None, TF, D_MODEL), lambda i, k: (l, s, k, 0)),
        ],
        out_specs=out_specs,
        out_shape=out_shape,
        scratch_shapes=[pltpu.VMEM((TM, D_MODEL), BF16)],
        compiler_params=_params(("arbitrary" if split_out else "parallel", "arbitrary"), blocks, scratch),
        name=f"ffn_l{l}_s{s}",
    )(*ys, mod, gains, ffn_w13, ffn_w13, ffn_w2)
    return out if split_out else out[0]


def _sconv_kernel(x_ref, mod_ref, g_ref, wb_ref, wc_ref, wx_ref, cw_ref, cb_ref, wo_ref, o_ref, h_ref, *, jc):
    i = pl.program_id(0)
    k = pl.program_id(1)
    ((_, r),) = _unit_rows(i, TM_CONV)
    seq = jnp.where(i < N_PROMPT // TM_CONV, SEQ, DEC_SEQ)
    blk = SEQ
    n_blk = TM_CONV // blk
    row = lax.broadcasted_iota(jnp.int32, (blk, TF_CONV), 0)
    zero_row = jnp.zeros((1, TF_CONV), F32)

    def step(first, last):
        wb = wb_ref[...].astype(BF16)
        wc = wc_ref[...].astype(BF16)
        wx = wx_ref[...].astype(BF16)
        wo = wo_ref[...].astype(BF16)
        gbs, us = [], []

        def project(bi):
            rows = slice(bi * blk, (bi + 1) * blk)
            if first:
                h_ref[rows, :] = _modulate_in(x_ref[rows, :], g_ref, mod_ref, r, 1).astype(BF16)
            h = h_ref[rows, :]
            gbs.append(_dot(h, wb))
            us.append(_dot(h, wc) * _dot(h, wx))

        project(0)
        for bi in range(n_blk):
            if bi + 1 < n_blk:
                project(bi + 1)
            rows = slice(bi * blk, (bi + 1) * blk)
            ub = us[bi]
            t = (row + bi * blk) & (seq - 1)
            above = us[bi - 1][blk - 1:blk, :] if bi > 0 else zero_row
            below = us[bi + 1][0:1, :] if bi < n_blk - 1 else zero_row
            u_prev = jnp.where(t == 0, 0.0, jnp.where(row == 0, above, pltpu.roll(ub, 1, 0)))
            u_next = jnp.where(t == seq - 1, 0.0, jnp.where(row == blk - 1, below, pltpu.roll(ub, blk - 1, 0)))
            y = cb_ref[jc:jc + 1, :] + u_prev * cw_ref[0:1, :] + ub * cw_ref[1:2, :] + u_next * cw_ref[2:3, :]
            d = _dot(gbs[bi] * y, wo)
            acc = d if first else o_ref[rows, :] + d
            o_ref[rows, :] = _gated_residual(x_ref[rows, :], acc, g_ref, mod_ref, r, 1, 1.0) if last else acc

    n_steps = SC_WIDTH // TF_CONV
    for kk in range(n_steps):
        pl.when(k == kk)(functools.partial(step, kk == 0, kk == n_steps - 1))


def _sconv_sublayer(y, mod, gains, sc_w_in, sc_conv_w, sc_conv_b, sc_w_out, jc, l):
    tm, tf = TM_CONV, TF_CONV
    nk = SC_WIDTH // tf
    blocks = 4 * (2 * tm * D_MODEL + 4 * D_MODEL * tf)
    scratch = 2 * tm * D_MODEL
    return pl.pallas_call(
        functools.partial(_sconv_kernel, jc=jc),
        grid=(N_TOK // tm, nk),
        in_specs=[
            pl.BlockSpec((tm, D_MODEL), lambda i, k: (i, 0)),
            *_cond_specs(l),
            pl.BlockSpec((None, D_MODEL, tf), lambda i, k: (jc, 0, k)),
            pl.BlockSpec((None, D_MODEL, tf), lambda i, k: (jc, 0, k + nk)),
            pl.BlockSpec((None, D_MODEL, tf), lambda i, k: (jc, 0, k + 2 * nk)),
            pl.BlockSpec((None, CONV_WIDTH, tf), lambda i, k: (jc, 0, k)),
            pl.BlockSpec((N_C, tf), lambda i, k: (0, k)),
            pl.BlockSpec((None, tf, D_MODEL), lambda i, k: (jc, k, 0)),
        ],
        out_specs=pl.BlockSpec((tm, D_MODEL), lambda i, k: (i, 0)),
        out_shape=jax.ShapeDtypeStruct((N_TOK, D_MODEL), F32),
        scratch_shapes=[pltpu.VMEM((tm, D_MODEL), BF16)],
        compiler_params=_params(("parallel", "arbitrary"), blocks, scratch),
        name=f"sconv_{jc}",
    )(y, mod, gains, sc_w_in, sc_w_in, sc_w_in, sc_conv_w, sc_conv_b, sc_w_out)


def _inproj_kernel(x_ref, mod_ref, g_ref, w_ref, wk_ref, wg_ref, gb_ref, o_ref, okt_ref, og_ref, h_ref):
    i = pl.program_id(0)
    n = pl.program_id(1)
    n_main = len(PROJ_BLOCKS)

    @pl.when(n == 0)
    def _():
        for rows, r in _sub_rows(i):
            h_ref[rows, :] = _modulate_in(x_ref[rows, :], g_ref, mod_ref, r, 1).astype(BF16)
            o_ref[rows, :] = _dot_nt(h_ref[rows, :], w_ref[...])

    @pl.when((n > 0) & (n < n_main))
    def _():
        o_ref[...] = _dot_nt(h_ref[...], w_ref[...])

    @pl.when(n == n_main)
    def _():
        w_kg = jnp.concatenate([wk_ref[...], wg_ref[...]], axis=0)
        ktg = _dot_nt(w_kg, h_ref[...])
        k_scale = ML_HEAD_DIM ** -0.5
        for c in range(TM // ML_CHUNK):
            okt_ref[c] = ktg[:ML_WIDTH, c * ML_CHUNK:(c + 1) * ML_CHUNK] * k_scale
        g = ktg[ML_WIDTH:, :] + gb_ref[...]
        gate = lax.broadcasted_iota(jnp.int32, g.shape, 0)
        is_forget = ((gate >= ML_HEADS) & (gate < 2 * ML_HEADS)) | (gate >= 3 * ML_HEADS)
        log_sig = jnp.minimum(g, 0.0) - jnp.log1p(jnp.exp(-jnp.abs(g)))
        g = jnp.where(is_forget, log_sig, g)
        gates_t = jnp.concatenate([g, jnp.zeros((LANES - N_GATES, TM), F32)], axis=0)
        og_ref[...] = gates_t.T


def _inproj(y, mod, gains, w_in_t, gate_b_cols, jb, l):
    n_main = len(PROJ_BLOCKS)
    assert PROJ_BLOCKS == tuple(b for b in range(n_main + 1) if b != KM_BLOCK)

    def w_block(i, n):
        m = jnp.minimum(n, n_main - 1)
        return (jb, m + (m >= KM_BLOCK).astype(jnp.int32), 0)

    cpt = TM // ML_CHUNK
    blocks = 4 * (TM * D_MODEL + 2 * TN_PROJ * D_MODEL + N_GATES * D_MODEL + TM * TN_PROJ
                  + ML_WIDTH * TM + TM * LANES)
    return pl.pallas_call(
        _inproj_kernel,
        grid=(N_TOK // TM, n_main + 1),
        in_specs=[
            pl.BlockSpec((TM, D_MODEL), lambda i, n: (i, 0)),
            *_cond_specs(l),
            pl.BlockSpec((None, TN_PROJ, D_MODEL), w_block),
            pl.BlockSpec((None, ML_WIDTH, D_MODEL), lambda i, n: (jb, KM_BLOCK, 0)),
            pl.BlockSpec((None, N_GATES, D_MODEL), lambda i, n: (jb, AB_MAIN // N_GATES, 0)),
            pl.BlockSpec((None, N_GATES, 1), lambda i, n: (jb, 0, 0)),
        ],
        out_specs=[
            pl.BlockSpec((TM, TN_PROJ), lambda i, n: (i, jnp.minimum(n, n_main - 1))),
            pl.BlockSpec((cpt, ML_WIDTH, ML_CHUNK), lambda i, n: (i, 0, 0)),
            pl.BlockSpec((TM, LANES), lambda i, n: (i, 0)),
        ],
        out_shape=[jax.ShapeDtypeStruct((N_TOK, n_main * TN_PROJ), F32),
                   jax.ShapeDtypeStruct((N_TOK // ML_CHUNK, ML_WIDTH, ML_CHUNK), F32),
                   jax.ShapeDtypeStruct((N_TOK, LANES), F32)],
        scratch_shapes=[pltpu.VMEM((TM, D_MODEL), BF16)],
        compiler_params=_params(("parallel", "arbitrary"), blocks, 2 * TM * D_MODEL),
        name=f"inproj_{jb}",
    )(y, mod, gains, w_in_t, w_in_t, w_in_t, gate_b_cols)


def _stack_entry(ref, jb, fills_stack):
    if not fills_stack:
        return ref
    for j in range(ref.shape[0]):
        if j != jb:
            ref[j] = jnp.zeros(ref.shape[1:], ref.dtype)
    return ref.at[jb]


def _stack_spec(entry_shape, jb, fills_stack, n_stack=N_AB, lead=None):
    zeros = (0,) * len(entry_shape)
    if fills_stack:
        return pl.BlockSpec((lead, n_stack) + entry_shape, lambda b: (b, 0) + zeros)
    return pl.BlockSpec((lead, None) + entry_shape, lambda b: (b, jb) + zeros)


def _ctx_attn_kernel(q_ref, k_ref, v_ref, *rest, jb, fills_stack, spb):
    o_ref = rest[-3]
    scale = NA_HEAD_DIM ** -0.5
    lane = lax.broadcasted_iota(jnp.int32, (SEQ, LANES), 1)
    ones = jnp.ones((SEQ, LANES), BF16)
    for s in range(spb):
        rows = slice(s * SEQ, (s + 1) * SEQ)
        kc_ref = _stack_entry(rest[-2].at[s], jb, fills_stack)
        vc_ref = _stack_entry(rest[-1].at[s], jb, fills_stack)
        k_t = k_ref[rows, :].T
        v_t = v_ref[rows, :].T
        for h in range(NA_HEADS):
            sl = slice(h * NA_HEAD_DIM, (h + 1) * NA_HEAD_DIM)
            kc_ref[h] = k_t[sl, :]
            vc_ref[h] = v_t[sl, :]
        for hp in range(NA_WIDTH // LANES):
            pair = slice(hp * LANES, (hp + 1) * LANES)
            q_pair = q_ref[rows, pair] * scale
            v_aug = jnp.concatenate([v_ref[rows, pair].astype(BF16), ones], axis=1)
            outs = []
            for hh in range(LANES // NA_HEAD_DIM):
                own = (lane >= hh * NA_HEAD_DIM) & (lane < (hh + 1) * NA_HEAD_DIM)
                logits = _dot(jnp.where(own, q_pair, 0.0), k_t[pair, :])
                r = _dot(jnp.exp(logits - jnp.max(logits, axis=-1, keepdims=True)), v_aug)
                outs.append(r[:, :LANES] / r[:, LANES:])
            o_ref[rows, pair] = jnp.where(lane < NA_HEAD_DIM, outs[0], outs[1]).astype(o_ref.dtype)


def _ctx_attention(proj, jb, prev):
    spb = CTX_SEQS_PER_STEP
    rows = spb * SEQ
    cache = jax.ShapeDtypeStruct((BATCH, N_AB, NA_HEADS, NA_HEAD_DIM, SEQ), F32)
    cache_spec = _stack_spec((NA_HEADS, NA_HEAD_DIM, SEQ), jb, prev is None, lead=spb)
    in_specs = [pl.BlockSpec((rows, NA_WIDTH), lambda b, c=c: (b, c)) for c in (P_QA, P_KA, P_VA)]
    args = [proj, proj, proj]
    aliases = {}
    if prev is not None:
        in_specs += [pl.BlockSpec(memory_space=pl.ANY)] * 2
        args += list(prev)
        aliases = {3: 1, 4: 2}
    blocks = 4 * (4 * rows * NA_WIDTH + 2 * N_AB * NA_WIDTH * rows)
    return pl.pallas_call(
        functools.partial(_ctx_attn_kernel, jb=jb, fills_stack=prev is None, spb=spb),
        grid=(BATCH // spb,),
        in_specs=in_specs,
        out_specs=[pl.BlockSpec((rows, NA_WIDTH), lambda b: (b, 0)), cache_spec, cache_spec],
        out_shape=[jax.ShapeDtypeStruct((N_PROMPT, NA_WIDTH), BF16), cache, cache],
        input_output_aliases=aliases,
        compiler_params=_params(("parallel",), blocks),
        name=f"ctx_attention_{jb}",
    )(*args)


def _na_bias_kernel(rpb_ref, p_ref):
    layer = pl.program_id(0)
    head = pl.program_id(1)
    shape = (GRID_W, LANES)
    lane = lax.broadcasted_iota(jnp.int32, shape, 1)
    qc = lax.broadcasted_iota(jnp.int32, shape, 0)
    kc = lane & (GRID_W - 1)
    upper = lane >= GRID_W
    cs = jnp.clip(qc - NA_WIN_W // 2, 0, GRID_W - NA_WIN_W)
    col_in = (kc >= cs) & (kc < cs + NA_WIN_W)
    j = lax.broadcasted_iota(jnp.int32, (1, LANES), 1)
    dcol = jnp.clip(j - (GRID_W - 1), -(NA_WIN_W - 1), NA_WIN_W - 1) + (NA_WIN_W - 1)

    def toeplitz(dr, half):
        b = jnp.zeros((1, LANES), F32)
        for d in range(N_DC):
            b = jnp.where(dcol == d, rpb_ref[layer, head, dr, d], b)
        shift = (LANES - (GRID_W - 1) + half * GRID_W) % LANES
        return pltpu.roll(jnp.broadcast_to(b, shape), shift, 1, stride=1, stride_axis=0)

    blocks = [[toeplitz(dr, half) for half in range(2)] for dr in range(N_DR)]
    for pair in range(N_PAIR):
        both = jnp.where(upper, blocks[pair + 1][1], blocks[pair][0])
        p_ref[pair] = jnp.where(col_in, both, -jnp.inf)


def _na_bias(na_rpb):
    return pl.pallas_call(
        _na_bias_kernel,
        grid=(N_AB, NA_HEADS),
        in_specs=[pl.BlockSpec(memory_space=pltpu.SMEM)],
        out_specs=pl.BlockSpec((None, None, N_PAIR, GRID_W, LANES), lambda j, h: (j, h, 0, 0, 0)),
        out_shape=jax.ShapeDtypeStruct((N_AB, NA_HEADS, N_PAIR, GRID_W, LANES), F32),
        compiler_params=_params(("parallel", "parallel"), 4 * N_PAIR * GRID_W * LANES),
        name="na_bias",
    )(na_rpb)


def _na_window_start(r):
    return min(max(r - NA_KH // 2, 0), GRID_ROWS - NA_KH)


def _na_bias_block(p_ref, hh, qr, kr):
    rs = _na_window_start(qr)
    lo_ok = rs <= kr < rs + NA_KH
    hi_ok = rs <= kr + 1 < rs + NA_KH
    if not (lo_ok or hi_ok):
        return jnp.full((GRID_W, LANES), -jnp.inf, F32)
    dr = kr - qr + (NA_WIN_H - 1)
    assert 0 <= dr < N_PAIR
    blk = p_ref[hh, dr]
    if lo_ok and hi_ok:
        return blk
    lane = lax.broadcasted_iota(jnp.int32, (GRID_W, LANES), 1)
    keep = (lane < GRID_W) if lo_ok else (lane >= GRID_W)
    return jnp.where(keep, blk, -jnp.inf)


def _na_kernel(q_ref, k_ref, v_ref, kc_ref, vc_ref, p_ref, o_ref):
    scale = NA_HEAD_DIM ** -0.5
    n_q = NA_QROWS * GRID_W
    lane = lax.broadcasted_iota(jnp.int32, (n_q, LANES), 1)
    k_ctx_t = kc_ref[...].reshape(LANES, PAST_LEN)
    v_ctx_t = vc_ref[...].reshape(LANES, PAST_LEN)
    v_aug_ctx_t = jnp.concatenate([v_ctx_t.astype(BF16), jnp.ones((LANES, PAST_LEN), BF16)], axis=0)
    for q0 in range(0, GRID_ROWS, NA_QROWS):
        k_lo = _na_window_start(q0) // 2 * 2
        k_hi = -(-(_na_window_start(q0 + NA_QROWS - 1) + NA_KH) // 2) * 2
        q_rows = slice(q0 * GRID_W, (q0 + NA_QROWS) * GRID_W)
        k_rows = slice(k_lo * GRID_W, k_hi * GRID_W)
        q_pair = q_ref[q_rows, :] * scale
        n_k = (k_hi - k_lo) * GRID_W
        v_aug_loc = jnp.concatenate([v_ref[k_rows, :].astype(BF16), jnp.ones((n_k, LANES), BF16)], axis=1)
        outs = []
        for hh in range(LANES // NA_HEAD_DIM):
            bias = jnp.concatenate(
                [jnp.concatenate([_na_bias_block(p_ref, hh, qr, kr) for kr in range(k_lo, k_hi, 2)], axis=1)
                 for qr in range(q0, q0 + NA_QROWS)], axis=0)
            own = (lane >= hh * NA_HEAD_DIM) & (lane < (hh + 1) * NA_HEAD_DIM)
            q = jnp.where(own, q_pair, 0.0)
            s_loc = _dot_nt(q, k_ref[k_rows, :]) + bias
            s_ctx = _dot(q, k_ctx_t)
            m = jnp.maximum(jnp.max(s_loc, axis=-1, keepdims=True), jnp.max(s_ctx, axis=-1, keepdims=True))
            r = _dot(jnp.exp(s_loc - m), v_aug_loc) + _dot_nt(jnp.exp(s_ctx - m), v_aug_ctx_t)
            outs.append(r[:, :LANES] / r[:, LANES:])
        o_ref[q_rows, :] = jnp.where(lane < NA_HEAD_DIM, outs[0], outs[1]).astype(o_ref.dtype)


def _na_attention(proj, cache_k_t, cache_v_t, bias_pairs, jb):
    hp = LANES // NA_HEAD_DIM
    s0 = N_PROMPT // DEC_SEQ
    nb = NA_WIDTH // LANES
    ctx_spec = pl.BlockSpec((None, None, hp, NA_HEAD_DIM, PAST_LEN), lambda b, c: (b, jb, c, 0, 0))
    blocks = 4 * (4 * DEC_SEQ * LANES + 2 * hp * NA_HEAD_DIM * PAST_LEN + hp * N_PAIR * GRID_W * LANES)
    return pl.pallas_call(
        _na_kernel,
        grid=(DEC_BATCH, nb),
        in_specs=[
            pl.BlockSpec((DEC_SEQ, LANES), lambda b, c: (s0 + b, P_QA * nb + c)),
            pl.BlockSpec((DEC_SEQ, LANES), lambda b, c: (s0 + b, P_KA * nb + c)),
            pl.BlockSpec((DEC_SEQ, LANES), lambda b, c: (s0 + b, P_VA * nb + c)),
            ctx_spec, ctx_spec,
            pl.BlockSpec((None, hp, N_PAIR, GRID_W, LANES), lambda b, c: (jb, c, 0, 0, 0)),
        ],
        out_specs=pl.BlockSpec((DEC_SEQ, LANES), lambda b, c: (b, c)),
        out_shape=jax.ShapeDtypeStruct((N_SAMPLE, NA_WIDTH), BF16),
        compiler_params=_params(("parallel", "parallel"), blocks),
        name=f"na_attention_{jb}",
    )(proj, proj, proj, cache_k_t, cache_v_t, bias_pairs)


def _running_rows(x, op, identity, reverse):
    n = x.shape[0]
    ridx = lax.broadcasted_iota(jnp.int32, x.shape, 0)
    sh = 1
    while sh < n:
        if reverse:
            x = op(x, jnp.where(ridx < n - sh, pltpu.roll(x, n - sh, 0), identity))
        else:
            x = op(x, jnp.where(ridx >= sh, pltpu.roll(x, sh, 0), identity))
        sh *= 2
    return x


def _mlstm_kernel(*refs, n_chunks, has_state, jb, fills_stack, spb, layer):
    if has_state:
        (q_ref, kt_ref, v_ref, o_ref, gt_ref, hn_ref, c0_ref, n0_ref, m0_ref,
         out_ref, cout_ref, nout_ref, mout_ref, hdir_ref, caug_ref) = refs
    else:
        (q_ref, kt_ref, v_ref, o_ref, gt_ref, hn_ref, *_unused,
         out_ref, cout_ref, nout_ref, mout_ref, hdir_ref, caug_ref) = refs
    cout_refs = [_stack_entry(cout_ref.at[s], jb, fills_stack) for s in range(spb)]
    nout_refs = [_stack_entry(nout_ref.at[s], jb, fills_stack) for s in range(spb)]
    mout_refs = [_stack_entry(mout_ref.at[s], jb, fills_stack) for s in range(spb)]
    T = ML_CHUNK
    dh = ML_HEAD_DIM
    seq_len = n_chunks * T
    b = pl.program_id(0)
    row = lax.broadcasted_iota(jnp.int32, (T, T), 0)
    col = lax.broadcasted_iota(jnp.int32, (T, T), 1)
    lane = lax.broadcasted_iota(jnp.int32, (T, LANES), 1)
    lane_row = lax.broadcasted_iota(jnp.int32, (1, LANES), 1)
    ones = jnp.ones((T, dh), F32)

    def gate_lanes(d, h):
        gi = 2 * d * ML_HEADS + h
        return gi, gi + ML_HEADS

    scans = [(s, d) for s in range(spb) for d in range(2)]
    m_init = []
    for s, d in scans:
        m_row = jnp.zeros((1, LANES), F32)
        for h in range(ML_HEADS):
            if has_state:
                n_bcast = jnp.broadcast_to(n0_ref[s, d, h:h + 1, :], (dh, dh)).T
                caug_ref[s, d, h] = jnp.concatenate([c0_ref[s, d, h], n_bcast], axis=1)
                m0 = m0_ref[b * spb + s, layer, d, h]
                m_row = jnp.where(lane_row == gate_lanes(d, h)[1], m0, m_row)
            else:
                caug_ref[s, d, h] = jnp.zeros((dh, 2 * dh), F32)
        m_init.append(m_row)

    def chunk_body(ci, m_rows):
        m_next = []
        for scan, (s, d) in enumerate(scans):
            causal = (col <= row) if d == 0 else (col >= row)
            last = T - 1 if d == 0 else 0
            c = ci if d == 0 else n_chunks - 1 - ci
            rows = pl.ds(pl.multiple_of(s * seq_len + c * T, T), T)
            m_prev = m_rows[scan]
            mine = (lane >= (2 * d + 1) * ML_HEADS) & (lane < (2 * d + 2) * ML_HEADS)
            g = gt_ref[rows, :]
            cum = _running_rows(g, jnp.add, 0.0, reverse=(d == 1))
            i_al = pltpu.roll(g, ML_HEADS, 1)
            cvec = jnp.where(mine, i_al - cum, 0.0)
            cum = jnp.where(mine, cum, 0.0)
            m_run = jnp.maximum(m_prev, _running_rows(cvec, jnp.maximum, -jnp.inf, reverse=(d == 1)))
            m_t = cum + m_run
            e_negm = jnp.exp(-m_t)
            m_run_last = m_run[last:last + 1, :]
            wi_last = jnp.exp(m_prev - m_run_last)
            cvec_t = cvec.T
            for h in range(ML_HEADS):
                _, gf = gate_lanes(d, h)
                hl = slice(h * dh, (h + 1) * dh)
                m_run_b = jnp.broadcast_to(m_run[:, gf:gf + 1], (T, T))
                e_negm_b = jnp.broadcast_to(e_negm[:, gf:gf + 1], (T, dh))
                c_row = cvec_t[gf:gf + 1, :]
                w_d = jnp.exp(jnp.where(causal, c_row - m_run_b, -jnp.inf))
                w_inter_b = jnp.exp(m_prev[:, gf:gf + 1] - m_run_b)
                qh = q_ref[rows, hl]
                kh_t = kt_ref[s * n_chunks + c, hl, :]
                v_aug = jnp.concatenate([v_ref[rows, hl], ones], axis=1)
                sc = _dot(qh, kh_t) * w_d
                c_aug = caug_ref[s, d, h]
                nd = jnp.concatenate([w_inter_b, w_inter_b], axis=1) * _dot(qh, c_aug) + _dot(sc, v_aug)
                hdir_ref[d, rows, hl] = nd[:, :dh] / jnp.maximum(jnp.abs(nd[:, dh:]), e_negm_b)
                kw_t = kh_t * jnp.exp(c_row - m_run_last[:, gf:gf + 1])
                caug_ref[s, d, h] = wi_last[:, gf:gf + 1] * c_aug + _dot(kw_t, v_aug)
            m_next.append(m_t[last:last + 1, :])
        return tuple(m_next)

    m_fin = lax.fori_loop(0, n_chunks, chunk_body, tuple(m_init), unroll=2)
    for scan, (s, d) in enumerate(scans):
        for h in range(ML_HEADS):
            c_aug = caug_ref[s, d, h]
            cout_refs[s][d, h] = c_aug[:, :dh]
            nout_refs[s][d, h:h + 1, :] = c_aug[:, dh:].T[0:1, :]
    for s in range(spb):
        m_heads = [pltpu.roll(m_fin[2 * s + d], LANES - gate_lanes(d, 0)[1], 1) for d in range(2)]
        mout_refs[s][...] = jnp.concatenate(m_heads + [jnp.zeros((COND_ROWS - 2, LANES), F32)], axis=0)

    for h in range(ML_HEADS):
        hl = slice(h * dh, (h + 1) * dh)
        h_sum = hdir_ref[0, :, hl] + hdir_ref[1, :, hl]
        gated = jax.nn.sigmoid(o_ref[:, hl]) * (_rms_scale(h_sum) * hn_ref[layer:layer + 1, hl])
        out_ref[:, hl] = gated.astype(out_ref.dtype)


def _mlstm(proj, keys_t, gates, hnorm_rows, layer, seq_len, n_seq, row0, spb, n_stack=1, jb=0,
           prev=None, state=None):
    has_state = state is not None
    assert n_seq % spb == 0 and row0 % (spb * seq_len) == 0
    r0 = row0 // (spb * seq_len)
    rows = spb * seq_len
    nb = TN_PROJ // ML_WIDTH
    n_chunks = seq_len // ML_CHUNK
    in_specs = [
        pl.BlockSpec((rows, ML_WIDTH), lambda b: (r0 + b, P_QM * nb)),
        pl.BlockSpec((spb * n_chunks, ML_WIDTH, ML_CHUNK), lambda b: (r0 + b, 0, 0)),
        pl.BlockSpec((rows, ML_WIDTH), lambda b: (r0 + b, P_VM * nb)),
        pl.BlockSpec((rows, ML_WIDTH), lambda b: (r0 + b, P_OM * nb)),
        pl.BlockSpec((rows, LANES), lambda b: (r0 + b, 0)),
        pl.BlockSpec((N_AB, ML_WIDTH), lambda b: (0, 0)),
    ]
    args = [proj, keys_t, proj, proj, gates, hnorm_rows]
    aliases = {}
    if has_state:
        st_c, st_n, st_m = state
        in_specs += [
            pl.BlockSpec((spb, None, 2, ML_HEADS, ML_HEAD_DIM, ML_HEAD_DIM), lambda b: (b, layer, 0, 0, 0, 0)),
            pl.BlockSpec((spb, None, 2, ML_HEADS, ML_HEAD_DIM), lambda b: (b, layer, 0, 0, 0)),
            pl.BlockSpec(memory_space=pltpu.SMEM),
        ]
        args += [st_c, st_n, st_m]
    elif prev is not None:
        aliases = {len(args) + k: 1 + k for k in range(3)}
        in_specs += [pl.BlockSpec(memory_space=pl.ANY)] * 3
        args += list(prev)
    fills_stack = prev is None
    c_bytes = 2 * ML_HEADS * ML_HEAD_DIM * ML_HEAD_DIM
    blocks = 4 * (5 * rows * ML_WIDTH + rows * LANES + spb * (1 + n_stack) * c_bytes)
    scratch = 4 * (2 * rows * ML_WIDTH + spb * 2 * c_bytes)
    return pl.pallas_call(
        functools.partial(_mlstm_kernel, n_chunks=n_chunks, has_state=has_state, jb=jb, fills_stack=fills_stack,
                          spb=spb, layer=layer),
        grid=(n_seq // spb,),
        in_specs=in_specs,
        out_specs=[
            pl.BlockSpec((rows, ML_WIDTH), lambda b: (b, 0)),
            _stack_spec((2, ML_HEADS, ML_HEAD_DIM, ML_HEAD_DIM), jb, fills_stack, n_stack, spb),
            _stack_spec((2, ML_HEADS, ML_HEAD_DIM), jb, fills_stack, n_stack, spb),
            _stack_spec((COND_ROWS, LANES), jb, fills_stack, n_stack, spb),
        ],
        out_shape=[
            jax.ShapeDtypeStruct((n_seq * seq_len, ML_WIDTH), BF16),
            jax.ShapeDtypeStruct((n_seq, n_stack, 2, ML_HEADS, ML_HEAD_DIM, ML_HEAD_DIM), F32),
            jax.ShapeDtypeStruct((n_seq, n_stack, 2, ML_HEADS, ML_HEAD_DIM), F32),
            jax.ShapeDtypeStruct((n_seq, n_stack, COND_ROWS, LANES), F32),
        ],
        scratch_shapes=[pltpu.VMEM((2, rows, ML_WIDTH), F32),
                        pltpu.VMEM((spb, 2, ML_HEADS, ML_HEAD_DIM, 2 * ML_HEAD_DIM), F32)],
        input_output_aliases=aliases,
        compiler_params=_params(("parallel",), blocks, scratch),
        name=f"mlstm_{'latent' if has_state else 'context'}_{jb}",
    )(*args)


def _outproj_kernel(x_ref, mod_ref, g_ref, ap_ref, mp_ref, as_ref, ms_ref, wa_ref, wm_ref, o_ref):
    i = pl.program_id(0)

    def finish(a_ref, m_ref):
        wa = wa_ref[...].astype(BF16)
        wm = wm_ref[...].astype(BF16)
        for rows, r in _sub_rows(i, TM_OUT):
            out = _dot(a_ref[rows, :], wa) + _dot(m_ref[rows, :], wm)
            o_ref[rows, :] = _gated_residual(x_ref[rows, :], out, g_ref, mod_ref, r, 1, 1.0)

    is_prompt = i < N_PROMPT // TM_OUT
    pl.when(is_prompt)(functools.partial(finish, ap_ref, mp_ref))
    pl.when(jnp.logical_not(is_prompt))(functools.partial(finish, as_ref, ms_ref))


def _outproj(y, mod, gains, a_p, m_p, a_s, m_s, ab_w_out, jb, l):
    tm = TM_OUT
    n_p = N_PROMPT // tm
    n_s = N_SAMPLE // tm
    p_spec = pl.BlockSpec((tm, NA_WIDTH), lambda i: (jnp.minimum(i, n_p - 1), 0))
    s_spec = pl.BlockSpec((tm, NA_WIDTH), lambda i: (jnp.clip(i - n_p, 0, n_s - 1), 0))
    blocks = 4 * (2 * tm * D_MODEL + 4 * tm * NA_WIDTH + D_MODEL * D_MODEL)
    return pl.pallas_call(
        _outproj_kernel,
        grid=(N_TOK // tm,),
        in_specs=[
            pl.BlockSpec((tm, D_MODEL), lambda i: (i, 0)),
            *_cond_specs(l),
            p_spec, p_spec, s_spec, s_spec,
            pl.BlockSpec((None, NA_WIDTH, D_MODEL), lambda i: (jb, 0, 0)),
            pl.BlockSpec((None, ML_WIDTH, D_MODEL), lambda i: (jb, 1, 0)),
        ],
        out_specs=pl.BlockSpec((tm, D_MODEL), lambda i: (i, 0)),
        out_shape=jax.ShapeDtypeStruct((N_TOK, D_MODEL), F32),
        compiler_params=_params(("parallel",), blocks),
        name=f"outproj_{jb}",
    )(y, mod, gains, a_p, m_p, a_s, m_s, ab_w_out, ab_w_out)


def kernel(x_prompt, x_sample, cache_k, cache_v, state_C, state_n, state_m, c, c_ctx, ada_w, ada_b, norm_g,
           ffn_w13, ffn_w2, ab_w_in, ab_w_out, na_rpb, ml_gate_b, ml_hnorm, sc_w_in, sc_conv_w, sc_conv_b,
           sc_w_out):
    assert NA_WIDTH == ML_WIDTH == TN_PROJ and AB_MAIN % N_GATES == 0
    assert math.frexp(NA_HEAD_DIM ** -0.5)[0] == 0.5
    assert LANES == 2 * NA_HEAD_DIM
    assert N_PROMPT % TM == 0 and N_SAMPLE % TM == 0 and TM % COND_UNIT == 0
    cond = jnp.concatenate([c_ctx[None, :], c, jnp.zeros((COND_ROWS - 1 - DEC_BATCH, D_MODEL), F32)], axis=0)
    mod = _modulation(cond, ada_w, ada_b)
    gains = norm_g.reshape(DEPTH, N_GAINS, D_MODEL)
    bias_pairs = _na_bias(na_rpb)
    w_in_t = jnp.swapaxes(ab_w_in, 1, 2)
    cache_k_t = jnp.swapaxes(cache_k, 3, 4)
    cache_v_t = jnp.swapaxes(cache_v, 3, 4)
    gate_b_cols = ml_gate_b.reshape(N_AB, N_GATES, 1)
    hnorm_rows = ml_hnorm
    state = (state_C, state_n, state_m)

    kv_new = None
    st_new = None
    y = None
    y = (x_prompt.reshape(N_PROMPT, D_MODEL), x_sample.reshape(N_SAMPLE, D_MODEL))
    for l in range(DEPTH):
        y = _ffn_sublayer(y, mod, gains, ffn_w13, ffn_w2, l, 0)
        j = l // 2
        if l % 2 == 0:
            proj, keys_t, gates = _inproj(y, mod, gains, w_in_t, gate_b_cols, j, l)
            a_p, *kv_new = _ctx_attention(proj, j, kv_new)
            m_p, *st_new = _mlstm(proj, keys_t, gates, hnorm_rows, j, SEQ, BATCH, 0,
                                  ML_CTX_SEQS_PER_STEP, n_stack=N_AB, jb=j, prev=st_new)
            a_s = _na_attention(proj, cache_k_t, cache_v_t, bias_pairs, j)
            m_s, _, _, _ = _mlstm(proj, keys_t, gates, hnorm_rows, j, DEC_SEQ, DEC_BATCH, N_PROMPT,
                                  ML_SEQS_PER_STEP, state=state)
            y = _outproj(y, mod, gains, a_p, m_p, a_s, m_s, ab_w_out, j, l)
        else:
            y = _sconv_sublayer(y, mod, gains, sc_w_in, sc_conv_w, sc_conv_b, sc_w_out, j, l)
        if l < DEPTH - 1:
            y = _ffn_sublayer(y, mod, gains, ffn_w13, ffn_w2, l, 1)

    y_p, y_s = _ffn_sublayer(y, mod, gains, ffn_w13, ffn_w2, DEPTH - 1, 1, split_out=True)
    y_p = y_p.reshape(BATCH, SEQ, D_MODEL)
    y_s = y_s.reshape(DEC_BATCH, DEC_SEQ, D_MODEL)
    new_c, new_n, m_rows = st_new
    new_m = m_rows[:, :, :2, :ML_HEADS]
    return (y_p, y_s, jnp.swapaxes(kv_new[0], 3, 4), jnp.swapaxes(kv_new[1], 3, 4), new_c, new_n, new_m)
```

```python
import functools
import math

import jax
import jax.numpy as jnp
from jax import lax
from jax.experimental import pallas as pl
from jax.experimental.pallas import tpu as pltpu

D_MODEL = 1024
BATCH = 16
SEQ = 256
DEPTH = 4
DEC_BATCH = 2
DEC_SEQ = 1024
PAST_LEN = 256
GRID_W = 64
NA_HEADS = 8
NA_HEAD_DIM = 64
NA_WIN_H = 8
NA_WIN_W = 16
ML_HEADS = 4
ML_HEAD_DIM = 128
ML_CHUNK = 128
CONV_WIDTH = 3
D_FF = 2816
N_MOD = 9
EPS = 1e-6
N_AB = (DEPTH + 1) // 2
N_C = DEPTH // 2
NA_WIDTH = NA_HEADS * NA_HEAD_DIM
ML_WIDTH = ML_HEADS * ML_HEAD_DIM
AB_MAIN = 3 * NA_WIDTH + 4 * ML_WIDTH
N_GATES = 4 * ML_HEADS
SC_WIDTH = D_MODEL
N_GAINS = 3 * 2

N_PROMPT = BATCH * SEQ
N_SAMPLE = DEC_BATCH * DEC_SEQ
N_TOK = N_PROMPT + N_SAMPLE
GRID_ROWS = DEC_SEQ // GRID_W
NA_KH = min(NA_WIN_H, GRID_ROWS)
N_DR = 2 * NA_WIN_H - 1
N_DC = 2 * NA_WIN_W - 1
N_PAIR = N_DR - 1

LANES = 128
COND_ROWS = 8
VMEM_CAP = 60 * 1024 * 1024

COND_UNIT = DEC_SEQ
TM = 2048
SUB_ROWS = 512
TM_OUT = 1024
TM_CONV = COND_UNIT
TF_CONV = 512
TF = 256
TN_PROJ = 512
TN_MOD = 3072
NA_QROWS = 8
CTX_SEQS_PER_STEP = 4
ML_SEQS_PER_STEP = 2

PROJ_BLOCKS = (0, 1, 2, 3, 5, 6)
KM_BLOCK = 4
P_QA, P_KA, P_VA, P_QM, P_VM, P_OM = range(6)

F32 = jnp.float32
BF16 = jnp.bfloat16


def _vmem_limit(block_bytes, scratch_bytes):
    assert 2 * block_bytes + scratch_bytes <= VMEM_CAP
    return VMEM_CAP


def _params(semantics, block_bytes, scratch_bytes=0):
    return pltpu.CompilerParams(dimension_semantics=semantics,
                                vmem_limit_bytes=_vmem_limit(block_bytes, scratch_bytes))


def _unit_rows(i, tm=TM):
    first_latent = N_PROMPT // COND_UNIT
    units = tm // COND_UNIT
    return [(slice(u * COND_UNIT, (u + 1) * COND_UNIT), jnp.maximum(i * units + u - (first_latent - 1), 0))
            for u in range(units)]


def _sub_rows(i, tm=TM):
    per_unit = COND_UNIT // SUB_ROWS
    return [(slice(rows.start + s * SUB_ROWS, rows.start + (s + 1) * SUB_ROWS), r)
            for rows, r in _unit_rows(i, tm) for s in range(per_unit)]


def _rms_scale(x):
    return x * lax.rsqrt(jnp.mean(x * x, axis=-1, keepdims=True) + EPS)


def _modulate_in(x, g_ref, mod_ref, r, j):
    gain = g_ref[2 * j:2 * j + 1, :] * (1.0 + mod_ref[r, 3 * j + 1:3 * j + 2, :])
    return _rms_scale(x) * gain + mod_ref[r, 3 * j:3 * j + 1, :]


def _gated_residual(x, out, g_ref, mod_ref, r, j, weight):
    gain = (weight * mod_ref[r, 3 * j + 2:3 * j + 3, :]) * g_ref[2 * j + 1:2 * j + 2, :]
    return x + _rms_scale(out) * gain


def _cond_specs(l):
    return [pl.BlockSpec((None, COND_ROWS, N_MOD, D_MODEL), lambda *_: (l, 0, 0, 0)),
            pl.BlockSpec((None, N_GAINS, D_MODEL), lambda *_: (l, 0, 0))]


def _dot(a, b):
    return jnp.dot(a.astype(BF16), b.astype(BF16), preferred_element_type=F32)


def _dot_nt(a, b):
    return lax.dot_general(a.astype(BF16), b.astype(BF16), (((1,), (1,)), ((), ())),
                           preferred_element_type=F32)


def _mod_kernel(c_ref, w_ref, b_ref, o_ref):
    c = c_ref[...]
    s = c * jax.nn.sigmoid(c)
    o_ref[...] = _dot(s, w_ref[...]) + b_ref[...]


def _modulation(cond, ada_w, ada_b):
    n = N_MOD * D_MODEL
    out = pl.pallas_call(
        _mod_kernel,
        grid=(DEPTH, n // TN_MOD),
        in_specs=[
            pl.BlockSpec((COND_ROWS, D_MODEL), lambda l, t: (0, 0)),
            pl.BlockSpec((None, D_MODEL, TN_MOD), lambda l, t: (l, 0, t)),
            pl.BlockSpec((None, 1, TN_MOD), lambda l, t: (l, 0, t)),
        ],
        out_specs=pl.BlockSpec((None, COND_ROWS, TN_MOD), lambda l, t: (l, 0, t)),
        out_shape=jax.ShapeDtypeStruct((DEPTH, COND_ROWS, n), F32),
        compiler_params=_params(("parallel", "parallel"), 4 * D_MODEL * TN_MOD),
        name="modulation",
    )(cond, ada_w, ada_b.reshape(DEPTH, 1, n))
    return out.reshape(DEPTH, COND_ROWS, N_MOD, D_MODEL)


def _ffn_kernel(*refs, j, n_x, n_o, n_head):
    i = pl.program_id(0)
    k = pl.program_id(1)
    last = pl.num_programs(1) - 1
    x_refs = refs[:n_x]
    mod_ref, g_ref, wa_ref, wg_ref, w2_ref = refs[n_x:n_x + 5]
    o_refs = refs[n_x + 5:n_x + 5 + n_o]
    h_ref = refs[-1]

    def per_part(fn):
        if n_x == 1 and n_o == 1:
            fn(x_refs[0], o_refs[0])
        else:
            pl.when(i < n_head)(functools.partial(fn, x_refs[0], o_refs[0]))
            pl.when(i >= n_head)(functools.partial(fn, x_refs[-1], o_refs[-1]))

    def hidden_step(o_ref, rows, init):
        h = h_ref[rows, :]
        a = _dot(h, wa_ref[...])
        g = _dot(h, wg_ref[...])
        d = _dot((g * jax.nn.sigmoid(g)) * a, w2_ref[...])
        if init:
            o_ref[rows, :] = d
        else:
            o_ref[rows, :] += d

    def first_step(x_ref, o_ref):
        for rows, r in _sub_rows(i):
            h_ref[rows, :] = _modulate_in(x_ref[rows, :], g_ref, mod_ref, r, j).astype(BF16)
            hidden_step(o_ref, rows, init=True)

    def middle_step(x_ref, o_ref):
        for rows, _ in _unit_rows(i):
            hidden_step(o_ref, rows, init=False)

    def last_step(x_ref, o_ref):
        for rows, r in _sub_rows(i):
            hidden_step(o_ref, rows, init=False)
            o_ref[rows, :] = _gated_residual(x_ref[rows, :], o_ref[rows, :], g_ref, mod_ref, r, j, 0.5)

    pl.when(k == 0)(functools.partial(per_part, first_step))
    pl.when((k > 0) & (k < last))(functools.partial(per_part, middle_step))
    pl.when(k == last)(functools.partial(per_part, last_step))


def _ffn_sublayer(ys, mod, gains, ffn_w13, ffn_w2, l, s, split_out=False):
    j = 2 * s
    nk = D_FF // TF
    n_tiles = N_TOK // TM
    n_head = N_PROMPT // TM
    ys = ys if isinstance(ys, (tuple, list)) else [ys]
    head_spec = pl.BlockSpec((TM, D_MODEL), lambda i, k: (jnp.minimum(i, n_head - 1), 0))
    tail_spec = pl.BlockSpec((TM, D_MODEL), lambda i, k: (jnp.maximum(i - n_head, 0), 0))
    whole_spec = pl.BlockSpec((TM, D_MODEL), lambda i, k: (i, 0))
    x_specs = [whole_spec] if len(ys) == 1 else [head_spec, tail_spec]
    if split_out:
        out_specs = [head_spec, tail_spec]
        out_shape = [jax.ShapeDtypeStruct((n_head * TM, D_MODEL), F32),
                     jax.ShapeDtypeStruct(((n_tiles - n_head) * TM, D_MODEL), F32)]
    else:
        out_specs = [whole_spec]
        out_shape = [jax.ShapeDtypeStruct((N_TOK, D_MODEL), F32)]
    blocks = 4 * ((len(x_specs) + len(out_specs)) * TM * D_MODEL + 3 * D_MODEL * TF)
    scratch = 2 * TM * D_MODEL
    out = pl.pallas_call(
        functools.partial(_ffn_kernel, j=j, n_x=len(x_specs), n_o=len(out_specs), n_head=n_head),
        grid=(n_tiles, nk),
        in_specs=[
            *x_specs,
            *_cond_specs(l),
            pl.BlockSpec((None, None, D_MODEL, TF), lambda i, k: (l, s, 0, k)),
            pl.BlockSpec((None, None, D_MODEL, TF), lambda i, k: (l, s, 0, k + nk)),
            pl.BlockSpec((None, None, TF, D_MODEL), lambda i, k: (l, s, k, 0)),
        ],
        out_specs=out_specs,
        out_shape=out_shape,
        scratch_shapes=[pltpu.VMEM((TM, D_MODEL), BF16)],
        compiler_params=_params(("arbitrary" if split_out else "parallel", "arbitrary"), blocks, scratch),
        name=f"ffn_l{l}_s{s}",
    )(*ys, mod, gains, ffn_w13, ffn_w13, ffn_w2)
    return out if split_out else out[0]


def _sconv_kernel(x_ref, mod_ref, g_ref, wb_ref, wc_ref, wx_ref, cw_ref, cb_ref, wo_ref, o_ref, h_ref, *, jc):
    i = pl.program_id(0)
    k = pl.program_id(1)
    ((_, r),) = _unit_rows(i, TM_CONV)
    seq = jnp.where(i < N_PROMPT // TM_CONV, SEQ, DEC_SEQ)
    blk = SEQ
    n_blk = TM_CONV // blk
    row = lax.broadcasted_iota(jnp.int32, (blk, TF_CONV), 0)
    zero_row = jnp.zeros((1, TF_CONV), F32)

    def step(first, last):
        wb = wb_ref[...].astype(BF16)
        wc = wc_ref[...].astype(BF16)
        wx = wx_ref[...].astype(BF16)
        wo = wo_ref[...].astype(BF16)
        gbs, us = [], []

        def project(bi):
            rows = slice(bi * blk, (bi + 1) * blk)
            if first:
                h_ref[rows, :] = _modulate_in(x_ref[rows, :], g_ref, mod_ref, r, 1).astype(BF16)
            h = h_ref[rows, :]
            gbs.append(_dot(h, wb))
            us.append(_dot(h, wc) * _dot(h, wx))

        project(0)
        for bi in range(n_blk):
            if bi + 1 < n_blk:
                project(bi + 1)
            rows = slice(bi * blk, (bi + 1) * blk)
            ub = us[bi]
            t = (row + bi * blk) & (seq - 1)
            above = us[bi - 1][blk - 1:blk, :] if bi > 0 else zero_row
            below = us[bi + 1][0:1, :] if bi < n_blk - 1 else zero_row
            u_prev = jnp.where(t == 0, 0.0, jnp.where(row == 0, above, pltpu.roll(ub, 1, 0)))
            u_next = jnp.where(t == seq - 1, 0.0, jnp.where(row == blk - 1, below, pltpu.roll(ub, blk - 1, 0)))
            y = cb_ref[jc:jc + 1, :] + u_prev * cw_ref[0:1, :] + ub * cw_ref[1:2, :] + u_next * cw_ref[2:3, :]
            d = _dot(gbs[bi] * y, wo)
            acc = d if first else o_ref[rows, :] + d
            o_ref[rows, :] = _gated_residual(x_ref[rows, :], acc, g_ref, mod_ref, r, 1, 1.0) if last else acc

    n_steps = SC_WIDTH // TF_CONV
    for kk in range(n_steps):
        pl.when(k == kk)(functools.partial(step, kk == 0, kk == n_steps - 1))


def _sconv_sublayer(y, mod, gains, sc_w_in, sc_conv_w, sc_conv_b, sc_w_out, jc, l):
    tm, tf = TM_CONV, TF_CONV
    nk = SC_WIDTH // tf
    blocks = 4 * (2 * tm * D_MODEL + 4 * D_MODEL * tf)
    scratch = 2 * tm * D_MODEL
    return pl.pallas_call(
        functools.partial(_sconv_kernel, jc=jc),
        grid=(N_TOK // tm, nk),
        in_specs=[
            pl.BlockSpec((tm, D_MODEL), lambda i, k: (i, 0)),
            *_cond_specs(l),
            pl.BlockSpec((None, D_MODEL, tf), lambda i, k: (jc, 0, k)),
            pl.BlockSpec((None, D_MODEL, tf), lambda i, k: (jc, 0, k + nk)),
            pl.BlockSpec((None, D_MODEL, tf), lambda i, k: (jc, 0, k + 2 * nk)),
            pl.BlockSpec((None, CONV_WIDTH, tf), lambda i, k: (jc, 0, k)),
            pl.BlockSpec((N_C, tf), lambda i, k: (0, k)),
            pl.BlockSpec((None, tf, D_MODEL), lambda i, k: (jc, k, 0)),
        ],
        out_specs=pl.BlockSpec((tm, D_MODEL), lambda i, k: (i, 0)),
        out_shape=jax.ShapeDtypeStruct((N_TOK, D_MODEL), F32),
        scratch_shapes=[pltpu.VMEM((tm, D_MODEL), BF16)],
        compiler_params=_params(("parallel", "arbitrary"), blocks, scratch),
        name=f"sconv_{jc}",
    )(y, mod, gains, sc_w_in, sc_w_in, sc_w_in, sc_conv_w, sc_conv_b, sc_w_out)


def _inproj_kernel(x_ref, mod_ref, g_ref, w_ref, wk_ref, wg_ref, gb_ref, o_ref, okt_ref, og_ref, h_ref):
    i = pl.program_id(0)
    n = pl.program_id(1)
    n_main = len(PROJ_BLOCKS)

    @pl.when(n == 0)
    def _():
        for rows, r in _sub_rows(i):
            h_ref[rows, :] = _modulate_in(x_ref[rows, :], g_ref, mod_ref, r, 1).astype(BF16)
            o_ref[rows, :] = _dot_nt(h_ref[rows, :], w_ref[...])

    @pl.when((n > 0) & (n < n_main))
    def _():
        o_ref[...] = _dot_nt(h_ref[...], w_ref[...])

    @pl.when(n == n_main)
    def _():
        w_kg = jnp.concatenate([wk_ref[...], wg_ref[...]], axis=0)
        ktg = _dot_nt(w_kg, h_ref[...])
        k_scale = ML_HEAD_DIM ** -0.5
        for c in range(TM // ML_CHUNK):
            okt_ref[c] = ktg[:ML_WIDTH, c * ML_CHUNK:(c + 1) * ML_CHUNK] * k_scale
        g = ktg[ML_WIDTH:, :] + gb_ref[...]
        gate = lax.broadcasted_iota(jnp.int32, g.shape, 0)
        is_forget = ((gate >= ML_HEADS) & (gate < 2 * ML_HEADS)) | (gate >= 3 * ML_HEADS)
        log_sig = jnp.minimum(g, 0.0) - jnp.log1p(jnp.exp(-jnp.abs(g)))
        g = jnp.where(is_forget, log_sig, g)
        gates_t = jnp.concatenate([g, jnp.zeros((LANES - N_GATES, TM), F32)], axis=0)
        og_ref[...] = gates_t.T


def _inproj(y, mod, gains, w_in_t, gate_b_cols, jb, l):
    n_main = len(PROJ_BLOCKS)
    assert PROJ_BLOCKS == tuple(b for b in range(n_main + 1) if b != KM_BLOCK)

    def w_block(i, n):
        m = jnp.minimum(n, n_main - 1)
        return (jb, m + (m >= KM_BLOCK).astype(jnp.int32), 0)

    cpt = TM // ML_CHUNK
    blocks = 4 * (TM * D_MODEL + 2 * TN_PROJ * D_MODEL + N_GATES * D_MODEL + TM * TN_PROJ
                  + ML_WIDTH * TM + TM * LANES)
    return pl.pallas_call(
        _inproj_kernel,
        grid=(N_TOK // TM, n_main + 1),
        in_specs=[
            pl.BlockSpec((TM, D_MODEL), lambda i, n: (i, 0)),
            *_cond_specs(l),
            pl.BlockSpec((None, TN_PROJ, D_MODEL), w_block),
            pl.BlockSpec((None, ML_WIDTH, D_MODEL), lambda i, n: (jb, KM_BLOCK, 0)),
            pl.BlockSpec((None, N_GATES, D_MODEL), lambda i, n: (jb, AB_MAIN // N_GATES, 0)),
            pl.BlockSpec((None, N_GATES, 1), lambda i, n: (jb, 0, 0)),
        ],
        out_specs=[
            pl.BlockSpec((TM, TN_PROJ), lambda i, n: (i, jnp.minimum(n, n_main - 1))),
            pl.BlockSpec((cpt, ML_WIDTH, ML_CHUNK), lambda i, n: (i, 0, 0)),
            pl.BlockSpec((TM, LANES), lambda i, n: (i, 0)),
        ],
        out_shape=[jax.ShapeDtypeStruct((N_TOK, n_main * TN_PROJ), F32),
                   jax.ShapeDtypeStruct((N_TOK // ML_CHUNK, ML_WIDTH, ML_CHUNK), F32),
                   jax.ShapeDtypeStruct((N_TOK, LANES), F32)],
        scratch_shapes=[pltpu.VMEM((TM, D_MODEL), BF16)],
        compiler_params=_params(("parallel", "arbitrary"), blocks, 2 * TM * D_MODEL),
        name=f"inproj_{jb}",
    )(y, mod, gains, w_in_t, w_in_t, w_in_t, gate_b_cols)


def _stack_entry(ref, jb, fills_stack):
    if not fills_stack:
        return ref
    for j in range(ref.shape[0]):
        if j != jb:
            ref[j] = jnp.zeros(ref.shape[1:], ref.dtype)
    return ref.at[jb]


def _stack_spec(entry_shape, jb, fills_stack, n_stack=N_AB, lead=None):
    zeros = (0,) * len(entry_shape)
    if fills_stack:
        return pl.BlockSpec((lead, n_stack) + entry_shape, lambda b: (b, 0) + zeros)
    return pl.BlockSpec((lead, None) + entry_shape, lambda b: (b, jb) + zeros)


def _ctx_attn_kernel(q_ref, k_ref, v_ref, *rest, jb, fills_stack, spb):
    o_ref = rest[-3]
    scale = NA_HEAD_DIM ** -0.5
    lane = lax.broadcasted_iota(jnp.int32, (SEQ, LANES), 1)
    ones = jnp.ones((SEQ, LANES), BF16)
    for s in range(spb):
        rows = slice(s * SEQ, (s + 1) * SEQ)
        kc_ref = _stack_entry(rest[-2].at[s], jb, fills_stack)
        vc_ref = _stack_entry(rest[-1].at[s], jb, fills_stack)
        k_t = k_ref[rows, :].T
        v_t = v_ref[rows, :].T
        for h in range(NA_HEADS):
            sl = slice(h * NA_HEAD_DIM, (h + 1) * NA_HEAD_DIM)
            kc_ref[h] = k_t[sl, :]
            vc_ref[h] = v_t[sl, :]
        for hp in range(NA_WIDTH // LANES):
            pair = slice(hp * LANES, (hp + 1) * LANES)
            q_pair = q_ref[rows, pair] * scale
            v_aug = jnp.concatenate([v_ref[rows, pair].astype(BF16), ones], axis=1)
            outs = []
            for hh in range(LANES // NA_HEAD_DIM):
                own = (lane >= hh * NA_HEAD_DIM) & (lane < (hh + 1) * NA_HEAD_DIM)
                logits = _dot(jnp.where(own, q_pair, 0.0), k_t[pair, :])
                r = _dot(jnp.exp(logits - jnp.max(logits, axis=-1, keepdims=True)), v_aug)
                outs.append(r[:, :LANES] / r[:, LANES:])
            o_ref[rows, pair] = jnp.where(lane < NA_HEAD_DIM, outs[0], outs[1]).astype(o_ref.dtype)


def _ctx_attention(proj, jb, prev):
    spb = CTX_SEQS_PER_STEP
    rows = spb * SEQ
    cache = jax.ShapeDtypeStruct((BATCH, N_AB, NA_HEADS, NA_HEAD_DIM, SEQ), F32)
    cache_spec = _stack_spec((NA_HEADS, NA_HEAD_DIM, SEQ), jb, prev is None, lead=spb)
    in_specs = [pl.BlockSpec((rows, NA_WIDTH), lambda b, c=c: (b, c)) for c in (P_QA, P_KA, P_VA)]
    args = [proj, proj, proj]
    aliases = {}
    if prev is not None:
        in_specs += [pl.BlockSpec(memory_space=pl.ANY)] * 2
        args += list(prev)
        aliases = {3: 1, 4: 2}
    blocks = 4 * (4 * rows * NA_WIDTH + 2 * N_AB * NA_WIDTH * rows)
    return pl.pallas_call(
        functools.partial(_ctx_attn_kernel, jb=jb, fills_stack=prev is None, spb=spb),
        grid=(BATCH // spb,),
        in_specs=in_specs,
        out_specs=[pl.BlockSpec((rows, NA_WIDTH), lambda b: (b, 0)), cache_spec, cache_spec],
        out_shape=[jax.ShapeDtypeStruct((N_PROMPT, NA_WIDTH), BF16), cache, cache],
        input_output_aliases=aliases,
        compiler_params=_params(("parallel",), blocks),
        name=f"ctx_attention_{jb}",
    )(*args)


def _na_bias_kernel(rpb_ref, p_ref):
    layer = pl.program_id(0)
    head = pl.program_id(1)
    shape = (GRID_W, LANES)
    lane = lax.broadcasted_iota(jnp.int32, shape, 1)
    qc = lax.broadcasted_iota(jnp.int32, shape, 0)
    kc = lane & (GRID_W - 1)
    upper = lane >= GRID_W
    cs = jnp.clip(qc - NA_WIN_W // 2, 0, GRID_W - NA_WIN_W)
    col_in = (kc >= cs) & (kc < cs + NA_WIN_W)
    j = lax.broadcasted_iota(jnp.int32, (1, LANES), 1)
    dcol = jnp.clip(j - (GRID_W - 1), -(NA_WIN_W - 1), NA_WIN_W - 1) + (NA_WIN_W - 1)

    def toeplitz(dr, half):
        b = jnp.zeros((1, LANES), F32)
        for d in range(N_DC):
            b = jnp.where(dcol == d, rpb_ref[layer, head, dr, d], b)
        shift = (LANES - (GRID_W - 1) + half * GRID_W) % LANES
        return pltpu.roll(jnp.broadcast_to(b, shape), shift, 1, stride=1, stride_axis=0)

    blocks = [[toeplitz(dr, half) for half in range(2)] for dr in range(N_DR)]
    for pair in range(N_PAIR):
        both = jnp.where(upper, blocks[pair + 1][1], blocks[pair][0])
        p_ref[pair] = jnp.where(col_in, both, -jnp.inf)


def _na_bias(na_rpb):
    return pl.pallas_call(
        _na_bias_kernel,
        grid=(N_AB, NA_HEADS),
        in_specs=[pl.BlockSpec(memory_space=pltpu.SMEM)],
        out_specs=pl.BlockSpec((None, None, N_PAIR, GRID_W, LANES), lambda j, h: (j, h, 0, 0, 0)),
        out_shape=jax.ShapeDtypeStruct((N_AB, NA_HEADS, N_PAIR, GRID_W, LANES), F32),
        compiler_params=_params(("parallel", "parallel"), 4 * N_PAIR * GRID_W * LANES),
        name="na_bias",
    )(na_rpb)


def _na_window_start(r):
    return min(max(r - NA_KH // 2, 0), GRID_ROWS - NA_KH)


def _na_bias_block(p_ref, hh, qr, kr):
    rs = _na_window_start(qr)
    lo_ok = rs <= kr < rs + NA_KH
    hi_ok = rs <= kr + 1 < rs + NA_KH
    if not (lo_ok or hi_ok):
        return jnp.full((GRID_W, LANES), -jnp.inf, F32)
    dr = kr - qr + (NA_WIN_H - 1)
    assert 0 <= dr < N_PAIR
    blk = p_ref[hh, dr]
    if lo_ok and hi_ok:
        return blk
    lane = lax.broadcasted_iota(jnp.int32, (GRID_W, LANES), 1)
    keep = (lane < GRID_W) if lo_ok else (lane >= GRID_W)
    return jnp.where(keep, blk, -jnp.inf)


def _na_kernel(q_ref, k_ref, v_ref, kc_ref, vc_ref, p_ref, o_ref):
    scale = NA_HEAD_DIM ** -0.5
    n_q = NA_QROWS * GRID_W
    lane = lax.broadcasted_iota(jnp.int32, (n_q, LANES), 1)
    k_ctx_t = kc_ref[...].reshape(LANES, PAST_LEN)
    v_ctx_t = vc_ref[...].reshape(LANES, PAST_LEN)
    v_aug_ctx_t = jnp.concatenate([v_ctx_t.astype(BF16), jnp.ones((LANES, PAST_LEN), BF16)], axis=0)
    for q0 in range(0, GRID_ROWS, NA_QROWS):
        k_lo = _na_window_start(q0) // 2 * 2
        k_hi = -(-(_na_window_start(q0 + NA_QROWS - 1) + NA_KH) // 2) * 2
        q_rows = slice(q0 * GRID_W, (q0 + NA_QROWS) * GRID_W)
        k_rows = slice(k_lo * GRID_W, k_hi * GRID_W)
        q_pair = q_ref[q_rows, :] * scale
        n_k = (k_hi - k_lo) * GRID_W
        v_aug_loc = jnp.concatenate([v_ref[k_rows, :].astype(BF16), jnp.ones((n_k, LANES), BF16)], axis=1)
        outs = []
        for hh in range(LANES // NA_HEAD_DIM):
            bias = jnp.concatenate(
                [jnp.concatenate([_na_bias_block(p_ref, hh, qr, kr) for kr in range(k_lo, k_hi, 2)], axis=1)
                 for qr in range(q0, q0 + NA_QROWS)], axis=0)
            own = (lane >= hh * NA_HEAD_DIM) & (lane < (hh + 1) * NA_HEAD_DIM)
            q = jnp.where(own, q_pair, 0.0)
            s_loc = _dot_nt(q, k_ref[k_rows, :]) + bias
            s_ctx = _dot(q, k_ctx_t)
            m = jnp.maximum(jnp.max(s_loc, axis=-1, keepdims=True), jnp.max(s_ctx, axis=-1, keepdims=True))
            r = _dot(jnp.exp(s_loc - m), v_aug_loc) + _dot_nt(jnp.exp(s_ctx - m), v_aug_ctx_t)
            outs.append(r[:, :LANES] / r[:, LANES:])
        o_ref[q_rows, :] = jnp.where(lane < NA_HEAD_DIM, outs[0], outs[1]).astype(o_ref.dtype)


def _na_attention(proj, cache_k_t, cache_v_t, bias_pairs, jb):
    hp = LANES // NA_HEAD_DIM
    s0 = N_PROMPT // DEC_SEQ
    nb = NA_WIDTH // LANES
    ctx_spec = pl.BlockSpec((None, None, hp, NA_HEAD_DIM, PAST_LEN), lambda b, c: (b, jb, c, 0, 0))
    blocks = 4 * (4 * DEC_SEQ * LANES + 2 * hp * NA_HEAD_DIM * PAST_LEN + hp * N_PAIR * GRID_W * LANES)
    return pl.pallas_call(
        _na_kernel,
        grid=(DEC_BATCH, nb),
        in_specs=[
            pl.BlockSpec((DEC_SEQ, LANES), lambda b, c: (s0 + b, P_QA * nb + c)),
            pl.BlockSpec((DEC_SEQ, LANES), lambda b, c: (s0 + b, P_KA * nb + c)),
            pl.BlockSpec((DEC_SEQ, LANES), lambda b, c: (s0 + b, P_VA * nb + c)),
            ctx_spec, ctx_spec,
            pl.BlockSpec((None, hp, N_PAIR, GRID_W, LANES), lambda b, c: (jb, c, 0, 0, 0)),
        ],
        out_specs=pl.BlockSpec((DEC_SEQ, LANES), lambda b, c: (b, c)),
        out_shape=jax.ShapeDtypeStruct((N_SAMPLE, NA_WIDTH), BF16),
        compiler_params=_params(("parallel", "parallel"), blocks),
        name=f"na_attention_{jb}",
    )(proj, proj, proj, cache_k_t, cache_v_t, bias_pairs)


def _running_rows(x, op, identity, reverse):
    n = x.shape[0]
    ridx = lax.broadcasted_iota(jnp.int32, x.shape, 0)
    sh = 1
    while sh < n:
        if reverse:
            x = op(x, jnp.where(ridx < n - sh, pltpu.roll(x, n - sh, 0), identity))
        else:
            x = op(x, jnp.where(ridx >= sh, pltpu.roll(x, sh, 0), identity))
        sh *= 2
    return x


def _mlstm_kernel(*refs, n_chunks, has_state, jb, fills_stack, spb, layer):
    if has_state:
        (q_ref, kt_ref, v_ref, o_ref, gt_ref, hn_ref, c0_ref, n0_ref, m0_ref,
         out_ref, cout_ref, nout_ref, mout_ref, hdir_ref, caug_ref) = refs
    else:
        (q_ref, kt_ref, v_ref, o_ref, gt_ref, hn_ref, *_unused,
         out_ref, cout_ref, nout_ref, mout_ref, hdir_ref, caug_ref) = refs
    cout_refs = [_stack_entry(cout_ref.at[s], jb, fills_stack) for s in range(spb)]
    nout_refs = [_stack_entry(nout_ref.at[s], jb, fills_stack) for s in range(spb)]
    mout_refs = [_stack_entry(mout_ref.at[s], jb, fills_stack) for s in range(spb)]
    T = ML_CHUNK
    dh = ML_HEAD_DIM
    seq_len = n_chunks * T
    b = pl.program_id(0)
    row = lax.broadcasted_iota(jnp.int32, (T, T), 0)
    col = lax.broadcasted_iota(jnp.int32, (T, T), 1)
    lane = lax.broadcasted_iota(jnp.int32, (T, LANES), 1)
    lane_row = lax.broadcasted_iota(jnp.int32, (1, LANES), 1)
    ones = jnp.ones((T, dh), F32)

    def gate_lanes(d, h):
        gi = 2 * d * ML_HEADS + h
        return gi, gi + ML_HEADS

    scans = [(s, d) for s in range(spb) for d in range(2)]
    m_init = []
    for s, d in scans:
        m_row = jnp.zeros((1, LANES), F32)
        for h in range(ML_HEADS):
            if has_state:
                n_bcast = jnp.broadcast_to(n0_ref[s, d, h:h + 1, :], (dh, dh)).T
                caug_ref[s, d, h] = jnp.concatenate([c0_ref[s, d, h], n_bcast], axis=1)
                m0 = m0_ref[b * spb + s, layer, d, h]
                m_row = jnp.where(lane_row == gate_lanes(d, h)[1], m0, m_row)
            else:
                caug_ref[s, d, h] = jnp.zeros((dh, 2 * dh), F32)
        m_init.append(m_row)

    def chunk_body(ci, m_rows):
        m_next = []
        for scan, (s, d) in enumerate(scans):
            causal = (col <= row) if d == 0 else (col >= row)
            last = T - 1 if d == 0 else 0
            c = ci if d == 0 else n_chunks - 1 - ci
            rows = pl.ds(pl.multiple_of(s * seq_len + c * T, T), T)
            m_prev = m_rows[scan]
            mine = (lane >= (2 * d + 1) * ML_HEADS) & (lane < (2 * d + 2) * ML_HEADS)
            g = gt_ref[rows, :]
            cum = _running_rows(g, jnp.add, 0.0, reverse=(d == 1))
            i_al = pltpu.roll(g, ML_HEADS, 1)
            cvec = jnp.where(mine, i_al - cum, 0.0)
            cum = jnp.where(mine, cum, 0.0)
            m_run = jnp.maximum(m_prev, _running_rows(cvec, jnp.maximum, -jnp.inf, reverse=(d == 1)))
            m_t = cum + m_run
            e_negm = jnp.exp(-m_t)
            m_run_last = m_run[last:last + 1, :]
            wi_last = jnp.exp(m_prev - m_run_last)
            cvec_t = cvec.T
            for h in range(ML_HEADS):
                _, gf = gate_lanes(d, h)
                hl = slice(h * dh, (h + 1) * dh)
                m_run_b = jnp.broadcast_to(m_run[:, gf:gf + 1], (T, T))
                e_negm_b = jnp.broadcast_to(e_negm[:, gf:gf + 1], (T, dh))
                c_row = cvec_t[gf:gf + 1, :]
                w_d = jnp.exp(jnp.where(causal, c_row - m_run_b, -jnp.inf))
                w_inter_b = jnp.exp(m_prev[:, gf:gf + 1] - m_run_b)
                qh = q_ref[rows, hl]
                kh_t = kt_ref[s * n_chunks + c, hl, :]
                v_aug = jnp.concatenate([v_ref[rows, hl], ones], axis=1)
                sc = _dot(qh, kh_t) * w_d
                c_aug = caug_ref[s, d, h]
                nd = jnp.concatenate([w_inter_b, w_inter_b], axis=1) * _dot(qh, c_aug) + _dot(sc, v_aug)
                hdir_ref[d, rows, hl] = nd[:, :dh] / jnp.maximum(jnp.abs(nd[:, dh:]), e_negm_b)
                kw_t = kh_t * jnp.exp(c_row - m_run_last[:, gf:gf + 1])
                caug_ref[s, d, h] = wi_last[:, gf:gf + 1] * c_aug + _dot(kw_t, v_aug)
            m_next.append(m_t[last:last + 1, :])
        return tuple(m_next)

    m_fin = lax.fori_loop(0, n_chunks, chunk_body, tuple(m_init), unroll=2)
    for scan, (s, d) in enumerate(scans):
        for h in range(ML_HEADS):
            c_aug = caug_ref[s, d, h]
            cout_refs[s][d, h] = c_aug[:, :dh]
            nout_refs[s][d, h:h + 1, :] = c_aug[:, dh:].T[0:1, :]
    for s in range(spb):
        m_heads = [pltpu.roll(m_fin[2 * s + d], LANES - gate_lanes(d, 0)[1], 1) for d in range(2)]
        mout_refs[s][...] = jnp.concatenate(m_heads + [jnp.zeros((COND_ROWS - 2, LANES), F32)], axis=0)

    for h in range(ML_HEADS):
        hl = slice(h * dh, (h + 1) * dh)
        h_sum = hdir_ref[0, :, hl] + hdir_ref[1, :, hl]
        gated = jax.nn.sigmoid(o_ref[:, hl]) * (_rms_scale(h_sum) * hn_ref[layer:layer + 1, hl])
        out_ref[:, hl] = gated.astype(out_ref.dtype)


def _mlstm(proj, keys_t, gates, hnorm_rows, layer, seq_len, n_seq, row0, spb, n_stack=1, jb=0,
           prev=None, state=None):
    has_state = state is not None
    assert n_seq % spb == 0 and row0 % (spb * seq_len) == 0
    r0 = row0 // (spb * seq_len)
    rows = spb * seq_len
    nb = TN_PROJ // ML_WIDTH
    n_chunks = seq_len // ML_CHUNK
    in_specs = [
        pl.BlockSpec((rows, ML_WIDTH), lambda b: (r0 + b, P_QM * nb)),
        pl.BlockSpec((spb * n_chunks, ML_WIDTH, ML_CHUNK), lambda b: (r0 + b, 0, 0)),
        pl.BlockSpec((rows, ML_WIDTH), lambda b: (r0 + b, P_VM * nb)),
        pl.BlockSpec((rows, ML_WIDTH), lambda b: (r0 + b, P_OM * nb)),
        pl.BlockSpec((rows, LANES), lambda b: (r0 + b, 0)),
        pl.BlockSpec((N_AB, ML_WIDTH), lambda b: (0, 0)),
    ]
    args = [proj, keys_t, proj, proj, gates, hnorm_rows]
    aliases = {}
    if has_state:
        st_c, st_n, st_m = state
        in_specs += [
            pl.BlockSpec((spb, None, 2, ML_HEADS, ML_HEAD_DIM, ML_HEAD_DIM), lambda b: (b, layer, 0, 0, 0, 0)),
            pl.BlockSpec((spb, None, 2, ML_HEADS, ML_HEAD_DIM), lambda b: (b, layer, 0, 0, 0)),
            pl.BlockSpec(memory_space=pltpu.SMEM),
        ]
        args += [st_c, st_n, st_m]
    elif prev is not None:
        aliases = {len(args) + k: 1 + k for k in range(3)}
        in_specs += [pl.BlockSpec(memory_space=pl.ANY)] * 3
        args += list(prev)
    fills_stack = prev is None
    c_bytes = 2 * ML_HEADS * ML_HEAD_DIM * ML_HEAD_DIM
    blocks = 4 * (5 * rows * ML_WIDTH + rows * LANES + spb * (1 + n_stack) * c_bytes)
    scratch = 4 * (2 * rows * ML_WIDTH + spb * 2 * c_bytes)
    return pl.pallas_call(
        functools.partial(_mlstm_kernel, n_chunks=n_chunks, has_state=has_state, jb=jb, fills_stack=fills_stack,
                          spb=spb, layer=layer),
        grid=(n_seq // spb,),
        in_specs=in_specs,
        out_specs=[
            pl.BlockSpec((rows, ML_WIDTH), lambda b: (b, 0)),
            _stack_spec((2, ML_HEADS, ML_HEAD_DIM, ML_HEAD_DIM), jb, fills_stack, n_stack, spb),
            _stack_spec((2, ML_HEADS, ML_HEAD_DIM), jb, fills_stack, n_stack, spb),
            _stack_spec((COND_ROWS, LANES), jb, fills_stack, n_stack, spb),
        ],
        out_shape=[
            jax.ShapeDtypeStruct((n_seq * seq_len, ML_WIDTH), BF16),
            jax.ShapeDtypeStruct((n_seq, n_stack, 2, ML_HEADS, ML_HEAD_DIM, ML_HEAD_DIM), F32),
            jax.ShapeDtypeStruct((n_seq, n_stack, 2, ML_HEADS, ML_HEAD_DIM), F32),
            jax.ShapeDtypeStruct((n_seq, n_stack, COND_ROWS, LANES), F32),
        ],
        scratch_shapes=[pltpu.VMEM((2, rows, ML_WIDTH), F32),
                        pltpu.VMEM((spb, 2, ML_HEADS, ML_HEAD_DIM, 2 * ML_HEAD_DIM), F32)],
        input_output_aliases=aliases,
        compiler_params=_params(("parallel",), blocks, scratch),
        name=f"mlstm_{'latent' if has_state else 'context'}_{jb}",
    )(*args)


def _outproj_kernel(x_ref, mod_ref, g_ref, ap_ref, mp_ref, as_ref, ms_ref, wa_ref, wm_ref, o_ref):
    i = pl.program_id(0)

    def finish(a_ref, m_ref):
        wa = wa_ref[...].astype(BF16)
        wm = wm_ref[...].astype(BF16)
        for rows, r in _sub_rows(i, TM_OUT):
            out = _dot(a_ref[rows, :], wa) + _dot(m_ref[rows, :], wm)
            o_ref[rows, :] = _gated_residual(x_ref[rows, :], out, g_ref, mod_ref, r, 1, 1.0)

    is_prompt = i < N_PROMPT // TM_OUT
    pl.when(is_prompt)(functools.partial(finish, ap_ref, mp_ref))
    pl.when(jnp.logical_not(is_prompt))(functools.partial(finish, as_ref, ms_ref))


def _outproj(y, mod, gains, a_p, m_p, a_s, m_s, ab_w_out, jb, l):
    tm = TM_OUT
    n_p = N_PROMPT // tm
    n_s = N_SAMPLE // tm
    p_spec = pl.BlockSpec((tm, NA_WIDTH), lambda i: (jnp.minimum(i, n_p - 1), 0))
    s_spec = pl.BlockSpec((tm, NA_WIDTH), lambda i: (jnp.clip(i - n_p, 0, n_s - 1), 0))
    blocks = 4 * (2 * tm * D_MODEL + 4 * tm * NA_WIDTH + D_MODEL * D_MODEL)
    return pl.pallas_call(
        _outproj_kernel,
        grid=(N_TOK // tm,),
        in_specs=[
            pl.BlockSpec((tm, D_MODEL), lambda i: (i, 0)),
            *_cond_specs(l),
            p_spec, p_spec, s_spec, s_spec,
            pl.BlockSpec((None, NA_WIDTH, D_MODEL), lambda i: (jb, 0, 0)),
            pl.BlockSpec((None, ML_WIDTH, D_MODEL), lambda i: (jb, 1, 0)),
        ],
        out_specs=pl.BlockSpec((tm, D_MODEL), lambda i: (i, 0)),
        out_shape=jax.ShapeDtypeStruct((N_TOK, D_MODEL), F32),
        compiler_params=_params(("parallel",), blocks),
        name=f"outproj_{jb}",
    )(y, mod, gains, a_p, m_p, a_s, m_s, ab_w_out, ab_w_out)


def kernel(x_prompt, x_sample, cache_k, cache_v, state_C, state_n, state_m, c, c_ctx, ada_w, ada_b, norm_g,
           ffn_w13, ffn_w2, ab_w_in, ab_w_out, na_rpb, ml_gate_b, ml_hnorm, sc_w_in, sc_conv_w, sc_conv_b,
           sc_w_out):
    assert NA_WIDTH == ML_WIDTH == TN_PROJ and AB_MAIN % N_GATES == 0
    assert math.frexp(NA_HEAD_DIM ** -0.5)[0] == 0.5
    assert LANES == 2 * NA_HEAD_DIM
    assert N_PROMPT % TM == 0 and N_SAMPLE % TM == 0 and TM % COND_UNIT == 0
    cond = jnp.concatenate([c_ctx[None, :], c, jnp.zeros((COND_ROWS - 1 - DEC_BATCH, D_MODEL), F32)], axis=0)
    mod = _modulation(cond, ada_w, ada_b)
    gains = norm_g.reshape(DEPTH, N_GAINS, D_MODEL)
    bias_pairs = _na_bias(na_rpb)
    w_in_t = jnp.swapaxes(ab_w_in, 1, 2)
    cache_k_t = jnp.swapaxes(cache_k, 3, 4)
    cache_v_t = jnp.swapaxes(cache_v, 3, 4)
    gate_b_cols = ml_gate_b.reshape(N_AB, N_GATES, 1)
    hnorm_rows = ml_hnorm
    state = (state_C, state_n, state_m)

    kv_new = None
    st_new = None
    y = None
    y = (x_prompt.reshape(N_PROMPT, D_MODEL), x_sample.reshape(N_SAMPLE, D_MODEL))
    for l in range(DEPTH):
        y = _ffn_sublayer(y, mod, gains, ffn_w13, ffn_w2, l, 0)
        j = l // 2
        if l % 2 == 0:
            proj, keys_t, gates = _inproj(y, mod, gains, w_in_t, gate_b_cols, j, l)
            a_p, *kv_new = _ctx_attention(proj, j, kv_new)
            m_p, *st_new = _mlstm(proj, keys_t, gates, hnorm_rows, j, SEQ, BATCH, 0,
                                  ML_SEQS_PER_STEP, n_stack=N_AB, jb=j, prev=st_new)
            a_s = _na_attention(proj, cache_k_t, cache_v_t, bias_pairs, j)
            m_s, _, _, _ = _mlstm(proj, keys_t, gates, hnorm_rows, j, DEC_SEQ, DEC_BATCH, N_PROMPT,
                                  ML_SEQS_PER_STEP, state=state)
            y = _outproj(y, mod, gains, a_p, m_p, a_s, m_s, ab_w_out, j, l)
        else:
            y = _sconv_sublayer(y, mod, gains, sc_w_in, sc_conv_w, sc_conv_b, sc_w_out, j, l)
        if l < DEPTH - 1:
            y = _ffn_sublayer(y, mod, gains, ffn_w13, ffn_w2, l, 1)

    y_p, y_s = _ffn_sublayer(y, mod, gains, ffn_w13, ffn_w2, DEPTH - 1, 1, split_out=True)
    y_p = y_p.reshape(BATCH, SEQ, D_MODEL)
    y_s = y_s.reshape(DEC_BATCH, DEC_SEQ, D_MODEL)
    new_c, new_n, m_rows = st_new
    new_m = m_rows[:, :, :2, :ML_HEADS]
    return (y_p, y_s, jnp.swapaxes(kv_new[0], 3, 4), jnp.swapaxes(kv_new[1], 3, 4), new_c, new_n, new_m)
```

```python
import functools
import math

import jax
import jax.numpy as jnp
from jax import lax
from jax.experimental import pallas as pl
from jax.experimental.pallas import tpu as pltpu

D_MODEL = 1024
BATCH = 16
SEQ = 256
DEPTH = 4
DEC_BATCH = 2
DEC_SEQ = 1024
PAST_LEN = 256
GRID_W = 64
NA_HEADS = 8
NA_HEAD_DIM = 64
NA_WIN_H = 8
NA_WIN_W = 16
ML_HEADS = 4
ML_HEAD_DIM = 128
ML_CHUNK = 128
CONV_WIDTH = 3
D_FF = 2816
N_MOD = 9
EPS = 1e-6
N_AB = (DEPTH + 1) // 2
N_C = DEPTH // 2
NA_WIDTH = NA_HEADS * NA_HEAD_DIM
ML_WIDTH = ML_HEADS * ML_HEAD_DIM
AB_MAIN = 3 * NA_WIDTH + 4 * ML_WIDTH
N_GATES = 4 * ML_HEADS
SC_WIDTH = D_MODEL
N_GAINS = 3 * 2

N_PROMPT = BATCH * SEQ
N_SAMPLE = DEC_BATCH * DEC_SEQ
N_TOK = N_PROMPT + N_SAMPLE
GRID_ROWS = DEC_SEQ // GRID_W
NA_KH = min(NA_WIN_H, GRID_ROWS)
N_DR = 2 * NA_WIN_H - 1
N_DC = 2 * NA_WIN_W - 1
N_PAIR = N_DR - 1

LANES = 128
COND_ROWS = 8
VMEM_CAP = 60 * 1024 * 1024

COND_UNIT = DEC_SEQ
TM = 2048
SUB_ROWS = 512
TM_OUT = 1024
TM_CONV = COND_UNIT
TF_CONV = 512
TF = 256
TN_PROJ = 512
TN_MOD = 1536
NA_QROWS = 8
CTX_SEQS_PER_STEP = 4
ML_SEQS_PER_STEP = 2

PROJ_BLOCKS = (0, 1, 2, 3, 5, 6)
KM_BLOCK = 4
P_QA, P_KA, P_VA, P_QM, P_VM, P_OM = range(6)

F32 = jnp.float32
BF16 = jnp.bfloat16


def _vmem_limit(block_bytes, scratch_bytes):
    assert 2 * block_bytes + scratch_bytes <= VMEM_CAP
    return VMEM_CAP


def _params(semantics, block_bytes, scratch_bytes=0):
    return pltpu.CompilerParams(dimension_semantics=semantics,
                                vmem_limit_bytes=_vmem_limit(block_bytes, scratch_bytes))


def _unit_rows(i, tm=TM):
    first_latent = N_PROMPT // COND_UNIT
    units = tm // COND_UNIT
    return [(slice(u * COND_UNIT, (u + 1) * COND_UNIT), jnp.maximum(i * units + u - (first_latent - 1), 0))
            for u in range(units)]


def _sub_rows(i, tm=TM):
    per_unit = COND_UNIT // SUB_ROWS
    return [(slice(rows.start + s * SUB_ROWS, rows.start + (s + 1) * SUB_ROWS), r)
            for rows, r in _unit_rows(i, tm) for s in range(per_unit)]


def _rms_scale(x):
    return x * lax.rsqrt(jnp.mean(x * x, axis=-1, keepdims=True) + EPS)


def _modulate_in(x, g_ref, mod_ref, r, j):
    gain = g_ref[2 * j:2 * j + 1, :] * (1.0 + mod_ref[r, 3 * j + 1:3 * j + 2, :])
    return _rms_scale(x) * gain + mod_ref[r, 3 * j:3 * j + 1, :]


def _gated_residual(x, out, g_ref, mod_ref, r, j, weight):
    gain = (weight * mod_ref[r, 3 * j + 2:3 * j + 3, :]) * g_ref[2 * j + 1:2 * j + 2, :]
    return x + _rms_scale(out) * gain


def _cond_specs(l):
    return [pl.BlockSpec((None, COND_ROWS, N_MOD, D_MODEL), lambda *_: (l, 0, 0, 0)),
            pl.BlockSpec((None, N_GAINS, D_MODEL), lambda *_: (l, 0, 0))]


def _dot(a, b):
    return jnp.dot(a.astype(BF16), b.astype(BF16), preferred_element_type=F32)


def _dot_nt(a, b):
    return lax.dot_general(a.astype(BF16), b.astype(BF16), (((1,), (1,)), ((), ())),
                           preferred_element_type=F32)


def _mod_kernel(c_ref, w_ref, b_ref, o_ref):
    c = c_ref[...]
    s = c * jax.nn.sigmoid(c)
    o_ref[...] = _dot(s, w_ref[...]) + b_ref[...]


def _modulation(cond, ada_w, ada_b):
    n = N_MOD * D_MODEL
    out = pl.pallas_call(
        _mod_kernel,
        grid=(DEPTH, n // TN_MOD),
        in_specs=[
            pl.BlockSpec((COND_ROWS, D_MODEL), lambda l, t: (0, 0)),
            pl.BlockSpec((None, D_MODEL, TN_MOD), lambda l, t: (l, 0, t)),
            pl.BlockSpec((None, 1, TN_MOD), lambda l, t: (l, 0, t)),
        ],
        out_specs=pl.BlockSpec((None, COND_ROWS, TN_MOD), lambda l, t: (l, 0, t)),
        out_shape=jax.ShapeDtypeStruct((DEPTH, COND_ROWS, n), F32),
        compiler_params=_params(("parallel", "parallel"), 4 * D_MODEL * TN_MOD),
        name="modulation",
    )(cond, ada_w, ada_b.reshape(DEPTH, 1, n))
    return out.reshape(DEPTH, COND_ROWS, N_MOD, D_MODEL)


def _ffn_kernel(*refs, j, n_x, n_o, n_head):
    i = pl.program_id(0)
    k = pl.program_id(1)
    last = pl.num_programs(1) - 1
    x_refs = refs[:n_x]
    mod_ref, g_ref, wa_ref, wg_ref, w2_ref = refs[n_x:n_x + 5]
    o_refs = refs[n_x + 5:n_x + 5 + n_o]
    h_ref = refs[-1]

    def per_part(fn):
        if n_x == 1 and n_o == 1:
            fn(x_refs[0], o_refs[0])
        else:
            pl.when(i < n_head)(functools.partial(fn, x_refs[0], o_refs[0]))
            pl.when(i >= n_head)(functools.partial(fn, x_refs[-1], o_refs[-1]))

    def hidden_step(o_ref, rows, init):
        h = h_ref[rows, :]
        a = _dot(h, wa_ref[...])
        g = _dot(h, wg_ref[...])
        d = _dot((g * jax.nn.sigmoid(g)) * a, w2_ref[...])
        if init:
            o_ref[rows, :] = d
        else:
            o_ref[rows, :] += d

    def first_step(x_ref, o_ref):
        for rows, r in _sub_rows(i):
            h_ref[rows, :] = _modulate_in(x_ref[rows, :], g_ref, mod_ref, r, j).astype(BF16)
            hidden_step(o_ref, rows, init=True)

    def middle_step(x_ref, o_ref):
        for rows, _ in _unit_rows(i):
            hidden_step(o_ref, rows, init=False)

    def last_step(x_ref, o_ref):
        for rows, r in _sub_rows(i):
            hidden_step(o_ref, rows, init=False)
            o_ref[rows, :] = _gated_residual(x_ref[rows, :], o_ref[rows, :], g_ref, mod_ref, r, j, 0.5)

    pl.when(k == 0)(functools.partial(per_part, first_step))
    pl.when((k > 0) & (k < last))(functools.partial(per_part, middle_step))
    pl.when(k == last)(functools.partial(per_part, last_step))


def _ffn_sublayer(ys, mod, gains, ffn_w13, ffn_w2, l, s, split_out=False):
    j = 2 * s
    nk = D_FF // TF
    n_tiles = N_TOK // TM
    n_head = N_PROMPT // TM
    ys = ys if isinstance(ys, (tuple, list)) else [ys]
    head_spec = pl.BlockSpec((TM, D_MODEL), lambda i, k: (jnp.minimum(i, n_head - 1), 0))
    tail_spec = pl.BlockSpec((TM, D_MODEL), lambda i, k: (jnp.maximum(i - n_head, 0), 0))
    whole_spec = pl.BlockSpec((TM, D_MODEL), lambda i, k: (i, 0))
    x_specs = [whole_spec] if len(ys) == 1 else [head_spec, tail_spec]
    if split_out:
        out_specs = [head_spec, tail_spec]
        out_shape = [jax.ShapeDtypeStruct((n_head * TM, D_MODEL), F32),
                     jax.ShapeDtypeStruct(((n_tiles - n_head) * TM, D_MODEL), F32)]
    else:
        out_specs = [whole_spec]
        out_shape = [jax.ShapeDtypeStruct((N_TOK, D_MODEL), F32)]
    blocks = 4 * ((len(x_specs) + len(out_specs)) * TM * D_MODEL + 3 * D_MODEL * TF)
    scratch = 2 * TM * D_MODEL
    out = pl.pallas_call(
        functools.partial(_ffn_kernel, j=j, n_x=len(x_specs), n_o=len(out_specs), n_head=n_head),
        grid=(n_tiles, nk),
        in_specs=[
            *x_specs,
            *_cond_specs(l),
            pl.BlockSpec((None, None, D_MODEL, TF), lambda i, k: (l, s, 0, k)),
            pl.BlockSpec((None, None, D_MODEL, TF), lambda i, k: (l, s, 0, k + nk)),
            pl.BlockSpec((None, None, TF, D_MODEL), lambda i, k: (l, s, k, 0)),
        ],
        out_specs=out_specs,
        out_shape=out_shape,
        scratch_shapes=[pltpu.VMEM((TM, D_MODEL), BF16)],
        compiler_params=_params(("arbitrary" if split_out else "parallel", "arbitrary"), blocks, scratch),
        name=f"ffn_l{l}_s{s}",
    )(*ys, mod, gains, ffn_w13, ffn_w13, ffn_w2)
    return out if split_out else out[0]


def _sconv_kernel(x_ref, mod_ref, g_ref, wb_ref, wc_ref, wx_ref, cw_ref, cb_ref, wo_ref, o_ref, h_ref, *, jc):
    i = pl.program_id(0)
    k = pl.program_id(1)
    ((_, r),) = _unit_rows(i, TM_CONV)
    seq = jnp.where(i < N_PROMPT // TM_CONV, SEQ, DEC_SEQ)
    blk = SEQ
    n_blk = TM_CONV // blk
    row = lax.broadcasted_iota(jnp.int32, (blk, TF_CONV), 0)
    zero_row = jnp.zeros((1, TF_CONV), F32)

    def step(first, last):
        wb = wb_ref[...].astype(BF16)
        wc = wc_ref[...].astype(BF16)
        wx = wx_ref[...].astype(BF16)
        wo = wo_ref[...].astype(BF16)
        gbs, us = [], []

        def project(bi):
            rows = slice(bi * blk, (bi + 1) * blk)
            if first:
                h_ref[rows, :] = _modulate_in(x_ref[rows, :], g_ref, mod_ref, r, 1).astype(BF16)
            h = h_ref[rows, :]
            gbs.append(_dot(h, wb))
            us.append(_dot(h, wc) * _dot(h, wx))

        project(0)
        for bi in range(n_blk):
            if bi + 1 < n_blk:
                project(bi + 1)
            rows = slice(bi * blk, (bi + 1) * blk)
            ub = us[bi]
            t = (row + bi * blk) & (seq - 1)
            above = us[bi - 1][blk - 1:blk, :] if bi > 0 else zero_row
            below = us[bi + 1][0:1, :] if bi < n_blk - 1 else zero_row
            u_prev = jnp.where(t == 0, 0.0, jnp.where(row == 0, above, pltpu.roll(ub, 1, 0)))
            u_next = jnp.where(t == seq - 1, 0.0, jnp.where(row == blk - 1, below, pltpu.roll(ub, blk - 1, 0)))
            y = cb_ref[jc:jc + 1, :] + u_prev * cw_ref[0:1, :] + ub * cw_ref[1:2, :] + u_next * cw_ref[2:3, :]
            d = _dot(gbs[bi] * y, wo)
            acc = d if first else o_ref[rows, :] + d
            o_ref[rows, :] = _gated_residual(x_ref[rows, :], acc, g_ref, mod_ref, r, 1, 1.0) if last else acc

    n_steps = SC_WIDTH // TF_CONV
    for kk in range(n_steps):
        pl.when(k == kk)(functools.partial(step, kk == 0, kk == n_steps - 1))


def _sconv_sublayer(y, mod, gains, sc_w_in, sc_conv_w, sc_conv_b, sc_w_out, jc, l):
    tm, tf = TM_CONV, TF_CONV
    nk = SC_WIDTH // tf
    blocks = 4 * (2 * tm * D_MODEL + 4 * D_MODEL * tf)
    scratch = 2 * tm * D_MODEL
    return pl.pallas_call(
        functools.partial(_sconv_kernel, jc=jc),
        grid=(N_TOK // tm, nk),
        in_specs=[
            pl.BlockSpec((tm, D_MODEL), lambda i, k: (i, 0)),
            *_cond_specs(l),
            pl.BlockSpec((None, D_MODEL, tf), lambda i, k: (jc, 0, k)),
            pl.BlockSpec((None, D_MODEL, tf), lambda i, k: (jc, 0, k + nk)),
            pl.BlockSpec((None, D_MODEL, tf), lambda i, k: (jc, 0, k + 2 * nk)),
            pl.BlockSpec((None, CONV_WIDTH, tf), lambda i, k: (jc, 0, k)),
            pl.BlockSpec((N_C, tf), lambda i, k: (0, k)),
            pl.BlockSpec((None, tf, D_MODEL), lambda i, k: (jc, k, 0)),
        ],
        out_specs=pl.BlockSpec((tm, D_MODEL), lambda i, k: (i, 0)),
        out_shape=jax.ShapeDtypeStruct((N_TOK, D_MODEL), F32),
        scratch_shapes=[pltpu.VMEM((tm, D_MODEL), BF16)],
        compiler_params=_params(("parallel", "arbitrary"), blocks, scratch),
        name=f"sconv_{jc}",
    )(y, mod, gains, sc_w_in, sc_w_in, sc_w_in, sc_conv_w, sc_conv_b, sc_w_out)


def _inproj_kernel(x_ref, mod_ref, g_ref, w_ref, wk_ref, wg_ref, gb_ref, o_ref, okt_ref, og_ref, h_ref):
    i = pl.program_id(0)
    n = pl.program_id(1)
    n_main = len(PROJ_BLOCKS)

    @pl.when(n == 0)
    def _():
        for rows, r in _sub_rows(i):
            h_ref[rows, :] = _modulate_in(x_ref[rows, :], g_ref, mod_ref, r, 1).astype(BF16)
            o_ref[rows, :] = _dot_nt(h_ref[rows, :], w_ref[...])

    @pl.when((n > 0) & (n < n_main))
    def _():
        o_ref[...] = _dot_nt(h_ref[...], w_ref[...])

    @pl.when(n == n_main)
    def _():
        w_kg = jnp.concatenate([wk_ref[...], wg_ref[...]], axis=0)
        ktg = _dot_nt(w_kg, h_ref[...])
        k_scale = ML_HEAD_DIM ** -0.5
        for c in range(TM // ML_CHUNK):
            okt_ref[c] = ktg[:ML_WIDTH, c * ML_CHUNK:(c + 1) * ML_CHUNK] * k_scale
        g = ktg[ML_WIDTH:, :] + gb_ref[...]
        gate = lax.broadcasted_iota(jnp.int32, g.shape, 0)
        is_forget = ((gate >= ML_HEADS) & (gate < 2 * ML_HEADS)) | (gate >= 3 * ML_HEADS)
        log_sig = jnp.minimum(g, 0.0) - jnp.log1p(jnp.exp(-jnp.abs(g)))
        g = jnp.where(is_forget, log_sig, g)
        gates_t = jnp.concatenate([g, jnp.zeros((LANES - N_GATES, TM), F32)], axis=0)
        og_ref[...] = gates_t.T


def _inproj(y, mod, gains, w_in_t, gate_b_cols, jb, l):
    n_main = len(PROJ_BLOCKS)
    assert PROJ_BLOCKS == tuple(b for b in range(n_main + 1) if b != KM_BLOCK)

    def w_block(i, n):
        m = jnp.minimum(n, n_main - 1)
        return (jb, m + (m >= KM_BLOCK).astype(jnp.int32), 0)

    cpt = TM // ML_CHUNK
    blocks = 4 * (TM * D_MODEL + 2 * TN_PROJ * D_MODEL + N_GATES * D_MODEL + TM * TN_PROJ
                  + ML_WIDTH * TM + TM * LANES)
    return pl.pallas_call(
        _inproj_kernel,
        grid=(N_TOK // TM, n_main + 1),
        in_specs=[
            pl.BlockSpec((TM, D_MODEL), lambda i, n: (i, 0)),
            *_cond_specs(l),
            pl.BlockSpec((None, TN_PROJ, D_MODEL), w_block),
            pl.BlockSpec((None, ML_WIDTH, D_MODEL), lambda i, n: (jb, KM_BLOCK, 0)),
            pl.BlockSpec((None, N_GATES, D_MODEL), lambda i, n: (jb, AB_MAIN // N_GATES, 0)),
            pl.BlockSpec((None, N_GATES, 1), lambda i, n: (jb, 0, 0)),
        ],
        out_specs=[
            pl.BlockSpec((TM, TN_PROJ), lambda i, n: (i, jnp.minimum(n, n_main - 1))),
            pl.BlockSpec((cpt, ML_WIDTH, ML_CHUNK), lambda i, n: (i, 0, 0)),
            pl.BlockSpec((TM, LANES), lambda i, n: (i, 0)),
        ],
        out_shape=[jax.ShapeDtypeStruct((N_TOK, n_main * TN_PROJ), F32),
                   jax.ShapeDtypeStruct((N_TOK // ML_CHUNK, ML_WIDTH, ML_CHUNK), F32),
                   jax.ShapeDtypeStruct((N_TOK, LANES), F32)],
        scratch_shapes=[pltpu.VMEM((TM, D_MODEL), BF16)],
        compiler_params=_params(("parallel", "arbitrary"), blocks, 2 * TM * D_MODEL),
        name=f"inproj_{jb}",
    )(y, mod, gains, w_in_t, w_in_t, w_in_t, gate_b_cols)


def _stack_entry(ref, jb, fills_stack):
    if not fills_stack:
        return ref
    for j in range(ref.shape[0]):
        if j != jb:
            ref[j] = jnp.zeros(ref.shape[1:], ref.dtype)
    return ref.at[jb]


def _stack_spec(entry_shape, jb, fills_stack, n_stack=N_AB, lead=None):
    zeros = (0,) * len(entry_shape)
    if fills_stack:
        return pl.BlockSpec((lead, n_stack) + entry_shape, lambda b: (b, 0) + zeros)
    return pl.BlockSpec((lead, None) + entry_shape, lambda b: (b, jb) + zeros)


def _ctx_attn_kernel(q_ref, k_ref, v_ref, *rest, jb, fills_stack, spb):
    o_ref = rest[-3]
    scale = NA_HEAD_DIM ** -0.5
    lane = lax.broadcasted_iota(jnp.int32, (SEQ, LANES), 1)
    ones = jnp.ones((SEQ, LANES), BF16)
    for s in range(spb):
        rows = slice(s * SEQ, (s + 1) * SEQ)
        kc_ref = _stack_entry(rest[-2].at[s], jb, fills_stack)
        vc_ref = _stack_entry(rest[-1].at[s], jb, fills_stack)
        k_t = k_ref[rows, :].T
        v_t = v_ref[rows, :].T
        for h in range(NA_HEADS):
            sl = slice(h * NA_HEAD_DIM, (h + 1) * NA_HEAD_DIM)
            kc_ref[h] = k_t[sl, :]
            vc_ref[h] = v_t[sl, :]
        for hp in range(NA_WIDTH // LANES):
            pair = slice(hp * LANES, (hp + 1) * LANES)
            q_pair = q_ref[rows, pair] * scale
            v_aug = jnp.concatenate([v_ref[rows, pair].astype(BF16), ones], axis=1)
            outs = []
            for hh in range(LANES // NA_HEAD_DIM):
                own = (lane >= hh * NA_HEAD_DIM) & (lane < (hh + 1) * NA_HEAD_DIM)
                logits = _dot(jnp.where(own, q_pair, 0.0), k_t[pair, :])
                r = _dot(jnp.exp(logits - jnp.max(logits, axis=-1, keepdims=True)), v_aug)
                outs.append(r[:, :LANES] / r[:, LANES:])
            o_ref[rows, pair] = jnp.where(lane < NA_HEAD_DIM, outs[0], outs[1]).astype(o_ref.dtype)


def _ctx_attention(proj, jb, prev):
    spb = CTX_SEQS_PER_STEP
    rows = spb * SEQ
    cache = jax.ShapeDtypeStruct((BATCH, N_AB, NA_HEADS, NA_HEAD_DIM, SEQ), F32)
    cache_spec = _stack_spec((NA_HEADS, NA_HEAD_DIM, SEQ), jb, prev is None, lead=spb)
    in_specs = [pl.BlockSpec((rows, NA_WIDTH), lambda b, c=c: (b, c)) for c in (P_QA, P_KA, P_VA)]
    args = [proj, proj, proj]
    aliases = {}
    if prev is not None:
        in_specs += [pl.BlockSpec(memory_space=pl.ANY)] * 2
        args += list(prev)
        aliases = {3: 1, 4: 2}
    blocks = 4 * (4 * rows * NA_WIDTH + 2 * N_AB * NA_WIDTH * rows)
    return pl.pallas_call(
        functools.partial(_ctx_attn_kernel, jb=jb, fills_stack=prev is None, spb=spb),
        grid=(BATCH // spb,),
        in_specs=in_specs,
        out_specs=[pl.BlockSpec((rows, NA_WIDTH), lambda b: (b, 0)), cache_spec, cache_spec],
        out_shape=[jax.ShapeDtypeStruct((N_PROMPT, NA_WIDTH), BF16), cache, cache],
        input_output_aliases=aliases,
        compiler_params=_params(("parallel",), blocks),
        name=f"ctx_attention_{jb}",
    )(*args)


def _na_bias_kernel(rpb_ref, p_ref):
    layer = pl.program_id(0)
    head = pl.program_id(1)
    shape = (GRID_W, LANES)
    lane = lax.broadcasted_iota(jnp.int32, shape, 1)
    qc = lax.broadcasted_iota(jnp.int32, shape, 0)
    kc = lane & (GRID_W - 1)
    upper = lane >= GRID_W
    cs = jnp.clip(qc - NA_WIN_W // 2, 0, GRID_W - NA_WIN_W)
    col_in = (kc >= cs) & (kc < cs + NA_WIN_W)
    j = lax.broadcasted_iota(jnp.int32, (1, LANES), 1)
    dcol = jnp.clip(j - (GRID_W - 1), -(NA_WIN_W - 1), NA_WIN_W - 1) + (NA_WIN_W - 1)

    def toeplitz(dr, half):
        b = jnp.zeros((1, LANES), F32)
        for d in range(N_DC):
            b = jnp.where(dcol == d, rpb_ref[layer, head, dr, d], b)
        shift = (LANES - (GRID_W - 1) + half * GRID_W) % LANES
        return pltpu.roll(jnp.broadcast_to(b, shape), shift, 1, stride=1, stride_axis=0)

    blocks = [[toeplitz(dr, half) for half in range(2)] for dr in range(N_DR)]
    for pair in range(N_PAIR):
        both = jnp.where(upper, blocks[pair + 1][1], blocks[pair][0])
        p_ref[pair] = jnp.where(col_in, both, -jnp.inf)


def _na_bias(na_rpb):
    return pl.pallas_call(
        _na_bias_kernel,
        grid=(N_AB, NA_HEADS),
        in_specs=[pl.BlockSpec(memory_space=pltpu.SMEM)],
        out_specs=pl.BlockSpec((None, None, N_PAIR, GRID_W, LANES), lambda j, h: (j, h, 0, 0, 0)),
        out_shape=jax.ShapeDtypeStruct((N_AB, NA_HEADS, N_PAIR, GRID_W, LANES), F32),
        compiler_params=_params(("parallel", "parallel"), 4 * N_PAIR * GRID_W * LANES),
        name="na_bias",
    )(na_rpb)


def _na_window_start(r):
    return min(max(r - NA_KH // 2, 0), GRID_ROWS - NA_KH)


def _na_bias_block(p_ref, hh, qr, kr):
    rs = _na_window_start(qr)
    lo_ok = rs <= kr < rs + NA_KH
    hi_ok = rs <= kr + 1 < rs + NA_KH
    if not (lo_ok or hi_ok):
        return jnp.full((GRID_W, LANES), -jnp.inf, F32)
    dr = kr - qr + (NA_WIN_H - 1)
    assert 0 <= dr < N_PAIR
    blk = p_ref[hh, dr]
    if lo_ok and hi_ok:
        return blk
    lane = lax.broadcasted_iota(jnp.int32, (GRID_W, LANES), 1)
    keep = (lane < GRID_W) if lo_ok else (lane >= GRID_W)
    return jnp.where(keep, blk, -jnp.inf)


def _na_kernel(q_ref, k_ref, v_ref, kc_ref, vc_ref, p_ref, o_ref):
    scale = NA_HEAD_DIM ** -0.5
    n_q = NA_QROWS * GRID_W
    lane = lax.broadcasted_iota(jnp.int32, (n_q, LANES), 1)
    k_ctx_t = [kc_ref[b].reshape(LANES, PAST_LEN) for b in range(DEC_BATCH)]
    v_aug_ctx_t = [jnp.concatenate([vc_ref[b].reshape(LANES, PAST_LEN).astype(BF16),
                                    jnp.ones((LANES, PAST_LEN), BF16)], axis=0) for b in range(DEC_BATCH)]
    for q0 in range(0, GRID_ROWS, NA_QROWS):
        k_lo = _na_window_start(q0) // 2 * 2
        k_hi = -(-(_na_window_start(q0 + NA_QROWS - 1) + NA_KH) // 2) * 2
        n_k = (k_hi - k_lo) * GRID_W
        biases = [jnp.concatenate(
            [jnp.concatenate([_na_bias_block(p_ref, hh, qr, kr) for kr in range(k_lo, k_hi, 2)], axis=1)
             for qr in range(q0, q0 + NA_QROWS)], axis=0) for hh in range(LANES // NA_HEAD_DIM)]
        for b in range(DEC_BATCH):
            q_rows = slice(b * DEC_SEQ + q0 * GRID_W, b * DEC_SEQ + (q0 + NA_QROWS) * GRID_W)
            k_rows = slice(b * DEC_SEQ + k_lo * GRID_W, b * DEC_SEQ + k_hi * GRID_W)
            q_pair = q_ref[q_rows, :] * scale
            v_aug_loc = jnp.concatenate([v_ref[k_rows, :].astype(BF16), jnp.ones((n_k, LANES), BF16)], axis=1)
            outs = []
            for hh in range(LANES // NA_HEAD_DIM):
                own = (lane >= hh * NA_HEAD_DIM) & (lane < (hh + 1) * NA_HEAD_DIM)
                q = jnp.where(own, q_pair, 0.0)
                s_loc = _dot_nt(q, k_ref[k_rows, :]) + biases[hh]
                s_ctx = _dot(q, k_ctx_t[b])
                m = jnp.maximum(jnp.max(s_loc, axis=-1, keepdims=True), jnp.max(s_ctx, axis=-1, keepdims=True))
                r = _dot(jnp.exp(s_loc - m), v_aug_loc) + _dot_nt(jnp.exp(s_ctx - m), v_aug_ctx_t[b])
                outs.append(r[:, :LANES] / r[:, LANES:])
            o_ref[q_rows, :] = jnp.where(lane < NA_HEAD_DIM, outs[0], outs[1]).astype(o_ref.dtype)


def _na_attention(proj, cache_k_t, cache_v_t, bias_pairs, jb):
    hp = LANES // NA_HEAD_DIM
    assert N_PROMPT % N_SAMPLE == 0
    s0 = N_PROMPT // N_SAMPLE
    nb = NA_WIDTH // LANES
    ctx_spec = pl.BlockSpec((DEC_BATCH, None, hp, NA_HEAD_DIM, PAST_LEN), lambda c: (0, jb, c, 0, 0))
    blocks = 4 * (4 * N_SAMPLE * LANES + 2 * DEC_BATCH * hp * NA_HEAD_DIM * PAST_LEN
                  + hp * N_PAIR * GRID_W * LANES)
    return pl.pallas_call(
        _na_kernel,
        grid=(nb,),
        in_specs=[
            pl.BlockSpec((N_SAMPLE, LANES), lambda c: (s0, P_QA * nb + c)),
            pl.BlockSpec((N_SAMPLE, LANES), lambda c: (s0, P_KA * nb + c)),
            pl.BlockSpec((N_SAMPLE, LANES), lambda c: (s0, P_VA * nb + c)),
            ctx_spec, ctx_spec,
            pl.BlockSpec((None, hp, N_PAIR, GRID_W, LANES), lambda c: (jb, c, 0, 0, 0)),
        ],
        out_specs=pl.BlockSpec((N_SAMPLE, LANES), lambda c: (0, c)),
        out_shape=jax.ShapeDtypeStruct((N_SAMPLE, NA_WIDTH), BF16),
        compiler_params=_params(("parallel",), blocks),
        name=f"na_attention_{jb}",
    )(proj, proj, proj, cache_k_t, cache_v_t, bias_pairs)


def _running_rows(x, op, identity, reverse):
    n = x.shape[0]
    ridx = lax.broadcasted_iota(jnp.int32, x.shape, 0)
    sh = 1
    while sh < n:
        if reverse:
            x = op(x, jnp.where(ridx < n - sh, pltpu.roll(x, n - sh, 0), identity))
        else:
            x = op(x, jnp.where(ridx >= sh, pltpu.roll(x, sh, 0), identity))
        sh *= 2
    return x


def _mlstm_kernel(*refs, n_chunks, has_state, jb, fills_stack, spb, layer):
    if has_state:
        (q_ref, kt_ref, v_ref, o_ref, gt_ref, hn_ref, c0_ref, n0_ref, m0_ref,
         out_ref, cout_ref, nout_ref, mout_ref, hdir_ref, caug_ref) = refs
    else:
        (q_ref, kt_ref, v_ref, o_ref, gt_ref, hn_ref, *_unused,
         out_ref, cout_ref, nout_ref, mout_ref, hdir_ref, caug_ref) = refs
    cout_refs = [_stack_entry(cout_ref.at[s], jb, fills_stack) for s in range(spb)]
    nout_refs = [_stack_entry(nout_ref.at[s], jb, fills_stack) for s in range(spb)]
    mout_refs = [_stack_entry(mout_ref.at[s], jb, fills_stack) for s in range(spb)]
    T = ML_CHUNK
    dh = ML_HEAD_DIM
    seq_len = n_chunks * T
    b = pl.program_id(0)
    row = lax.broadcasted_iota(jnp.int32, (T, T), 0)
    col = lax.broadcasted_iota(jnp.int32, (T, T), 1)
    lane = lax.broadcasted_iota(jnp.int32, (T, LANES), 1)
    lane_row = lax.broadcasted_iota(jnp.int32, (1, LANES), 1)
    ones = jnp.ones((T, dh), F32)

    def gate_lanes(d, h):
        gi = 2 * d * ML_HEADS + h
        return gi, gi + ML_HEADS

    scans = [(s, d) for s in range(spb) for d in range(2)]
    m_init = []
    for s, d in scans:
        m_row = jnp.zeros((1, LANES), F32)
        for h in range(ML_HEADS):
            if has_state:
                n_bcast = jnp.broadcast_to(n0_ref[s, d, h:h + 1, :], (dh, dh)).T
                caug_ref[s, d, h] = jnp.concatenate([c0_ref[s, d, h], n_bcast], axis=1)
                m0 = m0_ref[b * spb + s, layer, d, h]
                m_row = jnp.where(lane_row == gate_lanes(d, h)[1], m0, m_row)
            else:
                caug_ref[s, d, h] = jnp.zeros((dh, 2 * dh), F32)
        m_init.append(m_row)

    def chunk_body(ci, m_rows):
        m_next = []
        for scan, (s, d) in enumerate(scans):
            causal = (col <= row) if d == 0 else (col >= row)
            last = T - 1 if d == 0 else 0
            c = ci if d == 0 else n_chunks - 1 - ci
            rows = pl.ds(pl.multiple_of(s * seq_len + c * T, T), T)
            m_prev = m_rows[scan]
            mine = (lane >= (2 * d + 1) * ML_HEADS) & (lane < (2 * d + 2) * ML_HEADS)
            g = gt_ref[rows, :]
            cum = _running_rows(g, jnp.add, 0.0, reverse=(d == 1))
            i_al = pltpu.roll(g, ML_HEADS, 1)
            cvec = jnp.where(mine, i_al - cum, 0.0)
            cum = jnp.where(mine, cum, 0.0)
            m_run = jnp.maximum(m_prev, _running_rows(cvec, jnp.maximum, -jnp.inf, reverse=(d == 1)))
            m_t = cum + m_run
            e_negm = jnp.exp(-m_t)
            m_run_last = m_run[last:last + 1, :]
            wi_last = jnp.exp(m_prev - m_run_last)
            cvec_t = cvec.T
            for h in range(ML_HEADS):
                _, gf = gate_lanes(d, h)
                hl = slice(h * dh, (h + 1) * dh)
                m_run_b = jnp.broadcast_to(m_run[:, gf:gf + 1], (T, T))
                e_negm_b = jnp.broadcast_to(e_negm[:, gf:gf + 1], (T, dh))
                c_row = cvec_t[gf:gf + 1, :]
                w_d = jnp.exp(jnp.where(causal, c_row - m_run_b, -jnp.inf))
                w_inter_b = jnp.exp(m_prev[:, gf:gf + 1] - m_run_b)
                qh = q_ref[rows, hl]
                kh_t = kt_ref[s * n_chunks + c, hl, :]
                v_aug = jnp.concatenate([v_ref[rows, hl], ones], axis=1)
                sc = _dot(qh, kh_t) * w_d
                c_aug = caug_ref[s, d, h]
                nd = jnp.concatenate([w_inter_b, w_inter_b], axis=1) * _dot(qh, c_aug) + _dot(sc, v_aug)
                hdir_ref[d, rows, hl] = nd[:, :dh] / jnp.maximum(jnp.abs(nd[:, dh:]), e_negm_b)
                kw_t = kh_t * jnp.exp(c_row - m_run_last[:, gf:gf + 1])
                caug_ref[s, d, h] = wi_last[:, gf:gf + 1] * c_aug + _dot(kw_t, v_aug)
            m_next.append(m_t[last:last + 1, :])
        return tuple(m_next)

    m_fin = lax.fori_loop(0, n_chunks, chunk_body, tuple(m_init), unroll=2)
    for scan, (s, d) in enumerate(scans):
        for h in range(ML_HEADS):
            c_aug = caug_ref[s, d, h]
            cout_refs[s][d, h] = c_aug[:, :dh]
            nout_refs[s][d, h:h + 1, :] = c_aug[:, dh:].T[0:1, :]
    for s in range(spb):
        m_heads = [pltpu.roll(m_fin[2 * s + d], LANES - gate_lanes(d, 0)[1], 1) for d in range(2)]
        mout_refs[s][...] = jnp.concatenate(m_heads + [jnp.zeros((COND_ROWS - 2, LANES), F32)], axis=0)

    for h in range(ML_HEADS):
        hl = slice(h * dh, (h + 1) * dh)
        h_sum = hdir_ref[0, :, hl] + hdir_ref[1, :, hl]
        gated = jax.nn.sigmoid(o_ref[:, hl]) * (_rms_scale(h_sum) * hn_ref[layer:layer + 1, hl])
        out_ref[:, hl] = gated.astype(out_ref.dtype)


def _mlstm(proj, keys_t, gates, hnorm_rows, layer, seq_len, n_seq, row0, spb, n_stack=1, jb=0,
           prev=None, state=None):
    has_state = state is not None
    assert n_seq % spb == 0 and row0 % (spb * seq_len) == 0
    r0 = row0 // (spb * seq_len)
    rows = spb * seq_len
    nb = TN_PROJ // ML_WIDTH
    n_chunks = seq_len // ML_CHUNK
    in_specs = [
        pl.BlockSpec((rows, ML_WIDTH), lambda b: (r0 + b, P_QM * nb)),
        pl.BlockSpec((spb * n_chunks, ML_WIDTH, ML_CHUNK), lambda b: (r0 + b, 0, 0)),
        pl.BlockSpec((rows, ML_WIDTH), lambda b: (r0 + b, P_VM * nb)),
        pl.BlockSpec((rows, ML_WIDTH), lambda b: (r0 + b, P_OM * nb)),
        pl.BlockSpec((rows, LANES), lambda b: (r0 + b, 0)),
        pl.BlockSpec((N_AB, ML_WIDTH), lambda b: (0, 0)),
    ]
    args = [proj, keys_t, proj, proj, gates, hnorm_rows]
    aliases = {}
    if has_state:
        st_c, st_n, st_m = state
        in_specs += [
            pl.BlockSpec((spb, None, 2, ML_HEADS, ML_HEAD_DIM, ML_HEAD_DIM), lambda b: (b, layer, 0, 0, 0, 0)),
            pl.BlockSpec((spb, None, 2, ML_HEADS, ML_HEAD_DIM), lambda b: (b, layer, 0, 0, 0)),
            pl.BlockSpec(memory_space=pltpu.SMEM),
        ]
        args += [st_c, st_n, st_m]
    elif prev is not None:
        aliases = {len(args) + k: 1 + k for k in range(3)}
        in_specs += [pl.BlockSpec(memory_space=pl.ANY)] * 3
        args += list(prev)
    fills_stack = prev is None
    c_bytes = 2 * ML_HEADS * ML_HEAD_DIM * ML_HEAD_DIM
    blocks = 4 * (5 * rows * ML_WIDTH + rows * LANES + spb * (1 + n_stack) * c_bytes)
    scratch = 4 * (2 * rows * ML_WIDTH + spb * 2 * c_bytes)
    return pl.pallas_call(
        functools.partial(_mlstm_kernel, n_chunks=n_chunks, has_state=has_state, jb=jb, fills_stack=fills_stack,
                          spb=spb, layer=layer),
        grid=(n_seq // spb,),
        in_specs=in_specs,
        out_specs=[
            pl.BlockSpec((rows, ML_WIDTH), lambda b: (b, 0)),
            _stack_spec((2, ML_HEADS, ML_HEAD_DIM, ML_HEAD_DIM), jb, fills_stack, n_stack, spb),
            _stack_spec((2, ML_HEADS, ML_HEAD_DIM), jb, fills_stack, n_stack, spb),
            _stack_spec((COND_ROWS, LANES), jb, fills_stack, n_stack, spb),
        ],
        out_shape=[
            jax.ShapeDtypeStruct((n_seq * seq_len, ML_WIDTH), BF16),
            jax.ShapeDtypeStruct((n_seq, n_stack, 2, ML_HEADS, ML_HEAD_DIM, ML_HEAD_DIM), F32),
            jax.ShapeDtypeStruct((n_seq, n_stack, 2, ML_HEADS, ML_HEAD_DIM), F32),
            jax.ShapeDtypeStruct((n_seq, n_stack, COND_ROWS, LANES), F32),
        ],
        scratch_shapes=[pltpu.VMEM((2, rows, ML_WIDTH), F32),
                        pltpu.VMEM((spb, 2, ML_HEADS, ML_HEAD_DIM, 2 * ML_HEAD_DIM), F32)],
        input_output_aliases=aliases,
        compiler_params=_params(("parallel",), blocks, scratch),
        name=f"mlstm_{'latent' if has_state else 'context'}_{jb}",
    )(*args)


def _outproj_kernel(x_ref, mod_ref, g_ref, ap_ref, mp_ref, as_ref, ms_ref, wa_ref, wm_ref, o_ref):
    i = pl.program_id(0)

    def finish(a_ref, m_ref):
        wa = wa_ref[...].astype(BF16)
        wm = wm_ref[...].astype(BF16)
        for rows, r in _sub_rows(i, TM_OUT):
            out = _dot(a_ref[rows, :], wa) + _dot(m_ref[rows, :], wm)
            o_ref[rows, :] = _gated_residual(x_ref[rows, :], out, g_ref, mod_ref, r, 1, 1.0)

    is_prompt = i < N_PROMPT // TM_OUT
    pl.when(is_prompt)(functools.partial(finish, ap_ref, mp_ref))
    pl.when(jnp.logical_not(is_prompt))(functools.partial(finish, as_ref, ms_ref))


def _outproj(y, mod, gains, a_p, m_p, a_s, m_s, ab_w_out, jb, l):
    tm = TM_OUT
    n_p = N_PROMPT // tm
    n_s = N_SAMPLE // tm
    p_spec = pl.BlockSpec((tm, NA_WIDTH), lambda i: (jnp.minimum(i, n_p - 1), 0))
    s_spec = pl.BlockSpec((tm, NA_WIDTH), lambda i: (jnp.clip(i - n_p, 0, n_s - 1), 0))
    blocks = 4 * (2 * tm * D_MODEL + 4 * tm * NA_WIDTH + D_MODEL * D_MODEL)
    return pl.pallas_call(
        _outproj_kernel,
        grid=(N_TOK // tm,),
        in_specs=[
            pl.BlockSpec((tm, D_MODEL), lambda i: (i, 0)),
            *_cond_specs(l),
            p_spec, p_spec, s_spec, s_spec,
            pl.BlockSpec((None, NA_WIDTH, D_MODEL), lambda i: (jb, 0, 0)),
            pl.BlockSpec((None, ML_WIDTH, D_MODEL), lambda i: (jb, 1, 0)),
        ],
        out_specs=pl.BlockSpec((tm, D_MODEL), lambda i: (i, 0)),
        out_shape=jax.ShapeDtypeStruct((N_TOK, D_MODEL), F32),
        compiler_params=_params(("parallel",), blocks),
        name=f"outproj_{jb}",
    )(y, mod, gains, a_p, m_p, a_s, m_s, ab_w_out, ab_w_out)


def kernel(x_prompt, x_sample, cache_k, cache_v, state_C, state_n, state_m, c, c_ctx, ada_w, ada_b, norm_g,
           ffn_w13, ffn_w2, ab_w_in, ab_w_out, na_rpb, ml_gate_b, ml_hnorm, sc_w_in, sc_conv_w, sc_conv_b,
           sc_w_out):
    assert NA_WIDTH == ML_WIDTH == TN_PROJ and AB_MAIN % N_GATES == 0
    assert math.frexp(NA_HEAD_DIM ** -0.5)[0] == 0.5
    assert LANES == 2 * NA_HEAD_DIM
    assert N_PROMPT % TM == 0 and N_SAMPLE % TM == 0 and TM % COND_UNIT == 0
    cond = jnp.concatenate([c_ctx[None, :], c, jnp.zeros((COND_ROWS - 1 - DEC_BATCH, D_MODEL), F32)], axis=0)
    mod = _modulation(cond, ada_w, ada_b)
    gains = norm_g.reshape(DEPTH, N_GAINS, D_MODEL)
    bias_pairs = _na_bias(na_rpb)
    w_in_t = jnp.swapaxes(ab_w_in, 1, 2)
    cache_k_t = jnp.swapaxes(cache_k, 3, 4)
    cache_v_t = jnp.swapaxes(cache_v, 3, 4)
    gate_b_cols = ml_gate_b.reshape(N_AB, N_GATES, 1)
    hnorm_rows = ml_hnorm
    state = (state_C, state_n, state_m)

    kv_new = None
    st_new = None
    y = None
    y = (x_prompt.reshape(N_PROMPT, D_MODEL), x_sample.reshape(N_SAMPLE, D_MODEL))
    for l in range(DEPTH):
        y = _ffn_sublayer(y, mod, gains, ffn_w13, ffn_w2, l, 0)
        j = l // 2
        if l % 2 == 0:
            proj, keys_t, gates = _inproj(y, mod, gains, w_in_t, gate_b_cols, j, l)
            a_p, *kv_new = _ctx_attention(proj, j, kv_new)
            m_p, *st_new = _mlstm(proj, keys_t, gates, hnorm_rows, j, SEQ, BATCH, 0,
                                  ML_SEQS_PER_STEP, n_stack=N_AB, jb=j, prev=st_new)
            a_s = _na_attention(proj, cache_k_t, cache_v_t, bias_pairs, j)
            m_s, _, _, _ = _mlstm(proj, keys_t, gates, hnorm_rows, j, DEC_SEQ, DEC_BATCH, N_PROMPT,
                                  ML_SEQS_PER_STEP, state=state)
            y = _outproj(y, mod, gains, a_p, m_p, a_s, m_s, ab_w_out, j, l)
        else:
            y = _sconv_sublayer(y, mod, gains, sc_w_in, sc_conv_w, sc_conv_b, sc_w_out, j, l)
        if l < DEPTH - 1:
            y = _ffn_sublayer(y, mod, gains, ffn_w13, ffn_w2, l, 1)

    y_p, y_s = _ffn_sublayer(y, mod, gains, ffn_w13, ffn_w2, DEPTH - 1, 1, split_out=True)
    y_p = y_p.reshape(BATCH, SEQ, D_MODEL)
    y_s = y_s.reshape(DEC_BATCH, DEC_SEQ, D_MODEL)
    new_c, new_n, m_rows = st_new
    new_m = m_rows[:, :, :2, :ML_HEADS]
    return (y_p, y_s, jnp.swapaxes(kv_new[0], 3, 4), jnp.swapaxes(kv_new[1], 3, 4), new_c, new_n, new_m)
```
